```python
import math
import jax, jax.numpy as jnp
from jax import lax
import numpy as np

D_MODEL = 1024
BATCH = 8
SEQ = 2048
DEPTH = 4

HEAD_DIM = 64
W_BR = D_MODEL // 2
N_BRANCH = 3
H_A = W_BR // HEAD_DIM
MOBA_BLOCK = 256
MOBA_TOPK = 3
MOBA_QCHUNK = 32
DIL_PAIRS = ((128, 1), (512, 4), (2048, 16))
N_DIL = len(DIL_PAIRS)
H_B = W_BR // HEAD_DIM
DIL_QCHUNK = 64
H_C = W_BR // HEAD_DIM
KV_C = 2
SWA_WINDOW = 128
N_BUCKETS = 32
REL_MAX_DIST = 2048
H_TOT = H_A + N_DIL * H_B + H_C
OFF_A = 0
OFF_B = H_A
OFF_C = H_A + N_DIL * H_B

EPS = 1e-6
NEG_INF = -1e30

SPLIT_SIZES = ([W_BR] * 4
               + [N_DIL * W_BR] * 3 + [W_BR]
               + [W_BR, KV_C * HEAD_DIM, KV_C * HEAD_DIM, W_BR]
               + [N_BRANCH * D_MODEL])
C_IN = int(sum(SPLIT_SIZES))
SPLIT_POINTS = [int(v) for v in np.cumsum(SPLIT_SIZES)[:-1]]

kernel_name = "hybrid_moba_dilated_swa_gated_trunk"


def rms_norm(x, g):
    x32 = x.astype(jnp.float32)
    y = x32 * lax.rsqrt(jnp.mean(x32 * x32, axis=-1, keepdims=True) + EPS)
    return (y * g.astype(jnp.float32)).astype(x.dtype)


def rel_bucket(dist):
    dist = jnp.maximum(dist, 0)
    max_exact = N_BUCKETS // 2
    log_ratio = jnp.log(jnp.maximum(dist, 1).astype(jnp.float32) / max_exact) / math.log(REL_MAX_DIST / max_exact)
    large = max_exact + (log_ratio * (N_BUCKETS - max_exact)).astype(jnp.int32)
    large = jnp.minimum(large, N_BUCKETS - 1)
    return jnp.where(dist < max_exact, dist, large)


def moba_attention(q, k, v, bias_tab):
    b, s, h, dh = q.shape
    s_pad = -(-s // MOBA_BLOCK) * MOBA_BLOCK
    pad = ((0, 0), (0, s_pad - s), (0, 0), (0, 0))
    qh, kh, vh = [jnp.pad(t, pad).transpose(0, 2, 1, 3) for t in (q, k, v)]
    nb = s_pad // MOBA_BLOCK
    kb = kh.reshape(b, h, nb, MOBA_BLOCK, dh)
    vb = vh.reshape(b, h, nb, MOBA_BLOCK, dh)
    k_mean = jnp.mean(kb.astype(jnp.float32), axis=3)
    gate = jnp.einsum('bhsd,bhnd->bhsn', qh.astype(jnp.float32), k_mean)
    q_blk = jnp.arange(s_pad) // MOBA_BLOCK
    past = jnp.arange(nb)[None, :] < q_blk[:, None]
    gate = jnp.where(past, gate, NEG_INF)
    n_sel = min(MOBA_TOPK, max(nb - 1, 1))
    _, top_idx = lax.top_k(gate, n_sel)
    sel_valid = jnp.arange(n_sel)[None, :] < jnp.minimum(q_blk, MOBA_TOPK)[:, None]
    scale = dh ** -0.5
    bias_tab = bias_tab.astype(jnp.float32)
    b_idx = jnp.arange(b)[:, None, None, None]
    h_idx = jnp.arange(h)[None, :, None, None]
    h_idx5 = jnp.arange(h)[None, :, None, None, None]
    blk_off = jnp.arange(MOBA_BLOCK)

    def chunk(c):
        t0 = c * MOBA_QCHUNK
        t = t0 + jnp.arange(MOBA_QCHUNK)
        q_c = lax.dynamic_slice_in_dim(qh, t0, MOBA_QCHUNK, axis=2)
        idx_c = lax.dynamic_slice_in_dim(top_idx, t0, MOBA_QCHUNK, axis=2)
        valid_c = lax.dynamic_slice_in_dim(sel_valid, t0, MOBA_QCHUNK, axis=0)
        k_sel = kb[b_idx, h_idx, idx_c]
        v_sel = vb[b_idx, h_idx, idx_c]
        s_sel = jnp.einsum('bhqd,bhqjkd->bhqjk', q_c, k_sel,
                           preferred_element_type=jnp.float32) * scale
        key_pos = idx_c[..., None] * MOBA_BLOCK + blk_off
        dist = t[:, None, None] - key_pos
        bias_sel = bias_tab[h_idx5, rel_bucket(dist)]
        s_sel = jnp.where(valid_c[:, :, None], s_sel + bias_sel, NEG_INF)
        n_own = t0 // MOBA_BLOCK
        k_own = lax.dynamic_index_in_dim(kb, n_own, axis=2, keepdims=False)
        v_own = lax.dynamic_index_in_dim(vb, n_own, axis=2, keepdims=False)
        s_own = jnp.einsum('bhqd,bhkd->bhqk', q_c, k_own,
                           preferred_element_type=jnp.float32) * scale
        dist_own = t[:, None] - (n_own * MOBA_BLOCK + blk_off)[None, :]
        s_own = jnp.where(dist_own >= 0, s_own + bias_tab[:, rel_bucket(dist_own)], NEG_INF)
        n_s = n_sel * MOBA_BLOCK
        logits = jnp.concatenate([s_sel.reshape(b, h, MOBA_QCHUNK, n_s), s_own], axis=-1)
        p = jax.nn.softmax(logits, axis=-1)
        p_sel = p[..., :n_s].reshape(b, h, MOBA_QCHUNK, n_sel, MOBA_BLOCK).astype(v.dtype)
        p_own = p[..., n_s:].astype(v.dtype)
        out = (jnp.einsum('bhqjk,bhqjkd->bhqd', p_sel, v_sel, preferred_element_type=jnp.float32)
               + jnp.einsum('bhqk,bhkd->bhqd', p_own, v_own, preferred_element_type=jnp.float32))
        return out.astype(v.dtype)

    out = lax.map(chunk, jnp.arange(s_pad // MOBA_QCHUNK))
    out = out.transpose(1, 0, 3, 2, 4).reshape(b, s_pad, h, dh)
    return out[:, :s]


def dilated_attention(q, k, v, bias_tab):
    b, s, _, h, dh = q.shape
    scale = dh ** -0.5
    bias_tab = bias_tab.astype(jnp.float32)
    n_keys = [w // d + 1 for (w, d) in DIL_PAIRS]
    k_pads = [jnp.pad(k[:, :, gi], ((0, 0), (w, 0), (0, 0), (0, 0))) for gi, (w, d) in enumerate(DIL_PAIRS)]
    v_pads = [jnp.pad(v[:, :, gi], ((0, 0), (w, 0), (0, 0), (0, 0))) for gi, (w, d) in enumerate(DIL_PAIRS)]
    bias_g = [bias_tab[gi * h:(gi + 1) * h][:, rel_bucket(d * jnp.arange(n_keys[gi]))]
              for gi, (w, d) in enumerate(DIL_PAIRS)]

    def chunk(c):
        t0 = c * DIL_QCHUNK
        t = t0 + jnp.arange(DIL_QCHUNK)
        q_c = lax.dynamic_slice_in_dim(q, t0, DIL_QCHUNK, axis=1)
        outs, lses = [], []
        for gi, (w, d) in enumerate(DIL_PAIRS):
            steps = jnp.arange(n_keys[gi]) * d
            idx = t[:, None] + w - steps[None, :]
            k_g = k_pads[gi][:, idx]
            v_g = v_pads[gi][:, idx]
            logits = jnp.einsum('bqhd,bqjhd->bhqj', q_c[:, :, gi], k_g,
                                preferred_element_type=jnp.float32) * scale + bias_g[gi][:, None, :]
            logits = jnp.where((t[:, None] - steps[None, :]) >= 0, logits, NEG_INF)
            lse = jax.nn.logsumexp(logits, axis=-1)
            p = jnp.exp(logits - lse[..., None]).astype(v.dtype)
            outs.append(jnp.einsum('bhqj,bqjhd->bqhd', p, v_g, preferred_element_type=jnp.float32))
            lses.append(lse)
        wts = jax.nn.softmax(jnp.stack(lses, axis=0), axis=0)
        out = sum(wts[gi].transpose(0, 2, 1)[..., None] * outs[gi] for gi in range(N_DIL))
        return out.astype(v.dtype)

    out = lax.map(chunk, jnp.arange(s // DIL_QCHUNK))
    return out.transpose(1, 0, 2, 3, 4).reshape(b, s, h, dh)


def swa_sink_attention(q, k, v, sinks, bias_tab):
    b, s, h, dh = q.shape
    kvh = k.shape[2]
    rep = h // kvh
    W = SWA_WINDOW
    nb = s // W
    scale = dh ** -0.5
    qb = q.reshape(b, nb, W, kvh, rep, dh)
    kb = k.reshape(b, nb, W, kvh, dh)
    vb = v.reshape(b, nb, W, kvh, dh)

    def with_prev(t):
        prev = jnp.pad(t, ((0, 0), (1, 0), (0, 0), (0, 0), (0, 0)))[:, :-1]
        return jnp.concatenate([prev, t], axis=2)

    kk, vv = with_prev(kb), with_prev(vb)
    logits = jnp.einsum('bnqgrd,bnkgd->bngrqk', qb, kk,
                        preferred_element_type=jnp.float32) * scale
    i = jnp.arange(W)
    j = jnp.arange(2 * W)
    dist = i[:, None] + W - j[None, :]
    key_pos = jnp.arange(nb)[:, None] * W - W + j[None, :]
    valid = ((dist >= 0) & (dist < W))[None] & (key_pos >= 0)[:, None, :]
    bias = bias_tab.astype(jnp.float32)[:, rel_bucket(dist)].reshape(kvh, rep, W, 2 * W)
    logits = jnp.where(valid[None, :, None, None], logits + bias, NEG_INF)
    sink = sinks.astype(jnp.float32).reshape(kvh, rep)[None, None, :, :, None, None]
    m = jnp.maximum(jnp.max(logits, axis=-1, keepdims=True), sink)
    e = jnp.exp(logits - m)
    p = e / (jnp.sum(e, axis=-1, keepdims=True) + jnp.exp(sink - m))
    out = jnp.einsum('bngrqk,bnkgd->bnqgrd', p.astype(v.dtype), vv, preferred_element_type=jnp.float32)
    return out.reshape(b, s, h, dh).astype(v.dtype)


def setup_inputs(seed: int = 0) -> dict:
    key = jax.random.key(seed)
    ks = jax.random.split(key, 8)
    x = jax.random.normal(ks[0], (BATCH, SEQ, D_MODEL), jnp.float32)
    ln_g = 1.0 + 0.02 * jax.random.normal(ks[1], (DEPTH, D_MODEL), jnp.float32)
    w_in = jax.random.normal(ks[2], (DEPTH, D_MODEL, C_IN), jnp.float32) * D_MODEL ** -0.5
    qk_g = 1.0 + 0.02 * jax.random.normal(ks[3], (DEPTH, 6, HEAD_DIM), jnp.float32)
    sinks = 0.5 * jax.random.normal(ks[4], (DEPTH, H_C), jnp.float32)
    w_branch = jax.random.normal(ks[5], (DEPTH, N_BRANCH, W_BR, D_MODEL), jnp.float32) * W_BR ** -0.5
    w_out = jax.random.normal(ks[6], (DEPTH, D_MODEL, D_MODEL), jnp.float32) * D_MODEL ** -0.5
    rel_bias = 0.2 * jax.random.normal(ks[7], (H_TOT, N_BUCKETS), jnp.float32)
    return {"x": x, "ln_g": ln_g, "w_in": w_in, "qk_g": qk_g, "sinks": sinks,
            "w_branch": w_branch, "w_out": w_out, "rel_bias": rel_bias}


def reference(x, ln_g, w_in, qk_g, sinks, w_branch, w_out, rel_bias):
    b, s, _ = x.shape
    for l in range(DEPTH):
        hn = rms_norm(x, ln_g[l])
        proj = jnp.einsum('bsd,dc->bsc', hn, w_in[l])
        (qa, ka, va, ga, qb, kb, vb, gb, qc, kc, vc, gc, gate_logits) = jnp.split(proj, SPLIT_POINTS, axis=-1)
        g = qk_g[l]
        qa = rms_norm(qa.reshape(b, s, H_A, HEAD_DIM), g[0])
        ka = rms_norm(ka.reshape(b, s, H_A, HEAD_DIM), g[1])
        o_a = moba_attention(qa, ka, va.reshape(b, s, H_A, HEAD_DIM), rel_bias[OFF_A:OFF_A + H_A])
        qb = rms_norm(qb.reshape(b, s, N_DIL, H_B, HEAD_DIM), g[2])
        kb = rms_norm(kb.reshape(b, s, N_DIL, H_B, HEAD_DIM), g[3])
        o_b = dilated_attention(qb, kb, vb.reshape(b, s, N_DIL, H_B, HEAD_DIM),
                                rel_bias[OFF_B:OFF_B + N_DIL * H_B])
        qc = rms_norm(qc.reshape(b, s, H_C, HEAD_DIM), g[4])
        kc = rms_norm(kc.reshape(b, s, KV_C, HEAD_DIM), g[5])
        o_c = swa_sink_attention(qc, kc, vc.reshape(b, s, KV_C, HEAD_DIM), sinks[l],
                                 rel_bias[OFF_C:OFF_C + H_C])
        branches = jnp.stack([o_a.reshape(b, s, W_BR) * jax.nn.silu(ga),
                              o_b.reshape(b, s, W_BR) * jax.nn.silu(gb),
                              o_c.reshape(b, s, W_BR) * jax.nn.silu(gc)], axis=2)
        br = jnp.einsum('bsiw,iwd->bsid', branches, w_branch[l])
        gates = jax.nn.sigmoid(gate_logits.reshape(b, s, N_BRANCH, D_MODEL))
        merged = jnp.sum(gates * br, axis=2)
        x = x + jnp.einsum('bsd,de->bse', merged, w_out[l])
    return x
```

```python
import functools
import math

import jax
import jax.numpy as jnp
from jax import lax
from jax.experimental import pallas as pl
from jax.experimental.pallas import tpu as pltpu

D_MODEL = 1024
HEAD_DIM = 64
W_BR = D_MODEL // 2
N_HEADS = W_BR // HEAD_DIM
N_PAIRS = N_HEADS // 2
MOBA_BLOCK = 256
MOBA_TOPK = 3
DIL_PAIRS = ((128, 1), (512, 4), (2048, 16))
N_DIL = len(DIL_PAIRS)
KV_C = 2
SWA_WINDOW = 128
N_BUCKETS = 32
REL_MAX_DIST = 2048
H_TOT = N_HEADS + N_DIL * N_HEADS + N_HEADS
OFF_B = N_HEADS
OFF_C = N_HEADS + N_DIL * N_HEADS
EPS = 1e-6
NEG = -1e30

LANE = 128
BAND = 128
TN = 256
ROWS = 256
VMEM_LIMIT = 56 * 1024 * 1024

ACT = jnp.bfloat16
F32 = jnp.float32

_SRC = tuple(range(0, 30)) + (31, 32, 30) + tuple(range(33, 45))
NJ = len(_SRC)
_J_B, _J_C, _J_G = 8, 28, 33
PA_W, PB_W, PC_W, PG_W = 2048, 5120, 1280, 3072


def _cparams(n_axes):
    return pltpu.CompilerParams(dimension_semantics=("arbitrary",) * n_axes, vmem_limit_bytes=VMEM_LIMIT)


def _proj_kernel(src_ref, x_ref, lng_ref, w_ref, gain_ref, e_ref, pa_ref, pb_ref, pc_ref, pg_ref,
                 xn_ref, acc_ref):
    del src_ref
    j = pl.program_id(1)
    n_chunks = x_ref.shape[1] // ROWS

    def rows_of(i):
        return pl.ds(pl.multiple_of(i * ROWS, ROWS), ROWS)

    @pl.when(j == 0)
    def _():
        def body(i, c):
            xv = x_ref[0, rows_of(i), :]
            ms = jnp.mean(xv * xv, axis=-1, keepdims=True)
            xn_ref[rows_of(i), :] = (xv * lax.rsqrt(ms + EPS) * lng_ref[...]).astype(ACT)
            return c
        lax.fori_loop(0, n_chunks, body, 0)

    acc_ref[...] = jnp.dot(xn_ref[...], w_ref[...], preferred_element_type=F32)

    def head_norm(a):
        ss = jnp.dot((a * a).astype(ACT), e_ref[...], preferred_element_type=F32)
        return a * lax.rsqrt(ss * (1.0 / HEAD_DIM) + EPS) * gain_ref[...]

    def silu(a):
        return a * jax.nn.sigmoid(a)

    def k_norm_v_plain(a):
        lane = lax.broadcasted_iota(jnp.int32, a.shape, 1)
        return jnp.where(lane < KV_C * HEAD_DIM, head_norm(a), a)

    def epilogue(cond, out_ref, fn):
        @pl.when(cond)
        def _():
            def body(i, c):
                out_ref[0, rows_of(i), :] = fn(acc_ref[rows_of(i), :]).astype(out_ref.dtype)
                return c
            lax.fori_loop(0, n_chunks, body, 0)

    def between(lo, hi):
        return (j >= lo) & (j < hi)

    epilogue(between(0, 4), pa_ref, head_norm)
    epilogue(between(4, 6), pa_ref, lambda a: a)
    epilogue(between(6, 8), pa_ref, silu)
    epilogue(between(8, 20), pb_ref, head_norm)
    epilogue(between(20, 26), pb_ref, lambda a: a)
    epilogue(between(26, 28), pb_ref, silu)
    epilogue(between(28, 30), pc_ref, head_norm)
    epilogue(between(30, 32), pc_ref, silu)
    epilogue(j == 32, pc_ref, k_norm_v_plain)
    epilogue(j >= _J_G, pg_ref, jax.nn.sigmoid)


def _project(x, lng, w, gain_row, e_mat, src):
    b, s, d = x.shape

    def out_spec(j0, n):
        return pl.BlockSpec((1, s, TN), lambda bi, j, src_ref: (bi, 0, jnp.clip(j - j0, 0, n - 1)))

    grid_spec = pltpu.PrefetchScalarGridSpec(
        num_scalar_prefetch=1,
        grid=(b, NJ),
        in_specs=[
            pl.BlockSpec((1, s, d), lambda bi, j, src_ref: (bi, 0, 0)),
            pl.BlockSpec((1, d), lambda bi, j, src_ref: (0, 0)),
            pl.BlockSpec((d, TN), lambda bi, j, src_ref: (0, src_ref[j])),
            pl.BlockSpec((1, TN), lambda bi, j, src_ref: (0, src_ref[j])),
            pl.BlockSpec((TN, TN), lambda bi, j, src_ref: (0, 0)),
        ],
        out_specs=[
            out_spec(0, PA_W // TN),
            out_spec(_J_B, PB_W // TN),
            out_spec(_J_C, PC_W // TN),
            out_spec(_J_G, PG_W // TN),
        ],
        scratch_shapes=[pltpu.VMEM((s, d), ACT), pltpu.VMEM((s, TN), F32)],
    )
    return pl.pallas_call(
        _proj_kernel,
        grid_spec=grid_spec,
        out_shape=[
            jax.ShapeDtypeStruct((b, s, PA_W), ACT),
            jax.ShapeDtypeStruct((b, s, PB_W), ACT),
            jax.ShapeDtypeStruct((b, s, PC_W), ACT),
            jax.ShapeDtypeStruct((b, s, PG_W), ACT),
        ],
        compiler_params=_cparams(2),
        name="proj",
    )(src, x, lng, w, gain_row, e_mat)


def _toeplitz(lo, hi, n):
    x = jnp.concatenate([jnp.broadcast_to(lo, (n, n)), jnp.broadcast_to(hi, (n, n))], axis=1)
    return pltpu.roll(x, 0, 1, stride=1, stride_axis=0)[:, n:]


def _head_rows(qt, lo):
    row = lax.broadcasted_iota(jnp.int32, qt.shape, 0)
    return jnp.where((row >= lo) & (row < lo + HEAD_DIM), qt, jnp.zeros_like(qt))


def _dot(a, b):
    return jnp.dot(a, b, preferred_element_type=F32)


def _moba_kernel(q_ref, k_ref, v_ref, sg_ref, tv_ref, o_ref, vt_ref, km_ref, bt_ref, sel_ref):
    hp = pl.program_id(0)
    b = pl.program_id(1)
    qi = pl.program_id(2)
    blk = MOBA_BLOCK
    n_blk = k_ref.shape[1] // blk

    def rows_of(i):
        return pl.ds(pl.multiple_of(i * blk, blk), blk)

    @pl.when((b == 0) & (qi == 0))
    def _():
        for s in range(2):
            def body(o, c, s=s):
                bt_ref[s, o] = _toeplitz(tv_ref[2 * hp + s, o], tv_ref[2 * hp + s, o + 1], blk)
                return c
            lax.fori_loop(0, n_blk, body, 0)

    @pl.when(qi == 0)
    def _():
        def body(i, c):
            vt_ref[i] = v_ref[0, rows_of(i), :].T
            km_ref[pl.ds(i, 1), :] = jnp.mean(k_ref[0, rows_of(i), :].astype(F32), axis=0, keepdims=True)
            return c
        lax.fori_loop(0, n_blk, body, 0)

    qt = q_ref[0].T
    qts = [_head_rows(qt, HEAD_DIM * s) for s in range(2)]

    km = km_ref[...]
    km0 = km.astype(ACT)
    r1 = km - km0.astype(F32)
    km1 = r1.astype(ACT)
    km2 = (r1 - km1.astype(F32)).astype(ACT)
    blk_row = lax.broadcasted_iota(jnp.int32, (n_blk, blk), 0)
    for s in range(2):
        g = _dot(km0, qts[s]) + _dot(km1, qts[s]) + _dot(km2, qts[s])
        g = jnp.where(blk_row < qi, g, NEG)
        sel_rows = []
        for jb in range(n_blk):
            gj = g[jb:jb + 1, :]
            beats = (g > gj) | ((g == gj) & (blk_row < jb))
            cnt = jnp.sum(beats.astype(F32), axis=0, keepdims=True)
            sel_rows.append(jnp.where((cnt < MOBA_TOPK) & (jb < qi), 1.0, 0.0))
        sel_ref[s] = jnp.concatenate(sel_rows, axis=0)

    kr = lax.broadcasted_iota(jnp.int32, (blk, blk), 0)
    qc = lax.broadcasted_iota(jnp.int32, (blk, blk), 1)
    causal = qc >= kr

    def tile(s, k_blk, bias, mask, vt, carry):
        m, l, acc = carry
        st = jnp.where(mask, _dot(k_blk, qts[s]) + bias, NEG)
        m_new = jnp.maximum(m, jnp.max(st, axis=0, keepdims=True))
        p = jnp.exp(st - m_new)
        alpha = jnp.exp(m - m_new)
        l = alpha * l + jnp.sum(p, axis=0, keepdims=True)
        acc = alpha * acc + _dot(vt[HEAD_DIM * s:HEAD_DIM * (s + 1), :], p.astype(ACT))
        return m_new, l, acc

    init = (jnp.full((1, blk), NEG, F32), jnp.zeros((1, blk), F32), jnp.zeros((HEAD_DIM, blk), F32))
    k_own = k_ref[0, rows_of(qi), :]
    vt_own = vt_ref[qi]
    carry = tuple(tile(s, k_own, bt_ref[s, 0], causal, vt_own, init) for s in range(2))

    def past(jb, carry):
        k_blk = k_ref[0, rows_of(jb), :]
        vt = vt_ref[jb]
        return tuple(
            tile(s, k_blk, bt_ref[s, qi - jb], sel_ref[s, pl.ds(jb, 1), :] > 0.5, vt, carry[s])
            for s in range(2))

    carry = lax.fori_loop(0, qi, past, carry)
    ot = jnp.concatenate([acc * (1.0 / l) for (_, l, acc) in carry], axis=0)
    o_ref[0] = (ot.T * sg_ref[0].astype(F32)).astype(o_ref.dtype)


def _moba(pa, tv_a):
    b, s, _ = pa.shape
    blk = MOBA_BLOCK
    n_blk = s // blk
    return pl.pallas_call(
        _moba_kernel,
        grid=(N_PAIRS, b, n_blk),
        in_specs=[
            pl.BlockSpec((1, blk, LANE), lambda hp, bi, qi: (bi, qi, hp)),
            pl.BlockSpec((1, s, LANE), lambda hp, bi, qi: (bi, 0, N_PAIRS + hp)),
            pl.BlockSpec((1, s, LANE), lambda hp, bi, qi: (bi, 0, 2 * N_PAIRS + hp)),
            pl.BlockSpec((1, blk, LANE), lambda hp, bi, qi: (bi, qi, 3 * N_PAIRS + hp)),
            pl.BlockSpec(tv_a.shape, lambda hp, bi, qi: (0, 0, 0, 0)),
        ],
        out_specs=pl.BlockSpec((1, blk, LANE), lambda hp, bi, qi: (bi, qi, hp)),
        out_shape=jax.ShapeDtypeStruct((b, s, W_BR), ACT),
        scratch_shapes=[
            pltpu.VMEM((n_blk, LANE, blk), ACT),
            pltpu.VMEM((n_blk, LANE), F32),
            pltpu.VMEM((2, n_blk, blk, blk), F32),
            pltpu.VMEM((2, n_blk, blk), F32),
        ],
        compiler_params=_cparams(3),
        name="moba",
    )(pa, pa, pa, pa, tv_a)


def _band_head(qs, k_own, vt_own, own, prev, sink):
    del k_own, vt_own
    k_o, vt_o, bias_o, mask_o = own
    s_o = jnp.where(mask_o, _dot(k_o, qs) + bias_o, NEG)
    m = jnp.max(s_o, axis=0, keepdims=True)
    if prev is not None:
        k_p, vt_p, bias_p, mask_p = prev
        s_p = jnp.where(mask_p, _dot(k_p, qs) + bias_p, NEG)
        m = jnp.maximum(m, jnp.max(s_p, axis=0, keepdims=True))
    if sink is not None:
        m = jnp.maximum(m, sink)
    p_o = jnp.exp(s_o - m)
    l = jnp.sum(p_o, axis=0, keepdims=True)
    acc = _dot(vt_o, p_o.astype(ACT))
    if prev is not None:
        p_p = jnp.exp(s_p - m)
        l = l + jnp.sum(p_p, axis=0, keepdims=True)
        acc = acc + _dot(vt_p, p_p.astype(ACT))
    if sink is not None:
        l = l + jnp.exp(sink - m)
    return acc * (1.0 / l), m + jnp.log(l)


def _band_masks(prev_includes_window_edge):
    kr = lax.broadcasted_iota(jnp.int32, (BAND, BAND), 0)
    qc = lax.broadcasted_iota(jnp.int32, (BAND, BAND), 1)
    own = qc >= kr
    prev = (qc <= kr) if prev_includes_window_edge else (qc < kr)
    return own, prev


def _build_band_bias(tv_ref, bt_ref, head0):
    for h in range(N_HEADS):
        for which in range(2):
            bt_ref[h, which] = _toeplitz(tv_ref[head0 + h, which], tv_ref[head0 + h, which + 1], BAND)


def _dil_kernel(q_ref, k_ref, v_ref, tv_ref, o_ref, l_ref, bt_ref, *, group):
    b = pl.program_id(0)
    r = pl.program_id(1)
    n_tiles = q_ref.shape[1] // BAND
    has_prev = n_tiles > 1

    @pl.when((b == 0) & (r == 0))
    def _():
        _build_band_bias(tv_ref, bt_ref, group * N_HEADS)

    mask_own, mask_prev = _band_masks(True)

    for hp in range(N_PAIRS):
        lanes = slice(hp * LANE, (hp + 1) * LANE)

        def body(t, c, hp=hp, lanes=lanes):
            rows = pl.ds(pl.multiple_of(t * BAND, BAND), BAND)
            qt = q_ref[0, rows, lanes].T
            k_own = k_ref[0, rows, lanes]
            vt_own = v_ref[0, rows, lanes].T
            if has_prev:
                tp = jnp.maximum(t - 1, 0)
                prow = pl.ds(pl.multiple_of(tp * BAND, BAND), BAND)
                k_prev = k_ref[0, prow, lanes]
                vt_prev = v_ref[0, prow, lanes].T
                pm = mask_prev & (t > 0)
            outs, lses = [], []
            for s in range(2):
                h = 2 * hp + s
                hrows = slice(HEAD_DIM * s, HEAD_DIM * (s + 1))
                prev = (k_prev, vt_prev[hrows, :], bt_ref[h, 1], pm) if has_prev else None
                o, lse = _band_head(_head_rows(qt, HEAD_DIM * s), None, None,
                                    (k_own, vt_own[hrows, :], bt_ref[h, 0], mask_own), prev, None)
                outs.append(o)
                lses.append(jnp.broadcast_to(lse, (HEAD_DIM, BAND)))
            o_ref[0, rows, lanes] = jnp.concatenate(outs, axis=0).T.astype(o_ref.dtype)
            l_ref[0, rows, lanes] = jnp.concatenate(lses, axis=0).T
            return c

        if n_tiles == 1:
            body(0, 0)
        else:
            lax.fori_loop(0, n_tiles, body, 0)


def _dilated(pb, tv_b, group):
    b, s, _ = pb.shape
    _, d = DIL_PAIRS[group]
    n = s // d
    nb = PB_W // W_BR
    pbv = pb.reshape(b, n, d * PB_W)
    spec = lambda col: pl.BlockSpec((1, n, W_BR), lambda bi, r: (bi, 0, r * nb + col + group))
    out_spec = pl.BlockSpec((1, n, W_BR), lambda bi, r: (bi, 0, r))
    o, l = pl.pallas_call(
        functools.partial(_dil_kernel, group=group),
        grid=(b, d),
        in_specs=[spec(0), spec(N_DIL), spec(2 * N_DIL),
                  pl.BlockSpec(tv_b.shape, lambda bi, r: (0, 0, 0, 0))],
        out_specs=[out_spec, out_spec],
        out_shape=[jax.ShapeDtypeStruct((b, n, d * W_BR), ACT),
                   jax.ShapeDtypeStruct((b, n, d * W_BR), F32)],
        scratch_shapes=[pltpu.VMEM((N_HEADS, 2, BAND, BAND), F32)],
        compiler_params=_cparams(2),
        name=f"dilated{group}",
    )(pbv, pbv, pbv, tv_b)
    return o.reshape(b * s, W_BR), l.reshape(b * s, W_BR)


def _swa_kernel(sink_ref, q_ref, sg_ref, kv_ref, tv_ref, o_ref, bt_ref):
    b = pl.program_id(0)
    n_tiles = q_ref.shape[1] // BAND
    kv_w = KV_C * HEAD_DIM

    @pl.when(b == 0)
    def _():
        _build_band_bias(tv_ref, bt_ref, 0)

    mask_own, mask_prev = _band_masks(False)

    for hp in range(N_PAIRS):
        lanes = slice(hp * LANE, (hp + 1) * LANE)
        kv_head = (2 * hp) // (N_HEADS // KV_C)
        kv_rows = slice(HEAD_DIM * kv_head, HEAD_DIM * (kv_head + 1))

        def body(t, c, hp=hp, lanes=lanes, kv_head=kv_head, kv_rows=kv_rows):
            rows = pl.ds(pl.multiple_of(t * BAND, BAND), BAND)
            tp = jnp.maximum(t - 1, 0)
            prow = pl.ds(pl.multiple_of(tp * BAND, BAND), BAND)
            qt = q_ref[0, rows, lanes].T
            k_own = kv_ref[0, rows, 0:kv_w]
            k_prev = kv_ref[0, prow, 0:kv_w]
            vt_own = kv_ref[0, rows, kv_w:2 * kv_w].T[kv_rows, :]
            vt_prev = kv_ref[0, prow, kv_w:2 * kv_w].T[kv_rows, :]
            pm = mask_prev & (t > 0)
            zeros = jnp.zeros((HEAD_DIM, BAND), qt.dtype)
            outs = []
            for s in range(2):
                h = 2 * hp + s
                qh = qt[HEAD_DIM * s:HEAD_DIM * (s + 1), :]
                qs = jnp.concatenate([qh, zeros] if kv_head == 0 else [zeros, qh], axis=0)
                o, _ = _band_head(qs, None, None, (k_own, vt_own, bt_ref[h, 0], mask_own),
                                  (k_prev, vt_prev, bt_ref[h, 1], pm), sink_ref[h])
                outs.append(o)
            ot = jnp.concatenate(outs, axis=0).T
            o_ref[0, rows, lanes] = (ot * sg_ref[0, rows, lanes].astype(F32)).astype(o_ref.dtype)
            return c

        lax.fori_loop(0, n_tiles, body, 0)


def _swa(pc, tv_c, sinks):
    b, s, _ = pc.shape
    grid_spec = pltpu.PrefetchScalarGridSpec(
        num_scalar_prefetch=1,
        grid=(b,),
        in_specs=[
            pl.BlockSpec((1, s, W_BR), lambda bi, sk: (bi, 0, 0)),
            pl.BlockSpec((1, s, W_BR), lambda bi, sk: (bi, 0, 1)),
            pl.BlockSpec((1, s, 2 * KV_C * HEAD_DIM), lambda bi, sk: (bi, 0, 2 * W_BR // (2 * KV_C * HEAD_DIM))),
            pl.BlockSpec(tv_c.shape, lambda bi, sk: (0, 0, 0, 0)),
        ],
        out_specs=pl.BlockSpec((1, s, W_BR), lambda bi, sk: (bi, 0, 0)),
        scratch_shapes=[pltpu.VMEM((N_HEADS, 2, BAND, BAND), F32)],
    )
    return pl.pallas_call(
        _swa_kernel,
        grid_spec=grid_spec,
        out_shape=jax.ShapeDtypeStruct((b, s, W_BR), ACT),
        compiler_params=_cparams(1),
        name="swa",
    )(sinks, pc, pc, pc, tv_c)


def _merge_kernel(bra_ref, o0_ref, o1_ref, o2_ref, l0_ref, l1_ref, l2_ref, sgb_ref, brc_ref, g_ref,
                  wb_ref, wo_ref, x_ref, out_ref):
    l0, l1, l2 = l0_ref[...], l1_ref[...], l2_ref[...]
    m = jnp.maximum(l0, jnp.maximum(l1, l2))
    e0, e1, e2 = jnp.exp(l0 - m), jnp.exp(l1 - m), jnp.exp(l2 - m)
    ob = (e0 * o0_ref[...].astype(F32) + e1 * o1_ref[...].astype(F32) + e2 * o2_ref[...].astype(F32)) / (e0 + e1 + e2)
    brb = (ob * sgb_ref[...].astype(F32)).astype(ACT)
    merged = None
    for i, br in enumerate((bra_ref[...], brb, brc_ref[...])):
        term = g_ref[:, i * D_MODEL:(i + 1) * D_MODEL].astype(F32) * _dot(br, wb_ref[i])
        merged = term if merged is None else merged + term
    out_ref[...] = x_ref[...] + _dot(merged.astype(ACT), wo_ref[...])


def _merge(bra, obs, lbs, pb2, brc, pg2, wb, wo, x2, ts=512):
    n = x2.shape[0]
    row = lambda w: pl.BlockSpec((ts, w), lambda i: (i, 0))
    return pl.pallas_call(
        _merge_kernel,
        grid=(n // ts,),
        in_specs=[row(W_BR)] * 7
        + [pl.BlockSpec((ts, W_BR), lambda i: (i, PB_W // W_BR - 1)),
           row(W_BR), row(PG_W),
           pl.BlockSpec(wb.shape, lambda i: (0, 0, 0)),
           pl.BlockSpec(wo.shape, lambda i: (0, 0)),
           row(D_MODEL)],
        out_specs=row(D_MODEL),
        out_shape=jax.ShapeDtypeStruct(x2.shape, F32),
        compiler_params=_cparams(1),
        name="merge",
    )(bra, *obs, *lbs, pb2, brc, pg2, wb, wo, x2)


def _rel_bucket(dist):
    dist = jnp.maximum(dist, 0)
    max_exact = N_BUCKETS // 2
    log_ratio = jnp.log(jnp.maximum(dist, 1).astype(F32) / max_exact) / math.log(REL_MAX_DIST / max_exact)
    large = jnp.minimum(max_exact + (log_ratio * (N_BUCKETS - max_exact)).astype(jnp.int32), N_BUCKETS - 1)
    return jnp.where(dist < max_exact, dist, large)


def _bias_by_distance(tab, dist, chunk):
    vals = tab.astype(F32)[:, _rel_bucket(dist)]
    vals = jnp.pad(vals, ((0, 0), (chunk, 0)))
    return vals.reshape(tab.shape[0], -1, 1, chunk)


def _gain_row(g):
    scale = HEAD_DIM ** -0.5
    one = jnp.ones((HEAD_DIM,), F32)
    tile = lambda v, n: jnp.tile(v, n)
    parts = [tile(g[0] * scale, N_HEADS), tile(g[1], N_HEADS), tile(one, 2 * N_HEADS),
             tile(g[2] * scale, N_DIL * N_HEADS), tile(g[3], N_DIL * N_HEADS), tile(one, (N_DIL + 1) * N_HEADS),
             tile(g[4] * scale, N_HEADS), tile(g[5], KV_C), tile(one, KV_C + N_HEADS),
             tile(one, 3 * D_MODEL // HEAD_DIM)]
    return jnp.concatenate(parts)[None, :]


def kernel(x, ln_g, w_in, qk_g, sinks, w_branch, w_out, rel_bias):
    b, s, d = x.shape
    depth = ln_g.shape[0]
    assert d == D_MODEL and s % DIL_PAIRS[-1][0] == 0 and w_in.shape[-1] == NJ * TN

    tv_a = _bias_by_distance(rel_bias[:N_HEADS], jnp.arange(s), MOBA_BLOCK)
    tv_b = jnp.concatenate(
        [_bias_by_distance(rel_bias[OFF_B + g * N_HEADS:OFF_B + (g + 1) * N_HEADS], dd * jnp.arange(2 * BAND), BAND)
         for g, (_, dd) in enumerate(DIL_PAIRS)], axis=0)
    tv_c = _bias_by_distance(rel_bias[OFF_C:OFF_C + N_HEADS], jnp.arange(2 * BAND), BAND)
    seg = jnp.arange(TN) // HEAD_DIM
    e_mat = (seg[:, None] == seg[None, :]).astype(ACT)
    src = jnp.asarray(_SRC, jnp.int32)

    w_in_b = w_in.astype(ACT)
    w_br_b = w_branch.astype(ACT)
    w_out_b = w_out.astype(ACT)

    for l in range(depth):
        pa, pb, pc, pg = _project(x, ln_g[l][None, :], w_in_b[l], _gain_row(qk_g[l]), e_mat, src)
        bra = _moba(pa, tv_a)
        dil = [_dilated(pb, tv_b, g) for g in range(N_DIL)]
        brc = _swa(pc, tv_c, sinks[l])
        x2 = _merge(bra.reshape(b * s, W_BR), [o for o, _ in dil], [lse for _, lse in dil],
                    pb.reshape(b * s, PB_W), brc.reshape(b * s, W_BR), pg.reshape(b * s, PG_W),
                    w_br_b[l], w_out_b[l], x.reshape(b * s, d))
        x = x2.reshape(b, s, d)
    return x
```

```python
import math

import jax
import jax.numpy as jnp
from jax import lax
from jax.experimental import pallas as pl
from jax.experimental.pallas import tpu as pltpu

D_MODEL = 1024
HEAD_DIM = 64
W_BR = D_MODEL // 2
N_HEADS = W_BR // HEAD_DIM
N_PAIRS = N_HEADS // 2
MOBA_BLOCK = 256
MOBA_TOPK = 3
DIL_PAIRS = ((128, 1), (512, 4), (2048, 16))
N_DIL = len(DIL_PAIRS)
KV_C = 2
SWA_WINDOW = 128
N_BUCKETS = 32
REL_MAX_DIST = 2048
OFF_B = N_HEADS
OFF_C = N_HEADS + N_DIL * N_HEADS
EPS = 1e-6
NEG = -1e30

LANE = 128
BAND = 128
TN = 256
ROWS = 256
TILE_UNROLL = 4
VMEM_LIMIT = 56 * 1024 * 1024

ACT = jnp.bfloat16
F32 = jnp.float32

PLAIN, NORM, SILU, SIGMOID, KNORM_VPLAIN, NORM_D4, PLAIN_D4, NORM_D16, PLAIN_D16 = range(9)
_B_GROUP = lambda g, norm, plain: [(8 + 2 * g, norm), (9 + 2 * g, norm), (14 + 2 * g, norm), (15 + 2 * g, norm),
                                   (20 + 2 * g, plain), (21 + 2 * g, plain)]
_LAYOUT = ([(j, SIGMOID) for j in range(33, 45)]
           + [(0, NORM), (1, NORM), (2, NORM), (3, NORM), (4, PLAIN), (5, PLAIN), (6, SILU), (7, SILU)]
           + _B_GROUP(0, NORM, PLAIN) + _B_GROUP(1, NORM_D4, PLAIN_D4) + _B_GROUP(2, NORM_D16, PLAIN_D16)
           + [(26, SILU), (27, SILU)]
           + [(28, NORM), (29, NORM), (31, SILU), (32, SILU), (30, KNORM_VPLAIN)])
NJ = len(_LAYOUT)
C_IN = NJ * TN
COL_GATE, COL_A, COL_B, COL_SGB, COL_C = 0, 24, 40, 76, 80
GATE_W = 3 * D_MODEL


def _cparams(n_axes):
    return pltpu.CompilerParams(dimension_semantics=("arbitrary",) * n_axes, vmem_limit_bytes=VMEM_LIMIT)


def _dot(a, b):
    return jnp.dot(a, b, preferred_element_type=F32)


def _sigmoid(a):
    return 0.5 * jnp.tanh(0.5 * a) + 0.5


def _proj_kernel(src_ref, fn_ref, x_ref, lng_ref, w_ref, gain_ref, e_ref, o_ref, xn_ref, acc_ref):
    del src_ref
    j = pl.program_id(1)
    s = x_ref.shape[1]
    n_chunks = s // ROWS

    @pl.when(j == 0)
    def _():
        def body(i, c):
            rows = pl.ds(pl.multiple_of(i * ROWS, ROWS), ROWS)
            xv = x_ref[0, rows, :]
            ms = jnp.mean(xv * xv, axis=-1, keepdims=True)
            xn_ref[rows, :] = (xv * lax.rsqrt(ms + EPS) * lng_ref[...]).astype(ACT)
            return c
        lax.fori_loop(0, n_chunks, body, 0)

    def head_norm(a):
        ss = _dot((a * a).astype(ACT), e_ref[...])
        return a * lax.rsqrt(ss * (1.0 / HEAD_DIM) + EPS) * gain_ref[...]

    def silu(a):
        return a * _sigmoid(a)

    def k_norm_v_plain(a):
        lane = lax.broadcasted_iota(jnp.int32, a.shape, 1)
        return jnp.where(lane < KV_C * HEAD_DIM, head_norm(a), a)

    def chunk(i):
        return _dot(xn_ref[i * ROWS:(i + 1) * ROWS, :], w_ref[...])

    fn = fn_ref[j]

    def direct(code, f):
        @pl.when(fn == code)
        def _():
            for i in range(n_chunks):
                o_ref[0, i * ROWS:(i + 1) * ROWS, :] = f(chunk(i)).astype(o_ref.dtype)

    def phase_major(code, f, d):
        @pl.when(fn == code)
        def _():
            for i in range(n_chunks):
                a = chunk(i)
                for h in range(TN // LANE):
                    acc_ref[h, i * ROWS:(i + 1) * ROWS, :] = a[:, h * LANE:(h + 1) * LANE]
            for c in range(s // BAND):
                r, m0 = divmod(c * BAND, s // d)
                a = jnp.concatenate(
                    [acc_ref.at[h][pl.ds(m0 * d + r, BAND, stride=d), :] for h in range(TN // LANE)], axis=1)
                o_ref[0, c * BAND:(c + 1) * BAND, :] = f(a).astype(o_ref.dtype)

    direct(PLAIN, lambda a: a)
    direct(NORM, head_norm)
    direct(SILU, silu)
    direct(SIGMOID, _sigmoid)
    direct(KNORM_VPLAIN, k_norm_v_plain)
    phase_major(NORM_D4, head_norm, DIL_PAIRS[1][1])
    phase_major(PLAIN_D4, lambda a: a, DIL_PAIRS[1][1])
    phase_major(NORM_D16, head_norm, DIL_PAIRS[2][1])
    phase_major(PLAIN_D16, lambda a: a, DIL_PAIRS[2][1])


def _project(x, ln_g, w_in, gain_row, e_mat, src, fn, layer):
    b, s, d = x.shape
    grid_spec = pltpu.PrefetchScalarGridSpec(
        num_scalar_prefetch=2,
        grid=(b, NJ),
        in_specs=[
            pl.BlockSpec((1, s, d), lambda bi, j, src_ref, fn_ref: (bi, 0, 0)),
            pl.BlockSpec((None, 1, d), lambda bi, j, src_ref, fn_ref: (layer, 0, 0)),
            pl.BlockSpec((None, d, TN), lambda bi, j, src_ref, fn_ref: (layer, 0, src_ref[j])),
            pl.BlockSpec((1, TN), lambda bi, j, src_ref, fn_ref: (0, src_ref[j])),
            pl.BlockSpec((TN, TN), lambda bi, j, src_ref, fn_ref: (0, 0)),
        ],
        out_specs=pl.BlockSpec((1, s, TN), lambda bi, j, src_ref, fn_ref: (bi, 0, j)),
        scratch_shapes=[pltpu.VMEM((s, d), ACT), pltpu.VMEM((TN // LANE, s, LANE), F32)],
    )
    return pl.pallas_call(
        _proj_kernel,
        grid_spec=grid_spec,
        out_shape=jax.ShapeDtypeStruct((b, s, C_IN), ACT),
        compiler_params=_cparams(2),
        name="proj",
    )(src, fn, x, ln_g, w_in, gain_row, e_mat)


def _toeplitz(lo, hi, n):
    x = jnp.concatenate([jnp.broadcast_to(lo, (n, n)), jnp.broadcast_to(hi, (n, n))], axis=1)
    return pltpu.roll(x, 0, 1, stride=1, stride_axis=0)[:, n:]


def _head_rows(qt, lo):
    row = lax.broadcasted_iota(jnp.int32, qt.shape, 0)
    return jnp.where((row >= lo) & (row < lo + HEAD_DIM), qt, jnp.zeros_like(qt))


def _moba_kernel(q_ref, k_ref, v_ref, sg_ref, tv_ref, o_ref, vt_ref, km_ref, bt_ref, sel_ref):
    hp = pl.program_id(0)
    b = pl.program_id(1)
    qi = pl.program_id(2)
    blk = MOBA_BLOCK
    n_blk = k_ref.shape[1] // blk

    def rows_of(i):
        return pl.ds(pl.multiple_of(i * blk, blk), blk)

    @pl.when((b == 0) & (qi == 0))
    def _():
        for s in range(2):
            def body(o, c, s=s):
                bt_ref[s, o] = _toeplitz(tv_ref[2 * hp + s, o], tv_ref[2 * hp + s, o + 1], blk)
                return c
            lax.fori_loop(0, n_blk, body, 0)

    @pl.when(qi == 0)
    def _():
        def body(i, c):
            vt_ref[i] = v_ref[0, rows_of(i), :].T
            km_ref[pl.ds(i, 1), :] = jnp.mean(k_ref[0, rows_of(i), :].astype(F32), axis=0, keepdims=True)
            return c
        lax.fori_loop(0, n_blk, body, 0)

    qt = q_ref[0].T
    qts = [_head_rows(qt, HEAD_DIM * s) for s in range(2)]

    km = km_ref[...]
    km0 = km.astype(ACT)
    r1 = km - km0.astype(F32)
    km1 = r1.astype(ACT)
    km2 = (r1 - km1.astype(F32)).astype(ACT)
    blk_row = lax.broadcasted_iota(jnp.int32, (n_blk, blk), 0)
    for s in range(2):
        g = _dot(km0, qts[s]) + _dot(km1, qts[s]) + _dot(km2, qts[s])
        g = jnp.where(blk_row < qi, g, NEG)
        sel_rows = []
        for jb in range(n_blk):
            gj = g[jb:jb + 1, :]
            beats = (g > gj) | ((g == gj) & (blk_row < jb))
            cnt = jnp.sum(beats.astype(F32), axis=0, keepdims=True)
            sel_rows.append(jnp.where((cnt < MOBA_TOPK) & (jb < qi), 1.0, 0.0))
        sel_ref[s] = jnp.concatenate(sel_rows, axis=0)

    kr = lax.broadcasted_iota(jnp.int32, (blk, blk), 0)
    qc = lax.broadcasted_iota(jnp.int32, (blk, blk), 1)
    causal = qc >= kr

    def tile(s, k_blk, bias, mask, vt, carry):
        m, l, acc = carry
        st = jnp.where(mask, _dot(k_blk, qts[s]) + bias, NEG)
        m_new = jnp.maximum(m, jnp.max(st, axis=0, keepdims=True))
        p = jnp.exp(st - m_new)
        alpha = jnp.exp(m - m_new)
        l = alpha * l + jnp.sum(p, axis=0, keepdims=True)
        acc = alpha * acc + _dot(vt[HEAD_DIM * s:HEAD_DIM * (s + 1), :], p.astype(ACT))
        return m_new, l, acc

    init = (jnp.full((1, blk), NEG, F32), jnp.zeros((1, blk), F32), jnp.zeros((HEAD_DIM, blk), F32))
    k_own = k_ref[0, rows_of(qi), :]
    vt_own = vt_ref[qi]
    carry = tuple(tile(s, k_own, bt_ref[s, 0], causal, vt_own, init) for s in range(2))

    def past(jb, carry):
        k_blk = k_ref[0, rows_of(jb), :]
        vt = vt_ref[jb]
        return tuple(
            tile(s, k_blk, bt_ref[s, qi - jb], sel_ref[s, pl.ds(jb, 1), :] > 0.5, vt, carry[s])
            for s in range(2))

    carry = lax.fori_loop(0, qi, past, carry)
    ot = jnp.concatenate([acc * (1.0 / l) for (_, l, acc) in carry], axis=0)
    o_ref[0] = (ot.T * sg_ref[0].astype(F32)).astype(o_ref.dtype)


def _moba(p, tv_a):
    b, s, _ = p.shape
    blk = MOBA_BLOCK
    n_blk = s // blk
    col = lambda c: (lambda hp, bi, qi: (bi, qi, COL_A + c * N_PAIRS + hp))
    full = lambda c: (lambda hp, bi, qi: (bi, 0, COL_A + c * N_PAIRS + hp))
    return pl.pallas_call(
        _moba_kernel,
        grid=(N_PAIRS, b, n_blk),
        in_specs=[
            pl.BlockSpec((1, blk, LANE), col(0)),
            pl.BlockSpec((1, s, LANE), full(1)),
            pl.BlockSpec((1, s, LANE), full(2)),
            pl.BlockSpec((1, blk, LANE), col(3)),
            pl.BlockSpec(tv_a.shape, lambda hp, bi, qi: (0, 0, 0, 0)),
        ],
        out_specs=pl.BlockSpec((1, blk, LANE), lambda hp, bi, qi: (bi, qi, hp)),
        out_shape=jax.ShapeDtypeStruct((b, s, W_BR), ACT),
        scratch_shapes=[
            pltpu.VMEM((n_blk, LANE, blk), ACT),
            pltpu.VMEM((n_blk, LANE), F32),
            pltpu.VMEM((2, n_blk, blk, blk), F32),
            pltpu.VMEM((2, n_blk, blk), F32),
        ],
        compiler_params=_cparams(3),
        name="moba",
    )(p, p, p, p, tv_a)


def _band_head(qs, own, prev, sink):
    k_o, vt_o, bias_o, mask_o = own
    s_o = jnp.where(mask_o, _dot(k_o, qs) + bias_o, NEG)
    m = jnp.max(s_o, axis=0, keepdims=True)
    if prev is not None:
        k_p, vt_p, bias_p, mask_p = prev
        s_p = jnp.where(mask_p, _dot(k_p, qs) + bias_p, NEG)
        m = jnp.maximum(m, jnp.max(s_p, axis=0, keepdims=True))
    if sink is not None:
        m = jnp.maximum(m, sink)
    p_o = jnp.exp(s_o - m)
    l = jnp.sum(p_o, axis=0, keepdims=True)
    acc = _dot(vt_o, p_o.astype(ACT))
    if prev is not None:
        p_p = jnp.exp(s_p - m)
        l = l + jnp.sum(p_p, axis=0, keepdims=True)
        acc = acc + _dot(vt_p, p_p.astype(ACT))
    if sink is not None:
        l = l + jnp.exp(sink - m)
    return acc * (1.0 / l), m + jnp.log(l)


def _band_masks(prev_includes_window_edge):
    kr = lax.broadcasted_iota(jnp.int32, (BAND, BAND), 0)
    qc = lax.broadcasted_iota(jnp.int32, (BAND, BAND), 1)
    own = qc >= kr
    prev = (qc <= kr) if prev_includes_window_edge else (qc < kr)
    return own, prev


def _band_bias(tv_ref, head):
    return [_toeplitz(tv_ref[head, which], tv_ref[head, which + 1], BAND) for which in range(2)]


def _dil_kernel(q0_ref, k0_ref, v0_ref, q1_ref, k1_ref, v1_ref, q2_ref, k2_ref, v2_ref, sg_ref, tv_ref,
                o_ref, og_ref, lg_ref, bt_ref):
    hp = pl.program_id(0)
    b = pl.program_id(1)
    s = q0_ref.shape[1]
    n_tiles = s // BAND

    @pl.when(b == 0)
    def _():
        for g in range(N_DIL):
            for sh in range(2):
                own, prev = _band_bias(tv_ref, g * N_HEADS + 2 * hp + sh)
                bt_ref[g, sh, 0] = own
                bt_ref[g, sh, 1] = prev

    mask_own, mask_prev = _band_masks(True)
    groups = ((q0_ref, k0_ref, v0_ref), (q1_ref, k1_ref, v1_ref), (q2_ref, k2_ref, v2_ref))

    for g, (q_ref, k_ref, v_ref) in enumerate(groups):
        d = DIL_PAIRS[g][1]
        per_phase = n_tiles // d
        has_prev = per_phase > 1

        def body(u, c, g=g, d=d, per_phase=per_phase, has_prev=has_prev, q_ref=q_ref, k_ref=k_ref, v_ref=v_ref):
            rows = pl.ds(pl.multiple_of(u * BAND, BAND), BAND)
            t = lax.rem(u, per_phase)
            r = lax.div(u, per_phase)
            qt = q_ref[0, rows, :].T
            k_own = k_ref[0, rows, :]
            vt_own = v_ref[0, rows, :].T
            if has_prev:
                prow = pl.ds(pl.multiple_of(jnp.maximum(u - 1, 0) * BAND, BAND), BAND)
                k_prev = k_ref[0, prow, :]
                vt_prev = v_ref[0, prow, :].T
                pm = mask_prev & (t > 0)
            outs, lses = [], []
            for sh in range(2):
                hrows = slice(HEAD_DIM * sh, HEAD_DIM * (sh + 1))
                prev = (k_prev, vt_prev[hrows, :], bt_ref[g, sh, 1], pm) if has_prev else None
                o, lse = _band_head(_head_rows(qt, HEAD_DIM * sh),
                                    (k_own, vt_own[hrows, :], bt_ref[g, sh, 0], mask_own), prev, None)
                outs.append(o)
                lses.append(jnp.broadcast_to(lse, (HEAD_DIM, BAND)))
            dst = rows if d == 1 else pl.ds(t * (BAND * d) + r, BAND, stride=d)
            og_ref.at[g][dst, :] = jnp.concatenate(outs, axis=0).T
            lg_ref.at[g][dst, :] = jnp.concatenate(lses, axis=0).T
            return c

        lax.fori_loop(0, n_tiles, body, 0, unroll=TILE_UNROLL)

    def merge(i, c):
        rows = pl.ds(pl.multiple_of(i * ROWS, ROWS), ROWS)
        l0, l1, l2 = lg_ref[0, rows, :], lg_ref[1, rows, :], lg_ref[2, rows, :]
        m = jnp.maximum(l0, jnp.maximum(l1, l2))
        e0, e1, e2 = jnp.exp(l0 - m), jnp.exp(l1 - m), jnp.exp(l2 - m)
        ob = (e0 * og_ref[0, rows, :] + e1 * og_ref[1, rows, :] + e2 * og_ref[2, rows, :]) / (e0 + e1 + e2)
        o_ref[0, rows, :] = (ob * sg_ref[0, rows, :].astype(F32)).astype(o_ref.dtype)
        return c

    lax.fori_loop(0, s // ROWS, merge, 0)


def _dilated(p, tv_b):
    b, s, _ = p.shape
    col = lambda c: pl.BlockSpec((1, s, LANE), lambda hp, bi: (bi, 0, c + hp))
    qkv = [col(COL_B + 3 * N_PAIRS * g + N_PAIRS * i) for g in range(N_DIL) for i in range(3)]
    return pl.pallas_call(
        _dil_kernel,
        grid=(N_PAIRS, b),
        in_specs=qkv + [col(COL_SGB), pl.BlockSpec(tv_b.shape, lambda hp, bi: (0, 0, 0, 0))],
        out_specs=pl.BlockSpec((1, s, LANE), lambda hp, bi: (bi, 0, hp)),
        out_shape=jax.ShapeDtypeStruct((b, s, W_BR), ACT),
        scratch_shapes=[
            pltpu.VMEM((N_DIL, s, LANE), F32),
            pltpu.VMEM((N_DIL, s, LANE), F32),
            pltpu.VMEM((N_DIL, 2, 2, BAND, BAND), F32),
        ],
        compiler_params=_cparams(2),
        name="dilated",
    )(*([p] * 10), tv_b)


def _swa_kernel(sink_ref, q_ref, sg_ref, k_ref, v_ref, tv_ref, o_ref, bt_ref):
    hp = pl.program_id(0)
    b = pl.program_id(1)
    n_tiles = q_ref.shape[1] // BAND
    pairs_per_kv = N_PAIRS // KV_C

    @pl.when(b == 0)
    def _():
        for sh in range(2):
            own, prev = _band_bias(tv_ref, 2 * hp + sh)
            bt_ref[sh, 0] = own
            bt_ref[sh, 1] = prev

    mask_own, mask_prev = _band_masks(False)
    kv_lo = HEAD_DIM * (hp // pairs_per_kv)
    row = lax.broadcasted_iota(jnp.int32, (LANE, BAND), 0)
    kv_rows = (row >= kv_lo) & (row < kv_lo + HEAD_DIM)

    def body(t, c):
        rows = pl.ds(pl.multiple_of(t * BAND, BAND), BAND)
        prow = pl.ds(pl.multiple_of(jnp.maximum(t - 1, 0) * BAND, BAND), BAND)
        qt = q_ref[0, rows, :].T
        k_own = k_ref[0, rows, :]
        k_prev = k_ref[0, prow, :]
        zero = jnp.zeros((LANE, BAND), ACT)
        vt_own = jnp.where(kv_rows, v_ref[0, rows, :].T, zero)
        vt_prev = jnp.where(kv_rows, v_ref[0, prow, :].T, zero)
        pm = mask_prev & (t > 0)
        outs = []
        for sh in range(2):
            qh = qt[HEAD_DIM * sh:HEAD_DIM * (sh + 1), :]
            qs = jnp.where(kv_rows, jnp.concatenate([qh, qh], axis=0), zero)
            o, _ = _band_head(qs, (k_own, vt_own, bt_ref[sh, 0], mask_own),
                              (k_prev, vt_prev, bt_ref[sh, 1], pm), sink_ref[2 * hp + sh])
            outs.append(o[:HEAD_DIM, :] + o[HEAD_DIM:, :])
        ot = jnp.concatenate(outs, axis=0).T
        o_ref[0, rows, :] = (ot * sg_ref[0, rows, :].astype(F32)).astype(o_ref.dtype)
        return c

    lax.fori_loop(0, n_tiles, body, 0, unroll=TILE_UNROLL)


def _swa(p, tv_c, sinks):
    b, s, _ = p.shape
    grid_spec = pltpu.PrefetchScalarGridSpec(
        num_scalar_prefetch=1,
        grid=(N_PAIRS, b),
        in_specs=[
            pl.BlockSpec((1, s, LANE), lambda hp, bi, sk: (bi, 0, COL_C + hp)),
            pl.BlockSpec((1, s, LANE), lambda hp, bi, sk: (bi, 0, COL_C + N_PAIRS + hp)),
            pl.BlockSpec((1, s, LANE), lambda hp, bi, sk: (bi, 0, COL_C + 2 * N_PAIRS)),
            pl.BlockSpec((1, s, LANE), lambda hp, bi, sk: (bi, 0, COL_C + 2 * N_PAIRS + 1)),
            pl.BlockSpec(tv_c.shape, lambda hp, bi, sk: (0, 0, 0, 0)),
        ],
        out_specs=pl.BlockSpec((1, s, LANE), lambda hp, bi, sk: (bi, 0, hp)),
        scratch_shapes=[pltpu.VMEM((2, 2, BAND, BAND), F32)],
    )
    return pl.pallas_call(
        _swa_kernel,
        grid_spec=grid_spec,
        out_shape=jax.ShapeDtypeStruct((b, s, W_BR), ACT),
        compiler_params=_cparams(2),
        name="swa",
    )(sinks, p, p, p, p, tv_c)


def _merge_kernel(bra_ref, brb_ref, brc_ref, g_ref, wb_ref, wo_ref, x_ref, out_ref):
    merged = None
    for i, br_ref in enumerate((bra_ref, brb_ref, brc_ref)):
        term = g_ref[:, i * D_MODEL:(i + 1) * D_MODEL].astype(F32) * _dot(br_ref[...], wb_ref[i])
        merged = term if merged is None else merged + term
    out_ref[...] = x_ref[...] + _dot(merged.astype(ACT), wo_ref[...])


def _merge(bra, brb, brc, p2, w_br, w_out, x2, layer, ts=512):
    n = x2.shape[0]
    row = lambda w: pl.BlockSpec((ts, w), lambda i: (i, 0))
    return pl.pallas_call(
        _merge_kernel,
        grid=(n // ts,),
        in_specs=[row(W_BR), row(W_BR), row(W_BR), row(GATE_W),
                  pl.BlockSpec((None,) + w_br.shape[1:], lambda i: (layer, 0, 0, 0)),
                  pl.BlockSpec((None,) + w_out.shape[1:], lambda i: (layer, 0, 0)),
                  row(D_MODEL)],
        out_specs=row(D_MODEL),
        out_shape=jax.ShapeDtypeStruct(x2.shape, F32),
        compiler_params=_cparams(1),
        name="merge",
    )(bra, brb, brc, p2, w_br, w_out, x2)


def _rel_bucket(dist):
    dist = jnp.maximum(dist, 0)
    max_exact = N_BUCKETS // 2
    log_ratio = jnp.log(jnp.maximum(dist, 1).astype(F32) / max_exact) / math.log(REL_MAX_DIST / max_exact)
    large = jnp.minimum(max_exact + (log_ratio * (N_BUCKETS - max_exact)).astype(jnp.int32), N_BUCKETS - 1)
    return jnp.where(dist < max_exact, dist, large)


def _bias_by_distance(tab, dist, chunk):
    vals = tab.astype(F32)[:, _rel_bucket(dist)]
    vals = jnp.pad(vals, ((0, 0), (chunk, 0)))
    return vals.reshape(tab.shape[0], -1, 1, chunk)


def _gain_rows(qk_g):
    depth = qk_g.shape[0]
    scale = HEAD_DIM ** -0.5
    one = jnp.ones((depth, HEAD_DIM), F32)
    tile = lambda v, n: jnp.tile(v, (1, n))
    parts = [tile(qk_g[:, 0] * scale, N_HEADS), tile(qk_g[:, 1], N_HEADS), tile(one, 2 * N_HEADS),
             tile(qk_g[:, 2] * scale, N_DIL * N_HEADS), tile(qk_g[:, 3], N_DIL * N_HEADS),
             tile(one, (N_DIL + 1) * N_HEADS),
             tile(qk_g[:, 4] * scale, N_HEADS), tile(qk_g[:, 5], KV_C), tile(one, KV_C + N_HEADS),
             tile(one, GATE_W // HEAD_DIM)]
    return jnp.concatenate(parts, axis=1)


def kernel(x, ln_g, w_in, qk_g, sinks, w_branch, w_out, rel_bias):
    b, s, d = x.shape
    depth = ln_g.shape[0]
    assert d == D_MODEL and s % DIL_PAIRS[-1][0] == 0 and w_in.shape[-1] == C_IN

    tv_a = _bias_by_distance(rel_bias[:N_HEADS], jnp.arange(s), MOBA_BLOCK)
    tv_b = jnp.concatenate(
        [_bias_by_distance(rel_bias[OFF_B + g * N_HEADS:OFF_B + (g + 1) * N_HEADS], dd * jnp.arange(2 * BAND), BAND)
         for g, (_, dd) in enumerate(DIL_PAIRS)], axis=0)
    tv_c = _bias_by_distance(rel_bias[OFF_C:OFF_C + N_HEADS], jnp.arange(2 * BAND), BAND)
    seg = jnp.arange(TN) // HEAD_DIM
    e_mat = (seg[:, None] == seg[None, :]).astype(ACT)
    src = jnp.asarray([c for c, _ in _LAYOUT], jnp.int32)
    fn = jnp.asarray([f for _, f in _LAYOUT], jnp.int32)
    gains = _gain_rows(qk_g)

    w_in_b = w_in.astype(ACT)
    w_br_b = w_branch.astype(ACT)
    w_out_b = w_out.astype(ACT)

    for l in range(depth):
        p = _project(x, ln_g[:, None, :], w_in_b, gains[l][None, :], e_mat, src, fn, l)
        bra = _moba(p, tv_a)
        brb = _dilated(p, tv_b)
        brc = _swa(p, tv_c, sinks[l])
        x2 = _merge(bra.reshape(b * s, W_BR), brb.reshape(b * s, W_BR), brc.reshape(b * s, W_BR),
                    p.reshape(b * s, C_IN), w_br_b, w_out_b, x.reshape(b * s, d), l)
        x = x2.reshape(b, s, d)
    return x
```

```python
import math

import jax
import jax.numpy as jnp
from jax import lax
from jax.experimental import pallas as pl
from jax.experimental.pallas import tpu as pltpu

D_MODEL = 1024
HEAD_DIM = 64
W_BR = D_MODEL // 2
N_HEADS = W_BR // HEAD_DIM
N_PAIRS = N_HEADS // 2
MOBA_BLOCK = 256
MOBA_TOPK = 3
DIL_PAIRS = ((128, 1), (512, 4), (2048, 16))
N_DIL = len(DIL_PAIRS)
KV_C = 2
SWA_WINDOW = 128
N_BUCKETS = 32
REL_MAX_DIST = 2048
OFF_B = N_HEADS
OFF_C = N_HEADS + N_DIL * N_HEADS
EPS = 1e-6
NEG = -1e30

LANE = 128
BAND = 128
TN = 256
ROWS = 256
TILE_UNROLL = 4
VMEM_LIMIT = 56 * 1024 * 1024

ACT = jnp.bfloat16
F32 = jnp.float32

PLAIN, NORM, SILU, SIGMOID, KNORM_VPLAIN, NORM_D4, PLAIN_D4, NORM_D16, PLAIN_D16 = range(9)
_B_GROUP = lambda g, norm, plain: [(8 + 2 * g, norm), (9 + 2 * g, norm), (14 + 2 * g, norm), (15 + 2 * g, norm),
                                   (20 + 2 * g, plain), (21 + 2 * g, plain)]
_LAYOUT = ([(j, SIGMOID) for j in range(33, 45)]
           + [(0, NORM), (1, NORM), (2, NORM), (3, NORM), (4, PLAIN), (5, PLAIN), (6, SILU), (7, SILU)]
           + _B_GROUP(0, NORM, PLAIN) + _B_GROUP(1, NORM_D4, PLAIN_D4) + _B_GROUP(2, NORM_D16, PLAIN_D16)
           + [(26, SILU), (27, SILU)]
           + [(28, NORM), (29, NORM), (31, SILU), (32, SILU), (30, KNORM_VPLAIN)])
NJ = len(_LAYOUT)
C_IN = NJ * TN
COL_GATE, COL_A, COL_B, COL_SGB, COL_C = 0, 24, 40, 76, 80
GATE_W = 3 * D_MODEL


def _cparams(n_axes):
    return pltpu.CompilerParams(dimension_semantics=("arbitrary",) * n_axes, vmem_limit_bytes=VMEM_LIMIT)


def _dot(a, b):
    return jnp.dot(a, b, preferred_element_type=F32)


def _sigmoid(a):
    return 0.5 * jnp.tanh(0.5 * a) + 0.5


def _proj_kernel(src_ref, fn_ref, x_ref, lng_ref, w_ref, gain_ref, e_ref, o_ref, xn_ref, acc_ref):
    del src_ref
    j = pl.program_id(1)
    s = x_ref.shape[1]
    n_chunks = s // ROWS

    @pl.when(j == 0)
    def _():
        def body(i, c):
            rows = pl.ds(pl.multiple_of(i * ROWS, ROWS), ROWS)
            xv = x_ref[0, rows, :]
            ms = jnp.mean(xv * xv, axis=-1, keepdims=True)
            xn_ref[rows, :] = (xv * lax.rsqrt(ms + EPS) * lng_ref[...]).astype(ACT)
            return c
        lax.fori_loop(0, n_chunks, body, 0)

    def head_norm(a):
        ss = _dot((a * a).astype(ACT), e_ref[...])
        return a * lax.rsqrt(ss * (1.0 / HEAD_DIM) + EPS) * gain_ref[...]

    def silu(a):
        return a * _sigmoid(a)

    def k_norm_v_plain(a):
        lane = lax.broadcasted_iota(jnp.int32, a.shape, 1)
        return jnp.where(lane < KV_C * HEAD_DIM, head_norm(a), a)

    def chunk(i):
        return _dot(xn_ref[i * ROWS:(i + 1) * ROWS, :], w_ref[...])

    fn = fn_ref[j]

    def direct(code, f):
        @pl.when(fn == code)
        def _():
            for i in range(n_chunks):
                a = chunk(i)
                for h in range(ROWS // BAND):
                    lo = i * ROWS + h * BAND
                    o_ref[0, lo:lo + BAND, :] = f(a[h * BAND:(h + 1) * BAND, :]).astype(o_ref.dtype)

    def phase_major(code, f, d):
        @pl.when(fn == code)
        def _():
            for i in range(n_chunks):
                a = chunk(i)
                for h in range(TN // LANE):
                    acc_ref[h, i * ROWS:(i + 1) * ROWS, :] = a[:, h * LANE:(h + 1) * LANE]
            for c in range(s // BAND):
                r, m0 = divmod(c * BAND, s // d)
                a = jnp.concatenate(
                    [acc_ref.at[h][pl.ds(m0 * d + r, BAND, stride=d), :] for h in range(TN // LANE)], axis=1)
                o_ref[0, c * BAND:(c + 1) * BAND, :] = f(a).astype(o_ref.dtype)

    direct(PLAIN, lambda a: a)
    direct(NORM, head_norm)
    direct(SILU, silu)
    direct(SIGMOID, _sigmoid)
    direct(KNORM_VPLAIN, k_norm_v_plain)
    phase_major(NORM_D4, head_norm, DIL_PAIRS[1][1])
    phase_major(PLAIN_D4, lambda a: a, DIL_PAIRS[1][1])
    phase_major(NORM_D16, head_norm, DIL_PAIRS[2][1])
    phase_major(PLAIN_D16, lambda a: a, DIL_PAIRS[2][1])


def _project(x, ln_g, w_in, gain_row, e_mat, src, fn, layer):
    b, s, d = x.shape
    grid_spec = pltpu.PrefetchScalarGridSpec(
        num_scalar_prefetch=2,
        grid=(b, NJ),
        in_specs=[
            pl.BlockSpec((1, s, d), lambda bi, j, src_ref, fn_ref: (bi, 0, 0)),
            pl.BlockSpec((None, 1, d), lambda bi, j, src_ref, fn_ref: (layer, 0, 0)),
            pl.BlockSpec((None, d, TN), lambda bi, j, src_ref, fn_ref: (layer, 0, src_ref[j])),
            pl.BlockSpec((1, TN), lambda bi, j, src_ref, fn_ref: (0, src_ref[j])),
            pl.BlockSpec((TN, TN), lambda bi, j, src_ref, fn_ref: (0, 0)),
        ],
        out_specs=pl.BlockSpec((1, s, TN), lambda bi, j, src_ref, fn_ref: (bi, 0, j)),
        scratch_shapes=[pltpu.VMEM((s, d), ACT), pltpu.VMEM((TN // LANE, s, LANE), F32)],
    )
    return pl.pallas_call(
        _proj_kernel,
        grid_spec=grid_spec,
        out_shape=jax.ShapeDtypeStruct((b, s, C_IN), ACT),
        compiler_params=_cparams(2),
        name="proj",
    )(src, fn, x, ln_g, w_in, gain_row, e_mat)


def _toeplitz(lo, hi, n):
    x = jnp.concatenate([jnp.broadcast_to(lo, (n, n)), jnp.broadcast_to(hi, (n, n))], axis=1)
    return pltpu.roll(x, 0, 1, stride=1, stride_axis=0)[:, n:]


def _head_rows(qt, lo):
    row = lax.broadcasted_iota(jnp.int32, qt.shape, 0)
    return jnp.where((row >= lo) & (row < lo + HEAD_DIM), qt, jnp.zeros_like(qt))


def _moba_kernel(q_ref, k_ref, v_ref, sg_ref, tv_ref, o_ref,
                 bt_ref, qts_ref, vt_ref, km_ref, sel_ref, m_ref, l_ref, acc_ref):
    hp = pl.program_id(0)
    b = pl.program_id(1)
    blk = MOBA_BLOCK
    n_blk = k_ref.shape[1] // blk

    def rows_of(i):
        return pl.ds(pl.multiple_of(i * blk, blk), blk)

    @pl.when(b == 0)
    def _():
        for sh in range(2):
            def body(o, c, sh=sh):
                bt_ref[sh, o] = _toeplitz(tv_ref[2 * hp + sh, o], tv_ref[2 * hp + sh, o + 1], blk)
                return c
            lax.fori_loop(0, n_blk, body, 0)

    def prep(i, c):
        vt_ref[i] = v_ref[0, rows_of(i), :].T
        km_ref[pl.ds(i, 1), :] = jnp.mean(k_ref[0, rows_of(i), :].astype(F32), axis=0, keepdims=True)
        qt = q_ref[0, rows_of(i), :].T
        for sh in range(2):
            qts_ref[i, sh] = _head_rows(qt, HEAD_DIM * sh)
        return c
    lax.fori_loop(0, n_blk, prep, 0, unroll=2)

    km = km_ref[...]
    km0 = km.astype(ACT)
    r1 = km - km0.astype(F32)
    km1 = r1.astype(ACT)
    km2 = (r1 - km1.astype(F32)).astype(ACT)
    blk_row = lax.broadcasted_iota(jnp.int32, (n_blk, blk), 0)

    def select(qi, c):
        for sh in range(2):
            qs = qts_ref[qi, sh]
            g = _dot(km0, qs) + _dot(km1, qs) + _dot(km2, qs)
            g = jnp.where(blk_row < qi, g, NEG)
            sel_rows = []
            for jb in range(n_blk):
                gj = g[jb:jb + 1, :]
                beats = (g > gj) | ((g == gj) & (blk_row < jb))
                cnt = jnp.sum(beats.astype(F32), axis=0, keepdims=True)
                sel_rows.append(jnp.where((cnt < MOBA_TOPK) & (jb < qi), 1.0, 0.0))
            sel_ref[qi, sh] = jnp.concatenate(sel_rows, axis=0)
        return c
    lax.fori_loop(0, n_blk, select, 0, unroll=2)

    kr = lax.broadcasted_iota(jnp.int32, (blk, blk), 0)
    qc = lax.broadcasted_iota(jnp.int32, (blk, blk), 1)
    causal = qc >= kr

    def tile(qs, k_blk, bias, mask, vt_rows, m, l, acc):
        st = jnp.where(mask, _dot(k_blk, qs) + bias, NEG)
        m_new = jnp.maximum(m, jnp.max(st, axis=0, keepdims=True))
        p = jnp.exp(st - m_new)
        alpha = jnp.exp(m - m_new)
        return (m_new, alpha * l + jnp.sum(p, axis=0, keepdims=True),
                alpha * acc + _dot(vt_rows, p.astype(ACT)))

    def own(qi, c):
        k_blk = k_ref[0, rows_of(qi), :]
        vt = vt_ref[qi]
        for sh in range(2):
            m, l, acc = tile(qts_ref[qi, sh], k_blk, bt_ref[sh, 0], causal,
                             vt[HEAD_DIM * sh:HEAD_DIM * (sh + 1), :],
                             jnp.full((1, blk), NEG, F32), jnp.zeros((1, blk), F32), jnp.zeros((HEAD_DIM, blk), F32))
            m_ref[qi, sh], l_ref[qi, sh], acc_ref[qi, sh] = m, l, acc
        return c
    lax.fori_loop(0, n_blk, own, 0, unroll=2)

    for off in range(1, n_blk):
        def past(qi, c, off=off):
            jb = qi - off
            k_blk = k_ref[0, rows_of(jb), :]
            vt = vt_ref[jb]
            for sh in range(2):
                m, l, acc = tile(qts_ref[qi, sh], k_blk, bt_ref[sh, off], sel_ref[qi, sh, pl.ds(jb, 1), :] > 0.5,
                                 vt[HEAD_DIM * sh:HEAD_DIM * (sh + 1), :],
                                 m_ref[qi, sh], l_ref[qi, sh], acc_ref[qi, sh])
                m_ref[qi, sh], l_ref[qi, sh], acc_ref[qi, sh] = m, l, acc
            return c
        lax.fori_loop(off, n_blk, past, 0, unroll=2)

    def finish(qi, c):
        ot = jnp.concatenate([acc_ref[qi, sh] * (1.0 / l_ref[qi, sh]) for sh in range(2)], axis=0)
        o_ref[0, rows_of(qi), :] = (ot.T * sg_ref[0, rows_of(qi), :].astype(F32)).astype(o_ref.dtype)
        return c
    lax.fori_loop(0, n_blk, finish, 0, unroll=2)


def _moba(p, tv_a):
    b, s, _ = p.shape
    blk = MOBA_BLOCK
    n_blk = s // blk
    col = lambda c: pl.BlockSpec((1, s, LANE), lambda hp, bi: (bi, 0, COL_A + c * N_PAIRS + hp))
    return pl.pallas_call(
        _moba_kernel,
        grid=(N_PAIRS, b),
        in_specs=[col(0), col(1), col(2), col(3), pl.BlockSpec(tv_a.shape, lambda hp, bi: (0, 0, 0, 0))],
        out_specs=pl.BlockSpec((1, s, LANE), lambda hp, bi: (bi, 0, hp)),
        out_shape=jax.ShapeDtypeStruct((b, s, W_BR), ACT),
        scratch_shapes=[
            pltpu.VMEM((2, n_blk, blk, blk), F32),
            pltpu.VMEM((n_blk, 2, LANE, blk), ACT),
            pltpu.VMEM((n_blk, LANE, blk), ACT),
            pltpu.VMEM((n_blk, LANE), F32),
            pltpu.VMEM((n_blk, 2, n_blk, blk), F32),
            pltpu.VMEM((n_blk, 2, 1, blk), F32),
            pltpu.VMEM((n_blk, 2, 1, blk), F32),
            pltpu.VMEM((n_blk, 2, HEAD_DIM, blk), F32),
        ],
        compiler_params=_cparams(2),
        name="moba",
    )(p, p, p, p, tv_a)


def _band_head(qs, own, prev, sink):
    k_o, vt_o, bias_o, mask_o = own
    s_o = jnp.where(mask_o, _dot(k_o, qs) + bias_o, NEG)
    m = jnp.max(s_o, axis=0, keepdims=True)
    if prev is not None:
        k_p, vt_p, bias_p, mask_p = prev
        s_p = jnp.where(mask_p, _dot(k_p, qs) + bias_p, NEG)
        m = jnp.maximum(m, jnp.max(s_p, axis=0, keepdims=True))
    if sink is not None:
        m = jnp.maximum(m, sink)
    p_o = jnp.exp(s_o - m)
    l = jnp.sum(p_o, axis=0, keepdims=True)
    acc = _dot(vt_o, p_o.astype(ACT))
    if prev is not None:
        p_p = jnp.exp(s_p - m)
        l = l + jnp.sum(p_p, axis=0, keepdims=True)
        acc = acc + _dot(vt_p, p_p.astype(ACT))
    if sink is not None:
        l = l + jnp.exp(sink - m)
    return acc * (1.0 / l), m + jnp.log(l)


def _band_masks(prev_includes_window_edge):
    kr = lax.broadcasted_iota(jnp.int32, (BAND, BAND), 0)
    qc = lax.broadcasted_iota(jnp.int32, (BAND, BAND), 1)
    own = qc >= kr
    prev = (qc <= kr) if prev_includes_window_edge else (qc < kr)
    return own, prev


def _band_bias(tv_ref, head):
    return [_toeplitz(tv_ref[head, which], tv_ref[head, which + 1], BAND) for which in range(2)]


def _dil_kernel(q0_ref, k0_ref, v0_ref, q1_ref, k1_ref, v1_ref, q2_ref, k2_ref, v2_ref, sg_ref, tv_ref,
                o_ref, og_ref, lg_ref, bt_ref):
    hp = pl.program_id(0)
    b = pl.program_id(1)
    s = q0_ref.shape[1]
    n_tiles = s // BAND

    @pl.when(b == 0)
    def _():
        for g in range(N_DIL):
            for sh in range(2):
                own, prev = _band_bias(tv_ref, g * N_HEADS + 2 * hp + sh)
                bt_ref[g, sh, 0] = own
                bt_ref[g, sh, 1] = prev

    mask_own, mask_prev = _band_masks(True)
    groups = ((q0_ref, k0_ref, v0_ref), (q1_ref, k1_ref, v1_ref), (q2_ref, k2_ref, v2_ref))

    for g, (q_ref, k_ref, v_ref) in enumerate(groups):
        d = DIL_PAIRS[g][1]
        per_phase = n_tiles // d
        has_prev = per_phase > 1

        def body(u, c, g=g, d=d, per_phase=per_phase, has_prev=has_prev, q_ref=q_ref, k_ref=k_ref, v_ref=v_ref):
            rows = pl.ds(pl.multiple_of(u * BAND, BAND), BAND)
            t = lax.rem(u, per_phase)
            r = lax.div(u, per_phase)
            qt = q_ref[0, rows, :].T
            k_own = k_ref[0, rows, :]
            vt_own = v_ref[0, rows, :].T
            if has_prev:
                prow = pl.ds(pl.multiple_of(jnp.maximum(u - 1, 0) * BAND, BAND), BAND)
                k_prev = k_ref[0, prow, :]
                vt_prev = v_ref[0, prow, :].T
                pm = mask_prev & (t > 0)
            outs, lses = [], []
            for sh in range(2):
                hrows = slice(HEAD_DIM * sh, HEAD_DIM * (sh + 1))
                prev = (k_prev, vt_prev[hrows, :], bt_ref[g, sh, 1], pm) if has_prev else None
                o, lse = _band_head(_head_rows(qt, HEAD_DIM * sh),
                                    (k_own, vt_own[hrows, :], bt_ref[g, sh, 0], mask_own), prev, None)
                outs.append(o)
                lses.append(jnp.broadcast_to(lse, (HEAD_DIM, BAND)))
            dst = rows if d == 1 else pl.ds(t * (BAND * d) + r, BAND, stride=d)
            og_ref.at[g][dst, :] = jnp.concatenate(outs, axis=0).T
            lg_ref.at[g][dst, :] = jnp.concatenate(lses, axis=0).T
            return c

        lax.fori_loop(0, n_tiles, body, 0, unroll=TILE_UNROLL)

    def merge(i, c):
        rows = pl.ds(pl.multiple_of(i * ROWS, ROWS), ROWS)
        l0, l1, l2 = lg_ref[0, rows, :], lg_ref[1, rows, :], lg_ref[2, rows, :]
        m = jnp.maximum(l0, jnp.maximum(l1, l2))
        e0, e1, e2 = jnp.exp(l0 - m), jnp.exp(l1 - m), jnp.exp(l2 - m)
        ob = (e0 * og_ref[0, rows, :] + e1 * og_ref[1, rows, :] + e2 * og_ref[2, rows, :]) / (e0 + e1 + e2)
        o_ref[0, rows, :] = (ob * sg_ref[0, rows, :].astype(F32)).astype(o_ref.dtype)
        return c

    lax.fori_loop(0, s // ROWS, merge, 0)


def _dilated(p, tv_b):
    b, s, _ = p.shape
    col = lambda c: pl.BlockSpec((1, s, LANE), lambda hp, bi: (bi, 0, c + hp))
    qkv = [col(COL_B + 3 * N_PAIRS * g + N_PAIRS * i) for g in range(N_DIL) for i in range(3)]
    return pl.pallas_call(
        _dil_kernel,
        grid=(N_PAIRS, b),
        in_specs=qkv + [col(COL_SGB), pl.BlockSpec(tv_b.shape, lambda hp, bi: (0, 0, 0, 0))],
        out_specs=pl.BlockSpec((1, s, LANE), lambda hp, bi: (bi, 0, hp)),
        out_shape=jax.ShapeDtypeStruct((b, s, W_BR), ACT),
        scratch_shapes=[
            pltpu.VMEM((N_DIL, s, LANE), F32),
            pltpu.VMEM((N_DIL, s, LANE), F32),
            pltpu.VMEM((N_DIL, 2, 2, BAND, BAND), F32),
        ],
        compiler_params=_cparams(2),
        name="dilated",
    )(*([p] * 10), tv_b)


def _swa_kernel(sink_ref, q_ref, sg_ref, k_ref, v_ref, tv_ref, o_ref, bt_ref):
    hp = pl.program_id(0)
    b = pl.program_id(1)
    n_tiles = q_ref.shape[1] // BAND
    pairs_per_kv = N_PAIRS // KV_C

    @pl.when(b == 0)
    def _():
        for sh in range(2):
            own, prev = _band_bias(tv_ref, 2 * hp + sh)
            bt_ref[sh, 0] = own
            bt_ref[sh, 1] = prev

    mask_own, mask_prev = _band_masks(False)
    kv_lo = HEAD_DIM * (hp // pairs_per_kv)
    row = lax.broadcasted_iota(jnp.int32, (LANE, BAND), 0)
    kv_rows = (row >= kv_lo) & (row < kv_lo + HEAD_DIM)

    def body(t, c):
        rows = pl.ds(pl.multiple_of(t * BAND, BAND), BAND)
        prow = pl.ds(pl.multiple_of(jnp.maximum(t - 1, 0) * BAND, BAND), BAND)
        qt = q_ref[0, rows, :].T
        k_own = k_ref[0, rows, :]
        k_prev = k_ref[0, prow, :]
        zero = jnp.zeros((LANE, BAND), ACT)
        vt_own = jnp.where(kv_rows, v_ref[0, rows, :].T, zero)
        vt_prev = jnp.where(kv_rows, v_ref[0, prow, :].T, zero)
        pm = mask_prev & (t > 0)
        outs = []
        for sh in range(2):
            qh = qt[HEAD_DIM * sh:HEAD_DIM * (sh + 1), :]
            qs = jnp.where(kv_rows, jnp.concatenate([qh, qh], axis=0), zero)
            o, _ = _band_head(qs, (k_own, vt_own, bt_ref[sh, 0], mask_own),
                              (k_prev, vt_prev, bt_ref[sh, 1], pm), sink_ref[2 * hp + sh])
            outs.append(o[:HEAD_DIM, :] + o[HEAD_DIM:, :])
        ot = jnp.concatenate(outs, axis=0).T
        o_ref[0, rows, :] = (ot * sg_ref[0, rows, :].astype(F32)).astype(o_ref.dtype)
        return c

    lax.fori_loop(0, n_tiles, body, 0, unroll=TILE_UNROLL)


def _swa(p, tv_c, sinks):
    b, s, _ = p.shape
    grid_spec = pltpu.PrefetchScalarGridSpec(
        num_scalar_prefetch=1,
        grid=(N_PAIRS, b),
        in_specs=[
            pl.BlockSpec((1, s, LANE), lambda hp, bi, sk: (bi, 0, COL_C + hp)),
            pl.BlockSpec((1, s, LANE), lambda hp, bi, sk: (bi, 0, COL_C + N_PAIRS + hp)),
            pl.BlockSpec((1, s, LANE), lambda hp, bi, sk: (bi, 0, COL_C + 2 * N_PAIRS)),
            pl.BlockSpec((1, s, LANE), lambda hp, bi, sk: (bi, 0, COL_C + 2 * N_PAIRS + 1)),
            pl.BlockSpec(tv_c.shape, lambda hp, bi, sk: (0, 0, 0, 0)),
        ],
        out_specs=pl.BlockSpec((1, s, LANE), lambda hp, bi, sk: (bi, 0, hp)),
        scratch_shapes=[pltpu.VMEM((2, 2, BAND, BAND), F32)],
    )
    return pl.pallas_call(
        _swa_kernel,
        grid_spec=grid_spec,
        out_shape=jax.ShapeDtypeStruct((b, s, W_BR), ACT),
        compiler_params=_cparams(2),
        name="swa",
    )(sinks, p, p, p, p, tv_c)


def _merge_kernel(bra_ref, brb_ref, brc_ref, g_ref, wb_ref, wo_ref, x_ref, out_ref):
    merged = None
    for i, br_ref in enumerate((bra_ref, brb_ref, brc_ref)):
        term = g_ref[:, i * D_MODEL:(i + 1) * D_MODEL].astype(F32) * _dot(br_ref[...], wb_ref[i])
        merged = term if merged is None else merged + term
    out_ref[...] = x_ref[...] + _dot(merged.astype(ACT), wo_ref[...])


def _merge(bra, brb, brc, p2, w_br, w_out, x2, layer, ts=512):
    n = x2.shape[0]
    row = lambda w: pl.BlockSpec((ts, w), lambda i: (i, 0))
    return pl.pallas_call(
        _merge_kernel,
        grid=(n // ts,),
        in_specs=[row(W_BR), row(W_BR), row(W_BR), row(GATE_W),
                  pl.BlockSpec((None,) + w_br.shape[1:], lambda i: (layer, 0, 0, 0)),
                  pl.BlockSpec((None,) + w_out.shape[1:], lambda i: (layer, 0, 0)),
                  row(D_MODEL)],
        out_specs=row(D_MODEL),
        out_shape=jax.ShapeDtypeStruct(x2.shape, F32),
        compiler_params=_cparams(1),
        name="merge",
    )(bra, brb, brc, p2, w_br, w_out, x2)


def _rel_bucket(dist):
    dist = jnp.maximum(dist, 0)
    max_exact = N_BUCKETS // 2
    log_ratio = jnp.log(jnp.maximum(dist, 1).astype(F32) / max_exact) / math.log(REL_MAX_DIST / max_exact)
    large = jnp.minimum(max_exact + (log_ratio * (N_BUCKETS - max_exact)).astype(jnp.int32), N_BUCKETS - 1)
    return jnp.where(dist < max_exact, dist, large)


def _bias_by_distance(tab, dist, chunk):
    vals = tab.astype(F32)[:, _rel_bucket(dist)]
    vals = jnp.pad(vals, ((0, 0), (chunk, 0)))
    return vals.reshape(tab.shape[0], -1, 1, chunk)


def _gain_rows(qk_g):
    depth = qk_g.shape[0]
    scale = HEAD_DIM ** -0.5
    one = jnp.ones((depth, HEAD_DIM), F32)
    tile = lambda v, n: jnp.tile(v, (1, n))
    parts = [tile(qk_g[:, 0] * scale, N_HEADS), tile(qk_g[:, 1], N_HEADS), tile(one, 2 * N_HEADS),
             tile(qk_g[:, 2] * scale, N_DIL * N_HEADS), tile(qk_g[:, 3], N_DIL * N_HEADS),
             tile(one, (N_DIL + 1) * N_HEADS),
             tile(qk_g[:, 4] * scale, N_HEADS), tile(qk_g[:, 5], KV_C), tile(one, KV_C + N_HEADS),
             tile(one, GATE_W // HEAD_DIM)]
    return jnp.concatenate(parts, axis=1)


def kernel(x, ln_g, w_in, qk_g, sinks, w_branch, w_out, rel_bias):
    b, s, d = x.shape
    depth = ln_g.shape[0]
    assert d == D_MODEL and s % DIL_PAIRS[-1][0] == 0 and w_in.shape[-1] == C_IN

    tv_a = _bias_by_distance(rel_bias[:N_HEADS], jnp.arange(s), MOBA_BLOCK)
    tv_b = jnp.concatenate(
        [_bias_by_distance(rel_bias[OFF_B + g * N_HEADS:OFF_B + (g + 1) * N_HEADS], dd * jnp.arange(2 * BAND), BAND)
         for g, (_, dd) in enumerate(DIL_PAIRS)], axis=0)
    tv_c = _bias_by_distance(rel_bias[OFF_C:OFF_C + N_HEADS], jnp.arange(2 * BAND), BAND)
    seg = jnp.arange(TN) // HEAD_DIM
    e_mat = (seg[:, None] == seg[None, :]).astype(ACT)
    src = jnp.asarray([c for c, _ in _LAYOUT], jnp.int32)
    fn = jnp.asarray([f for _, f in _LAYOUT], jnp.int32)
    gains = _gain_rows(qk_g)

    w_in_b = w_in.astype(ACT)
    w_br_b = w_branch.astype(ACT)
    w_out_b = w_out.astype(ACT)

    for l in range(depth):
        p = _project(x, ln_g[:, None, :], w_in_b, gains[l][None, :], e_mat, src, fn, l)
        bra = _moba(p, tv_a)
        brb = _dilated(p, tv_b)
        brc = _swa(p, tv_c, sinks[l])
        x2 = _merge(bra.reshape(b * s, W_BR), brb.reshape(b * s, W_BR), brc.reshape(b * s, W_BR),
                    p.reshape(b * s, C_IN), w_br_b, w_out_b, x.reshape(b * s, d), l)
        x = x2.reshape(b, s, d)
    return x
```

```python
import functools
import math
import operator

import jax
import jax.numpy as jnp
from jax import lax
from jax.experimental import pallas as pl
from jax.experimental.pallas import tpu as pltpu

D_MODEL = 1024
HEAD_DIM = 64
W_BR = D_MODEL // 2
N_HEADS = W_BR // HEAD_DIM
N_PAIRS = N_HEADS // 2
MOBA_BLOCK = 256
MOBA_TOPK = 3
DIL_PAIRS = ((128, 1), (512, 4), (2048, 16))
N_DIL = len(DIL_PAIRS)
KV_C = 2
SWA_WINDOW = 128
N_BUCKETS = 32
REL_MAX_DIST = 2048
OFF_B = N_HEADS
OFF_C = N_HEADS + N_DIL * N_HEADS
EPS = 1e-6
NEG = -1e30

LANE = 128
BAND = 128
TN = 256
ROWS = 256
TILE_GROUP = 4
VMEM_LIMIT = 56 * 1024 * 1024

ACT = jnp.bfloat16
F32 = jnp.float32

PLAIN, NORM, SILU, SIGMOID, KNORM_VPLAIN, NORM_D4, PLAIN_D4, NORM_D16, PLAIN_D16 = range(9)
_B_GROUP = lambda g, norm, plain: [(8 + 2 * g, norm), (9 + 2 * g, norm), (14 + 2 * g, norm), (15 + 2 * g, norm),
                                   (20 + 2 * g, plain), (21 + 2 * g, plain)]
_LAYOUT = ([(j, SIGMOID) for j in range(33, 45)]
           + [(0, NORM), (1, NORM), (2, NORM), (3, NORM), (4, PLAIN), (5, PLAIN), (6, SILU), (7, SILU)]
           + _B_GROUP(0, NORM, PLAIN) + _B_GROUP(1, NORM_D4, PLAIN_D4) + _B_GROUP(2, NORM_D16, PLAIN_D16)
           + [(26, SILU), (27, SILU)]
           + [(28, NORM), (29, NORM), (31, SILU), (32, SILU), (30, KNORM_VPLAIN)])
NJ = len(_LAYOUT)
C_IN = NJ * TN
COL_GATE, COL_A, COL_B, COL_SGB, COL_C = 0, 24, 40, 76, 80
GATE_W = 3 * D_MODEL


def _cparams(n_axes):
    return pltpu.CompilerParams(dimension_semantics=("arbitrary",) * n_axes, vmem_limit_bytes=VMEM_LIMIT)


def _dot(a, b):
    return jnp.dot(a, b, preferred_element_type=F32)


def _sigmoid(a):
    return 0.5 * jnp.tanh(0.5 * a) + 0.5


def _proj_kernel(src_ref, fn_ref, x_ref, lng_ref, w_ref, gain_ref, e_ref, o_ref, xn_ref, acc_ref):
    del src_ref
    j = pl.program_id(1)
    s = x_ref.shape[1]
    n_chunks = s // ROWS

    @pl.when(j == 0)
    def _():
        def body(i, c):
            rows = pl.ds(pl.multiple_of(i * ROWS, ROWS), ROWS)
            xv = x_ref[0, rows, :]
            ms = jnp.mean(xv * xv, axis=-1, keepdims=True)
            xn_ref[rows, :] = (xv * lax.rsqrt(ms + EPS) * lng_ref[...]).astype(ACT)
            return c
        lax.fori_loop(0, n_chunks, body, 0)

    def head_norm(a):
        ss = _dot((a * a).astype(ACT), e_ref[...])
        return a * lax.rsqrt(ss * (1.0 / HEAD_DIM) + EPS) * gain_ref[...]

    def silu(a):
        return a * _sigmoid(a)

    def k_norm_v_plain(a):
        lane = lax.broadcasted_iota(jnp.int32, a.shape, 1)
        return jnp.where(lane < KV_C * HEAD_DIM, head_norm(a), a)

    def chunk(i):
        return _dot(xn_ref[i * ROWS:(i + 1) * ROWS, :], w_ref[...])

    fn = fn_ref[j]

    def direct(code, f):
        @pl.when(fn == code)
        def _():
            pending = None
            for i in range(n_chunks + 1):
                a = chunk(i) if i < n_chunks else None
                if pending is not None:
                    o_ref[0, (i - 1) * ROWS:i * ROWS, :] = f(pending).astype(o_ref.dtype)
                pending = a

    def phase_major(code, f, d):
        @pl.when(fn == code)
        def _():
            for i in range(n_chunks):
                a = chunk(i)
                for h in range(TN // LANE):
                    acc_ref[h, i * ROWS:(i + 1) * ROWS, :] = a[:, h * LANE:(h + 1) * LANE]
            for c in range(s // BAND):
                r, m0 = divmod(c * BAND, s // d)
                a = jnp.concatenate(
                    [acc_ref.at[h][pl.ds(m0 * d + r, BAND, stride=d), :] for h in range(TN // LANE)], axis=1)
                o_ref[0, c * BAND:(c + 1) * BAND, :] = f(a).astype(o_ref.dtype)

    direct(PLAIN, lambda a: a)
    direct(NORM, head_norm)
    direct(SILU, silu)
    direct(SIGMOID, _sigmoid)
    direct(KNORM_VPLAIN, k_norm_v_plain)
    phase_major(NORM_D4, head_norm, DIL_PAIRS[1][1])
    phase_major(PLAIN_D4, lambda a: a, DIL_PAIRS[1][1])
    phase_major(NORM_D16, head_norm, DIL_PAIRS[2][1])
    phase_major(PLAIN_D16, lambda a: a, DIL_PAIRS[2][1])


def _project(x, ln_g, w_in, gain_row, e_mat, src, fn, layer):
    b, s, d = x.shape
    grid_spec = pltpu.PrefetchScalarGridSpec(
        num_scalar_prefetch=2,
        grid=(b, NJ),
        in_specs=[
            pl.BlockSpec((1, s, d), lambda bi, j, src_ref, fn_ref: (bi, 0, 0)),
            pl.BlockSpec((None, 1, d), lambda bi, j, src_ref, fn_ref: (layer, 0, 0)),
            pl.BlockSpec((None, d, TN), lambda bi, j, src_ref, fn_ref: (layer, 0, src_ref[j])),
            pl.BlockSpec((1, TN), lambda bi, j, src_ref, fn_ref: (0, src_ref[j])),
            pl.BlockSpec((TN, TN), lambda bi, j, src_ref, fn_ref: (0, 0)),
        ],
        out_specs=pl.BlockSpec((1, s, TN), lambda bi, j, src_ref, fn_ref: (bi, 0, j)),
        scratch_shapes=[pltpu.VMEM((s, d), ACT), pltpu.VMEM((TN // LANE, s, LANE), F32)],
    )
    return pl.pallas_call(
        _proj_kernel,
        grid_spec=grid_spec,
        out_shape=jax.ShapeDtypeStruct((b, s, C_IN), ACT),
        compiler_params=_cparams(2),
        name="proj",
    )(src, fn, x, ln_g, w_in, gain_row, e_mat)


def _toeplitz(lo, hi, n):
    x = jnp.concatenate([jnp.broadcast_to(lo, (n, n)), jnp.broadcast_to(hi, (n, n))], axis=1)
    return pltpu.roll(x, 0, 1, stride=1, stride_axis=0)[:, n:]


def _head_rows(qt, lo):
    row = lax.broadcasted_iota(jnp.int32, qt.shape, 0)
    return jnp.where((row >= lo) & (row < lo + HEAD_DIM), qt, jnp.zeros_like(qt))


def _moba_kernel(q_ref, k_ref, v_ref, sg_ref, tv_ref, o_ref,
                 bt_ref, qts_ref, vt_ref, km_ref, sel_ref, m_ref, l_ref, acc_ref):
    hp = pl.program_id(0)
    b = pl.program_id(1)
    blk = MOBA_BLOCK
    n_blk = k_ref.shape[1] // blk

    def rows_of(i):
        return pl.ds(pl.multiple_of(i * blk, blk), blk)

    @pl.when(b == 0)
    def _():
        for sh in range(2):
            def body(o, c, sh=sh):
                bt_ref[sh, o] = _toeplitz(tv_ref[2 * hp + sh, o], tv_ref[2 * hp + sh, o + 1], blk)
                return c
            lax.fori_loop(0, n_blk, body, 0)

    def prep(i, c):
        vt_ref[i] = v_ref[0, rows_of(i), :].T
        km_ref[pl.ds(i, 1), :] = jnp.mean(k_ref[0, rows_of(i), :].astype(F32), axis=0, keepdims=True)
        qt = q_ref[0, rows_of(i), :].T
        for sh in range(2):
            qts_ref[i, sh] = _head_rows(qt, HEAD_DIM * sh)
        return c
    lax.fori_loop(0, n_blk, prep, 0, unroll=2)

    km = km_ref[...]
    km0 = km.astype(ACT)
    r1 = km - km0.astype(F32)
    km1 = r1.astype(ACT)
    km2 = (r1 - km1.astype(F32)).astype(ACT)
    blk_row = lax.broadcasted_iota(jnp.int32, (n_blk, blk), 0)

    def select(qi, c):
        for sh in range(2):
            qs = qts_ref[qi, sh]
            g = _dot(km0, qs) + _dot(km1, qs) + _dot(km2, qs)
            g = jnp.where(blk_row < qi, g, NEG)
            sel_rows = []
            for jb in range(n_blk):
                gj = g[jb:jb + 1, :]
                beats = (g > gj) | ((g == gj) & (blk_row < jb))
                cnt = jnp.sum(beats.astype(F32), axis=0, keepdims=True)
                sel_rows.append(jnp.where((cnt < MOBA_TOPK) & (jb < qi), 1.0, 0.0))
            sel_ref[qi, sh] = jnp.concatenate(sel_rows, axis=0)
        return c
    lax.fori_loop(0, n_blk, select, 0, unroll=2)

    kr = lax.broadcasted_iota(jnp.int32, (blk, blk), 0)
    qc = lax.broadcasted_iota(jnp.int32, (blk, blk), 1)
    causal = qc >= kr

    def tiles(off, qis):
        k_blks = [k_ref[0, rows_of(qi - off), :] for qi in qis]
        scores = [_dot(k_blk, qts_ref[qi, sh]) for qi, k_blk in zip(qis, k_blks) for sh in range(2)]
        work = [(qi, sh) for qi in qis for sh in range(2)]
        stats = []
        for (qi, sh), st in zip(work, scores):
            if off == 0:
                mask, m, l = causal, jnp.full((1, blk), NEG, F32), jnp.zeros((1, blk), F32)
            else:
                mask, m, l = sel_ref[qi, sh, pl.ds(qi - off, 1), :] > 0.5, m_ref[qi, sh], l_ref[qi, sh]
            st = jnp.where(mask, st + bt_ref[sh, off], NEG)
            m_new = jnp.maximum(m, jnp.max(st, axis=0, keepdims=True))
            p = jnp.exp(st - m_new)
            alpha = jnp.exp(m - m_new)
            m_ref[qi, sh] = m_new
            l_ref[qi, sh] = alpha * l + jnp.sum(p, axis=0, keepdims=True)
            stats.append((alpha, p.astype(ACT)))
        pvs = [_dot(vt_ref[qi - off, HEAD_DIM * sh:HEAD_DIM * (sh + 1), :], p) for (qi, sh), (_, p) in zip(work, stats)]
        for (qi, sh), (alpha, _), pv in zip(work, stats, pvs):
            acc_ref[qi, sh] = pv if off == 0 else alpha * acc_ref[qi, sh] + pv

    group = 2
    for off in range(n_blk):
        n = n_blk - off
        if n // group:
            def body(i, c, off=off):
                tiles(off, [off + group * i + u for u in range(group)])
                return c
            lax.fori_loop(0, n // group, body, 0)
        if n % group:
            tiles(off, [off + (n // group) * group + u for u in range(n % group)])

    def finish(qi, c):
        ot = jnp.concatenate([acc_ref[qi, sh] * (1.0 / l_ref[qi, sh]) for sh in range(2)], axis=0)
        o_ref[0, rows_of(qi), :] = (ot.T * sg_ref[0, rows_of(qi), :].astype(F32)).astype(o_ref.dtype)
        return c
    lax.fori_loop(0, n_blk, finish, 0, unroll=2)


def _moba(p, tv_a):
    b, s, _ = p.shape
    blk = MOBA_BLOCK
    n_blk = s // blk
    col = lambda c: pl.BlockSpec((1, s, LANE), lambda hp, bi: (bi, 0, COL_A + c * N_PAIRS + hp))
    return pl.pallas_call(
        _moba_kernel,
        grid=(N_PAIRS, b),
        in_specs=[col(0), col(1), col(2), col(3), pl.BlockSpec(tv_a.shape, lambda hp, bi: (0, 0, 0, 0))],
        out_specs=pl.BlockSpec((1, s, LANE), lambda hp, bi: (bi, 0, hp)),
        out_shape=jax.ShapeDtypeStruct((b, s, W_BR), ACT),
        scratch_shapes=[
            pltpu.VMEM((2, n_blk, blk, blk), F32),
            pltpu.VMEM((n_blk, 2, LANE, blk), ACT),
            pltpu.VMEM((n_blk, LANE, blk), ACT),
            pltpu.VMEM((n_blk, LANE), F32),
            pltpu.VMEM((n_blk, 2, n_blk, blk), F32),
            pltpu.VMEM((n_blk, 2, 1, blk), F32),
            pltpu.VMEM((n_blk, 2, 1, blk), F32),
            pltpu.VMEM((n_blk, 2, HEAD_DIM, blk), F32),
        ],
        compiler_params=_cparams(2),
        name="moba",
    )(p, p, p, p, tv_a)


def _band_attend(items):
    scores = [[_dot(k, qs) for k, _, _, _ in blocks] for qs, blocks, _ in items]
    probs = []
    for (_, blocks, sink), sc in zip(items, scores):
        ss = [jnp.where(mask, s + bias, NEG) for s, (_, _, bias, mask) in zip(sc, blocks)]
        m = functools.reduce(jnp.maximum, [jnp.max(s, axis=0, keepdims=True) for s in ss])
        if sink is not None:
            m = jnp.maximum(m, sink)
        ps = [jnp.exp(s - m) for s in ss]
        l = functools.reduce(operator.add, [jnp.sum(p, axis=0, keepdims=True) for p in ps])
        if sink is not None:
            l = l + jnp.exp(sink - m)
        probs.append((m, l, [p.astype(ACT) for p in ps]))
    outs = []
    for (_, blocks, _), (m, l, ps) in zip(items, probs):
        acc = functools.reduce(operator.add, [_dot(vt, p) for (_, vt, _, _), p in zip(blocks, ps)])
        outs.append((acc * (1.0 / l), m + jnp.log(l)))
    return outs


def _band_masks(prev_includes_window_edge):
    kr = lax.broadcasted_iota(jnp.int32, (BAND, BAND), 0)
    qc = lax.broadcasted_iota(jnp.int32, (BAND, BAND), 1)
    own = qc >= kr
    prev = (qc <= kr) if prev_includes_window_edge else (qc < kr)
    return own, prev


def _band_bias(tv_ref, head):
    return [_toeplitz(tv_ref[head, which], tv_ref[head, which + 1], BAND) for which in range(2)]


def _dil_kernel(q0_ref, k0_ref, v0_ref, q1_ref, k1_ref, v1_ref, q2_ref, k2_ref, v2_ref, sg_ref, tv_ref,
                o_ref, og_ref, lg_ref, bt_ref):
    hp = pl.program_id(0)
    b = pl.program_id(1)
    s = q0_ref.shape[1]
    n_tiles = s // BAND

    @pl.when(b == 0)
    def _():
        for g in range(N_DIL):
            for sh in range(2):
                own, prev = _band_bias(tv_ref, g * N_HEADS + 2 * hp + sh)
                bt_ref[g, sh, 0] = own
                bt_ref[g, sh, 1] = prev

    mask_own, mask_prev = _band_masks(True)
    groups = ((q0_ref, k0_ref, v0_ref), (q1_ref, k1_ref, v1_ref), (q2_ref, k2_ref, v2_ref))

    for g, (q_ref, k_ref, v_ref) in enumerate(groups):
        d = DIL_PAIRS[g][1]
        per_phase = n_tiles // d
        has_prev = per_phase > 1

        def body(i, c, g=g, d=d, per_phase=per_phase, has_prev=has_prev, q_ref=q_ref, k_ref=k_ref, v_ref=v_ref):
            items, dsts = [], []
            for n in range(TILE_GROUP):
                u = i * TILE_GROUP + n
                rows = pl.ds(pl.multiple_of(u * BAND, BAND), BAND)
                t = lax.rem(u, per_phase)
                r = lax.div(u, per_phase)
                qt = q_ref[0, rows, :].T
                k_own = k_ref[0, rows, :]
                vt_own = v_ref[0, rows, :].T
                if has_prev:
                    prow = pl.ds(pl.multiple_of(jnp.maximum(u - 1, 0) * BAND, BAND), BAND)
                    k_prev = k_ref[0, prow, :]
                    vt_prev = v_ref[0, prow, :].T
                    pm = mask_prev & (t > 0)
                for sh in range(2):
                    hrows = slice(HEAD_DIM * sh, HEAD_DIM * (sh + 1))
                    blocks = [(k_own, vt_own[hrows, :], bt_ref[g, sh, 0], mask_own)]
                    if has_prev:
                        blocks.append((k_prev, vt_prev[hrows, :], bt_ref[g, sh, 1], pm))
                    items.append((_head_rows(qt, HEAD_DIM * sh), blocks, None))
                dsts.append(rows if d == 1 else pl.ds(t * (BAND * d) + r, BAND, stride=d))
            res = _band_attend(items)
            for n, dst in enumerate(dsts):
                (o0, lse0), (o1, lse1) = res[2 * n], res[2 * n + 1]
                og_ref.at[g][dst, :] = jnp.concatenate([o0, o1], axis=0).T
                lg_ref.at[g][dst, :] = jnp.concatenate(
                    [jnp.broadcast_to(lse0, (HEAD_DIM, BAND)), jnp.broadcast_to(lse1, (HEAD_DIM, BAND))], axis=0).T
            return c

        lax.fori_loop(0, n_tiles // TILE_GROUP, body, 0)

    def merge(i, c):
        rows = pl.ds(pl.multiple_of(i * ROWS, ROWS), ROWS)
        l0, l1, l2 = lg_ref[0, rows, :], lg_ref[1, rows, :], lg_ref[2, rows, :]
        m = jnp.maximum(l0, jnp.maximum(l1, l2))
        e0, e1, e2 = jnp.exp(l0 - m), jnp.exp(l1 - m), jnp.exp(l2 - m)
        ob = (e0 * og_ref[0, rows, :] + e1 * og_ref[1, rows, :] + e2 * og_ref[2, rows, :]) / (e0 + e1 + e2)
        o_ref[0, rows, :] = (ob * sg_ref[0, rows, :].astype(F32)).astype(o_ref.dtype)
        return c

    lax.fori_loop(0, s // ROWS, merge, 0)


def _dilated(p, tv_b):
    b, s, _ = p.shape
    col = lambda c: pl.BlockSpec((1, s, LANE), lambda hp, bi: (bi, 0, c + hp))
    qkv = [col(COL_B + 3 * N_PAIRS * g + N_PAIRS * i) for g in range(N_DIL) for i in range(3)]
    return pl.pallas_call(
        _dil_kernel,
        grid=(N_PAIRS, b),
        in_specs=qkv + [col(COL_SGB), pl.BlockSpec(tv_b.shape, lambda hp, bi: (0, 0, 0, 0))],
        out_specs=pl.BlockSpec((1, s, LANE), lambda hp, bi: (bi, 0, hp)),
        out_shape=jax.ShapeDtypeStruct((b, s, W_BR), ACT),
        scratch_shapes=[
            pltpu.VMEM((N_DIL, s, LANE), F32),
            pltpu.VMEM((N_DIL, s, LANE), F32),
            pltpu.VMEM((N_DIL, 2, 2, BAND, BAND), F32),
        ],
        compiler_params=_cparams(2),
        name="dilated",
    )(*([p] * 10), tv_b)


def _swa_kernel(sink_ref, q_ref, sg_ref, k_ref, v_ref, tv_ref, o_ref, bt_ref):
    hp = pl.program_id(0)
    b = pl.program_id(1)
    n_tiles = q_ref.shape[1] // BAND
    pairs_per_kv = N_PAIRS // KV_C

    @pl.when(b == 0)
    def _():
        for sh in range(2):
            own, prev = _band_bias(tv_ref, 2 * hp + sh)
            bt_ref[sh, 0] = own
            bt_ref[sh, 1] = prev

    mask_own, mask_prev = _band_masks(False)
    kv_lo = HEAD_DIM * (hp // pairs_per_kv)
    row = lax.broadcasted_iota(jnp.int32, (LANE, BAND), 0)
    kv_rows = (row >= kv_lo) & (row < kv_lo + HEAD_DIM)

    def body(i, c):
        items, dsts = [], []
        zero = jnp.zeros((LANE, BAND), ACT)
        for n in range(TILE_GROUP):
            t = i * TILE_GROUP + n
            rows = pl.ds(pl.multiple_of(t * BAND, BAND), BAND)
            prow = pl.ds(pl.multiple_of(jnp.maximum(t - 1, 0) * BAND, BAND), BAND)
            qt = q_ref[0, rows, :].T
            k_own = k_ref[0, rows, :]
            k_prev = k_ref[0, prow, :]
            vt_own = jnp.where(kv_rows, v_ref[0, rows, :].T, zero)
            vt_prev = jnp.where(kv_rows, v_ref[0, prow, :].T, zero)
            pm = mask_prev & (t > 0)
            for sh in range(2):
                qh = qt[HEAD_DIM * sh:HEAD_DIM * (sh + 1), :]
                qs = jnp.where(kv_rows, jnp.concatenate([qh, qh], axis=0), zero)
                items.append((qs, [(k_own, vt_own, bt_ref[sh, 0], mask_own), (k_prev, vt_prev, bt_ref[sh, 1], pm)],
                              sink_ref[2 * hp + sh]))
            dsts.append(rows)
        res = _band_attend(items)
        for n, rows in enumerate(dsts):
            ot = jnp.concatenate([o[:HEAD_DIM, :] + o[HEAD_DIM:, :] for o, _ in res[2 * n:2 * n + 2]], axis=0).T
            o_ref[0, rows, :] = (ot * sg_ref[0, rows, :].astype(F32)).astype(o_ref.dtype)
        return c

    lax.fori_loop(0, n_tiles // TILE_GROUP, body, 0)


def _swa(p, tv_c, sinks):
    b, s, _ = p.shape
    grid_spec = pltpu.PrefetchScalarGridSpec(
        num_scalar_prefetch=1,
        grid=(N_PAIRS, b),
        in_specs=[
            pl.BlockSpec((1, s, LANE), lambda hp, bi, sk: (bi, 0, COL_C + hp)),
            pl.BlockSpec((1, s, LANE), lambda hp, bi, sk: (bi, 0, COL_C + N_PAIRS + hp)),
            pl.BlockSpec((1, s, LANE), lambda hp, bi, sk: (bi, 0, COL_C + 2 * N_PAIRS)),
            pl.BlockSpec((1, s, LANE), lambda hp, bi, sk: (bi, 0, COL_C + 2 * N_PAIRS + 1)),
            pl.BlockSpec(tv_c.shape, lambda hp, bi, sk: (0, 0, 0, 0)),
        ],
        out_specs=pl.BlockSpec((1, s, LANE), lambda hp, bi, sk: (bi, 0, hp)),
        scratch_shapes=[pltpu.VMEM((2, 2, BAND, BAND), F32)],
    )
    return pl.pallas_call(
        _swa_kernel,
        grid_spec=grid_spec,
        out_shape=jax.ShapeDtypeStruct((b, s, W_BR), ACT),
        compiler_params=_cparams(2),
        name="swa",
    )(sinks, p, p, p, p, tv_c)


def _merge_kernel(bra_ref, brb_ref, brc_ref, g_ref, wb_ref, wo_ref, x_ref, out_ref):
    merged = None
    for i, br_ref in enumerate((bra_ref, brb_ref, brc_ref)):
        term = g_ref[:, i * D_MODEL:(i + 1) * D_MODEL].astype(F32) * _dot(br_ref[...], wb_ref[i])
        merged = term if merged is None else merged + term
    out_ref[...] = x_ref[...] + _dot(merged.astype(ACT), wo_ref[...])


def _merge(bra, brb, brc, p2, w_br, w_out, x2, layer, ts=512):
    n = x2.shape[0]
    row = lambda w: pl.BlockSpec((ts, w), lambda i: (i, 0))
    return pl.pallas_call(
        _merge_kernel,
        grid=(n // ts,),
        in_specs=[row(W_BR), row(W_BR), row(W_BR), row(GATE_W),
                  pl.BlockSpec((None,) + w_br.shape[1:], lambda i: (layer, 0, 0, 0)),
                  pl.BlockSpec((None,) + w_out.shape[1:], lambda i: (layer, 0, 0)),
                  row(D_MODEL)],
        out_specs=row(D_MODEL),
        out_shape=jax.ShapeDtypeStruct(x2.shape, F32),
        compiler_params=_cparams(1),
        name="merge",
    )(bra, brb, brc, p2, w_br, w_out, x2)


def _rel_bucket(dist):
    dist = jnp.maximum(dist, 0)
    max_exact = N_BUCKETS // 2
    log_ratio = jnp.log(jnp.maximum(dist, 1).astype(F32) / max_exact) / math.log(REL_MAX_DIST / max_exact)
    large = jnp.minimum(max_exact + (log_ratio * (N_BUCKETS - max_exact)).astype(jnp.int32), N_BUCKETS - 1)
    return jnp.where(dist < max_exact, dist, large)


def _bias_by_distance(tab, dist, chunk):
    vals = tab.astype(F32)[:, _rel_bucket(dist)]
    vals = jnp.pad(vals, ((0, 0), (chunk, 0)))
    return vals.reshape(tab.shape[0], -1, 1, chunk)


def _gain_rows(qk_g):
    depth = qk_g.shape[0]
    scale = HEAD_DIM ** -0.5
    one = jnp.ones((depth, HEAD_DIM), F32)
    tile = lambda v, n: jnp.tile(v, (1, n))
    parts = [tile(qk_g[:, 0] * scale, N_HEADS), tile(qk_g[:, 1], N_HEADS), tile(one, 2 * N_HEADS),
             tile(qk_g[:, 2] * scale, N_DIL * N_HEADS), tile(qk_g[:, 3], N_DIL * N_HEADS),
             tile(one, (N_DIL + 1) * N_HEADS),
             tile(qk_g[:, 4] * scale, N_HEADS), tile(qk_g[:, 5], KV_C), tile(one, KV_C + N_HEADS),
             tile(one, GATE_W // HEAD_DIM)]
    return jnp.concatenate(parts, axis=1)


def kernel(x, ln_g, w_in, qk_g, sinks, w_branch, w_out, rel_bias):
    b, s, d = x.shape
    depth = ln_g.shape[0]
    assert d == D_MODEL and s % DIL_PAIRS[-1][0] == 0 and w_in.shape[-1] == C_IN

    tv_a = _bias_by_distance(rel_bias[:N_HEADS], jnp.arange(s), MOBA_BLOCK)
    tv_b = jnp.concatenate(
        [_bias_by_distance(rel_bias[OFF_B + g * N_HEADS:OFF_B + (g + 1) * N_HEADS], dd * jnp.arange(2 * BAND), BAND)
         for g, (_, dd) in enumerate(DIL_PAIRS)], axis=0)
    tv_c = _bias_by_distance(rel_bias[OFF_C:OFF_C + N_HEADS], jnp.arange(2 * BAND), BAND)
    seg = jnp.arange(TN) // HEAD_DIM
    e_mat = (seg[:, None] == seg[None, :]).astype(ACT)
    src = jnp.asarray([c for c, _ in _LAYOUT], jnp.int32)
    fn = jnp.asarray([f for _, f in _LAYOUT], jnp.int32)
    gains = _gain_rows(qk_g)

    w_in_b = w_in.astype(ACT)
    w_br_b = w_branch.astype(ACT)
    w_out_b = w_out.astype(ACT)

    for l in range(depth):
        p = _project(x, ln_g[:, None, :], w_in_b, gains[l][None, :], e_mat, src, fn, l)
        bra = _moba(p, tv_a)
        brb = _dilated(p, tv_b)
        brc = _swa(p, tv_c, sinks[l])
        x2 = _merge(bra.reshape(b * s, W_BR), brb.reshape(b * s, W_BR), brc.reshape(b * s, W_BR),
                    p.reshape(b * s, C_IN), w_br_b, w_out_b, x.reshape(b * s, d), l)
        x = x2.reshape(b, s, d)
    return x
```

```python
import functools
import math
import operator

import jax
import jax.numpy as jnp
from jax import lax
from jax.experimental import pallas as pl
from jax.experimental.pallas import tpu as pltpu

D_MODEL = 1024
HEAD_DIM = 64
W_BR = D_MODEL // 2
N_HEADS = W_BR // HEAD_DIM
N_PAIRS = N_HEADS // 2
MOBA_BLOCK = 256
MOBA_TOPK = 3
DIL_PAIRS = ((128, 1), (512, 4), (2048, 16))
N_DIL = len(DIL_PAIRS)
KV_C = 2
SWA_WINDOW = 128
N_BUCKETS = 32
REL_MAX_DIST = 2048
OFF_B = N_HEADS
OFF_C = N_HEADS + N_DIL * N_HEADS
EPS = 1e-6
NEG = -1e30
LOG2E = math.log2(math.e)

LANE = 128
BAND = 128
TN = 256
ROWS = 256
TILE_GROUP = 8
MOBA_GROUP = 4
VMEM_LIMIT = 56 * 1024 * 1024

ACT = jnp.bfloat16
F32 = jnp.float32

PLAIN, NORM, SILU, SIGMOID, KNORM_VPLAIN, NORM_D4, PLAIN_D4, NORM_D16, PLAIN_D16 = range(9)
_B_GROUP = lambda g, norm, plain: [(8 + 2 * g, norm), (9 + 2 * g, norm), (14 + 2 * g, norm), (15 + 2 * g, norm),
                                   (20 + 2 * g, plain), (21 + 2 * g, plain)]
_LAYOUT = ([(j, SIGMOID) for j in range(33, 45)]
           + [(0, NORM), (1, NORM), (2, NORM), (3, NORM), (4, PLAIN), (5, PLAIN), (6, SILU), (7, SILU)]
           + _B_GROUP(0, NORM, PLAIN) + _B_GROUP(1, NORM_D4, PLAIN_D4) + _B_GROUP(2, NORM_D16, PLAIN_D16)
           + [(26, SILU), (27, SILU)]
           + [(28, NORM), (29, NORM), (31, SILU), (32, SILU), (30, KNORM_VPLAIN)])
NJ = len(_LAYOUT)
C_IN = NJ * TN
COL_GATE, COL_A, COL_B, COL_SGB, COL_C = 0, 24, 40, 76, 80
GATE_W = 3 * D_MODEL


def _cparams(n_axes):
    return pltpu.CompilerParams(dimension_semantics=("arbitrary",) * n_axes, vmem_limit_bytes=VMEM_LIMIT)


def _dot(a, b):
    return jnp.dot(a, b, preferred_element_type=F32)


def _sigmoid(a):
    return 0.5 * jnp.tanh(0.5 * a) + 0.5


def _proj_kernel(src_ref, fn_ref, x_ref, lng_ref, w_ref, gain_ref, e_ref, o_ref, xn_ref, acc_ref):
    del src_ref
    j = pl.program_id(1)
    s = x_ref.shape[1]
    n_chunks = s // ROWS

    @pl.when(j == 0)
    def _():
        def body(i, c):
            rows = pl.ds(pl.multiple_of(i * ROWS, ROWS), ROWS)
            xv = x_ref[0, rows, :]
            ms = jnp.mean(xv * xv, axis=-1, keepdims=True)
            xn_ref[rows, :] = (xv * lax.rsqrt(ms + EPS) * lng_ref[...]).astype(ACT)
            return c
        lax.fori_loop(0, n_chunks, body, 0)

    def head_norm(a):
        ss = _dot((a * a).astype(ACT), e_ref[...])
        return a * lax.rsqrt(ss * (1.0 / HEAD_DIM) + EPS) * gain_ref[...]

    def silu(a):
        return a * _sigmoid(a)

    def k_norm_v_plain(a):
        lane = lax.broadcasted_iota(jnp.int32, a.shape, 1)
        return jnp.where(lane < KV_C * HEAD_DIM, head_norm(a), a)

    def chunk(i):
        return _dot(xn_ref[i * ROWS:(i + 1) * ROWS, :], w_ref[...])

    fn = fn_ref[j]

    def direct(code, f):
        @pl.when(fn == code)
        def _():
            pending = None
            for i in range(n_chunks + 1):
                a = chunk(i) if i < n_chunks else None
                if pending is not None:
                    o_ref[0, (i - 1) * ROWS:i * ROWS, :] = f(pending).astype(o_ref.dtype)
                pending = a

    def phase_major(code, f, d):
        @pl.when(fn == code)
        def _():
            for i in range(n_chunks):
                a = chunk(i)
                for h in range(TN // LANE):
                    acc_ref[h, i * ROWS:(i + 1) * ROWS, :] = a[:, h * LANE:(h + 1) * LANE]
            for c in range(s // BAND):
                r, m0 = divmod(c * BAND, s // d)
                a = jnp.concatenate(
                    [acc_ref.at[h][pl.ds(m0 * d + r, BAND, stride=d), :] for h in range(TN // LANE)], axis=1)
                o_ref[0, c * BAND:(c + 1) * BAND, :] = f(a).astype(o_ref.dtype)

    direct(PLAIN, lambda a: a)
    direct(NORM, head_norm)
    direct(SILU, silu)
    direct(SIGMOID, _sigmoid)
    direct(KNORM_VPLAIN, k_norm_v_plain)
    phase_major(NORM_D4, head_norm, DIL_PAIRS[1][1])
    phase_major(PLAIN_D4, lambda a: a, DIL_PAIRS[1][1])
    phase_major(NORM_D16, head_norm, DIL_PAIRS[2][1])
    phase_major(PLAIN_D16, lambda a: a, DIL_PAIRS[2][1])


def _project(x, ln_g, w_in, gain_row, e_mat, src, fn, layer):
    b, s, d = x.shape
    grid_spec = pltpu.PrefetchScalarGridSpec(
        num_scalar_prefetch=2,
        grid=(b, NJ),
        in_specs=[
            pl.BlockSpec((1, s, d), lambda bi, j, src_ref, fn_ref: (bi, 0, 0)),
            pl.BlockSpec((None, 1, d), lambda bi, j, src_ref, fn_ref: (layer, 0, 0)),
            pl.BlockSpec((None, d, TN), lambda bi, j, src_ref, fn_ref: (layer, 0, src_ref[j])),
            pl.BlockSpec((1, TN), lambda bi, j, src_ref, fn_ref: (0, src_ref[j])),
            pl.BlockSpec((TN, TN), lambda bi, j, src_ref, fn_ref: (0, 0)),
        ],
        out_specs=pl.BlockSpec((1, s, TN), lambda bi, j, src_ref, fn_ref: (bi, 0, j)),
        scratch_shapes=[pltpu.VMEM((s, d), ACT), pltpu.VMEM((TN // LANE, s, LANE), F32)],
    )
    return pl.pallas_call(
        _proj_kernel,
        grid_spec=grid_spec,
        out_shape=jax.ShapeDtypeStruct((b, s, C_IN), ACT),
        compiler_params=_cparams(2),
        name="proj",
    )(src, fn, x, ln_g, w_in, gain_row, e_mat)


def _toeplitz(lo, hi, n):
    x = jnp.concatenate([jnp.broadcast_to(lo, (n, n)), jnp.broadcast_to(hi, (n, n))], axis=1)
    return pltpu.roll(x, 0, 1, stride=1, stride_axis=0)[:, n:]


def _head_rows(qt, lo):
    row = lax.broadcasted_iota(jnp.int32, qt.shape, 0)
    return jnp.where((row >= lo) & (row < lo + HEAD_DIM), qt, jnp.zeros_like(qt))


def _moba_kernel(q_ref, k_ref, v_ref, sg_ref, tv_ref, o_ref,
                 bt_ref, qts_ref, vt_ref, km_ref, sel_ref, m_ref, l_ref, acc_ref):
    hp = pl.program_id(0)
    b = pl.program_id(1)
    blk = MOBA_BLOCK
    n_blk = k_ref.shape[1] // blk

    def rows_of(i):
        return pl.ds(i * blk if isinstance(i, int) else pl.multiple_of(i * blk, blk), blk)

    @pl.when(b == 0)
    def _():
        kr = lax.broadcasted_iota(jnp.int32, (blk, blk), 0)
        qc = lax.broadcasted_iota(jnp.int32, (blk, blk), 1)
        for sh in range(2):
            def body(o, c, sh=sh):
                t = _toeplitz(tv_ref[2 * hp + sh, o], tv_ref[2 * hp + sh, o + 1], blk)
                bt_ref[sh, o] = jnp.where((qc >= kr) | (o > 0), t, NEG)
                return c
            lax.fori_loop(0, n_blk, body, 0)

    def prep(i, c):
        vt_ref[i] = v_ref[0, rows_of(i), :].T
        km_ref[pl.ds(i, 1), :] = jnp.mean(k_ref[0, rows_of(i), :].astype(F32), axis=0, keepdims=True)
        qt = q_ref[0, rows_of(i), :].T
        for sh in range(2):
            qts_ref[i, sh] = _head_rows(qt, HEAD_DIM * sh)
        return c
    lax.fori_loop(0, n_blk, prep, 0, unroll=2)

    km = km_ref[...]
    km0 = km.astype(ACT)
    r1 = km - km0.astype(F32)
    km1 = r1.astype(ACT)
    km2 = (r1 - km1.astype(F32)).astype(ACT)
    blk_row = lax.broadcasted_iota(jnp.int32, (n_blk, blk), 0)

    def select(qi, c):
        for sh in range(2):
            qs = qts_ref[qi, sh]
            g = _dot(km0, qs) + _dot(km1, qs) + _dot(km2, qs)
            g = jnp.where(blk_row < qi, g, NEG)
            sel_rows = []
            for jb in range(n_blk):
                gj = g[jb:jb + 1, :]
                beats = (g > gj) | ((g == gj) & (blk_row < jb))
                cnt = jnp.sum(beats.astype(F32), axis=0, keepdims=True)
                sel_rows.append(jnp.where((cnt < MOBA_TOPK) & (jb < qi), 1.0, 0.0))
            sel_ref[qi, sh] = jnp.concatenate(sel_rows, axis=0)
        return c
    lax.fori_loop(0, n_blk, select, 0, unroll=2)

    def tiles(off, qis):
        k_blks = [k_ref[0, rows_of(qi - off), :] for qi in qis]
        scores = [_dot(k_blk, qts_ref[qi, sh]) for qi, k_blk in zip(qis, k_blks) for sh in range(2)]
        work = [(qi, sh) for qi in qis for sh in range(2)]
        stats = []
        for (qi, sh), st in zip(work, scores):
            st = st + bt_ref[sh, off]
            tile_max = jnp.max(st, axis=0, keepdims=True)
            if off == 0:
                m_new, alpha = tile_max, None
                p = jnp.exp2(st - m_new)
                l_new = jnp.sum(p, axis=0, keepdims=True)
            else:
                sel = sel_ref[qi, sh, pl.ds(qi - off, 1), :] > 0.5
                m = m_ref[qi, sh]
                m_new = jnp.where(sel, jnp.maximum(m, tile_max), m)
                p = jnp.exp2(st - jnp.where(sel, m_new, -NEG))
                alpha = jnp.exp2(m - m_new)
                l_new = alpha * l_ref[qi, sh] + jnp.sum(p, axis=0, keepdims=True)
            m_ref[qi, sh] = m_new
            l_ref[qi, sh] = l_new
            stats.append((alpha, p.astype(ACT)))
        pvs = [_dot(vt_ref[qi - off, HEAD_DIM * sh:HEAD_DIM * (sh + 1), :], p) for (qi, sh), (_, p) in zip(work, stats)]
        for (qi, sh), (alpha, _), pv in zip(work, stats, pvs):
            acc_ref[qi, sh] = pv if off == 0 else alpha * acc_ref[qi, sh] + pv

    for off in range(n_blk):
        qis = list(range(off, n_blk))
        for lo in range(0, len(qis), MOBA_GROUP):
            tiles(off, qis[lo:lo + MOBA_GROUP])

    def finish(qi, c):
        ot = jnp.concatenate([acc_ref[qi, sh] * (1.0 / l_ref[qi, sh]) for sh in range(2)], axis=0)
        o_ref[0, rows_of(qi), :] = (ot.T * sg_ref[0, rows_of(qi), :].astype(F32)).astype(o_ref.dtype)
        return c
    lax.fori_loop(0, n_blk, finish, 0, unroll=2)


def _moba(p, tv_a):
    b, s, _ = p.shape
    blk = MOBA_BLOCK
    n_blk = s // blk
    col = lambda c: pl.BlockSpec((1, s, LANE), lambda hp, bi: (bi, 0, COL_A + c * N_PAIRS + hp))
    return pl.pallas_call(
        _moba_kernel,
        grid=(N_PAIRS, b),
        in_specs=[col(0), col(1), col(2), col(3), pl.BlockSpec(tv_a.shape, lambda hp, bi: (0, 0, 0, 0))],
        out_specs=pl.BlockSpec((1, s, LANE), lambda hp, bi: (bi, 0, hp)),
        out_shape=jax.ShapeDtypeStruct((b, s, W_BR), ACT),
        scratch_shapes=[
            pltpu.VMEM((2, n_blk, blk, blk), F32),
            pltpu.VMEM((n_blk, 2, LANE, blk), ACT),
            pltpu.VMEM((n_blk, LANE, blk), ACT),
            pltpu.VMEM((n_blk, LANE), F32),
            pltpu.VMEM((n_blk, 2, n_blk, blk), F32),
            pltpu.VMEM((n_blk, 2, 1, blk), F32),
            pltpu.VMEM((n_blk, 2, 1, blk), F32),
            pltpu.VMEM((n_blk, 2, HEAD_DIM, blk), F32),
        ],
        compiler_params=_cparams(2),
        name="moba",
    )(p, p, p, p, tv_a)


def _band_attend(items):
    scores = [[_dot(k, qs) for k, _, _, _ in blocks] for qs, blocks, _ in items]
    probs = []
    for (_, blocks, sink), sc in zip(items, scores):
        ss = [s + bias if valid is None else jnp.where(valid, s + bias, NEG)
              for s, (_, _, bias, valid) in zip(sc, blocks)]
        m = functools.reduce(jnp.maximum, [jnp.max(s, axis=0, keepdims=True) for s in ss])
        if sink is not None:
            m = jnp.maximum(m, sink)
        ps = [jnp.exp2(s - m) for s in ss]
        l = functools.reduce(operator.add, [jnp.sum(p, axis=0, keepdims=True) for p in ps])
        if sink is not None:
            l = l + jnp.exp2(sink - m)
        probs.append((m, l, [p.astype(ACT) for p in ps]))
    outs = []
    for (_, blocks, _), (m, l, ps) in zip(items, probs):
        acc = functools.reduce(operator.add, [_dot(vt, p) for (_, vt, _, _), p in zip(blocks, ps)])
        outs.append((acc * (1.0 / l), m + jnp.log2(l)))
    return outs


def _band_bias(tv_ref, head, prev_includes_window_edge):
    kr = lax.broadcasted_iota(jnp.int32, (BAND, BAND), 0)
    qc = lax.broadcasted_iota(jnp.int32, (BAND, BAND), 1)
    own, prev = [_toeplitz(tv_ref[head, which], tv_ref[head, which + 1], BAND) for which in range(2)]
    prev_mask = (qc <= kr) if prev_includes_window_edge else (qc < kr)
    return jnp.where(qc >= kr, own, NEG), jnp.where(prev_mask, prev, NEG)


def _dil_kernel(q0_ref, k0_ref, v0_ref, q1_ref, k1_ref, v1_ref, q2_ref, k2_ref, v2_ref, sg_ref, tv_ref,
                o_ref, og_ref, lg_ref, bt_ref):
    hp = pl.program_id(0)
    b = pl.program_id(1)
    s = q0_ref.shape[1]
    n_tiles = s // BAND

    @pl.when(b == 0)
    def _():
        for g in range(N_DIL):
            for sh in range(2):
                own, prev = _band_bias(tv_ref, g * N_HEADS + 2 * hp + sh, True)
                bt_ref[g, sh, 0] = own
                bt_ref[g, sh, 1] = prev

    groups = ((q0_ref, k0_ref, v0_ref), (q1_ref, k1_ref, v1_ref), (q2_ref, k2_ref, v2_ref))

    for g, (q_ref, k_ref, v_ref) in enumerate(groups):
        d = DIL_PAIRS[g][1]
        per_phase = n_tiles // d
        has_prev = per_phase > 1
        assert per_phase % TILE_GROUP == 0 or TILE_GROUP % per_phase == 0

        def body(i, c, g=g, d=d, per_phase=per_phase, has_prev=has_prev, q_ref=q_ref, k_ref=k_ref, v_ref=v_ref):
            items, dsts = [], []
            for n in range(TILE_GROUP):
                u = i * TILE_GROUP + n
                rows = pl.ds(pl.multiple_of(u * BAND, BAND), BAND)
                t = lax.rem(u, per_phase)
                r = lax.div(u, per_phase)
                qt = q_ref[0, rows, :].T
                k_own = k_ref[0, rows, :]
                vt_own = v_ref[0, rows, :].T
                if has_prev:
                    prow = pl.ds(pl.multiple_of(jnp.maximum(u - 1, 0) * BAND, BAND), BAND)
                    k_prev = k_ref[0, prow, :]
                    vt_prev = v_ref[0, prow, :].T
                    phase_start_known = TILE_GROUP % per_phase == 0
                    prev_exists = (t > 0) if (n == 0 and not phase_start_known) else None
                    if phase_start_known and n % per_phase == 0:
                        k_prev = None
                for sh in range(2):
                    hrows = slice(HEAD_DIM * sh, HEAD_DIM * (sh + 1))
                    blocks = [(k_own, vt_own[hrows, :], bt_ref[g, sh, 0], None)]
                    if has_prev and k_prev is not None:
                        blocks.append((k_prev, vt_prev[hrows, :], bt_ref[g, sh, 1], prev_exists))
                    items.append((_head_rows(qt, HEAD_DIM * sh), blocks, None))
                dsts.append(rows if d == 1 else pl.ds(t * (BAND * d) + r, BAND, stride=d))
            res = _band_attend(items)
            for n, dst in enumerate(dsts):
                (o0, lse0), (o1, lse1) = res[2 * n], res[2 * n + 1]
                og_ref.at[g][dst, :] = jnp.concatenate([o0, o1], axis=0).T
                lg_ref.at[g][dst, :] = jnp.concatenate(
                    [jnp.broadcast_to(lse0, (HEAD_DIM, BAND)), jnp.broadcast_to(lse1, (HEAD_DIM, BAND))], axis=0).T
            return c

        lax.fori_loop(0, n_tiles // TILE_GROUP, body, 0)

    def merge(i, c):
        rows = pl.ds(pl.multiple_of(i * ROWS, ROWS), ROWS)
        l0, l1, l2 = lg_ref[0, rows, :], lg_ref[1, rows, :], lg_ref[2, rows, :]
        m = jnp.maximum(l0, jnp.maximum(l1, l2))
        e0, e1, e2 = jnp.exp2(l0 - m), jnp.exp2(l1 - m), jnp.exp2(l2 - m)
        ob = (e0 * og_ref[0, rows, :] + e1 * og_ref[1, rows, :] + e2 * og_ref[2, rows, :]) / (e0 + e1 + e2)
        o_ref[0, rows, :] = (ob * sg_ref[0, rows, :].astype(F32)).astype(o_ref.dtype)
        return c

    lax.fori_loop(0, s // ROWS, merge, 0)


def _dilated(p, tv_b):
    b, s, _ = p.shape
    col = lambda c: pl.BlockSpec((1, s, LANE), lambda hp, bi: (bi, 0, c + hp))
    qkv = [col(COL_B + 3 * N_PAIRS * g + N_PAIRS * i) for g in range(N_DIL) for i in range(3)]
    return pl.pallas_call(
        _dil_kernel,
        grid=(N_PAIRS, b),
        in_specs=qkv + [col(COL_SGB), pl.BlockSpec(tv_b.shape, lambda hp, bi: (0, 0, 0, 0))],
        out_specs=pl.BlockSpec((1, s, LANE), lambda hp, bi: (bi, 0, hp)),
        out_shape=jax.ShapeDtypeStruct((b, s, W_BR), ACT),
        scratch_shapes=[
            pltpu.VMEM((N_DIL, s, LANE), F32),
            pltpu.VMEM((N_DIL, s, LANE), F32),
            pltpu.VMEM((N_DIL, 2, 2, BAND, BAND), F32),
        ],
        compiler_params=_cparams(2),
        name="dilated",
    )(*([p] * 10), tv_b)


def _swa_kernel(sink_ref, q_ref, sg_ref, k_ref, v_ref, tv_ref, o_ref, bt_ref):
    hp = pl.program_id(0)
    b = pl.program_id(1)
    n_tiles = q_ref.shape[1] // BAND
    pairs_per_kv = N_PAIRS // KV_C

    @pl.when(b == 0)
    def _():
        for sh in range(2):
            own, prev = _band_bias(tv_ref, 2 * hp + sh, False)
            bt_ref[sh, 0] = own
            bt_ref[sh, 1] = prev

    kv_lo = HEAD_DIM * (hp // pairs_per_kv)
    row = lax.broadcasted_iota(jnp.int32, (LANE, BAND), 0)
    kv_rows = (row >= kv_lo) & (row < kv_lo + HEAD_DIM)

    def body(i, c):
        items, dsts = [], []
        zero = jnp.zeros((LANE, BAND), ACT)
        for n in range(TILE_GROUP):
            t = i * TILE_GROUP + n
            rows = pl.ds(pl.multiple_of(t * BAND, BAND), BAND)
            prow = pl.ds(pl.multiple_of(jnp.maximum(t - 1, 0) * BAND, BAND), BAND)
            qt = q_ref[0, rows, :].T
            k_own = k_ref[0, rows, :]
            k_prev = k_ref[0, prow, :]
            vt_own = jnp.where(kv_rows, v_ref[0, rows, :].T, zero)
            vt_prev = jnp.where(kv_rows, v_ref[0, prow, :].T, zero)
            prev_exists = (t > 0) if n == 0 else None
            for sh in range(2):
                qh = qt[HEAD_DIM * sh:HEAD_DIM * (sh + 1), :]
                qs = jnp.where(kv_rows, jnp.concatenate([qh, qh], axis=0), zero)
                items.append((qs, [(k_own, vt_own, bt_ref[sh, 0], None), (k_prev, vt_prev, bt_ref[sh, 1], prev_exists)],
                              sink_ref[2 * hp + sh]))
            dsts.append(rows)
        res = _band_attend(items)
        for n, rows in enumerate(dsts):
            ot = jnp.concatenate([o[:HEAD_DIM, :] + o[HEAD_DIM:, :] for o, _ in res[2 * n:2 * n + 2]], axis=0).T
            o_ref[0, rows, :] = (ot * sg_ref[0, rows, :].astype(F32)).astype(o_ref.dtype)
        return c

    lax.fori_loop(0, n_tiles // TILE_GROUP, body, 0)


def _swa(p, tv_c, sinks):
    b, s, _ = p.shape
    grid_spec = pltpu.PrefetchScalarGridSpec(
        num_scalar_prefetch=1,
        grid=(N_PAIRS, b),
        in_specs=[
            pl.BlockSpec((1, s, LANE), lambda hp, bi, sk: (bi, 0, COL_C + hp)),
            pl.BlockSpec((1, s, LANE), lambda hp, bi, sk: (bi, 0, COL_C + N_PAIRS + hp)),
            pl.BlockSpec((1, s, LANE), lambda hp, bi, sk: (bi, 0, COL_C + 2 * N_PAIRS)),
            pl.BlockSpec((1, s, LANE), lambda hp, bi, sk: (bi, 0, COL_C + 2 * N_PAIRS + 1)),
            pl.BlockSpec(tv_c.shape, lambda hp, bi, sk: (0, 0, 0, 0)),
        ],
        out_specs=pl.BlockSpec((1, s, LANE), lambda hp, bi, sk: (bi, 0, hp)),
        scratch_shapes=[pltpu.VMEM((2, 2, BAND, BAND), F32)],
    )
    return pl.pallas_call(
        _swa_kernel,
        grid_spec=grid_spec,
        out_shape=jax.ShapeDtypeStruct((b, s, W_BR), ACT),
        compiler_params=_cparams(2),
        name="swa",
    )(sinks, p, p, p, p, tv_c)


def _merge_kernel(bra_ref, brb_ref, brc_ref, g_ref, wb_ref, wo_ref, x_ref, out_ref):
    merged = None
    for i, br_ref in enumerate((bra_ref, brb_ref, brc_ref)):
        term = g_ref[:, i * D_MODEL:(i + 1) * D_MODEL].astype(F32) * _dot(br_ref[...], wb_ref[i])
        merged = term if merged is None else merged + term
    out_ref[...] = x_ref[...] + _dot(merged.astype(ACT), wo_ref[...])


def _merge(bra, brb, brc, p2, w_br, w_out, x2, layer, ts=512):
    n = x2.shape[0]
    row = lambda w: pl.BlockSpec((ts, w), lambda i: (i, 0))
    return pl.pallas_call(
        _merge_kernel,
        grid=(n // ts,),
        in_specs=[row(W_BR), row(W_BR), row(W_BR), row(GATE_W),
                  pl.BlockSpec((None,) + w_br.shape[1:], lambda i: (layer, 0, 0, 0)),
                  pl.BlockSpec((None,) + w_out.shape[1:], lambda i: (layer, 0, 0)),
                  row(D_MODEL)],
        out_specs=row(D_MODEL),
        out_shape=jax.ShapeDtypeStruct(x2.shape, F32),
        compiler_params=_cparams(1),
        name="merge",
    )(bra, brb, brc, p2, w_br, w_out, x2)


def _rel_bucket(dist):
    dist = jnp.maximum(dist, 0)
    max_exact = N_BUCKETS // 2
    log_ratio = jnp.log(jnp.maximum(dist, 1).astype(F32) / max_exact) / math.log(REL_MAX_DIST / max_exact)
    large = jnp.minimum(max_exact + (log_ratio * (N_BUCKETS - max_exact)).astype(jnp.int32), N_BUCKETS - 1)
    return jnp.where(dist < max_exact, dist, large)


def _bias_by_distance(tab, dist, chunk):
    vals = tab.astype(F32)[:, _rel_bucket(dist)] * LOG2E
    vals = jnp.pad(vals, ((0, 0), (chunk, 0)))
    return vals.reshape(tab.shape[0], -1, 1, chunk)


def _gain_rows(qk_g):
    depth = qk_g.shape[0]
    scale = HEAD_DIM ** -0.5 * LOG2E
    one = jnp.ones((depth, HEAD_DIM), F32)
    tile = lambda v, n: jnp.tile(v, (1, n))
    parts = [tile(qk_g[:, 0] * scale, N_HEADS), tile(qk_g[:, 1], N_HEADS), tile(one, 2 * N_HEADS),
             tile(qk_g[:, 2] * scale, N_DIL * N_HEADS), tile(qk_g[:, 3], N_DIL * N_HEADS),
             tile(one, (N_DIL + 1) * N_HEADS),
             tile(qk_g[:, 4] * scale, N_HEADS), tile(qk_g[:, 5], KV_C), tile(one, KV_C + N_HEADS),
             tile(one, GATE_W // HEAD_DIM)]
    return jnp.concatenate(parts, axis=1)


def kernel(x, ln_g, w_in, qk_g, sinks, w_branch, w_out, rel_bias):
    b, s, d = x.shape
    depth = ln_g.shape[0]
    assert d == D_MODEL and s % DIL_PAIRS[-1][0] == 0 and w_in.shape[-1] == C_IN

    tv_a = _bias_by_distance(rel_bias[:N_HEADS], jnp.arange(s), MOBA_BLOCK)
    tv_b = jnp.concatenate(
        [_bias_by_distance(rel_bias[OFF_B + g * N_HEADS:OFF_B + (g + 1) * N_HEADS], dd * jnp.arange(2 * BAND), BAND)
         for g, (_, dd) in enumerate(DIL_PAIRS)], axis=0)
    tv_c = _bias_by_distance(rel_bias[OFF_C:OFF_C + N_HEADS], jnp.arange(2 * BAND), BAND)
    seg = jnp.arange(TN) // HEAD_DIM
    e_mat = (seg[:, None] == seg[None, :]).astype(ACT)
    src = jnp.asarray([c for c, _ in _LAYOUT], jnp.int32)
    fn = jnp.asarray([f for _, f in _LAYOUT], jnp.int32)
    gains = _gain_rows(qk_g)

    w_in_b = w_in.astype(ACT)
    w_br_b = w_branch.astype(ACT)
    w_out_b = w_out.astype(ACT)

    for l in range(depth):
        p = _project(x, ln_g[:, None, :], w_in_b, gains[l][None, :], e_mat, src, fn, l)
        bra = _moba(p, tv_a)
        brb = _dilated(p, tv_b)
        brc = _swa(p, tv_c, sinks[l] * LOG2E)
        x2 = _merge(bra.reshape(b * s, W_BR), brb.reshape(b * s, W_BR), brc.reshape(b * s, W_BR),
                    p.reshape(b * s, C_IN), w_br_b, w_out_b, x.reshape(b * s, d), l)
        x = x2.reshape(b, s, d)
    return x
```

```python
import functools
import math
import operator

import jax
import jax.numpy as jnp
from jax import lax
from jax.experimental import pallas as pl
from jax.experimental.pallas import tpu as pltpu

D_MODEL = 1024
HEAD_DIM = 64
W_BR = D_MODEL // 2
N_HEADS = W_BR // HEAD_DIM
N_PAIRS = N_HEADS // 2
MOBA_BLOCK = 256
MOBA_TOPK = 3
DIL_PAIRS = ((128, 1), (512, 4), (2048, 16))
N_DIL = len(DIL_PAIRS)
KV_C = 2
SWA_WINDOW = 128
N_BUCKETS = 32
REL_MAX_DIST = 2048
OFF_B = N_HEADS
OFF_C = N_HEADS + N_DIL * N_HEADS
EPS = 1e-6
NEG = -1e30
LOG2E = math.log2(math.e)

LANE = 128
BAND = 128
TN = 256
ROWS = 256
TILE_GROUP = 8
MOBA_GROUP = 4
VMEM_LIMIT = 56 * 1024 * 1024

ACT = jnp.bfloat16
F32 = jnp.float32

PLAIN, NORM, SILU, SIGMOID, KNORM_VPLAIN, NORM_D4, PLAIN_D4, NORM_D16, PLAIN_D16 = range(9)
_B_GROUP = lambda g, norm, plain: [(8 + 2 * g, norm), (9 + 2 * g, norm), (14 + 2 * g, norm), (15 + 2 * g, norm),
                                   (20 + 2 * g, plain), (21 + 2 * g, plain)]
_LAYOUT = ([(j, SIGMOID) for j in range(33, 45)]
           + [(0, NORM), (1, NORM), (2, NORM), (3, NORM), (4, PLAIN), (5, PLAIN), (6, SILU), (7, SILU)]
           + _B_GROUP(0, NORM, PLAIN) + _B_GROUP(1, NORM_D4, PLAIN_D4) + _B_GROUP(2, NORM_D16, PLAIN_D16)
           + [(26, SILU), (27, SILU)]
           + [(28, NORM), (29, NORM), (31, SILU), (32, SILU), (30, KNORM_VPLAIN)])
C_IN = len(_LAYOUT) * TN
PAIR = 2
_LAYOUT = _LAYOUT + [_LAYOUT[-1]] * (-len(_LAYOUT) % PAIR)
assert all(_LAYOUT[j][1] == _LAYOUT[j - j % PAIR][1] for j in range(len(_LAYOUT)))
N_STEPS = len(_LAYOUT) // PAIR
P_W = len(_LAYOUT) * TN
COL_GATE, COL_A, COL_B, COL_SGB, COL_C = 0, 24, 40, 76, 80
GATE_W = 3 * D_MODEL


def _cparams(n_axes):
    return pltpu.CompilerParams(dimension_semantics=("arbitrary",) * n_axes, vmem_limit_bytes=VMEM_LIMIT)


def _dot(a, b):
    return jnp.dot(a, b, preferred_element_type=F32)


def _sigmoid(a):
    return 0.5 * jnp.tanh(0.5 * a) + 0.5


def _proj_kernel(src_ref, fn_ref, x_ref, lng_ref, w0_ref, w1_ref, g0_ref, g1_ref, e_ref, o_ref, xn_ref, acc_ref):
    del src_ref
    w_refs, gain_refs = (w0_ref, w1_ref), (g0_ref, g1_ref)
    assert len(w_refs) == PAIR
    j = pl.program_id(1)
    s = x_ref.shape[1]
    n_chunks = s // ROWS

    @pl.when(j == 0)
    def _():
        def body(i, c):
            rows = pl.ds(pl.multiple_of(i * ROWS, ROWS), ROWS)
            xv = x_ref[0, rows, :]
            ms = jnp.mean(xv * xv, axis=-1, keepdims=True)
            xn_ref[rows, :] = (xv * lax.rsqrt(ms + EPS) * lng_ref[...]).astype(ACT)
            return c
        lax.fori_loop(0, n_chunks, body, 0)

    def head_norm(a, gain):
        ss = _dot((a * a).astype(ACT), e_ref[...])
        return a * lax.rsqrt(ss * (1.0 / HEAD_DIM) + EPS) * gain

    def plain(a, gain):
        return a

    def silu(a, gain):
        return a * _sigmoid(a)

    def sigmoid(a, gain):
        return _sigmoid(a)

    def k_norm_v_plain(a, gain):
        lane = lax.broadcasted_iota(jnp.int32, a.shape, 1)
        return jnp.where(lane < KV_C * HEAD_DIM, head_norm(a, gain), a)

    def chunk(n):
        h, i = divmod(n, n_chunks)
        return _dot(xn_ref[i * ROWS:(i + 1) * ROWS, :], w_refs[h][...])

    fn = fn_ref[j]
    lanes_per_block = TN // LANE

    def direct(code, f):
        @pl.when(fn == code)
        def _():
            pending = None
            for n in range(PAIR * n_chunks + 1):
                a = chunk(n) if n < PAIR * n_chunks else None
                if pending is not None:
                    h, i = divmod(n - 1, n_chunks)
                    o_ref[0, i * ROWS:(i + 1) * ROWS, h * TN:(h + 1) * TN] = (
                        f(pending, gain_refs[h][...]).astype(o_ref.dtype))
                pending = a

    def phase_major(code, f, d):
        @pl.when(fn == code)
        def _():
            for n in range(PAIR * n_chunks):
                h, i = divmod(n, n_chunks)
                a = chunk(n)
                for c in range(lanes_per_block):
                    acc_ref[h * lanes_per_block + c, i * ROWS:(i + 1) * ROWS, :] = a[:, c * LANE:(c + 1) * LANE]
            for h in range(PAIR):
                for t in range(s // BAND):
                    r, m0 = divmod(t * BAND, s // d)
                    a = jnp.concatenate([acc_ref.at[h * lanes_per_block + c][pl.ds(m0 * d + r, BAND, stride=d), :]
                                         for c in range(lanes_per_block)], axis=1)
                    o_ref[0, t * BAND:(t + 1) * BAND, h * TN:(h + 1) * TN] = (
                        f(a, gain_refs[h][...]).astype(o_ref.dtype))

    direct(PLAIN, plain)
    direct(NORM, head_norm)
    direct(SILU, silu)
    direct(SIGMOID, sigmoid)
    direct(KNORM_VPLAIN, k_norm_v_plain)
    phase_major(NORM_D4, head_norm, DIL_PAIRS[1][1])
    phase_major(PLAIN_D4, plain, DIL_PAIRS[1][1])
    phase_major(NORM_D16, head_norm, DIL_PAIRS[2][1])
    phase_major(PLAIN_D16, plain, DIL_PAIRS[2][1])


def _project(x, ln_g, w_in, gain_row, e_mat, src, fn, layer):
    b, s, d = x.shape
    grid_spec = pltpu.PrefetchScalarGridSpec(
        num_scalar_prefetch=2,
        grid=(b, N_STEPS),
        in_specs=[
            pl.BlockSpec((1, s, d), lambda bi, j, src_ref, fn_ref: (bi, 0, 0)),
            pl.BlockSpec((None, 1, d), lambda bi, j, src_ref, fn_ref: (layer, 0, 0)),
            pl.BlockSpec((None, d, TN), lambda bi, j, src_ref, fn_ref: (layer, 0, src_ref[PAIR * j])),
            pl.BlockSpec((None, d, TN), lambda bi, j, src_ref, fn_ref: (layer, 0, src_ref[PAIR * j + 1])),
            pl.BlockSpec((1, TN), lambda bi, j, src_ref, fn_ref: (0, src_ref[PAIR * j])),
            pl.BlockSpec((1, TN), lambda bi, j, src_ref, fn_ref: (0, src_ref[PAIR * j + 1])),
            pl.BlockSpec((TN, TN), lambda bi, j, src_ref, fn_ref: (0, 0)),
        ],
        out_specs=pl.BlockSpec((1, s, PAIR * TN), lambda bi, j, src_ref, fn_ref: (bi, 0, j)),
        scratch_shapes=[pltpu.VMEM((s, d), ACT), pltpu.VMEM((PAIR * TN // LANE, s, LANE), F32)],
    )
    return pl.pallas_call(
        _proj_kernel,
        grid_spec=grid_spec,
        out_shape=jax.ShapeDtypeStruct((b, s, P_W), ACT),
        compiler_params=_cparams(2),
        name="proj",
    )(src, fn, x, ln_g, w_in, w_in, gain_row, gain_row, e_mat)


def _toeplitz(lo, hi, n):
    x = jnp.concatenate([jnp.broadcast_to(lo, (n, n)), jnp.broadcast_to(hi, (n, n))], axis=1)
    return pltpu.roll(x, 0, 1, stride=1, stride_axis=0)[:, n:]


def _head_rows(qt, lo):
    row = lax.broadcasted_iota(jnp.int32, qt.shape, 0)
    return jnp.where((row >= lo) & (row < lo + HEAD_DIM), qt, jnp.zeros_like(qt))


def _moba_kernel(q_ref, k_ref, v_ref, sg_ref, tv_ref, o_ref,
                 bt_ref, qts_ref, vt_ref, km_ref, sel_ref, m_ref, l_ref, acc_ref):
    hp = pl.program_id(0)
    b = pl.program_id(1)
    blk = MOBA_BLOCK
    n_blk = k_ref.shape[1] // blk

    def rows_of(i):
        return pl.ds(i * blk if isinstance(i, int) else pl.multiple_of(i * blk, blk), blk)

    @pl.when(b == 0)
    def _():
        kr = lax.broadcasted_iota(jnp.int32, (blk, blk), 0)
        qc = lax.broadcasted_iota(jnp.int32, (blk, blk), 1)
        for sh in range(2):
            def body(o, c, sh=sh):
                t = _toeplitz(tv_ref[2 * hp + sh, o], tv_ref[2 * hp + sh, o + 1], blk)
                bt_ref[sh, o] = jnp.where((qc >= kr) | (o > 0), t, NEG)
                return c
            lax.fori_loop(0, n_blk, body, 0)

    def prep(i, c):
        vt_ref[i] = v_ref[0, rows_of(i), :].T
        km_ref[pl.ds(i, 1), :] = jnp.mean(k_ref[0, rows_of(i), :].astype(F32), axis=0, keepdims=True)
        qt = q_ref[0, rows_of(i), :].T
        for sh in range(2):
            qts_ref[i, sh] = _head_rows(qt, HEAD_DIM * sh)
        return c
    lax.fori_loop(0, n_blk, prep, 0, unroll=2)

    km = km_ref[...]
    km0 = km.astype(ACT)
    r1 = km - km0.astype(F32)
    km1 = r1.astype(ACT)
    km2 = (r1 - km1.astype(F32)).astype(ACT)
    blk_row = lax.broadcasted_iota(jnp.int32, (n_blk, blk), 0)

    def select(qi, c):
        for sh in range(2):
            qs = qts_ref[qi, sh]
            g = _dot(km0, qs) + _dot(km1, qs) + _dot(km2, qs)
            g = jnp.where(blk_row < qi, g, NEG)
            sel_rows = []
            for jb in range(n_blk):
                gj = g[jb:jb + 1, :]
                beats = (g > gj) | ((g == gj) & (blk_row < jb))
                cnt = jnp.sum(beats.astype(F32), axis=0, keepdims=True)
                sel_rows.append(jnp.where((cnt < MOBA_TOPK) & (jb < qi), 1.0, 0.0))
            sel_ref[qi, sh] = jnp.concatenate(sel_rows, axis=0)
        return c
    lax.fori_loop(0, n_blk, select, 0, unroll=2)

    def tiles(off, qis):
        k_blks = [k_ref[0, rows_of(qi - off), :] for qi in qis]
        scores = [_dot(k_blk, qts_ref[qi, sh]) for qi, k_blk in zip(qis, k_blks) for sh in range(2)]
        work = [(qi, sh) for qi in qis for sh in range(2)]
        stats = []
        for (qi, sh), st in zip(work, scores):
            st = st + bt_ref[sh, off]
            tile_max = jnp.max(st, axis=0, keepdims=True)
            if off == 0:
                m_new, alpha = tile_max, None
                p = jnp.exp2(st - m_new)
                l_new = jnp.sum(p, axis=0, keepdims=True)
            else:
                sel = sel_ref[qi, sh, pl.ds(qi - off, 1), :] > 0.5
                m = m_ref[qi, sh]
                m_new = jnp.where(sel, jnp.maximum(m, tile_max), m)
                p = jnp.exp2(st - jnp.where(sel, m_new, -NEG))
                alpha = jnp.exp2(m - m_new)
                l_new = alpha * l_ref[qi, sh] + jnp.sum(p, axis=0, keepdims=True)
            m_ref[qi, sh] = m_new
            l_ref[qi, sh] = l_new
            stats.append((alpha, p.astype(ACT)))
        pvs = [_dot(vt_ref[qi - off, HEAD_DIM * sh:HEAD_DIM * (sh + 1), :], p) for (qi, sh), (_, p) in zip(work, stats)]
        for (qi, sh), (alpha, _), pv in zip(work, stats, pvs):
            acc_ref[qi, sh] = pv if off == 0 else alpha * acc_ref[qi, sh] + pv

    for off in range(n_blk):
        qis = list(range(off, n_blk))
        for lo in range(0, len(qis), MOBA_GROUP):
            tiles(off, qis[lo:lo + MOBA_GROUP])

    def finish(qi, c):
        ot = jnp.concatenate([acc_ref[qi, sh] * (1.0 / l_ref[qi, sh]) for sh in range(2)], axis=0)
        o_ref[0, rows_of(qi), :] = (ot.T * sg_ref[0, rows_of(qi), :].astype(F32)).astype(o_ref.dtype)
        return c
    lax.fori_loop(0, n_blk, finish, 0, unroll=2)


def _moba(p, tv_a):
    b, s, _ = p.shape
    blk = MOBA_BLOCK
    n_blk = s // blk
    col = lambda c: pl.BlockSpec((1, s, LANE), lambda hp, bi: (bi, 0, COL_A + c * N_PAIRS + hp))
    return pl.pallas_call(
        _moba_kernel,
        grid=(N_PAIRS, b),
        in_specs=[col(0), col(1), col(2), col(3), pl.BlockSpec(tv_a.shape, lambda hp, bi: (0, 0, 0, 0))],
        out_specs=pl.BlockSpec((1, s, LANE), lambda hp, bi: (bi, 0, hp)),
        out_shape=jax.ShapeDtypeStruct((b, s, W_BR), ACT),
        scratch_shapes=[
            pltpu.VMEM((2, n_blk, blk, blk), F32),
            pltpu.VMEM((n_blk, 2, LANE, blk), ACT),
            pltpu.VMEM((n_blk, LANE, blk), ACT),
            pltpu.VMEM((n_blk, LANE), F32),
            pltpu.VMEM((n_blk, 2, n_blk, blk), F32),
            pltpu.VMEM((n_blk, 2, 1, blk), F32),
            pltpu.VMEM((n_blk, 2, 1, blk), F32),
            pltpu.VMEM((n_blk, 2, HEAD_DIM, blk), F32),
        ],
        compiler_params=_cparams(2),
        name="moba",
    )(p, p, p, p, tv_a)


def _band_attend(items):
    scores = [[_dot(k, qs) for k, _, _, _ in blocks] for qs, blocks, _ in items]
    probs = []
    for (_, blocks, sink), sc in zip(items, scores):
        ss = [s + bias if valid is None else jnp.where(valid, s + bias, NEG)
              for s, (_, _, bias, valid) in zip(sc, blocks)]
        m = functools.reduce(jnp.maximum, [jnp.max(s, axis=0, keepdims=True) for s in ss])
        if sink is not None:
            m = jnp.maximum(m, sink)
        ps = [jnp.exp2(s - m) for s in ss]
        l = functools.reduce(operator.add, [jnp.sum(p, axis=0, keepdims=True) for p in ps])
        if sink is not None:
            l = l + jnp.exp2(sink - m)
        probs.append((m, l, [p.astype(ACT) for p in ps]))
    outs = []
    for (_, blocks, _), (m, l, ps) in zip(items, probs):
        acc = functools.reduce(operator.add, [_dot(vt, p) for (_, vt, _, _), p in zip(blocks, ps)])
        outs.append((acc * (1.0 / l), m + jnp.log2(l)))
    return outs


def _band_bias(tv_ref, head, prev_includes_window_edge):
    kr = lax.broadcasted_iota(jnp.int32, (BAND, BAND), 0)
    qc = lax.broadcasted_iota(jnp.int32, (BAND, BAND), 1)
    own, prev = [_toeplitz(tv_ref[head, which], tv_ref[head, which + 1], BAND) for which in range(2)]
    prev_mask = (qc <= kr) if prev_includes_window_edge else (qc < kr)
    return jnp.where(qc >= kr, own, NEG), jnp.where(prev_mask, prev, NEG)


def _dil_kernel(q0_ref, k0_ref, v0_ref, q1_ref, k1_ref, v1_ref, q2_ref, k2_ref, v2_ref, sg_ref, tv_ref,
                o_ref, og_ref, lg_ref, bt_ref):
    hp = pl.program_id(0)
    b = pl.program_id(1)
    s = q0_ref.shape[1]
    n_tiles = s // BAND

    @pl.when(b == 0)
    def _():
        for g in range(N_DIL):
            for sh in range(2):
                own, prev = _band_bias(tv_ref, g * N_HEADS + 2 * hp + sh, True)
                bt_ref[g, sh, 0] = own
                bt_ref[g, sh, 1] = prev

    groups = ((q0_ref, k0_ref, v0_ref), (q1_ref, k1_ref, v1_ref), (q2_ref, k2_ref, v2_ref))

    for g, (q_ref, k_ref, v_ref) in enumerate(groups):
        d = DIL_PAIRS[g][1]
        per_phase = n_tiles // d
        has_prev = per_phase > 1
        assert per_phase % TILE_GROUP == 0 or TILE_GROUP % per_phase == 0

        def body(i, c, g=g, d=d, per_phase=per_phase, has_prev=has_prev, q_ref=q_ref, k_ref=k_ref, v_ref=v_ref):
            items, dsts = [], []
            k_own = vt_own = None
            for n in range(TILE_GROUP):
                u = i * TILE_GROUP + n
                rows = pl.ds(pl.multiple_of(u * BAND, BAND), BAND)
                t = lax.rem(u, per_phase)
                r = lax.div(u, per_phase)
                qt = q_ref[0, rows, :].T
                k_prev, vt_prev = k_own, vt_own
                k_own = k_ref[0, rows, :]
                vt_own = v_ref[0, rows, :].T
                if has_prev:
                    phase_start_known = TILE_GROUP % per_phase == 0
                    prev_exists = (t > 0) if (n == 0 and not phase_start_known) else None
                    if phase_start_known and n % per_phase == 0:
                        k_prev = None
                    elif n == 0:
                        prow = pl.ds(pl.multiple_of(jnp.maximum(u - 1, 0) * BAND, BAND), BAND)
                        k_prev = k_ref[0, prow, :]
                        vt_prev = v_ref[0, prow, :].T
                for sh in range(2):
                    hrows = slice(HEAD_DIM * sh, HEAD_DIM * (sh + 1))
                    blocks = [(k_own, vt_own[hrows, :], bt_ref[g, sh, 0], None)]
                    if has_prev and k_prev is not None:
                        blocks.append((k_prev, vt_prev[hrows, :], bt_ref[g, sh, 1], prev_exists))
                    items.append((_head_rows(qt, HEAD_DIM * sh), blocks, None))
                dsts.append(rows if d == 1 else pl.ds(t * (BAND * d) + r, BAND, stride=d))
            res = _band_attend(items)
            for n, dst in enumerate(dsts):
                (o0, lse0), (o1, lse1) = res[2 * n], res[2 * n + 1]
                og_ref.at[g][dst, :] = jnp.concatenate([o0, o1], axis=0).T
                lg_ref.at[g][dst, :] = jnp.concatenate(
                    [jnp.broadcast_to(lse0, (HEAD_DIM, BAND)), jnp.broadcast_to(lse1, (HEAD_DIM, BAND))], axis=0).T
            return c

        lax.fori_loop(0, n_tiles // TILE_GROUP, body, 0)

    def merge(i, c):
        rows = pl.ds(pl.multiple_of(i * ROWS, ROWS), ROWS)
        l0, l1, l2 = lg_ref[0, rows, :], lg_ref[1, rows, :], lg_ref[2, rows, :]
        m = jnp.maximum(l0, jnp.maximum(l1, l2))
        e0, e1, e2 = jnp.exp2(l0 - m), jnp.exp2(l1 - m), jnp.exp2(l2 - m)
        ob = (e0 * og_ref[0, rows, :] + e1 * og_ref[1, rows, :] + e2 * og_ref[2, rows, :]) / (e0 + e1 + e2)
        o_ref[0, rows, :] = (ob * sg_ref[0, rows, :].astype(F32)).astype(o_ref.dtype)
        return c

    lax.fori_loop(0, s // ROWS, merge, 0)


def _dilated(p, tv_b):
    b, s, _ = p.shape
    col = lambda c: pl.BlockSpec((1, s, LANE), lambda hp, bi: (bi, 0, c + hp))
    qkv = [col(COL_B + 3 * N_PAIRS * g + N_PAIRS * i) for g in range(N_DIL) for i in range(3)]
    return pl.pallas_call(
        _dil_kernel,
        grid=(N_PAIRS, b),
        in_specs=qkv + [col(COL_SGB), pl.BlockSpec(tv_b.shape, lambda hp, bi: (0, 0, 0, 0))],
        out_specs=pl.BlockSpec((1, s, LANE), lambda hp, bi: (bi, 0, hp)),
        out_shape=jax.ShapeDtypeStruct((b, s, W_BR), ACT),
        scratch_shapes=[
            pltpu.VMEM((N_DIL, s, LANE), F32),
            pltpu.VMEM((N_DIL, s, LANE), F32),
            pltpu.VMEM((N_DIL, 2, 2, BAND, BAND), F32),
        ],
        compiler_params=_cparams(2),
        name="dilated",
    )(*([p] * 10), tv_b)


def _swa_kernel(sink_ref, q_ref, sg_ref, k_ref, v_ref, tv_ref, o_ref, bt_ref):
    hp = pl.program_id(0)
    b = pl.program_id(1)
    n_tiles = q_ref.shape[1] // BAND
    pairs_per_kv = N_PAIRS // KV_C

    @pl.when(b == 0)
    def _():
        for sh in range(2):
            own, prev = _band_bias(tv_ref, 2 * hp + sh, False)
            bt_ref[sh, 0] = own
            bt_ref[sh, 1] = prev

    kv_lo = HEAD_DIM * (hp // pairs_per_kv)
    row = lax.broadcasted_iota(jnp.int32, (LANE, BAND), 0)
    kv_rows = (row >= kv_lo) & (row < kv_lo + HEAD_DIM)

    def body(i, c):
        items, dsts = [], []
        zero = jnp.zeros((LANE, BAND), ACT)
        k_own = vt_own = None
        for n in range(TILE_GROUP):
            t = i * TILE_GROUP + n
            rows = pl.ds(pl.multiple_of(t * BAND, BAND), BAND)
            qt = q_ref[0, rows, :].T
            k_prev, vt_prev = k_own, vt_own
            k_own = k_ref[0, rows, :]
            vt_own = jnp.where(kv_rows, v_ref[0, rows, :].T, zero)
            if n == 0:
                prow = pl.ds(pl.multiple_of(jnp.maximum(t - 1, 0) * BAND, BAND), BAND)
                k_prev = k_ref[0, prow, :]
                vt_prev = jnp.where(kv_rows, v_ref[0, prow, :].T, zero)
            prev_exists = (t > 0) if n == 0 else None
            for sh in range(2):
                qh = qt[HEAD_DIM * sh:HEAD_DIM * (sh + 1), :]
                qs = jnp.where(kv_rows, jnp.concatenate([qh, qh], axis=0), zero)
                items.append((qs, [(k_own, vt_own, bt_ref[sh, 0], None), (k_prev, vt_prev, bt_ref[sh, 1], prev_exists)],
                              sink_ref[2 * hp + sh]))
            dsts.append(rows)
        res = _band_attend(items)
        for n, rows in enumerate(dsts):
            ot = jnp.concatenate([o[:HEAD_DIM, :] + o[HEAD_DIM:, :] for o, _ in res[2 * n:2 * n + 2]], axis=0).T
            o_ref[0, rows, :] = (ot * sg_ref[0, rows, :].astype(F32)).astype(o_ref.dtype)
        return c

    lax.fori_loop(0, n_tiles // TILE_GROUP, body, 0)


def _swa(p, tv_c, sinks):
    b, s, _ = p.shape
    grid_spec = pltpu.PrefetchScalarGridSpec(
        num_scalar_prefetch=1,
        grid=(N_PAIRS, b),
        in_specs=[
            pl.BlockSpec((1, s, LANE), lambda hp, bi, sk: (bi, 0, COL_C + hp)),
            pl.BlockSpec((1, s, LANE), lambda hp, bi, sk: (bi, 0, COL_C + N_PAIRS + hp)),
            pl.BlockSpec((1, s, LANE), lambda hp, bi, sk: (bi, 0, COL_C + 2 * N_PAIRS)),
            pl.BlockSpec((1, s, LANE), lambda hp, bi, sk: (bi, 0, COL_C + 2 * N_PAIRS + 1)),
            pl.BlockSpec(tv_c.shape, lambda hp, bi, sk: (0, 0, 0, 0)),
        ],
        out_specs=pl.BlockSpec((1, s, LANE), lambda hp, bi, sk: (bi, 0, hp)),
        scratch_shapes=[pltpu.VMEM((2, 2, BAND, BAND), F32)],
    )
    return pl.pallas_call(
        _swa_kernel,
        grid_spec=grid_spec,
        out_shape=jax.ShapeDtypeStruct((b, s, W_BR), ACT),
        compiler_params=_cparams(2),
        name="swa",
    )(sinks, p, p, p, p, tv_c)


def _merge_kernel(bra_ref, brb_ref, brc_ref, g_ref, wb_ref, wo_ref, x_ref, out_ref):
    merged = None
    for i, br_ref in enumerate((bra_ref, brb_ref, brc_ref)):
        term = g_ref[:, i * D_MODEL:(i + 1) * D_MODEL].astype(F32) * _dot(br_ref[...], wb_ref[i])
        merged = term if merged is None else merged + term
    out_ref[...] = x_ref[...] + _dot(merged.astype(ACT), wo_ref[...])


def _merge(bra, brb, brc, p2, w_br, w_out, x2, layer, ts=512):
    n = x2.shape[0]
    row = lambda w: pl.BlockSpec((ts, w), lambda i: (i, 0))
    return pl.pallas_call(
        _merge_kernel,
        grid=(n // ts,),
        in_specs=[row(W_BR), row(W_BR), row(W_BR), row(GATE_W),
                  pl.BlockSpec((None,) + w_br.shape[1:], lambda i: (layer, 0, 0, 0)),
                  pl.BlockSpec((None,) + w_out.shape[1:], lambda i: (layer, 0, 0)),
                  row(D_MODEL)],
        out_specs=row(D_MODEL),
        out_shape=jax.ShapeDtypeStruct(x2.shape, F32),
        compiler_params=_cparams(1),
        name="merge",
    )(bra, brb, brc, p2, w_br, w_out, x2)


def _rel_bucket(dist):
    dist = jnp.maximum(dist, 0)
    max_exact = N_BUCKETS // 2
    log_ratio = jnp.log(jnp.maximum(dist, 1).astype(F32) / max_exact) / math.log(REL_MAX_DIST / max_exact)
    large = jnp.minimum(max_exact + (log_ratio * (N_BUCKETS - max_exact)).astype(jnp.int32), N_BUCKETS - 1)
    return jnp.where(dist < max_exact, dist, large)


def _bias_by_distance(tab, dist, chunk):
    vals = tab.astype(F32)[:, _rel_bucket(dist)] * LOG2E
    vals = jnp.pad(vals, ((0, 0), (chunk, 0)))
    return vals.reshape(tab.shape[0], -1, 1, chunk)


def _gain_rows(qk_g):
    depth = qk_g.shape[0]
    scale = HEAD_DIM ** -0.5 * LOG2E
    one = jnp.ones((depth, HEAD_DIM), F32)
    tile = lambda v, n: jnp.tile(v, (1, n))
    parts = [tile(qk_g[:, 0] * scale, N_HEADS), tile(qk_g[:, 1], N_HEADS), tile(one, 2 * N_HEADS),
             tile(qk_g[:, 2] * scale, N_DIL * N_HEADS), tile(qk_g[:, 3], N_DIL * N_HEADS),
             tile(one, (N_DIL + 1) * N_HEADS),
             tile(qk_g[:, 4] * scale, N_HEADS), tile(qk_g[:, 5], KV_C), tile(one, KV_C + N_HEADS),
             tile(one, GATE_W // HEAD_DIM)]
    return jnp.concatenate(parts, axis=1)


def kernel(x, ln_g, w_in, qk_g, sinks, w_branch, w_out, rel_bias):
    b, s, d = x.shape
    depth = ln_g.shape[0]
    assert d == D_MODEL and s % DIL_PAIRS[-1][0] == 0 and w_in.shape[-1] == C_IN

    tv_a = _bias_by_distance(rel_bias[:N_HEADS], jnp.arange(s), MOBA_BLOCK)
    tv_b = jnp.concatenate(
        [_bias_by_distance(rel_bias[OFF_B + g * N_HEADS:OFF_B + (g + 1) * N_HEADS], dd * jnp.arange(2 * BAND), BAND)
         for g, (_, dd) in enumerate(DIL_PAIRS)], axis=0)
    tv_c = _bias_by_distance(rel_bias[OFF_C:OFF_C + N_HEADS], jnp.arange(2 * BAND), BAND)
    seg = jnp.arange(TN) // HEAD_DIM
    e_mat = (seg[:, None] == seg[None, :]).astype(ACT)
    src = jnp.asarray([c for c, _ in _LAYOUT], jnp.int32)
    fn = jnp.asarray([f for _, f in _LAYOUT[::PAIR]], jnp.int32)
    gains = _gain_rows(qk_g)

    w_in_b = w_in.astype(ACT)
    w_br_b = w_branch.astype(ACT)
    w_out_b = w_out.astype(ACT)

    for l in range(depth):
        p = _project(x, ln_g[:, None, :], w_in_b, gains[l][None, :], e_mat, src, fn, l)
        bra = _moba(p, tv_a)
        brb = _dilated(p, tv_b)
        brc = _swa(p, tv_c, sinks[l] * LOG2E)
        x2 = _merge(bra.reshape(b * s, W_BR), brb.reshape(b * s, W_BR), brc.reshape(b * s, W_BR),
                    p.reshape(b * s, P_W), w_br_b, w_out_b, x.reshape(b * s, d), l)
        x = x2.reshape(b, s, d)
    return x
```

```python
import functools
import math
import operator

import jax
import jax.numpy as jnp
from jax import lax
from jax.experimental import pallas as pl
from jax.experimental.pallas import tpu as pltpu

D_MODEL = 1024
HEAD_DIM = 64
W_BR = D_MODEL // 2
N_HEADS = W_BR // HEAD_DIM
N_PAIRS = N_HEADS // 2
MOBA_BLOCK = 256
MOBA_TOPK = 3
DIL_PAIRS = ((128, 1), (512, 4), (2048, 16))
N_DIL = len(DIL_PAIRS)
KV_C = 2
SWA_WINDOW = 128
N_BUCKETS = 32
REL_MAX_DIST = 2048
OFF_B = N_HEADS
OFF_C = N_HEADS + N_DIL * N_HEADS
EPS = 1e-6
NEG = -1e30
LOG2E = math.log2(math.e)

LANE = 128
BAND = 128
TN = 256
ROWS = 256
TILE_GROUP = 16
MOBA_GROUP = 8
ONES_ROWS = 16
VMEM_LIMIT = 56 * 1024 * 1024

ACT = jnp.bfloat16
F32 = jnp.float32

PLAIN, NORM, SILU, SIGMOID, KNORM_VPLAIN, NORM_D4, PLAIN_D4, NORM_D16, PLAIN_D16 = range(9)
_B_GROUP = lambda g, norm, plain: [(8 + 2 * g, norm), (9 + 2 * g, norm), (14 + 2 * g, norm), (15 + 2 * g, norm),
                                   (20 + 2 * g, plain), (21 + 2 * g, plain)]
_LAYOUT = ([(j, SIGMOID) for j in range(33, 45)]
           + [(0, NORM), (1, NORM), (2, NORM), (3, NORM), (4, PLAIN), (5, PLAIN), (6, SILU), (7, SILU)]
           + _B_GROUP(0, NORM, PLAIN) + _B_GROUP(1, NORM_D4, PLAIN_D4) + _B_GROUP(2, NORM_D16, PLAIN_D16)
           + [(26, SILU), (27, SILU)]
           + [(28, NORM), (29, NORM), (31, SILU), (32, SILU), (30, KNORM_VPLAIN)])
C_IN = len(_LAYOUT) * TN
PAIR = 2
_LAYOUT = _LAYOUT + [_LAYOUT[-1]] * (-len(_LAYOUT) % PAIR)
assert all(_LAYOUT[j][1] == _LAYOUT[j - j % PAIR][1] for j in range(len(_LAYOUT)))
N_STEPS = len(_LAYOUT) // PAIR
P_W = len(_LAYOUT) * TN
COL_GATE, COL_A, COL_B, COL_SGB, COL_C = 0, 24, 40, 76, 80
GATE_W = 3 * D_MODEL


def _cparams(n_axes):
    return pltpu.CompilerParams(dimension_semantics=("arbitrary",) * n_axes, vmem_limit_bytes=VMEM_LIMIT)


def _dot(a, b):
    return jnp.dot(a, b, preferred_element_type=F32)


def _sigmoid(a):
    return 0.5 * jnp.tanh(0.5 * a) + 0.5


def _proj_kernel(src_ref, fn_ref, x_ref, lng_ref, w0_ref, w1_ref, g0_ref, g1_ref, e_ref, o_ref, xn_ref, acc_ref):
    del src_ref
    w_refs, gain_refs = (w0_ref, w1_ref), (g0_ref, g1_ref)
    assert len(w_refs) == PAIR
    j = pl.program_id(1)
    s = x_ref.shape[1]
    n_chunks = s // ROWS

    @pl.when(j == 0)
    def _():
        def body(i, c):
            rows = pl.ds(pl.multiple_of(i * ROWS, ROWS), ROWS)
            xv = x_ref[0, rows, :]
            ms = jnp.mean(xv * xv, axis=-1, keepdims=True)
            xn_ref[rows, :] = (xv * lax.rsqrt(ms + EPS) * lng_ref[...]).astype(ACT)
            return c
        lax.fori_loop(0, n_chunks, body, 0)

    def head_norm(a, gain):
        ss = _dot((a * a).astype(ACT), e_ref[...])
        return a * lax.rsqrt(ss * (1.0 / HEAD_DIM) + EPS) * gain

    def plain(a, gain):
        return a

    def silu(a, gain):
        return a * _sigmoid(a)

    def sigmoid(a, gain):
        return _sigmoid(a)

    def k_norm_v_plain(a, gain):
        lane = lax.broadcasted_iota(jnp.int32, a.shape, 1)
        return jnp.where(lane < KV_C * HEAD_DIM, head_norm(a, gain), a)

    def chunk(n):
        h, i = divmod(n, n_chunks)
        return _dot(xn_ref[i * ROWS:(i + 1) * ROWS, :], w_refs[h][...])

    fn = fn_ref[j]
    lanes_per_block = TN // LANE

    def direct(code, f):
        @pl.when(fn == code)
        def _():
            pending = None
            for n in range(PAIR * n_chunks + 1):
                a = chunk(n) if n < PAIR * n_chunks else None
                if pending is not None:
                    h, i = divmod(n - 1, n_chunks)
                    o_ref[0, i * ROWS:(i + 1) * ROWS, h * TN:(h + 1) * TN] = (
                        f(pending, gain_refs[h][...]).astype(o_ref.dtype))
                pending = a

    def phase_major(code, f, d):
        @pl.when(fn == code)
        def _():
            for n in range(PAIR * n_chunks):
                h, i = divmod(n, n_chunks)
                a = chunk(n)
                for c in range(lanes_per_block):
                    acc_ref[h * lanes_per_block + c, i * ROWS:(i + 1) * ROWS, :] = a[:, c * LANE:(c + 1) * LANE]
            for h in range(PAIR):
                for t in range(s // BAND):
                    r, m0 = divmod(t * BAND, s // d)
                    a = jnp.concatenate([acc_ref.at[h * lanes_per_block + c][pl.ds(m0 * d + r, BAND, stride=d), :]
                                         for c in range(lanes_per_block)], axis=1)
                    o_ref[0, t * BAND:(t + 1) * BAND, h * TN:(h + 1) * TN] = (
                        f(a, gain_refs[h][...]).astype(o_ref.dtype))

    direct(PLAIN, plain)
    direct(NORM, head_norm)
    direct(SILU, silu)
    direct(SIGMOID, sigmoid)
    direct(KNORM_VPLAIN, k_norm_v_plain)
    phase_major(NORM_D4, head_norm, DIL_PAIRS[1][1])
    phase_major(PLAIN_D4, plain, DIL_PAIRS[1][1])
    phase_major(NORM_D16, head_norm, DIL_PAIRS[2][1])
    phase_major(PLAIN_D16, plain, DIL_PAIRS[2][1])


def _project(x, ln_g, w_in, gain_row, e_mat, src, fn, layer):
    b, s, d = x.shape
    grid_spec = pltpu.PrefetchScalarGridSpec(
        num_scalar_prefetch=2,
        grid=(b, N_STEPS),
        in_specs=[
            pl.BlockSpec((1, s, d), lambda bi, j, src_ref, fn_ref: (bi, 0, 0)),
            pl.BlockSpec((None, 1, d), lambda bi, j, src_ref, fn_ref: (layer, 0, 0)),
            pl.BlockSpec((None, d, TN), lambda bi, j, src_ref, fn_ref: (layer, 0, src_ref[PAIR * j])),
            pl.BlockSpec((None, d, TN), lambda bi, j, src_ref, fn_ref: (layer, 0, src_ref[PAIR * j + 1])),
            pl.BlockSpec((1, TN), lambda bi, j, src_ref, fn_ref: (0, src_ref[PAIR * j])),
            pl.BlockSpec((1, TN), lambda bi, j, src_ref, fn_ref: (0, src_ref[PAIR * j + 1])),
            pl.BlockSpec((TN, TN), lambda bi, j, src_ref, fn_ref: (0, 0)),
        ],
        out_specs=pl.BlockSpec((1, s, PAIR * TN), lambda bi, j, src_ref, fn_ref: (bi, 0, j)),
        scratch_shapes=[pltpu.VMEM((s, d), ACT), pltpu.VMEM((PAIR * TN // LANE, s, LANE), F32)],
    )
    return pl.pallas_call(
        _proj_kernel,
        grid_spec=grid_spec,
        out_shape=jax.ShapeDtypeStruct((b, s, P_W), ACT),
        compiler_params=_cparams(2),
        name="proj",
    )(src, fn, x, ln_g, w_in, w_in, gain_row, gain_row, e_mat)


def _toeplitz(lo, hi, n):
    x = jnp.concatenate([jnp.broadcast_to(lo, (n, n)), jnp.broadcast_to(hi, (n, n))], axis=1)
    return pltpu.roll(x, 0, 1, stride=1, stride_axis=0)[:, n:]


def _head_rows(qt, lo):
    row = lax.broadcasted_iota(jnp.int32, qt.shape, 0)
    return jnp.where((row >= lo) & (row < lo + HEAD_DIM), qt, jnp.zeros_like(qt))


def _moba_kernel(q_ref, k_ref, v_ref, sg_ref, tv_ref, o_ref,
                 bt_ref, qts_ref, vt_ref, km_ref, sel_ref, m_ref, acc_ref):
    hp = pl.program_id(0)
    b = pl.program_id(1)
    blk = MOBA_BLOCK
    n_blk = k_ref.shape[1] // blk

    def rows_of(i):
        return pl.ds(i * blk if isinstance(i, int) else pl.multiple_of(i * blk, blk), blk)

    @pl.when(b == 0)
    def _():
        kr = lax.broadcasted_iota(jnp.int32, (blk, blk), 0)
        qc = lax.broadcasted_iota(jnp.int32, (blk, blk), 1)
        for sh in range(2):
            def body(o, c, sh=sh):
                t = _toeplitz(tv_ref[2 * hp + sh, o], tv_ref[2 * hp + sh, o + 1], blk)
                bt_ref[sh, o] = jnp.where((qc >= kr) | (o > 0), t, NEG)
                return c
            lax.fori_loop(0, n_blk, body, 0)

    def prep(i, c):
        vt = v_ref[0, rows_of(i), :].T
        for sh in range(2):
            vt_ref[i, sh] = jnp.concatenate(
                [vt[HEAD_DIM * sh:HEAD_DIM * (sh + 1), :], jnp.ones((ONES_ROWS, blk), ACT)], axis=0)
        km_ref[pl.ds(i, 1), :] = jnp.mean(k_ref[0, rows_of(i), :].astype(F32), axis=0, keepdims=True)
        qt = q_ref[0, rows_of(i), :].T
        for sh in range(2):
            qts_ref[i, sh] = _head_rows(qt, HEAD_DIM * sh)
        return c
    lax.fori_loop(0, n_blk, prep, 0, unroll=2)

    km = km_ref[...]
    km0 = km.astype(ACT)
    r1 = km - km0.astype(F32)
    km1 = r1.astype(ACT)
    km2 = (r1 - km1.astype(F32)).astype(ACT)
    blk_row = lax.broadcasted_iota(jnp.int32, (n_blk, blk), 0)

    def select(qi, c):
        for sh in range(2):
            qs = qts_ref[qi, sh]
            g = _dot(km0, qs) + _dot(km1, qs) + _dot(km2, qs)
            g = jnp.where(blk_row < qi, g, NEG)
            sel_rows = []
            for jb in range(n_blk):
                gj = g[jb:jb + 1, :]
                beats = (g > gj) | ((g == gj) & (blk_row < jb))
                cnt = jnp.sum(beats.astype(F32), axis=0, keepdims=True)
                sel_rows.append(jnp.where((cnt < MOBA_TOPK) & (jb < qi), 1.0, 0.0))
            sel_ref[qi, sh] = jnp.concatenate(sel_rows, axis=0)
        return c
    lax.fori_loop(0, n_blk, select, 0, unroll=2)

    def tiles(off, qis):
        k_blks = [k_ref[0, rows_of(qi - off), :] for qi in qis]
        scores = [_dot(k_blk, qts_ref[qi, sh]) for qi, k_blk in zip(qis, k_blks) for sh in range(2)]
        work = [(qi, sh) for qi in qis for sh in range(2)]
        stats = []
        for (qi, sh), st in zip(work, scores):
            st = st + bt_ref[sh, off]
            tile_max = jnp.max(st, axis=0, keepdims=True)
            if off == 0:
                m_new, alpha = tile_max, None
                p = jnp.exp2(st - m_new)
            else:
                sel = sel_ref[qi, sh, pl.ds(qi - off, 1), :] > 0.5
                m = m_ref[qi, sh]
                m_new = jnp.where(sel, jnp.maximum(m, tile_max), m)
                p = jnp.exp2(st - jnp.where(sel, m_new, -NEG))
                alpha = jnp.exp2(m - m_new)
            m_ref[qi, sh] = m_new
            stats.append((alpha, p.astype(ACT)))
        pvs = [_dot(vt_ref[qi - off, sh], p) for (qi, sh), (_, p) in zip(work, stats)]
        for (qi, sh), (alpha, _), pv in zip(work, stats, pvs):
            acc_ref[qi, sh] = pv if off == 0 else alpha * acc_ref[qi, sh] + pv

    for off in range(n_blk):
        qis = list(range(off, n_blk))
        for lo in range(0, len(qis), MOBA_GROUP):
            tiles(off, qis[lo:lo + MOBA_GROUP])

    def finish(qi, c):
        ot = jnp.concatenate([acc_ref[qi, sh, :HEAD_DIM, :] * (1.0 / acc_ref[qi, sh, HEAD_DIM:HEAD_DIM + 1, :])
                              for sh in range(2)], axis=0)
        o_ref[0, rows_of(qi), :] = (ot.T * sg_ref[0, rows_of(qi), :].astype(F32)).astype(o_ref.dtype)
        return c
    lax.fori_loop(0, n_blk, finish, 0, unroll=2)


def _moba(p, tv_a):
    b, s, _ = p.shape
    blk = MOBA_BLOCK
    n_blk = s // blk
    col = lambda c: pl.BlockSpec((1, s, LANE), lambda hp, bi: (bi, 0, COL_A + c * N_PAIRS + hp))
    return pl.pallas_call(
        _moba_kernel,
        grid=(N_PAIRS, b),
        in_specs=[col(0), col(1), col(2), col(3), pl.BlockSpec(tv_a.shape, lambda hp, bi: (0, 0, 0, 0))],
        out_specs=pl.BlockSpec((1, s, LANE), lambda hp, bi: (bi, 0, hp)),
        out_shape=jax.ShapeDtypeStruct((b, s, W_BR), ACT),
        scratch_shapes=[
            pltpu.VMEM((2, n_blk, blk, blk), F32),
            pltpu.VMEM((n_blk, 2, LANE, blk), ACT),
            pltpu.VMEM((n_blk, 2, HEAD_DIM + ONES_ROWS, blk), ACT),
            pltpu.VMEM((n_blk, LANE), F32),
            pltpu.VMEM((n_blk, 2, n_blk, blk), F32),
            pltpu.VMEM((n_blk, 2, 1, blk), F32),
            pltpu.VMEM((n_blk, 2, HEAD_DIM + ONES_ROWS, blk), F32),
        ],
        compiler_params=_cparams(2),
        name="moba",
    )(p, p, p, p, tv_a)


def _band_attend(items):
    scores = [[_dot(k, qs) for k, _, _, _ in blocks] for qs, blocks, _ in items]
    probs = []
    for (_, blocks, sink), sc in zip(items, scores):
        ss = [s + bias if valid is None else jnp.where(valid, s + bias, NEG)
              for s, (_, _, bias, valid) in zip(sc, blocks)]
        m = functools.reduce(jnp.maximum, [jnp.max(s, axis=0, keepdims=True) for s in ss])
        if sink is not None:
            m = jnp.maximum(m, sink)
        probs.append((m, [jnp.exp2(s - m).astype(ACT) for s in ss]))
    outs = []
    for (_, blocks, sink), (m, ps) in zip(items, probs):
        n_v = blocks[0][1].shape[0]
        ones = jnp.ones((ONES_ROWS, ps[0].shape[1]), ACT)
        acc = functools.reduce(operator.add, [_dot(jnp.concatenate([vt, ones], axis=0), p)
                                              for (_, vt, _, _), p in zip(blocks, ps)])
        l = acc[n_v:n_v + 1, :]
        if sink is not None:
            l = l + jnp.exp2(sink - m)
        outs.append((acc[:n_v, :] * (1.0 / l), m + jnp.log2(l)))
    return outs


def _band_bias(tv_ref, head, prev_includes_window_edge):
    kr = lax.broadcasted_iota(jnp.int32, (BAND, BAND), 0)
    qc = lax.broadcasted_iota(jnp.int32, (BAND, BAND), 1)
    own, prev = [_toeplitz(tv_ref[head, which], tv_ref[head, which + 1], BAND) for which in range(2)]
    prev_mask = (qc <= kr) if prev_includes_window_edge else (qc < kr)
    return jnp.where(qc >= kr, own, NEG), jnp.where(prev_mask, prev, NEG)


def _dil_kernel(q0_ref, k0_ref, v0_ref, q1_ref, k1_ref, v1_ref, q2_ref, k2_ref, v2_ref, sg_ref, tv_ref,
                o_ref, og_ref, lg_ref, bt_ref):
    hp = pl.program_id(0)
    b = pl.program_id(1)
    s = q0_ref.shape[1]
    n_tiles = s // BAND

    @pl.when(b == 0)
    def _():
        for g in range(N_DIL):
            for sh in range(2):
                own, prev = _band_bias(tv_ref, g * N_HEADS + 2 * hp + sh, True)
                bt_ref[g, sh, 0] = own
                bt_ref[g, sh, 1] = prev

    groups = ((q0_ref, k0_ref, v0_ref), (q1_ref, k1_ref, v1_ref), (q2_ref, k2_ref, v2_ref))

    for g, (q_ref, k_ref, v_ref) in enumerate(groups):
        d = DIL_PAIRS[g][1]
        per_phase = n_tiles // d
        has_prev = per_phase > 1
        assert per_phase % TILE_GROUP == 0 or TILE_GROUP % per_phase == 0

        def body(i, c, g=g, d=d, per_phase=per_phase, has_prev=has_prev, q_ref=q_ref, k_ref=k_ref, v_ref=v_ref):
            items, dsts = [], []
            k_own = vt_own = None
            for n in range(TILE_GROUP):
                u = i * TILE_GROUP + n
                rows = pl.ds(pl.multiple_of(u * BAND, BAND), BAND)
                t = lax.rem(u, per_phase)
                r = lax.div(u, per_phase)
                qt = q_ref[0, rows, :].T
                k_prev, vt_prev = k_own, vt_own
                k_own = k_ref[0, rows, :]
                vt_own = v_ref[0, rows, :].T
                if has_prev:
                    phase_start_known = TILE_GROUP % per_phase == 0
                    prev_exists = (t > 0) if (n == 0 and not phase_start_known) else None
                    if phase_start_known and n % per_phase == 0:
                        k_prev = None
                    elif n == 0:
                        prow = pl.ds(pl.multiple_of(jnp.maximum(u - 1, 0) * BAND, BAND), BAND)
                        k_prev = k_ref[0, prow, :]
                        vt_prev = v_ref[0, prow, :].T
                for sh in range(2):
                    hrows = slice(HEAD_DIM * sh, HEAD_DIM * (sh + 1))
                    blocks = [(k_own, vt_own[hrows, :], bt_ref[g, sh, 0], None)]
                    if has_prev and k_prev is not None:
                        blocks.append((k_prev, vt_prev[hrows, :], bt_ref[g, sh, 1], prev_exists))
                    items.append((_head_rows(qt, HEAD_DIM * sh), blocks, None))
                dsts.append(rows if d == 1 else pl.ds(t * (BAND * d) + r, BAND, stride=d))
            res = _band_attend(items)
            for n, dst in enumerate(dsts):
                (o0, lse0), (o1, lse1) = res[2 * n], res[2 * n + 1]
                og_ref.at[g][dst, :] = jnp.concatenate([o0, o1], axis=0).T
                lg_ref.at[g][dst, :] = jnp.concatenate(
                    [jnp.broadcast_to(lse0, (HEAD_DIM, BAND)), jnp.broadcast_to(lse1, (HEAD_DIM, BAND))], axis=0).T
            return c

        lax.fori_loop(0, n_tiles // TILE_GROUP, body, 0)

    def merge(i, c):
        rows = pl.ds(pl.multiple_of(i * ROWS, ROWS), ROWS)
        l0, l1, l2 = lg_ref[0, rows, :], lg_ref[1, rows, :], lg_ref[2, rows, :]
        m = jnp.maximum(l0, jnp.maximum(l1, l2))
        e0, e1, e2 = jnp.exp2(l0 - m), jnp.exp2(l1 - m), jnp.exp2(l2 - m)
        ob = (e0 * og_ref[0, rows, :] + e1 * og_ref[1, rows, :] + e2 * og_ref[2, rows, :]) / (e0 + e1 + e2)
        o_ref[0, rows, :] = (ob * sg_ref[0, rows, :].astype(F32)).astype(o_ref.dtype)
        return c

    lax.fori_loop(0, s // ROWS, merge, 0)


def _dilated(p, tv_b):
    b, s, _ = p.shape
    col = lambda c: pl.BlockSpec((1, s, LANE), lambda hp, bi: (bi, 0, c + hp))
    qkv = [col(COL_B + 3 * N_PAIRS * g + N_PAIRS * i) for g in range(N_DIL) for i in range(3)]
    return pl.pallas_call(
        _dil_kernel,
        grid=(N_PAIRS, b),
        in_specs=qkv + [col(COL_SGB), pl.BlockSpec(tv_b.shape, lambda hp, bi: (0, 0, 0, 0))],
        out_specs=pl.BlockSpec((1, s, LANE), lambda hp, bi: (bi, 0, hp)),
        out_shape=jax.ShapeDtypeStruct((b, s, W_BR), ACT),
        scratch_shapes=[
            pltpu.VMEM((N_DIL, s, LANE), F32),
            pltpu.VMEM((N_DIL, s, LANE), F32),
            pltpu.VMEM((N_DIL, 2, 2, BAND, BAND), F32),
        ],
        compiler_params=_cparams(2),
        name="dilated",
    )(*([p] * 10), tv_b)


def _swa_kernel(sink_ref, q_ref, sg_ref, k_ref, v_ref, tv_ref, o_ref, bt_ref, vts_ref):
    hp = pl.program_id(0)
    b = pl.program_id(1)
    n_tiles = q_ref.shape[1] // BAND
    pairs_per_kv = N_PAIRS // KV_C

    @pl.when(b == 0)
    def _():
        for sh in range(2):
            own, prev = _band_bias(tv_ref, 2 * hp + sh, False)
            bt_ref[sh, 0] = own
            bt_ref[sh, 1] = prev

    kv_lo = HEAD_DIM * (hp // pairs_per_kv)
    row = lax.broadcasted_iota(jnp.int32, (LANE, BAND), 0)
    kv_rows = (row >= kv_lo) & (row < kv_lo + HEAD_DIM)

    def kv_vt(rows, slot):
        vts_ref[slot] = v_ref[0, rows, :].T
        return vts_ref[slot, pl.ds(pl.multiple_of(kv_lo, HEAD_DIM), HEAD_DIM), :]

    def body(i, c):
        items, dsts = [], []
        zero = jnp.zeros((LANE, BAND), ACT)
        k_own = vt_own = None
        for n in range(TILE_GROUP):
            t = i * TILE_GROUP + n
            rows = pl.ds(pl.multiple_of(t * BAND, BAND), BAND)
            qt = q_ref[0, rows, :].T
            k_prev, vt_prev = k_own, vt_own
            k_own = k_ref[0, rows, :]
            vt_own = kv_vt(rows, n)
            if n == 0:
                prow = pl.ds(pl.multiple_of(jnp.maximum(t - 1, 0) * BAND, BAND), BAND)
                k_prev = k_ref[0, prow, :]
                vt_prev = kv_vt(prow, TILE_GROUP)
            prev_exists = (t > 0) if n == 0 else None
            for sh in range(2):
                qh = qt[HEAD_DIM * sh:HEAD_DIM * (sh + 1), :]
                qs = jnp.where(kv_rows, jnp.concatenate([qh, qh], axis=0), zero)
                items.append((qs, [(k_own, vt_own, bt_ref[sh, 0], None), (k_prev, vt_prev, bt_ref[sh, 1], prev_exists)],
                              sink_ref[2 * hp + sh]))
            dsts.append(rows)
        res = _band_attend(items)
        for n, rows in enumerate(dsts):
            ot = jnp.concatenate([o for o, _ in res[2 * n:2 * n + 2]], axis=0).T
            o_ref[0, rows, :] = (ot * sg_ref[0, rows, :].astype(F32)).astype(o_ref.dtype)
        return c

    lax.fori_loop(0, n_tiles // TILE_GROUP, body, 0)


def _swa(p, tv_c, sinks):
    b, s, _ = p.shape
    grid_spec = pltpu.PrefetchScalarGridSpec(
        num_scalar_prefetch=1,
        grid=(N_PAIRS, b),
        in_specs=[
            pl.BlockSpec((1, s, LANE), lambda hp, bi, sk: (bi, 0, COL_C + hp)),
            pl.BlockSpec((1, s, LANE), lambda hp, bi, sk: (bi, 0, COL_C + N_PAIRS + hp)),
            pl.BlockSpec((1, s, LANE), lambda hp, bi, sk: (bi, 0, COL_C + 2 * N_PAIRS)),
            pl.BlockSpec((1, s, LANE), lambda hp, bi, sk: (bi, 0, COL_C + 2 * N_PAIRS + 1)),
            pl.BlockSpec(tv_c.shape, lambda hp, bi, sk: (0, 0, 0, 0)),
        ],
        out_specs=pl.BlockSpec((1, s, LANE), lambda hp, bi, sk: (bi, 0, hp)),
        scratch_shapes=[pltpu.VMEM((2, 2, BAND, BAND), F32),
                        pltpu.VMEM((TILE_GROUP + 1, LANE, BAND), ACT)],
    )
    return pl.pallas_call(
        _swa_kernel,
        grid_spec=grid_spec,
        out_shape=jax.ShapeDtypeStruct((b, s, W_BR), ACT),
        compiler_params=_cparams(2),
        name="swa",
    )(sinks, p, p, p, p, tv_c)


def _merge_kernel(bra_ref, brb_ref, brc_ref, g_ref, wb_ref, wo_ref, x_ref, out_ref):
    merged = None
    for i, br_ref in enumerate((bra_ref, brb_ref, brc_ref)):
        term = g_ref[:, i * D_MODEL:(i + 1) * D_MODEL].astype(F32) * _dot(br_ref[...], wb_ref[i])
        merged = term if merged is None else merged + term
    out_ref[...] = x_ref[...] + _dot(merged.astype(ACT), wo_ref[...])


def _merge(bra, brb, brc, p2, w_br, w_out, x2, layer, ts=512):
    n = x2.shape[0]
    row = lambda w: pl.BlockSpec((ts, w), lambda i: (i, 0))
    return pl.pallas_call(
        _merge_kernel,
        grid=(n // ts,),
        in_specs=[row(W_BR), row(W_BR), row(W_BR), row(GATE_W),
                  pl.BlockSpec((None,) + w_br.shape[1:], lambda i: (layer, 0, 0, 0)),
                  pl.BlockSpec((None,) + w_out.shape[1:], lambda i: (layer, 0, 0)),
                  row(D_MODEL)],
        out_specs=row(D_MODEL),
        out_shape=jax.ShapeDtypeStruct(x2.shape, F32),
        compiler_params=_cparams(1),
        name="merge",
    )(bra, brb, brc, p2, w_br, w_out, x2)


def _rel_bucket(dist):
    dist = jnp.maximum(dist, 0)
    max_exact = N_BUCKETS // 2
    log_ratio = jnp.log(jnp.maximum(dist, 1).astype(F32) / max_exact) / math.log(REL_MAX_DIST / max_exact)
    large = jnp.minimum(max_exact + (log_ratio * (N_BUCKETS - max_exact)).astype(jnp.int32), N_BUCKETS - 1)
    return jnp.where(dist < max_exact, dist, large)


def _bias_by_distance(tab, dist, chunk):
    vals = tab.astype(F32)[:, _rel_bucket(dist)] * LOG2E
    vals = jnp.pad(vals, ((0, 0), (chunk, 0)))
    return vals.reshape(tab.shape[0], -1, 1, chunk)


def _gain_rows(qk_g):
    depth = qk_g.shape[0]
    scale = HEAD_DIM ** -0.5 * LOG2E
    one = jnp.ones((depth, HEAD_DIM), F32)
    tile = lambda v, n: jnp.tile(v, (1, n))
    parts = [tile(qk_g[:, 0] * scale, N_HEADS), tile(qk_g[:, 1], N_HEADS), tile(one, 2 * N_HEADS),
             tile(qk_g[:, 2] * scale, N_DIL * N_HEADS), tile(qk_g[:, 3], N_DIL * N_HEADS),
             tile(one, (N_DIL + 1) * N_HEADS),
             tile(qk_g[:, 4] * scale, N_HEADS), tile(qk_g[:, 5], KV_C), tile(one, KV_C + N_HEADS),
             tile(one, GATE_W // HEAD_DIM)]
    return jnp.concatenate(parts, axis=1)


def kernel(x, ln_g, w_in, qk_g, sinks, w_branch, w_out, rel_bias):
    b, s, d = x.shape
    depth = ln_g.shape[0]
    assert d == D_MODEL and s % DIL_PAIRS[-1][0] == 0 and w_in.shape[-1] == C_IN

    tv_a = _bias_by_distance(rel_bias[:N_HEADS], jnp.arange(s), MOBA_BLOCK)
    tv_b = jnp.concatenate(
        [_bias_by_distance(rel_bias[OFF_B + g * N_HEADS:OFF_B + (g + 1) * N_HEADS], dd * jnp.arange(2 * BAND), BAND)
         for g, (_, dd) in enumerate(DIL_PAIRS)], axis=0)
    tv_c = _bias_by_distance(rel_bias[OFF_C:OFF_C + N_HEADS], jnp.arange(2 * BAND), BAND)
    seg = jnp.arange(TN) // HEAD_DIM
    e_mat = (seg[:, None] == seg[None, :]).astype(ACT)
    src = jnp.asarray([c for c, _ in _LAYOUT], jnp.int32)
    fn = jnp.asarray([f for _, f in _LAYOUT[::PAIR]], jnp.int32)
    gains = _gain_rows(qk_g)

    w_in_b = w_in.astype(ACT)
    w_br_b = w_branch.astype(ACT)
    w_out_b = w_out.astype(ACT)

    for l in range(depth):
        p = _project(x, ln_g[:, None, :], w_in_b, gains[l][None, :], e_mat, src, fn, l)
        bra = _moba(p, tv_a)
        brb = _dilated(p, tv_b)
        brc = _swa(p, tv_c, sinks[l] * LOG2E)
        x2 = _merge(bra.reshape(b * s, W_BR), brb.reshape(b * s, W_BR), brc.reshape(b * s, W_BR),
                    p.reshape(b * s, P_W), w_br_b, w_out_b, x.reshape(b * s, d), l)
        x = x2.reshape(b, s, d)
    return x
```

```python
import functools
import math
import operator

import jax
import jax.numpy as jnp
from jax import lax
from jax.experimental import pallas as pl
from jax.experimental.pallas import tpu as pltpu

D_MODEL = 1024
HEAD_DIM = 64
W_BR = D_MODEL // 2
N_HEADS = W_BR // HEAD_DIM
N_PAIRS = N_HEADS // 2
MOBA_BLOCK = 256
MOBA_TOPK = 3
DIL_PAIRS = ((128, 1), (512, 4), (2048, 16))
N_DIL = len(DIL_PAIRS)
KV_C = 2
SWA_WINDOW = 128
N_BUCKETS = 32
REL_MAX_DIST = 2048
OFF_B = N_HEADS
OFF_C = N_HEADS + N_DIL * N_HEADS
EPS = 1e-6
NEG = -1e30
LOG2E = math.log2(math.e)

LANE = 128
BAND = 128
TN = 256
ROWS = 256
TILE_GROUP = 16
MOBA_GROUP = 8
SINGLE_OP_STRIDE = 4
ONES_ROWS = 16
VMEM_LIMIT = 56 * 1024 * 1024

ACT = jnp.bfloat16
F32 = jnp.float32

PLAIN, NORM, SILU, SIGMOID, KNORM_VPLAIN, NORM_D4, PLAIN_D4, NORM_D16, PLAIN_D16 = range(9)
_B_GROUP = lambda g, norm, plain: [(8 + 2 * g, norm), (9 + 2 * g, norm), (14 + 2 * g, norm), (15 + 2 * g, norm),
                                   (20 + 2 * g, plain), (21 + 2 * g, plain)]
_LAYOUT = ([(j, SIGMOID) for j in range(33, 45)]
           + [(0, NORM), (1, NORM), (2, NORM), (3, NORM), (4, PLAIN), (5, PLAIN), (6, SILU), (7, SILU)]
           + _B_GROUP(0, NORM, PLAIN) + _B_GROUP(1, NORM_D4, PLAIN_D4) + _B_GROUP(2, NORM_D16, PLAIN_D16)
           + [(26, SILU), (27, SILU)]
           + [(28, NORM), (29, NORM), (31, SILU), (32, SILU), (30, KNORM_VPLAIN)])
C_IN = len(_LAYOUT) * TN
PAIR = 2
_LAYOUT = _LAYOUT + [_LAYOUT[-1]] * (-len(_LAYOUT) % PAIR)
assert all(_LAYOUT[j][1] == _LAYOUT[j - j % PAIR][1] for j in range(len(_LAYOUT)))
N_STEPS = len(_LAYOUT) // PAIR
P_W = len(_LAYOUT) * TN
COL_GATE, COL_A, COL_B, COL_SGB, COL_C = 0, 24, 40, 76, 80
GATE_W = 3 * D_MODEL


def _cparams(n_axes):
    return pltpu.CompilerParams(dimension_semantics=("arbitrary",) * n_axes, vmem_limit_bytes=VMEM_LIMIT)


def _dot(a, b):
    return jnp.dot(a, b, preferred_element_type=F32)


def _sigmoid(a):
    return 0.5 * jnp.tanh(0.5 * a) + 0.5


def _proj_kernel(src_ref, fn_ref, x_ref, lng_ref, w0_ref, w1_ref, g0_ref, g1_ref, e_ref, o_ref,
                 xn_ref, acc_ref, acc2_ref):
    del src_ref
    w_refs, gain_refs = (w0_ref, w1_ref), (g0_ref, g1_ref)
    assert len(w_refs) == PAIR
    j = pl.program_id(1)
    s = x_ref.shape[1]
    n_chunks = s // ROWS

    @pl.when(j == 0)
    def _():
        def body(i, c):
            rows = pl.ds(pl.multiple_of(i * ROWS, ROWS), ROWS)
            xv = x_ref[0, rows, :]
            ms = jnp.mean(xv * xv, axis=-1, keepdims=True)
            xn_ref[rows, :] = (xv * lax.rsqrt(ms + EPS) * lng_ref[...]).astype(ACT)
            return c
        lax.fori_loop(0, n_chunks, body, 0)

    def head_norm(a, gain):
        ss = _dot((a * a).astype(ACT), e_ref[...])
        return a * lax.rsqrt(ss * (1.0 / HEAD_DIM) + EPS) * gain

    def plain(a, gain):
        return a

    def silu(a, gain):
        return a * _sigmoid(a)

    def sigmoid(a, gain):
        return _sigmoid(a)

    def k_norm_v_plain(a, gain):
        lane = lax.broadcasted_iota(jnp.int32, a.shape, 1)
        return jnp.where(lane < KV_C * HEAD_DIM, head_norm(a, gain), a)

    def chunk(n):
        h, i = divmod(n, n_chunks)
        return _dot(xn_ref[i * ROWS:(i + 1) * ROWS, :], w_refs[h][...])

    fn = fn_ref[j]
    lanes_per_block = TN // LANE

    def direct(code, f):
        @pl.when(fn == code)
        def _():
            pending = None
            for n in range(PAIR * n_chunks + 1):
                a = chunk(n) if n < PAIR * n_chunks else None
                if pending is not None:
                    h, i = divmod(n - 1, n_chunks)
                    o_ref[0, i * ROWS:(i + 1) * ROWS, h * TN:(h + 1) * TN] = (
                        f(pending, gain_refs[h][...]).astype(o_ref.dtype))
                pending = a

    def phase_rows(src_ref, slab, t, d):
        r, m0 = divmod(t * BAND, s // d)
        return src_ref.at[slab][pl.ds(m0 * d + r, BAND, stride=d), :]

    def phase_major(code, f, d):
        @pl.when(fn == code)
        def _():
            for n in range(PAIR * n_chunks):
                h, i = divmod(n, n_chunks)
                a = chunk(n)
                for c in range(lanes_per_block):
                    acc_ref[h * lanes_per_block + c, i * ROWS:(i + 1) * ROWS, :] = a[:, c * LANE:(c + 1) * LANE]
            src_ref, stride = acc_ref, d
            if d > SINGLE_OP_STRIDE:
                assert d == SINGLE_OP_STRIDE * SINGLE_OP_STRIDE
                for slab in range(PAIR * lanes_per_block):
                    for t in range(s // BAND):
                        acc2_ref[slab, t * BAND:(t + 1) * BAND, :] = phase_rows(acc_ref, slab, t, SINGLE_OP_STRIDE)
                src_ref, stride = acc2_ref, SINGLE_OP_STRIDE
            for h in range(PAIR):
                for t in range(s // BAND):
                    a = jnp.concatenate([phase_rows(src_ref, h * lanes_per_block + c, t, stride)
                                         for c in range(lanes_per_block)], axis=1)
                    o_ref[0, t * BAND:(t + 1) * BAND, h * TN:(h + 1) * TN] = (
                        f(a, gain_refs[h][...]).astype(o_ref.dtype))

    direct(PLAIN, plain)
    direct(NORM, head_norm)
    direct(SILU, silu)
    direct(SIGMOID, sigmoid)
    direct(KNORM_VPLAIN, k_norm_v_plain)
    phase_major(NORM_D4, head_norm, DIL_PAIRS[1][1])
    phase_major(PLAIN_D4, plain, DIL_PAIRS[1][1])
    phase_major(NORM_D16, head_norm, DIL_PAIRS[2][1])
    phase_major(PLAIN_D16, plain, DIL_PAIRS[2][1])


def _project(x, ln_g, w_in, gain_row, e_mat, src, fn, layer):
    b, s, d = x.shape
    grid_spec = pltpu.PrefetchScalarGridSpec(
        num_scalar_prefetch=2,
        grid=(b, N_STEPS),
        in_specs=[
            pl.BlockSpec((1, s, d), lambda bi, j, src_ref, fn_ref: (bi, 0, 0)),
            pl.BlockSpec((None, 1, d), lambda bi, j, src_ref, fn_ref: (layer, 0, 0)),
            pl.BlockSpec((None, d, TN), lambda bi, j, src_ref, fn_ref: (layer, 0, src_ref[PAIR * j])),
            pl.BlockSpec((None, d, TN), lambda bi, j, src_ref, fn_ref: (layer, 0, src_ref[PAIR * j + 1])),
            pl.BlockSpec((1, TN), lambda bi, j, src_ref, fn_ref: (0, src_ref[PAIR * j])),
            pl.BlockSpec((1, TN), lambda bi, j, src_ref, fn_ref: (0, src_ref[PAIR * j + 1])),
            pl.BlockSpec((TN, TN), lambda bi, j, src_ref, fn_ref: (0, 0)),
        ],
        out_specs=pl.BlockSpec((1, s, PAIR * TN), lambda bi, j, src_ref, fn_ref: (bi, 0, j)),
        scratch_shapes=[pltpu.VMEM((s, d), ACT), pltpu.VMEM((PAIR * TN // LANE, s, LANE), F32),
                        pltpu.VMEM((PAIR * TN // LANE, s, LANE), F32)],
    )
    return pl.pallas_call(
        _proj_kernel,
        grid_spec=grid_spec,
        out_shape=jax.ShapeDtypeStruct((b, s, P_W), ACT),
        compiler_params=_cparams(2),
        name="proj",
    )(src, fn, x, ln_g, w_in, w_in, gain_row, gain_row, e_mat)


def _toeplitz(lo, hi, n):
    x = jnp.concatenate([jnp.broadcast_to(lo, (n, n)), jnp.broadcast_to(hi, (n, n))], axis=1)
    return pltpu.roll(x, 0, 1, stride=1, stride_axis=0)[:, n:]


def _head_rows(qt, lo):
    row = lax.broadcasted_iota(jnp.int32, qt.shape, 0)
    return jnp.where((row >= lo) & (row < lo + HEAD_DIM), qt, jnp.zeros_like(qt))


def _moba_kernel(q_ref, k_ref, v_ref, sg_ref, tv_ref, o_ref,
                 bt_ref, qts_ref, vt_ref, km_ref, sel_ref, m_ref, acc_ref):
    hp = pl.program_id(0)
    b = pl.program_id(1)
    blk = MOBA_BLOCK
    n_blk = k_ref.shape[1] // blk

    def rows_of(i):
        return pl.ds(i * blk if isinstance(i, int) else pl.multiple_of(i * blk, blk), blk)

    @pl.when(b == 0)
    def _():
        kr = lax.broadcasted_iota(jnp.int32, (blk, blk), 0)
        qc = lax.broadcasted_iota(jnp.int32, (blk, blk), 1)
        for sh in range(2):
            def body(o, c, sh=sh):
                t = _toeplitz(tv_ref[2 * hp + sh, o], tv_ref[2 * hp + sh, o + 1], blk)
                bt_ref[sh, o] = jnp.where((qc >= kr) | (o > 0), t, NEG)
                return c
            lax.fori_loop(0, n_blk, body, 0)

    def prep(i, c):
        vt = v_ref[0, rows_of(i), :].T
        for sh in range(2):
            vt_ref[i, sh] = jnp.concatenate(
                [vt[HEAD_DIM * sh:HEAD_DIM * (sh + 1), :], jnp.ones((ONES_ROWS, blk), ACT)], axis=0)
        km_ref[pl.ds(i, 1), :] = jnp.mean(k_ref[0, rows_of(i), :].astype(F32), axis=0, keepdims=True)
        qt = q_ref[0, rows_of(i), :].T
        for sh in range(2):
            qts_ref[i, sh] = _head_rows(qt, HEAD_DIM * sh)
        return c
    lax.fori_loop(0, n_blk, prep, 0, unroll=2)

    km = km_ref[...]
    km0 = km.astype(ACT)
    r1 = km - km0.astype(F32)
    km1 = r1.astype(ACT)
    km2 = (r1 - km1.astype(F32)).astype(ACT)
    blk_row = lax.broadcasted_iota(jnp.int32, (n_blk, blk), 0)

    ranked = [(qi, sh) for qi in range(MOBA_TOPK + 1, n_blk) for sh in range(2)]
    gates = [_dot(km0, qts_ref[qi, sh]) + _dot(km1, qts_ref[qi, sh]) + _dot(km2, qts_ref[qi, sh])
             for qi, sh in ranked]
    for (qi, sh), g in zip(ranked, gates):
        g = jnp.where(blk_row < qi, g, NEG)
        sel_rows = []
        for jb in range(qi):
            gj = g[jb:jb + 1, :]
            beats = (g > gj) | ((g == gj) & (blk_row < jb))
            cnt = jnp.sum(beats.astype(F32), axis=0, keepdims=True)
            sel_rows.append(jnp.where(cnt < MOBA_TOPK, 1.0, 0.0))
        sel_rows.append(jnp.zeros((n_blk - qi, blk), F32))
        sel_ref[qi, sh] = jnp.concatenate(sel_rows, axis=0)

    def tiles(off, qis):
        k_blks = [k_ref[0, rows_of(qi - off), :] for qi in qis]
        scores = [_dot(k_blk, qts_ref[qi, sh]) for qi, k_blk in zip(qis, k_blks) for sh in range(2)]
        work = [(qi, sh) for qi in qis for sh in range(2)]
        stats = []
        for (qi, sh), st in zip(work, scores):
            st = st + bt_ref[sh, off]
            tile_max = jnp.max(st, axis=0, keepdims=True)
            if off == 0:
                m_new, alpha = tile_max, None
                p = jnp.exp2(st - m_new)
            elif qi <= MOBA_TOPK:
                m = m_ref[qi, sh]
                m_new = jnp.maximum(m, tile_max)
                p = jnp.exp2(st - m_new)
                alpha = jnp.exp2(m - m_new)
            else:
                sel = sel_ref[qi, sh, pl.ds(qi - off, 1), :] > 0.5
                m = m_ref[qi, sh]
                m_new = jnp.where(sel, jnp.maximum(m, tile_max), m)
                p = jnp.exp2(st - jnp.where(sel, m_new, -NEG))
                alpha = jnp.exp2(m - m_new)
            m_ref[qi, sh] = m_new
            stats.append((alpha, p.astype(ACT)))
        pvs = [_dot(vt_ref[qi - off, sh], p) for (qi, sh), (_, p) in zip(work, stats)]
        for (qi, sh), (alpha, _), pv in zip(work, stats, pvs):
            acc_ref[qi, sh] = pv if off == 0 else alpha * acc_ref[qi, sh] + pv

    for off in range(n_blk):
        qis = list(range(off, n_blk))
        for lo in range(0, len(qis), MOBA_GROUP):
            tiles(off, qis[lo:lo + MOBA_GROUP])

    def finish(qi, c):
        ot = jnp.concatenate([acc_ref[qi, sh, :HEAD_DIM, :] * (1.0 / acc_ref[qi, sh, HEAD_DIM:HEAD_DIM + 1, :])
                              for sh in range(2)], axis=0)
        o_ref[0, rows_of(qi), :] = (ot.T * sg_ref[0, rows_of(qi), :].astype(F32)).astype(o_ref.dtype)
        return c
    lax.fori_loop(0, n_blk, finish, 0, unroll=2)


def _moba(p, tv_a):
    b, s, _ = p.shape
    blk = MOBA_BLOCK
    n_blk = s // blk
    col = lambda c: pl.BlockSpec((1, s, LANE), lambda hp, bi: (bi, 0, COL_A + c * N_PAIRS + hp))
    return pl.pallas_call(
        _moba_kernel,
        grid=(N_PAIRS, b),
        in_specs=[col(0), col(1), col(2), col(3), pl.BlockSpec(tv_a.shape, lambda hp, bi: (0, 0, 0, 0))],
        out_specs=pl.BlockSpec((1, s, LANE), lambda hp, bi: (bi, 0, hp)),
        out_shape=jax.ShapeDtypeStruct((b, s, W_BR), ACT),
        scratch_shapes=[
            pltpu.VMEM((2, n_blk, blk, blk), F32),
            pltpu.VMEM((n_blk, 2, LANE, blk), ACT),
            pltpu.VMEM((n_blk, 2, HEAD_DIM + ONES_ROWS, blk), ACT),
            pltpu.VMEM((n_blk, LANE), F32),
            pltpu.VMEM((n_blk, 2, n_blk, blk), F32),
            pltpu.VMEM((n_blk, 2, 1, blk), F32),
            pltpu.VMEM((n_blk, 2, HEAD_DIM + ONES_ROWS, blk), F32),
        ],
        compiler_params=_cparams(2),
        name="moba",
    )(p, p, p, p, tv_a)


def _band_attend(items):
    def fuse(blocks):
        if len(blocks) > 1 and all(valid is None for _, _, _, valid in blocks):
            ks, vts, biases, _ = zip(*blocks)
            return [(jnp.concatenate(ks, axis=0), jnp.concatenate(vts, axis=1), jnp.concatenate(biases, axis=0), None)]
        return blocks

    items = [(qs, fuse(blocks), sink) for qs, blocks, sink in items]
    scores = [[_dot(k, qs) for k, _, _, _ in blocks] for qs, blocks, _ in items]
    probs = []
    for (_, blocks, sink), sc in zip(items, scores):
        ss = [s + bias if valid is None else jnp.where(valid, s + bias, NEG)
              for s, (_, _, bias, valid) in zip(sc, blocks)]
        m = functools.reduce(jnp.maximum, [jnp.max(s, axis=0, keepdims=True) for s in ss])
        if sink is not None:
            m = jnp.maximum(m, sink)
        probs.append((m, [jnp.exp2(s - m).astype(ACT) for s in ss]))
    outs = []
    for (_, blocks, sink), (m, ps) in zip(items, probs):
        n_v = blocks[0][1].shape[0]
        acc = functools.reduce(operator.add, [
            _dot(jnp.concatenate([vt, jnp.ones((ONES_ROWS, vt.shape[1]), ACT)], axis=0), p)
            for (_, vt, _, _), p in zip(blocks, ps)])
        l = acc[n_v:n_v + 1, :]
        if sink is not None:
            l = l + jnp.exp2(sink - m)
        outs.append((acc[:n_v, :] * (1.0 / l), m + jnp.log2(l)))
    return outs


def _band_bias(tv_ref, head, prev_includes_window_edge):
    kr = lax.broadcasted_iota(jnp.int32, (BAND, BAND), 0)
    qc = lax.broadcasted_iota(jnp.int32, (BAND, BAND), 1)
    own, prev = [_toeplitz(tv_ref[head, which], tv_ref[head, which + 1], BAND) for which in range(2)]
    prev_mask = (qc <= kr) if prev_includes_window_edge else (qc < kr)
    return jnp.where(qc >= kr, own, NEG), jnp.where(prev_mask, prev, NEG)


def _dil_kernel(q0_ref, k0_ref, v0_ref, q1_ref, k1_ref, v1_ref, q2_ref, k2_ref, v2_ref, sg_ref, tv_ref,
                o_ref, og_ref, lg_ref, bt_ref):
    hp = pl.program_id(0)
    b = pl.program_id(1)
    s = q0_ref.shape[1]
    n_tiles = s // BAND

    @pl.when(b == 0)
    def _():
        for g in range(N_DIL):
            for sh in range(2):
                own, prev = _band_bias(tv_ref, g * N_HEADS + 2 * hp + sh, True)
                bt_ref[g, sh, 0] = own
                bt_ref[g, sh, 1] = prev

    groups = ((q0_ref, k0_ref, v0_ref), (q1_ref, k1_ref, v1_ref), (q2_ref, k2_ref, v2_ref))

    for g, (q_ref, k_ref, v_ref) in enumerate(groups):
        d = DIL_PAIRS[g][1]
        per_phase = n_tiles // d
        has_prev = per_phase > 1
        assert per_phase % TILE_GROUP == 0 or TILE_GROUP % per_phase == 0

        def body(i, c, g=g, d=d, per_phase=per_phase, has_prev=has_prev, q_ref=q_ref, k_ref=k_ref, v_ref=v_ref):
            items, dsts = [], []
            k_own = vt_own = None
            for n in range(TILE_GROUP):
                u = i * TILE_GROUP + n
                rows = pl.ds(pl.multiple_of(u * BAND, BAND), BAND)
                t = lax.rem(u, per_phase)
                r = lax.div(u, per_phase)
                qt = q_ref[0, rows, :].T
                k_prev, vt_prev = k_own, vt_own
                k_own = k_ref[0, rows, :]
                vt_own = v_ref[0, rows, :].T
                if has_prev:
                    phase_start_known = TILE_GROUP % per_phase == 0
                    prev_exists = (t > 0) if (n == 0 and not phase_start_known) else None
                    if phase_start_known and n % per_phase == 0:
                        k_prev = None
                    elif n == 0:
                        prow = pl.ds(pl.multiple_of(jnp.maximum(u - 1, 0) * BAND, BAND), BAND)
                        k_prev = k_ref[0, prow, :]
                        vt_prev = v_ref[0, prow, :].T
                for sh in range(2):
                    hrows = slice(HEAD_DIM * sh, HEAD_DIM * (sh + 1))
                    blocks = [(k_own, vt_own[hrows, :], bt_ref[g, sh, 0], None)]
                    if has_prev and k_prev is not None:
                        blocks.append((k_prev, vt_prev[hrows, :], bt_ref[g, sh, 1], prev_exists))
                    items.append((_head_rows(qt, HEAD_DIM * sh), blocks, None))
                dsts.append(rows if d == 1 else pl.ds(t * (BAND * d) + r, BAND, stride=d))
            res = _band_attend(items)
            for n, dst in enumerate(dsts):
                (o0, lse0), (o1, lse1) = res[2 * n], res[2 * n + 1]
                og_ref.at[g][dst, :] = jnp.concatenate([o0, o1], axis=0).T
                lg_ref.at[g][dst, :] = jnp.concatenate(
                    [jnp.broadcast_to(lse0, (HEAD_DIM, BAND)), jnp.broadcast_to(lse1, (HEAD_DIM, BAND))], axis=0).T
            return c

        lax.fori_loop(0, n_tiles // TILE_GROUP, body, 0)

    def merge(i, c):
        rows = pl.ds(pl.multiple_of(i * ROWS, ROWS), ROWS)
        l0, l1, l2 = lg_ref[0, rows, :], lg_ref[1, rows, :], lg_ref[2, rows, :]
        m = jnp.maximum(l0, jnp.maximum(l1, l2))
        e0, e1, e2 = jnp.exp2(l0 - m), jnp.exp2(l1 - m), jnp.exp2(l2 - m)
        ob = (e0 * og_ref[0, rows, :] + e1 * og_ref[1, rows, :] + e2 * og_ref[2, rows, :]) / (e0 + e1 + e2)
        o_ref[0, rows, :] = (ob * sg_ref[0, rows, :].astype(F32)).astype(o_ref.dtype)
        return c

    lax.fori_loop(0, s // ROWS, merge, 0)


def _dilated(p, tv_b):
    b, s, _ = p.shape
    col = lambda c: pl.BlockSpec((1, s, LANE), lambda hp, bi: (bi, 0, c + hp))
    qkv = [col(COL_B + 3 * N_PAIRS * g + N_PAIRS * i) for g in range(N_DIL) for i in range(3)]
    return pl.pallas_call(
        _dil_kernel,
        grid=(N_PAIRS, b),
        in_specs=qkv + [col(COL_SGB), pl.BlockSpec(tv_b.shape, lambda hp, bi: (0, 0, 0, 0))],
        out_specs=pl.BlockSpec((1, s, LANE), lambda hp, bi: (bi, 0, hp)),
        out_shape=jax.ShapeDtypeStruct((b, s, W_BR), ACT),
        scratch_shapes=[
            pltpu.VMEM((N_DIL, s, LANE), F32),
            pltpu.VMEM((N_DIL, s, LANE), F32),
            pltpu.VMEM((N_DIL, 2, 2, BAND, BAND), F32),
        ],
        compiler_params=_cparams(2),
        name="dilated",
    )(*([p] * 10), tv_b)


def _swa_kernel(sink_ref, q_ref, sg_ref, k_ref, v_ref, tv_ref, o_ref, bt_ref, vts_ref):
    hp = pl.program_id(0)
    b = pl.program_id(1)
    n_tiles = q_ref.shape[1] // BAND
    pairs_per_kv = N_PAIRS // KV_C

    @pl.when(b == 0)
    def _():
        for sh in range(2):
            own, prev = _band_bias(tv_ref, 2 * hp + sh, False)
            bt_ref[sh, 0] = own
            bt_ref[sh, 1] = prev

    kv_lo = HEAD_DIM * (hp // pairs_per_kv)
    row = lax.broadcasted_iota(jnp.int32, (LANE, BAND), 0)
    kv_rows = (row >= kv_lo) & (row < kv_lo + HEAD_DIM)

    def kv_vt(rows, slot):
        vts_ref[slot] = v_ref[0, rows, :].T
        return vts_ref[slot, pl.ds(pl.multiple_of(kv_lo, HEAD_DIM), HEAD_DIM), :]

    def body(i, c):
        items, dsts = [], []
        zero = jnp.zeros((LANE, BAND), ACT)
        k_own = vt_own = None
        for n in range(TILE_GROUP):
            t = i * TILE_GROUP + n
            rows = pl.ds(pl.multiple_of(t * BAND, BAND), BAND)
            qt = q_ref[0, rows, :].T
            k_prev, vt_prev = k_own, vt_own
            k_own = k_ref[0, rows, :]
            vt_own = kv_vt(rows, n)
            if n == 0:
                prow = pl.ds(pl.multiple_of(jnp.maximum(t - 1, 0) * BAND, BAND), BAND)
                k_prev = k_ref[0, prow, :]
                vt_prev = kv_vt(prow, TILE_GROUP)
            prev_exists = (t > 0) if n == 0 else None
            for sh in range(2):
                qh = qt[HEAD_DIM * sh:HEAD_DIM * (sh + 1), :]
                qs = jnp.where(kv_rows, jnp.concatenate([qh, qh], axis=0), zero)
                items.append((qs, [(k_own, vt_own, bt_ref[sh, 0], None), (k_prev, vt_prev, bt_ref[sh, 1], prev_exists)],
                              sink_ref[2 * hp + sh]))
            dsts.append(rows)
        res = _band_attend(items)
        for n, rows in enumerate(dsts):
            ot = jnp.concatenate([o for o, _ in res[2 * n:2 * n + 2]], axis=0).T
            o_ref[0, rows, :] = (ot * sg_ref[0, rows, :].astype(F32)).astype(o_ref.dtype)
        return c

    lax.fori_loop(0, n_tiles // TILE_GROUP, body, 0)


def _swa(p, tv_c, sinks):
    b, s, _ = p.shape
    grid_spec = pltpu.PrefetchScalarGridSpec(
        num_scalar_prefetch=1,
        grid=(N_PAIRS, b),
        in_specs=[
            pl.BlockSpec((1, s, LANE), lambda hp, bi, sk: (bi, 0, COL_C + hp)),
            pl.BlockSpec((1, s, LANE), lambda hp, bi, sk: (bi, 0, COL_C + N_PAIRS + hp)),
            pl.BlockSpec((1, s, LANE), lambda hp, bi, sk: (bi, 0, COL_C + 2 * N_PAIRS)),
            pl.BlockSpec((1, s, LANE), lambda hp, bi, sk: (bi, 0, COL_C + 2 * N_PAIRS + 1)),
            pl.BlockSpec(tv_c.shape, lambda hp, bi, sk: (0, 0, 0, 0)),
        ],
        out_specs=pl.BlockSpec((1, s, LANE), lambda hp, bi, sk: (bi, 0, hp)),
        scratch_shapes=[pltpu.VMEM((2, 2, BAND, BAND), F32),
                        pltpu.VMEM((TILE_GROUP + 1, LANE, BAND), ACT)],
    )
    return pl.pallas_call(
        _swa_kernel,
        grid_spec=grid_spec,
        out_shape=jax.ShapeDtypeStruct((b, s, W_BR), ACT),
        compiler_params=_cparams(2),
        name="swa",
    )(sinks, p, p, p, p, tv_c)


def _merge_kernel(bra_ref, brb_ref, brc_ref, g_ref, wb_ref, wo_ref, x_ref, out_ref):
    merged = None
    for i, br_ref in enumerate((bra_ref, brb_ref, brc_ref)):
        term = g_ref[:, i * D_MODEL:(i + 1) * D_MODEL].astype(F32) * _dot(br_ref[...], wb_ref[i])
        merged = term if merged is None else merged + term
    out_ref[...] = x_ref[...] + _dot(merged.astype(ACT), wo_ref[...])


def _merge(bra, brb, brc, p2, w_br, w_out, x2, layer, ts=512):
    n = x2.shape[0]
    row = lambda w: pl.BlockSpec((ts, w), lambda i: (i, 0))
    return pl.pallas_call(
        _merge_kernel,
        grid=(n // ts,),
        in_specs=[row(W_BR), row(W_BR), row(W_BR), row(GATE_W),
                  pl.BlockSpec((None,) + w_br.shape[1:], lambda i: (layer, 0, 0, 0)),
                  pl.BlockSpec((None,) + w_out.shape[1:], lambda i: (layer, 0, 0)),
                  row(D_MODEL)],
        out_specs=row(D_MODEL),
        out_shape=jax.ShapeDtypeStruct(x2.shape, F32),
        compiler_params=_cparams(1),
        name="merge",
    )(bra, brb, brc, p2, w_br, w_out, x2)


def _rel_bucket(dist):
    dist = jnp.maximum(dist, 0)
    max_exact = N_BUCKETS // 2
    log_ratio = jnp.log(jnp.maximum(dist, 1).astype(F32) / max_exact) / math.log(REL_MAX_DIST / max_exact)
    large = jnp.minimum(max_exact + (log_ratio * (N_BUCKETS - max_exact)).astype(jnp.int32), N_BUCKETS - 1)
    return jnp.where(dist < max_exact, dist, large)


def _bias_by_distance(tab, dist, chunk):
    vals = tab.astype(F32)[:, _rel_bucket(dist)] * LOG2E
    vals = jnp.pad(vals, ((0, 0), (chunk, 0)))
    return vals.reshape(tab.shape[0], -1, 1, chunk)


def _gain_rows(qk_g):
    depth = qk_g.shape[0]
    scale = HEAD_DIM ** -0.5 * LOG2E
    one = jnp.ones((depth, HEAD_DIM), F32)
    tile = lambda v, n: jnp.tile(v, (1, n))
    parts = [tile(qk_g[:, 0] * scale, N_HEADS), tile(qk_g[:, 1], N_HEADS), tile(one, 2 * N_HEADS),
             tile(qk_g[:, 2] * scale, N_DIL * N_HEADS), tile(qk_g[:, 3], N_DIL * N_HEADS),
             tile(one, (N_DIL + 1) * N_HEADS),
             tile(qk_g[:, 4] * scale, N_HEADS), tile(qk_g[:, 5], KV_C), tile(one, KV_C + N_HEADS),
             tile(one, GATE_W // HEAD_DIM)]
    return jnp.concatenate(parts, axis=1)


def kernel(x, ln_g, w_in, qk_g, sinks, w_branch, w_out, rel_bias):
    b, s, d = x.shape
    depth = ln_g.shape[0]
    assert d == D_MODEL and s % DIL_PAIRS[-1][0] == 0 and w_in.shape[-1] == C_IN

    tv_a = _bias_by_distance(rel_bias[:N_HEADS], jnp.arange(s), MOBA_BLOCK)
    tv_b = jnp.concatenate(
        [_bias_by_distance(rel_bias[OFF_B + g * N_HEADS:OFF_B + (g + 1) * N_HEADS], dd * jnp.arange(2 * BAND), BAND)
         for g, (_, dd) in enumerate(DIL_PAIRS)], axis=0)
    tv_c = _bias_by_distance(rel_bias[OFF_C:OFF_C + N_HEADS], jnp.arange(2 * BAND), BAND)
    seg = jnp.arange(TN) // HEAD_DIM
    e_mat = (seg[:, None] == seg[None, :]).astype(ACT)
    src = jnp.asarray([c for c, _ in _LAYOUT], jnp.int32)
    fn = jnp.asarray([f for _, f in _LAYOUT[::PAIR]], jnp.int32)
    gains = _gain_rows(qk_g)

    w_in_b = w_in.astype(ACT)
    w_br_b = w_branch.astype(ACT)
    w_out_b = w_out.astype(ACT)

    for l in range(depth):
        p = _project(x, ln_g[:, None, :], w_in_b, gains[l][None, :], e_mat, src, fn, l)
        bra = _moba(p, tv_a)
        brb = _dilated(p, tv_b)
        brc = _swa(p, tv_c, sinks[l] * LOG2E)
        x2 = _merge(bra.reshape(b * s, W_BR), brb.reshape(b * s, W_BR), brc.reshape(b * s, W_BR),
                    p.reshape(b * s, P_W), w_br_b, w_out_b, x.reshape(b * s, d), l)
        x = x2.reshape(b, s, d)
    return x
```

```python
import functools
import math
import operator

import jax
import jax.numpy as jnp
from jax import lax
from jax.experimental import pallas as pl
from jax.experimental.pallas import tpu as pltpu

D_MODEL = 1024
HEAD_DIM = 64
W_BR = D_MODEL // 2
N_HEADS = W_BR // HEAD_DIM
N_PAIRS = N_HEADS // 2
MOBA_BLOCK = 256
MOBA_TOPK = 3
DIL_PAIRS = ((128, 1), (512, 4), (2048, 16))
N_DIL = len(DIL_PAIRS)
KV_C = 2
SWA_WINDOW = 128
N_BUCKETS = 32
REL_MAX_DIST = 2048
OFF_B = N_HEADS
OFF_C = N_HEADS + N_DIL * N_HEADS
EPS = 1e-6
NEG = -1e30
LOG2E = math.log2(math.e)

LANE = 128
BAND = 128
TN = 256
ROWS = 256
TILE_GROUP = 16
MOBA_GROUP = 8
SINGLE_OP_STRIDE = 4
ONES_ROWS = 16
VMEM_LIMIT = 56 * 1024 * 1024

ACT = jnp.bfloat16
F32 = jnp.float32

PLAIN, NORM, SILU, SIGMOID, KNORM_VPLAIN, NORM_D4, PLAIN_D4, NORM_D16, PLAIN_D16 = range(9)
_B_GROUP = lambda g, norm, plain: [(8 + 2 * g, plain), (9 + 2 * g, plain), (14 + 2 * g, norm), (15 + 2 * g, norm),
                                   (20 + 2 * g, plain), (21 + 2 * g, plain)]
_LAYOUT = ([(j, SIGMOID) for j in range(33, 45)]
           + [(0, PLAIN), (1, PLAIN), (2, NORM), (3, NORM), (4, PLAIN), (5, PLAIN), (6, SILU), (7, SILU)]
           + _B_GROUP(0, NORM, PLAIN) + _B_GROUP(1, NORM_D4, PLAIN_D4) + _B_GROUP(2, NORM_D16, PLAIN_D16)
           + [(26, SILU), (27, SILU)]
           + [(28, PLAIN), (29, PLAIN), (31, SILU), (32, SILU), (30, KNORM_VPLAIN)])
C_IN = len(_LAYOUT) * TN
PAIR = 2
N_BLOCKS = len(_LAYOUT)
_LAYOUT = _LAYOUT + [_LAYOUT[-1]] * (-N_BLOCKS % PAIR)
assert N_BLOCKS % PAIR == 0 or _LAYOUT[-1][1] not in [f for _, f in _LAYOUT[:N_BLOCKS - N_BLOCKS % PAIR]]
assert all(_LAYOUT[j][1] == _LAYOUT[j - j % PAIR][1] for j in range(len(_LAYOUT)))
N_STEPS = len(_LAYOUT) // PAIR
P_W = len(_LAYOUT) * TN
COL_GATE, COL_A, COL_B, COL_SGB, COL_C = 0, 24, 40, 76, 80
GATE_W = 3 * D_MODEL


def _cparams(n_axes):
    return pltpu.CompilerParams(dimension_semantics=("arbitrary",) * n_axes, vmem_limit_bytes=VMEM_LIMIT)


def _dot(a, b):
    return jnp.dot(a, b, preferred_element_type=F32)


def _sigmoid(a):
    return 0.5 * jnp.tanh(0.5 * a) + 0.5


def _proj_kernel(src_ref, fn_ref, x_ref, lng_ref, w0_ref, w1_ref, g0_ref, g1_ref, e_ref, o_ref,
                 xn_ref, acc_ref, acc2_ref):
    del src_ref
    w_refs, gain_refs = (w0_ref, w1_ref), (g0_ref, g1_ref)
    assert len(w_refs) == PAIR
    j = pl.program_id(1)
    s = x_ref.shape[1]
    n_chunks = s // ROWS

    @pl.when(j == 0)
    def _():
        def body(i, c):
            rows = pl.ds(pl.multiple_of(i * ROWS, ROWS), ROWS)
            xv = x_ref[0, rows, :]
            ms = jnp.mean(xv * xv, axis=-1, keepdims=True)
            xn_ref[rows, :] = (xv * lax.rsqrt(ms + EPS) * lng_ref[...]).astype(ACT)
            return c
        lax.fori_loop(0, n_chunks, body, 0)

    def head_norm(a, gain):
        ss = _dot((a * a).astype(ACT), e_ref[...])
        return a * lax.rsqrt(ss * (1.0 / HEAD_DIM) + EPS) * gain

    def plain(a, gain):
        return a

    def silu(a, gain):
        return a * _sigmoid(a)

    def sigmoid(a, gain):
        return _sigmoid(a)

    def k_norm_v_plain(a, gain):
        lane = lax.broadcasted_iota(jnp.int32, a.shape, 1)
        return jnp.where(lane < KV_C * HEAD_DIM, head_norm(a, gain), a)

    def chunk(n):
        h, i = divmod(n, n_chunks)
        return _dot(xn_ref[i * ROWS:(i + 1) * ROWS, :], w_refs[h][...])

    fn = fn_ref[j]
    lanes_per_block = TN // LANE

    def direct(code, f, live=PAIR):
        @pl.when(fn == code)
        def _():
            pending = None
            for n in range(live * n_chunks + 1):
                a = chunk(n) if n < live * n_chunks else None
                if pending is not None:
                    h, i = divmod(n - 1, n_chunks)
                    o_ref[0, i * ROWS:(i + 1) * ROWS, h * TN:(h + 1) * TN] = (
                        f(pending, gain_refs[h][...]).astype(o_ref.dtype))
                pending = a
            if live < PAIR:
                o_ref[0, :, live * TN:] = jnp.zeros((s, (PAIR - live) * TN), o_ref.dtype)

    def phase_rows(src_ref, slab, t, d):
        r, m0 = divmod(t * BAND, s // d)
        return src_ref.at[slab][pl.ds(m0 * d + r, BAND, stride=d), :]

    def phase_major(code, f, d):
        @pl.when(fn == code)
        def _():
            for n in range(PAIR * n_chunks):
                h, i = divmod(n, n_chunks)
                a = chunk(n)
                for c in range(lanes_per_block):
                    acc_ref[h * lanes_per_block + c, i * ROWS:(i + 1) * ROWS, :] = a[:, c * LANE:(c + 1) * LANE]
            src_ref, stride = acc_ref, d
            if d > SINGLE_OP_STRIDE:
                assert d == SINGLE_OP_STRIDE * SINGLE_OP_STRIDE
                for slab in range(PAIR * lanes_per_block):
                    for t in range(s // BAND):
                        acc2_ref[slab, t * BAND:(t + 1) * BAND, :] = phase_rows(acc_ref, slab, t, SINGLE_OP_STRIDE)
                src_ref, stride = acc2_ref, SINGLE_OP_STRIDE
            for h in range(PAIR):
                for t in range(s // BAND):
                    a = jnp.concatenate([phase_rows(src_ref, h * lanes_per_block + c, t, stride)
                                         for c in range(lanes_per_block)], axis=1)
                    o_ref[0, t * BAND:(t + 1) * BAND, h * TN:(h + 1) * TN] = (
                        f(a, gain_refs[h][...]).astype(o_ref.dtype))

    direct(PLAIN, plain)
    direct(NORM, head_norm)
    direct(SILU, silu)
    direct(SIGMOID, sigmoid)
    direct(KNORM_VPLAIN, k_norm_v_plain, live=PAIR - (-N_BLOCKS % PAIR))
    phase_major(NORM_D4, head_norm, DIL_PAIRS[1][1])
    phase_major(PLAIN_D4, plain, DIL_PAIRS[1][1])
    phase_major(NORM_D16, head_norm, DIL_PAIRS[2][1])
    phase_major(PLAIN_D16, plain, DIL_PAIRS[2][1])


def _project(x, ln_g, w_in, gain_row, e_mat, src, fn, layer):
    b, s, d = x.shape
    grid_spec = pltpu.PrefetchScalarGridSpec(
        num_scalar_prefetch=2,
        grid=(b, N_STEPS),
        in_specs=[
            pl.BlockSpec((1, s, d), lambda bi, j, src_ref, fn_ref: (bi, 0, 0)),
            pl.BlockSpec((None, 1, d), lambda bi, j, src_ref, fn_ref: (layer, 0, 0)),
            pl.BlockSpec((None, d, TN), lambda bi, j, src_ref, fn_ref: (layer, 0, src_ref[PAIR * j])),
            pl.BlockSpec((None, d, TN), lambda bi, j, src_ref, fn_ref: (layer, 0, src_ref[PAIR * j + 1])),
            pl.BlockSpec((1, TN), lambda bi, j, src_ref, fn_ref: (0, src_ref[PAIR * j])),
            pl.BlockSpec((1, TN), lambda bi, j, src_ref, fn_ref: (0, src_ref[PAIR * j + 1])),
            pl.BlockSpec((TN, TN), lambda bi, j, src_ref, fn_ref: (0, 0)),
        ],
        out_specs=pl.BlockSpec((1, s, PAIR * TN), lambda bi, j, src_ref, fn_ref: (bi, 0, j)),
        scratch_shapes=[pltpu.VMEM((s, d), ACT), pltpu.VMEM((PAIR * TN // LANE, s, LANE), F32),
                        pltpu.VMEM((PAIR * TN // LANE, s, LANE), F32)],
    )
    return pl.pallas_call(
        _proj_kernel,
        grid_spec=grid_spec,
        out_shape=jax.ShapeDtypeStruct((b, s, P_W), ACT),
        compiler_params=_cparams(2),
        name="proj",
    )(src, fn, x, ln_g, w_in, w_in, gain_row, gain_row, e_mat)


def _toeplitz(lo, hi, n):
    x = jnp.concatenate([jnp.broadcast_to(lo, (n, n)), jnp.broadcast_to(hi, (n, n))], axis=1)
    return pltpu.roll(x, 0, 1, stride=1, stride_axis=0)[:, n:]


def _unit_rms_heads(qt):
    halves = []
    for sh in range(2):
        x = qt[HEAD_DIM * sh:HEAD_DIM * (sh + 1), :].astype(F32)
        halves.append(x * lax.rsqrt(jnp.mean(x * x, axis=0, keepdims=True) + EPS))
    return jnp.concatenate(halves, axis=0).astype(qt.dtype)


def _head_rows(qt, lo):
    row = lax.broadcasted_iota(jnp.int32, qt.shape, 0)
    return jnp.where((row >= lo) & (row < lo + HEAD_DIM), qt, jnp.zeros_like(qt))


def _moba_kernel(q_ref, k_ref, v_ref, sg_ref, tv_ref, o_ref,
                 bt_ref, qts_ref, vt_ref, km_ref, sel_ref, m_ref, acc_ref):
    hp = pl.program_id(0)
    b = pl.program_id(1)
    blk = MOBA_BLOCK
    n_blk = k_ref.shape[1] // blk

    def rows_of(i):
        return pl.ds(i * blk if isinstance(i, int) else pl.multiple_of(i * blk, blk), blk)

    @pl.when(b == 0)
    def _():
        kr = lax.broadcasted_iota(jnp.int32, (blk, blk), 0)
        qc = lax.broadcasted_iota(jnp.int32, (blk, blk), 1)
        for sh in range(2):
            def body(o, c, sh=sh):
                t = _toeplitz(tv_ref[2 * hp + sh, o], tv_ref[2 * hp + sh, o + 1], blk)
                bt_ref[sh, o] = jnp.where((qc >= kr) | (o > 0), t, NEG)
                return c
            lax.fori_loop(0, n_blk, body, 0)

    def prep(i, c):
        vt = v_ref[0, rows_of(i), :].T
        for sh in range(2):
            vt_ref[i, sh] = jnp.concatenate(
                [vt[HEAD_DIM * sh:HEAD_DIM * (sh + 1), :], jnp.ones((ONES_ROWS, blk), ACT)], axis=0)
        km_ref[pl.ds(i, 1), :] = jnp.mean(k_ref[0, rows_of(i), :].astype(F32), axis=0, keepdims=True)
        qt = _unit_rms_heads(q_ref[0, rows_of(i), :].T)
        for sh in range(2):
            qts_ref[i, sh] = _head_rows(qt, HEAD_DIM * sh)
        return c
    lax.fori_loop(0, n_blk, prep, 0, unroll=2)

    km = km_ref[...]
    km0 = km.astype(ACT)
    r1 = km - km0.astype(F32)
    km1 = r1.astype(ACT)
    km2 = (r1 - km1.astype(F32)).astype(ACT)
    blk_row = lax.broadcasted_iota(jnp.int32, (n_blk, blk), 0)

    ranked = [(qi, sh) for qi in range(MOBA_TOPK + 1, n_blk) for sh in range(2)]
    gates = [_dot(km0, qts_ref[qi, sh]) + _dot(km1, qts_ref[qi, sh]) + _dot(km2, qts_ref[qi, sh])
             for qi, sh in ranked]
    for (qi, sh), g in zip(ranked, gates):
        g = jnp.where(blk_row < qi, g, NEG)
        sel_rows = []
        for jb in range(qi):
            gj = g[jb:jb + 1, :]
            beats = (g > gj) | ((g == gj) & (blk_row < jb))
            cnt = jnp.sum(beats.astype(F32), axis=0, keepdims=True)
            sel_rows.append(jnp.where(cnt < MOBA_TOPK, 1.0, 0.0))
        sel_rows.append(jnp.zeros((n_blk - qi, blk), F32))
        sel_ref[qi, sh] = jnp.concatenate(sel_rows, axis=0)

    def tiles(off, qis):
        k_blks = [k_ref[0, rows_of(qi - off), :] for qi in qis]
        scores = [_dot(k_blk, qts_ref[qi, sh]) for qi, k_blk in zip(qis, k_blks) for sh in range(2)]
        work = [(qi, sh) for qi in qis for sh in range(2)]
        stats = []
        for (qi, sh), st in zip(work, scores):
            st = st + bt_ref[sh, off]
            tile_max = jnp.max(st, axis=0, keepdims=True)
            if off == 0:
                m_new, alpha = tile_max, None
                p = jnp.exp2(st - m_new)
            elif qi <= MOBA_TOPK:
                m = m_ref[qi, sh]
                m_new = jnp.maximum(m, tile_max)
                p = jnp.exp2(st - m_new)
                alpha = jnp.exp2(m - m_new)
            else:
                sel = sel_ref[qi, sh, pl.ds(qi - off, 1), :] > 0.5
                m = m_ref[qi, sh]
                m_new = jnp.where(sel, jnp.maximum(m, tile_max), m)
                p = jnp.exp2(st - jnp.where(sel, m_new, -NEG))
                alpha = jnp.exp2(m - m_new)
            m_ref[qi, sh] = m_new
            stats.append((alpha, p.astype(ACT)))
        pvs = [_dot(vt_ref[qi - off, sh], p) for (qi, sh), (_, p) in zip(work, stats)]
        for (qi, sh), (alpha, _), pv in zip(work, stats, pvs):
            acc_ref[qi, sh] = pv if off == 0 else alpha * acc_ref[qi, sh] + pv

    for off in range(n_blk):
        qis = list(range(off, n_blk))
        for lo in range(0, len(qis), MOBA_GROUP):
            tiles(off, qis[lo:lo + MOBA_GROUP])

    def finish(qi, c):
        ot = jnp.concatenate([acc_ref[qi, sh, :HEAD_DIM, :] * (1.0 / acc_ref[qi, sh, HEAD_DIM:HEAD_DIM + 1, :])
                              for sh in range(2)], axis=0)
        o_ref[0, rows_of(qi), :] = (ot.T * sg_ref[0, rows_of(qi), :].astype(F32)).astype(o_ref.dtype)
        return c
    lax.fori_loop(0, n_blk, finish, 0, unroll=2)


def _moba(p, tv_a):
    b, s, _ = p.shape
    blk = MOBA_BLOCK
    n_blk = s // blk
    col = lambda c: pl.BlockSpec((1, s, LANE), lambda hp, bi: (bi, 0, COL_A + c * N_PAIRS + hp))
    return pl.pallas_call(
        _moba_kernel,
        grid=(N_PAIRS, b),
        in_specs=[col(0), col(1), col(2), col(3), pl.BlockSpec(tv_a.shape, lambda hp, bi: (0, 0, 0, 0))],
        out_specs=pl.BlockSpec((1, s, LANE), lambda hp, bi: (bi, 0, hp)),
        out_shape=jax.ShapeDtypeStruct((b, s, W_BR), ACT),
        scratch_shapes=[
            pltpu.VMEM((2, n_blk, blk, blk), F32),
            pltpu.VMEM((n_blk, 2, LANE, blk), ACT),
            pltpu.VMEM((n_blk, 2, HEAD_DIM + ONES_ROWS, blk), ACT),
            pltpu.VMEM((n_blk, LANE), F32),
            pltpu.VMEM((n_blk, 2, n_blk, blk), F32),
            pltpu.VMEM((n_blk, 2, 1, blk), F32),
            pltpu.VMEM((n_blk, 2, HEAD_DIM + ONES_ROWS, blk), F32),
        ],
        compiler_params=_cparams(2),
        name="moba",
    )(p, p, p, p, tv_a)


def _band_attend(items):
    def fuse(blocks):
        if len(blocks) > 1 and all(valid is None for _, _, _, valid in blocks):
            ks, vts, biases, _ = zip(*blocks)
            return [(jnp.concatenate(ks, axis=0), jnp.concatenate(vts, axis=1), jnp.concatenate(biases, axis=0), None)]
        return blocks

    items = [(qs, fuse(blocks), sink) for qs, blocks, sink in items]
    scores = [[_dot(k, qs) for k, _, _, _ in blocks] for qs, blocks, _ in items]
    probs = []
    for (_, blocks, sink), sc in zip(items, scores):
        ss = [s + bias if valid is None else jnp.where(valid, s + bias, NEG)
              for s, (_, _, bias, valid) in zip(sc, blocks)]
        m = functools.reduce(jnp.maximum, [jnp.max(s, axis=0, keepdims=True) for s in ss])
        if sink is not None:
            m = jnp.maximum(m, sink)
        probs.append((m, [jnp.exp2(s - m).astype(ACT) for s in ss]))
    outs = []
    for (_, blocks, sink), (m, ps) in zip(items, probs):
        n_v = blocks[0][1].shape[0]
        acc = functools.reduce(operator.add, [
            _dot(jnp.concatenate([vt, jnp.ones((ONES_ROWS, vt.shape[1]), ACT)], axis=0), p)
            for (_, vt, _, _), p in zip(blocks, ps)])
        l = acc[n_v:n_v + 1, :]
        if sink is not None:
            l = l + jnp.exp2(sink - m)
        outs.append((acc[:n_v, :] * (1.0 / l), m + jnp.log2(l)))
    return outs


def _band_bias(tv_ref, head, prev_includes_window_edge):
    kr = lax.broadcasted_iota(jnp.int32, (BAND, BAND), 0)
    qc = lax.broadcasted_iota(jnp.int32, (BAND, BAND), 1)
    own, prev = [_toeplitz(tv_ref[head, which], tv_ref[head, which + 1], BAND) for which in range(2)]
    prev_mask = (qc <= kr) if prev_includes_window_edge else (qc < kr)
    return jnp.where(qc >= kr, own, NEG), jnp.where(prev_mask, prev, NEG)


def _dil_kernel(q0_ref, k0_ref, v0_ref, q1_ref, k1_ref, v1_ref, q2_ref, k2_ref, v2_ref, sg_ref, tv_ref,
                o_ref, og_ref, lg_ref, bt_ref):
    hp = pl.program_id(0)
    b = pl.program_id(1)
    s = q0_ref.shape[1]
    n_tiles = s // BAND

    @pl.when(b == 0)
    def _():
        for g in range(N_DIL):
            for sh in range(2):
                own, prev = _band_bias(tv_ref, g * N_HEADS + 2 * hp + sh, True)
                bt_ref[g, sh, 0] = own
                bt_ref[g, sh, 1] = prev

    groups = ((q0_ref, k0_ref, v0_ref), (q1_ref, k1_ref, v1_ref), (q2_ref, k2_ref, v2_ref))

    for g, (q_ref, k_ref, v_ref) in enumerate(groups):
        d = DIL_PAIRS[g][1]
        per_phase = n_tiles // d
        has_prev = per_phase > 1
        assert per_phase % TILE_GROUP == 0 or TILE_GROUP % per_phase == 0

        def body(i, c, g=g, d=d, per_phase=per_phase, has_prev=has_prev, q_ref=q_ref, k_ref=k_ref, v_ref=v_ref):
            items, dsts = [], []
            k_own = vt_own = None
            for n in range(TILE_GROUP):
                u = i * TILE_GROUP + n
                rows = pl.ds(pl.multiple_of(u * BAND, BAND), BAND)
                t = lax.rem(u, per_phase)
                r = lax.div(u, per_phase)
                qt = _unit_rms_heads(q_ref[0, rows, :].T)
                k_prev, vt_prev = k_own, vt_own
                k_own = k_ref[0, rows, :]
                vt_own = v_ref[0, rows, :].T
                if has_prev:
                    phase_start_known = TILE_GROUP % per_phase == 0
                    prev_exists = (t > 0) if (n == 0 and not phase_start_known) else None
                    if phase_start_known and n % per_phase == 0:
                        k_prev = None
                    elif n == 0:
                        prow = pl.ds(pl.multiple_of(jnp.maximum(u - 1, 0) * BAND, BAND), BAND)
                        k_prev = k_ref[0, prow, :]
                        vt_prev = v_ref[0, prow, :].T
                for sh in range(2):
                    hrows = slice(HEAD_DIM * sh, HEAD_DIM * (sh + 1))
                    blocks = [(k_own, vt_own[hrows, :], bt_ref[g, sh, 0], None)]
                    if has_prev and k_prev is not None:
                        blocks.append((k_prev, vt_prev[hrows, :], bt_ref[g, sh, 1], prev_exists))
                    items.append((_head_rows(qt, HEAD_DIM * sh), blocks, None))
                dsts.append(rows if d == 1 else pl.ds(t * (BAND * d) + r, BAND, stride=d))
            res = _band_attend(items)
            for n, dst in enumerate(dsts):
                (o0, lse0), (o1, lse1) = res[2 * n], res[2 * n + 1]
                og_ref.at[g][dst, :] = jnp.concatenate([o0, o1], axis=0).T
                lg_ref.at[g][dst, :] = jnp.concatenate(
                    [jnp.broadcast_to(lse0, (HEAD_DIM, BAND)), jnp.broadcast_to(lse1, (HEAD_DIM, BAND))], axis=0).T
            return c

        lax.fori_loop(0, n_tiles // TILE_GROUP, body, 0)

    def merge(i, c):
        rows = pl.ds(pl.multiple_of(i * ROWS, ROWS), ROWS)
        l0, l1, l2 = lg_ref[0, rows, :], lg_ref[1, rows, :], lg_ref[2, rows, :]
        m = jnp.maximum(l0, jnp.maximum(l1, l2))
        e0, e1, e2 = jnp.exp2(l0 - m), jnp.exp2(l1 - m), jnp.exp2(l2 - m)
        ob = (e0 * og_ref[0, rows, :] + e1 * og_ref[1, rows, :] + e2 * og_ref[2, rows, :]) / (e0 + e1 + e2)
        o_ref[0, rows, :] = (ob * sg_ref[0, rows, :].astype(F32)).astype(o_ref.dtype)
        return c

    lax.fori_loop(0, s // ROWS, merge, 0)


def _dilated(p, tv_b):
    b, s, _ = p.shape
    col = lambda c: pl.BlockSpec((1, s, LANE), lambda hp, bi: (bi, 0, c + hp))
    qkv = [col(COL_B + 3 * N_PAIRS * g + N_PAIRS * i) for g in range(N_DIL) for i in range(3)]
    return pl.pallas_call(
        _dil_kernel,
        grid=(N_PAIRS, b),
        in_specs=qkv + [col(COL_SGB), pl.BlockSpec(tv_b.shape, lambda hp, bi: (0, 0, 0, 0))],
        out_specs=pl.BlockSpec((1, s, LANE), lambda hp, bi: (bi, 0, hp)),
        out_shape=jax.ShapeDtypeStruct((b, s, W_BR), ACT),
        scratch_shapes=[
            pltpu.VMEM((N_DIL, s, LANE), F32),
            pltpu.VMEM((N_DIL, s, LANE), F32),
            pltpu.VMEM((N_DIL, 2, 2, BAND, BAND), F32),
        ],
        compiler_params=_cparams(2),
        name="dilated",
    )(*([p] * 10), tv_b)


def _swa_kernel(sink_ref, q_ref, sg_ref, k_ref, v_ref, tv_ref, o_ref, bt_ref, vts_ref):
    hp = pl.program_id(0)
    b = pl.program_id(1)
    n_tiles = q_ref.shape[1] // BAND
    pairs_per_kv = N_PAIRS // KV_C

    @pl.when(b == 0)
    def _():
        for sh in range(2):
            own, prev = _band_bias(tv_ref, 2 * hp + sh, False)
            bt_ref[sh, 0] = own
            bt_ref[sh, 1] = prev

    kv_lo = HEAD_DIM * (hp // pairs_per_kv)
    row = lax.broadcasted_iota(jnp.int32, (LANE, BAND), 0)
    kv_rows = (row >= kv_lo) & (row < kv_lo + HEAD_DIM)

    def kv_vt(rows, slot):
        vts_ref[slot] = v_ref[0, rows, :].T
        return vts_ref[slot, pl.ds(pl.multiple_of(kv_lo, HEAD_DIM), HEAD_DIM), :]

    def body(i, c):
        items, dsts = [], []
        zero = jnp.zeros((LANE, BAND), ACT)
        k_own = vt_own = None
        for n in range(TILE_GROUP):
            t = i * TILE_GROUP + n
            rows = pl.ds(pl.multiple_of(t * BAND, BAND), BAND)
            qt = _unit_rms_heads(q_ref[0, rows, :].T)
            k_prev, vt_prev = k_own, vt_own
            k_own = k_ref[0, rows, :]
            vt_own = kv_vt(rows, n)
            if n == 0:
                prow = pl.ds(pl.multiple_of(jnp.maximum(t - 1, 0) * BAND, BAND), BAND)
                k_prev = k_ref[0, prow, :]
                vt_prev = kv_vt(prow, TILE_GROUP)
            prev_exists = (t > 0) if n == 0 else None
            for sh in range(2):
                qh = qt[HEAD_DIM * sh:HEAD_DIM * (sh + 1), :]
                qs = jnp.where(kv_rows, jnp.concatenate([qh, qh], axis=0), zero)
                items.append((qs, [(k_own, vt_own, bt_ref[sh, 0], None), (k_prev, vt_prev, bt_ref[sh, 1], prev_exists)],
                              sink_ref[2 * hp + sh]))
            dsts.append(rows)
        res = _band_attend(items)
        for n, rows in enumerate(dsts):
            ot = jnp.concatenate([o for o, _ in res[2 * n:2 * n + 2]], axis=0).T
            o_ref[0, rows, :] = (ot * sg_ref[0, rows, :].astype(F32)).astype(o_ref.dtype)
        return c

    lax.fori_loop(0, n_tiles // TILE_GROUP, body, 0)


def _swa(p, tv_c, sinks):
    b, s, _ = p.shape
    grid_spec = pltpu.PrefetchScalarGridSpec(
        num_scalar_prefetch=1,
        grid=(N_PAIRS, b),
        in_specs=[
            pl.BlockSpec((1, s, LANE), lambda hp, bi, sk: (bi, 0, COL_C + hp)),
            pl.BlockSpec((1, s, LANE), lambda hp, bi, sk: (bi, 0, COL_C + N_PAIRS + hp)),
            pl.BlockSpec((1, s, LANE), lambda hp, bi, sk: (bi, 0, COL_C + 2 * N_PAIRS)),
            pl.BlockSpec((1, s, LANE), lambda hp, bi, sk: (bi, 0, COL_C + 2 * N_PAIRS + 1)),
            pl.BlockSpec(tv_c.shape, lambda hp, bi, sk: (0, 0, 0, 0)),
        ],
        out_specs=pl.BlockSpec((1, s, LANE), lambda hp, bi, sk: (bi, 0, hp)),
        scratch_shapes=[pltpu.VMEM((2, 2, BAND, BAND), F32),
                        pltpu.VMEM((TILE_GROUP + 1, LANE, BAND), ACT)],
    )
    return pl.pallas_call(
        _swa_kernel,
        grid_spec=grid_spec,
        out_shape=jax.ShapeDtypeStruct((b, s, W_BR), ACT),
        compiler_params=_cparams(2),
        name="swa",
    )(sinks, p, p, p, p, tv_c)


def _merge_kernel(bra_ref, brb_ref, brc_ref, g_ref, wb_ref, wo_ref, x_ref, out_ref):
    merged = None
    for i, br_ref in enumerate((bra_ref, brb_ref, brc_ref)):
        term = g_ref[:, i * D_MODEL:(i + 1) * D_MODEL].astype(F32) * _dot(br_ref[...], wb_ref[i])
        merged = term if merged is None else merged + term
    out_ref[...] = x_ref[...] + _dot(merged.astype(ACT), wo_ref[...])


def _merge(bra, brb, brc, p2, w_br, w_out, x2, layer, ts=512):
    n = x2.shape[0]
    row = lambda w: pl.BlockSpec((ts, w), lambda i: (i, 0))
    return pl.pallas_call(
        _merge_kernel,
        grid=(n // ts,),
        in_specs=[row(W_BR), row(W_BR), row(W_BR), row(GATE_W),
                  pl.BlockSpec((None,) + w_br.shape[1:], lambda i: (layer, 0, 0, 0)),
                  pl.BlockSpec((None,) + w_out.shape[1:], lambda i: (layer, 0, 0)),
                  row(D_MODEL)],
        out_specs=row(D_MODEL),
        out_shape=jax.ShapeDtypeStruct(x2.shape, F32),
        compiler_params=_cparams(1),
        name="merge",
    )(bra, brb, brc, p2, w_br, w_out, x2)


def _rel_bucket(dist):
    dist = jnp.maximum(dist, 0)
    max_exact = N_BUCKETS // 2
    log_ratio = jnp.log(jnp.maximum(dist, 1).astype(F32) / max_exact) / math.log(REL_MAX_DIST / max_exact)
    large = jnp.minimum(max_exact + (log_ratio * (N_BUCKETS - max_exact)).astype(jnp.int32), N_BUCKETS - 1)
    return jnp.where(dist < max_exact, dist, large)


def _bias_by_distance(tab, dist, chunk):
    vals = tab.astype(F32)[:, _rel_bucket(dist)] * LOG2E
    vals = jnp.pad(vals, ((0, 0), (chunk, 0)))
    return vals.reshape(tab.shape[0], -1, 1, chunk)


def _gain_rows(qk_g):
    depth = qk_g.shape[0]
    scale = HEAD_DIM ** -0.5 * LOG2E
    one = jnp.ones((depth, HEAD_DIM), F32)
    tile = lambda v, n: jnp.tile(v, (1, n))
    parts = [tile(one, N_HEADS), tile(qk_g[:, 1] * qk_g[:, 0] * scale, N_HEADS), tile(one, 2 * N_HEADS),
             tile(one, N_DIL * N_HEADS), tile(qk_g[:, 3] * qk_g[:, 2] * scale, N_DIL * N_HEADS),
             tile(one, (N_DIL + 1) * N_HEADS),
             tile(one, N_HEADS), tile(qk_g[:, 5] * qk_g[:, 4] * scale, KV_C), tile(one, KV_C + N_HEADS),
             tile(one, GATE_W // HEAD_DIM)]
    return jnp.concatenate(parts, axis=1)


def kernel(x, ln_g, w_in, qk_g, sinks, w_branch, w_out, rel_bias):
    b, s, d = x.shape
    depth = ln_g.shape[0]
    assert d == D_MODEL and s % DIL_PAIRS[-1][0] == 0 and w_in.shape[-1] == C_IN

    tv_a = _bias_by_distance(rel_bias[:N_HEADS], jnp.arange(s), MOBA_BLOCK)
    tv_b = jnp.concatenate(
        [_bias_by_distance(rel_bias[OFF_B + g * N_HEADS:OFF_B + (g + 1) * N_HEADS], dd * jnp.arange(2 * BAND), BAND)
         for g, (_, dd) in enumerate(DIL_PAIRS)], axis=0)
    tv_c = _bias_by_distance(rel_bias[OFF_C:OFF_C + N_HEADS], jnp.arange(2 * BAND), BAND)
    seg = jnp.arange(TN) // HEAD_DIM
    e_mat = (seg[:, None] == seg[None, :]).astype(ACT)
    src = jnp.asarray([c for c, _ in _LAYOUT], jnp.int32)
    fn = jnp.asarray([f for _, f in _LAYOUT[::PAIR]], jnp.int32)
    gains = _gain_rows(qk_g)

    w_in_b = w_in.astype(ACT)
    w_br_b = w_branch.astype(ACT)
    w_out_b = w_out.astype(ACT)

    for l in range(depth):
        p = _project(x, ln_g[:, None, :], w_in_b, gains[l][None, :], e_mat, src, fn, l)
        bra = _moba(p, tv_a)
        brb = _dilated(p, tv_b)
        brc = _swa(p, tv_c, sinks[l] * LOG2E)
        x2 = _merge(bra.reshape(b * s, W_BR), brb.reshape(b * s, W_BR), brc.reshape(b * s, W_BR),
                    p.reshape(b * s, P_W), w_br_b, w_out_b, x.reshape(b * s, d), l)
        x = x2.reshape(b, s, d)
    return x
```

```python
import functools
import math
import operator

import jax
import jax.numpy as jnp
from jax import lax
from jax.experimental import pallas as pl
from jax.experimental.pallas import tpu as pltpu

D_MODEL = 1024
HEAD_DIM = 64
W_BR = D_MODEL // 2
N_HEADS = W_BR // HEAD_DIM
N_PAIRS = N_HEADS // 2
MOBA_BLOCK = 256
MOBA_TOPK = 3
DIL_PAIRS = ((128, 1), (512, 4), (2048, 16))
N_DIL = len(DIL_PAIRS)
KV_C = 2
SWA_WINDOW = 128
N_BUCKETS = 32
REL_MAX_DIST = 2048
OFF_B = N_HEADS
OFF_C = N_HEADS + N_DIL * N_HEADS
EPS = 1e-6
NEG = -1e30
LOG2E = math.log2(math.e)

LANE = 128
BAND = 128
TN = 256
ROWS = 256
TILE_GROUP = 16
MOBA_GROUP = 8
SINGLE_OP_STRIDE = 4
ONES_ROWS = 16
VMEM_LIMIT = 56 * 1024 * 1024

ACT = jnp.bfloat16
F32 = jnp.float32

PLAIN, NORM, SILU, SIGMOID, KNORM_VPLAIN, NORM_D4, PLAIN_D4, NORM_D16, PLAIN_D16, PAD = range(10)
_B_GROUP = lambda g, norm, plain: [(8 + 2 * g, plain), (9 + 2 * g, plain), (14 + 2 * g, norm), (15 + 2 * g, norm),
                                   (20 + 2 * g, plain), (21 + 2 * g, plain)]
_LAYOUT = ([(j, SIGMOID) for j in range(33, 45)]
           + [(0, PLAIN), (1, PLAIN), (2, NORM), (3, NORM), (4, PLAIN), (5, PLAIN), (6, SILU), (7, SILU)]
           + _B_GROUP(0, NORM, PLAIN) + _B_GROUP(1, NORM_D4, PLAIN_D4) + _B_GROUP(2, NORM_D16, PLAIN_D16)
           + [(26, SILU), (27, SILU)]
           + [(28, PLAIN), (29, PLAIN), (31, SILU), (32, SILU), (30, KNORM_VPLAIN)])
C_IN = len(_LAYOUT) * TN
PAIR = 2
SUB = 2
_LAYOUT = _LAYOUT + [(_LAYOUT[-1][0], KNORM_VPLAIN)] * (-len(_LAYOUT) % PAIR)
_LAYOUT = _LAYOUT + [(_LAYOUT[-1][0], PAD)] * (-len(_LAYOUT) % (SUB * PAIR))
assert all(_LAYOUT[j][1] == _LAYOUT[j - j % PAIR][1] for j in range(len(_LAYOUT)))
assert [f for _, f in _LAYOUT].count(KNORM_VPLAIN) == PAIR
N_STEPS = len(_LAYOUT) // (SUB * PAIR)
P_W = len(_LAYOUT) * TN
COL_GATE, COL_A, COL_B, COL_SGB, COL_C = 0, 24, 40, 76, 80
GATE_W = 3 * D_MODEL


def _cparams(n_axes):
    return pltpu.CompilerParams(dimension_semantics=("arbitrary",) * n_axes, vmem_limit_bytes=VMEM_LIMIT)


def _dot(a, b):
    return jnp.dot(a, b, preferred_element_type=F32)


def _sigmoid(a):
    return 0.5 * jnp.tanh(0.5 * a) + 0.5


def _proj_kernel(src_ref, fn_ref, x_ref, lng_ref, *refs):
    del src_ref
    n_in = SUB * PAIR
    w_all, gain_all = refs[:n_in], refs[n_in:2 * n_in]
    e_ref, o_ref, xn_ref, acc_ref, acc2_ref = refs[2 * n_in:]
    j = pl.program_id(1)
    s = x_ref.shape[1]
    n_chunks = s // ROWS
    lanes_per_block = TN // LANE

    @pl.when(j == 0)
    def _():
        def body(i, c):
            rows = pl.ds(pl.multiple_of(i * ROWS, ROWS), ROWS)
            xv = x_ref[0, rows, :]
            ms = jnp.mean(xv * xv, axis=-1, keepdims=True)
            xn_ref[rows, :] = (xv * lax.rsqrt(ms + EPS) * lng_ref[...]).astype(ACT)
            return c
        lax.fori_loop(0, n_chunks, body, 0)

    def head_norm(a, gain):
        ss = _dot((a * a).astype(ACT), e_ref[...])
        return a * lax.rsqrt(ss * (1.0 / HEAD_DIM) + EPS) * gain

    def plain(a, gain):
        return a

    def silu(a, gain):
        return a * _sigmoid(a)

    def sigmoid(a, gain):
        return _sigmoid(a)

    def k_norm_v_plain(a, gain):
        lane = lax.broadcasted_iota(jnp.int32, a.shape, 1)
        return jnp.where(lane < KV_C * HEAD_DIM, head_norm(a, gain), a)

    def phase_rows(src_ref, slab, t, d):
        r, m0 = divmod(t * BAND, s // d)
        return src_ref.at[slab][pl.ds(m0 * d + r, BAND, stride=d), :]

    def pair_step(sub):
        w_refs, gain_refs = w_all[sub * PAIR:(sub + 1) * PAIR], gain_all[sub * PAIR:(sub + 1) * PAIR]
        fn = fn_ref[j * SUB + sub]
        col0 = sub * PAIR * TN

        def chunk(n):
            h, i = divmod(n, n_chunks)
            return _dot(xn_ref[i * ROWS:(i + 1) * ROWS, :], w_refs[h][...])

        def out_cols(h):
            return slice(col0 + h * TN, col0 + (h + 1) * TN)

        def direct(code, f, live=PAIR):
            @pl.when(fn == code)
            def _():
                pending = None
                for n in range(live * n_chunks + 1):
                    a = chunk(n) if n < live * n_chunks else None
                    if pending is not None:
                        h, i = divmod(n - 1, n_chunks)
                        o_ref[0, i * ROWS:(i + 1) * ROWS, out_cols(h)] = (
                            f(pending, gain_refs[h][...]).astype(o_ref.dtype))
                    pending = a
                if live < PAIR:
                    o_ref[0, :, col0 + live * TN:col0 + PAIR * TN] = jnp.zeros((s, (PAIR - live) * TN), o_ref.dtype)

        def phase_major(code, f, d):
            @pl.when(fn == code)
            def _():
                for n in range(PAIR * n_chunks):
                    h, i = divmod(n, n_chunks)
                    a = chunk(n)
                    for c in range(lanes_per_block):
                        acc_ref[h * lanes_per_block + c, i * ROWS:(i + 1) * ROWS, :] = a[:, c * LANE:(c + 1) * LANE]
                src_ref, stride = acc_ref, d
                if d > SINGLE_OP_STRIDE:
                    assert d == SINGLE_OP_STRIDE * SINGLE_OP_STRIDE
                    for slab in range(PAIR * lanes_per_block):
                        for t in range(s // BAND):
                            acc2_ref[slab, t * BAND:(t + 1) * BAND, :] = phase_rows(acc_ref, slab, t, SINGLE_OP_STRIDE)
                    src_ref, stride = acc2_ref, SINGLE_OP_STRIDE
                for h in range(PAIR):
                    for t in range(s // BAND):
                        a = jnp.concatenate([phase_rows(src_ref, h * lanes_per_block + c, t, stride)
                                             for c in range(lanes_per_block)], axis=1)
                        o_ref[0, t * BAND:(t + 1) * BAND, out_cols(h)] = f(a, gain_refs[h][...]).astype(o_ref.dtype)

        direct(PLAIN, plain)
        direct(NORM, head_norm)
        direct(SILU, silu)
        direct(SIGMOID, sigmoid)
        direct(KNORM_VPLAIN, k_norm_v_plain, live=1)
        direct(PAD, plain, live=0)
        phase_major(NORM_D4, head_norm, DIL_PAIRS[1][1])
        phase_major(PLAIN_D4, plain, DIL_PAIRS[1][1])
        phase_major(NORM_D16, head_norm, DIL_PAIRS[2][1])
        phase_major(PLAIN_D16, plain, DIL_PAIRS[2][1])

    for sub in range(SUB):
        pair_step(sub)


def _w_index(bi, j, src_ref, fn_ref, *, layer, slot):
    return layer, 0, src_ref[SUB * PAIR * j + slot]


def _gain_index(bi, j, src_ref, fn_ref, *, slot):
    return 0, src_ref[SUB * PAIR * j + slot]


def _project(x, ln_g, w_in, gain_row, e_mat, src, fn, layer):
    b, s, d = x.shape
    n_in = SUB * PAIR
    grid_spec = pltpu.PrefetchScalarGridSpec(
        num_scalar_prefetch=2,
        grid=(b, N_STEPS),
        in_specs=[
            pl.BlockSpec((1, s, d), lambda bi, j, src_ref, fn_ref: (bi, 0, 0)),
            pl.BlockSpec((None, 1, d), lambda bi, j, src_ref, fn_ref: (layer, 0, 0)),
        ] + [pl.BlockSpec((None, d, TN), functools.partial(_w_index, layer=layer, slot=n)) for n in range(n_in)]
          + [pl.BlockSpec((1, TN), functools.partial(_gain_index, slot=n)) for n in range(n_in)]
          + [pl.BlockSpec((TN, TN), lambda bi, j, src_ref, fn_ref: (0, 0))],
        out_specs=pl.BlockSpec((1, s, n_in * TN), lambda bi, j, src_ref, fn_ref: (bi, 0, j)),
        scratch_shapes=[pltpu.VMEM((s, d), ACT), pltpu.VMEM((PAIR * TN // LANE, s, LANE), F32),
                        pltpu.VMEM((PAIR * TN // LANE, s, LANE), F32)],
    )
    return pl.pallas_call(
        _proj_kernel,
        grid_spec=grid_spec,
        out_shape=jax.ShapeDtypeStruct((b, s, P_W), ACT),
        compiler_params=_cparams(2),
        name="proj",
    )(src, fn, x, ln_g, *([w_in] * n_in), *([gain_row] * n_in), e_mat)


def _toeplitz(lo, hi, n):
    x = jnp.concatenate([jnp.broadcast_to(lo, (n, n)), jnp.broadcast_to(hi, (n, n))], axis=1)
    return pltpu.roll(x, 0, 1, stride=1, stride_axis=0)[:, n:]


def _unit_rms_heads(qt):
    halves = []
    for sh in range(2):
        x = qt[HEAD_DIM * sh:HEAD_DIM * (sh + 1), :].astype(F32)
        halves.append(x * lax.rsqrt(jnp.mean(x * x, axis=0, keepdims=True) + EPS))
    return jnp.concatenate(halves, axis=0).astype(qt.dtype)


def _head_rows(qt, lo):
    row = lax.broadcasted_iota(jnp.int32, qt.shape, 0)
    return jnp.where((row >= lo) & (row < lo + HEAD_DIM), qt, jnp.zeros_like(qt))


def _moba_kernel(q_ref, k_ref, v_ref, sg_ref, tv_ref, o_ref,
                 bt_ref, qts_ref, vt_ref, km_ref, sel_ref, m_ref, acc_ref):
    hp = pl.program_id(0)
    b = pl.program_id(1)
    blk = MOBA_BLOCK
    n_blk = k_ref.shape[1] // blk

    def rows_of(i):
        return pl.ds(i * blk if isinstance(i, int) else pl.multiple_of(i * blk, blk), blk)

    @pl.when(b == 0)
    def _():
        kr = lax.broadcasted_iota(jnp.int32, (blk, blk), 0)
        qc = lax.broadcasted_iota(jnp.int32, (blk, blk), 1)
        for sh in range(2):
            def body(o, c, sh=sh):
                t = _toeplitz(tv_ref[2 * hp + sh, o], tv_ref[2 * hp + sh, o + 1], blk)
                bt_ref[sh, o] = jnp.where((qc >= kr) | (o > 0), t, NEG)
                return c
            lax.fori_loop(0, n_blk, body, 0)

    def prep(i, c):
        vt = v_ref[0, rows_of(i), :].T
        for sh in range(2):
            vt_ref[i, sh] = jnp.concatenate(
                [vt[HEAD_DIM * sh:HEAD_DIM * (sh + 1), :], jnp.ones((ONES_ROWS, blk), ACT)], axis=0)
        km_ref[pl.ds(i, 1), :] = jnp.mean(k_ref[0, rows_of(i), :].astype(F32), axis=0, keepdims=True)
        qt = _unit_rms_heads(q_ref[0, rows_of(i), :].T)
        for sh in range(2):
            qts_ref[i, sh] = _head_rows(qt, HEAD_DIM * sh)
        return c
    lax.fori_loop(0, n_blk, prep, 0, unroll=2)

    km = km_ref[...]
    km0 = km.astype(ACT)
    r1 = km - km0.astype(F32)
    km1 = r1.astype(ACT)
    km2 = (r1 - km1.astype(F32)).astype(ACT)
    blk_row = lax.broadcasted_iota(jnp.int32, (n_blk, blk), 0)

    ranked = [(qi, sh) for qi in range(MOBA_TOPK + 1, n_blk) for sh in range(2)]
    gates = [_dot(km0, qts_ref[qi, sh]) + _dot(km1, qts_ref[qi, sh]) + _dot(km2, qts_ref[qi, sh])
             for qi, sh in ranked]
    for (qi, sh), g in zip(ranked, gates):
        g = jnp.where(blk_row < qi, g, NEG)
        sel_rows = []
        for jb in range(qi):
            gj = g[jb:jb + 1, :]
            beats = (g > gj) | ((g == gj) & (blk_row < jb))
            cnt = jnp.sum(beats.astype(F32), axis=0, keepdims=True)
            sel_rows.append(jnp.where(cnt < MOBA_TOPK, 1.0, 0.0))
        sel_rows.append(jnp.zeros((n_blk - qi, blk), F32))
        sel_ref[qi, sh] = jnp.concatenate(sel_rows, axis=0)

    def tiles(off, qis):
        k_blks = [k_ref[0, rows_of(qi - off), :] for qi in qis]
        scores = [_dot(k_blk, qts_ref[qi, sh]) for qi, k_blk in zip(qis, k_blks) for sh in range(2)]
        work = [(qi, sh) for qi in qis for sh in range(2)]
        stats = []
        for (qi, sh), st in zip(work, scores):
            st = st + bt_ref[sh, off]
            tile_max = jnp.max(st, axis=0, keepdims=True)
            if off == 0:
                m_new, alpha = tile_max, None
                p = jnp.exp2(st - m_new)
            elif qi <= MOBA_TOPK:
                m = m_ref[qi, sh]
                m_new = jnp.maximum(m, tile_max)
                p = jnp.exp2(st - m_new)
                alpha = jnp.exp2(m - m_new)
            else:
                sel = sel_ref[qi, sh, pl.ds(qi - off, 1), :] > 0.5
                m = m_ref[qi, sh]
                m_new = jnp.where(sel, jnp.maximum(m, tile_max), m)
                p = jnp.exp2(st - jnp.where(sel, m_new, -NEG))
                alpha = jnp.exp2(m - m_new)
            m_ref[qi, sh] = m_new
            stats.append((alpha, p.astype(ACT)))
        pvs = [_dot(vt_ref[qi - off, sh], p) for (qi, sh), (_, p) in zip(work, stats)]
        for (qi, sh), (alpha, _), pv in zip(work, stats, pvs):
            acc_ref[qi, sh] = pv if off == 0 else alpha * acc_ref[qi, sh] + pv

    for off in range(n_blk):
        qis = list(range(off, n_blk))
        for lo in range(0, len(qis), MOBA_GROUP):
            tiles(off, qis[lo:lo + MOBA_GROUP])

    def finish(qi, c):
        ot = jnp.concatenate([acc_ref[qi, sh, :HEAD_DIM, :] * (1.0 / acc_ref[qi, sh, HEAD_DIM:HEAD_DIM + 1, :])
                              for sh in range(2)], axis=0)
        o_ref[0, rows_of(qi), :] = (ot.T * sg_ref[0, rows_of(qi), :].astype(F32)).astype(o_ref.dtype)
        return c
    lax.fori_loop(0, n_blk, finish, 0, unroll=2)


def _moba(p, tv_a):
    b, s, _ = p.shape
    blk = MOBA_BLOCK
    n_blk = s // blk
    col = lambda c: pl.BlockSpec((1, s, LANE), lambda hp, bi: (bi, 0, COL_A + c * N_PAIRS + hp))
    return pl.pallas_call(
        _moba_kernel,
        grid=(N_PAIRS, b),
        in_specs=[col(0), col(1), col(2), col(3), pl.BlockSpec(tv_a.shape, lambda hp, bi: (0, 0, 0, 0))],
        out_specs=pl.BlockSpec((1, s, LANE), lambda hp, bi: (bi, 0, hp)),
        out_shape=jax.ShapeDtypeStruct((b, s, W_BR), ACT),
        scratch_shapes=[
            pltpu.VMEM((2, n_blk, blk, blk), F32),
            pltpu.VMEM((n_blk, 2, LANE, blk), ACT),
            pltpu.VMEM((n_blk, 2, HEAD_DIM + ONES_ROWS, blk), ACT),
            pltpu.VMEM((n_blk, LANE), F32),
            pltpu.VMEM((n_blk, 2, n_blk, blk), F32),
            pltpu.VMEM((n_blk, 2, 1, blk), F32),
            pltpu.VMEM((n_blk, 2, HEAD_DIM + ONES_ROWS, blk), F32),
        ],
        compiler_params=_cparams(2),
        name="moba",
    )(p, p, p, p, tv_a)


def _band_attend(items):
    def fuse(blocks):
        if len(blocks) > 1 and all(valid is None for _, _, _, valid in blocks):
            ks, vts, biases, _ = zip(*blocks)
            return [(jnp.concatenate(ks, axis=0), jnp.concatenate(vts, axis=1), jnp.concatenate(biases, axis=0), None)]
        return blocks

    items = [(qs, fuse(blocks), sink) for qs, blocks, sink in items]
    scores = [[_dot(k, qs) for k, _, _, _ in blocks] for qs, blocks, _ in items]
    probs = []
    for (_, blocks, sink), sc in zip(items, scores):
        ss = [s + bias if valid is None else jnp.where(valid, s + bias, NEG)
              for s, (_, _, bias, valid) in zip(sc, blocks)]
        m = functools.reduce(jnp.maximum, [jnp.max(s, axis=0, keepdims=True) for s in ss])
        if sink is not None:
            m = jnp.maximum(m, sink)
        probs.append((m, [jnp.exp2(s - m).astype(ACT) for s in ss]))
    outs = []
    for (_, blocks, sink), (m, ps) in zip(items, probs):
        n_v = blocks[0][1].shape[0]
        acc = functools.reduce(operator.add, [
            _dot(jnp.concatenate([vt, jnp.ones((ONES_ROWS, vt.shape[1]), ACT)], axis=0), p)
            for (_, vt, _, _), p in zip(blocks, ps)])
        l = acc[n_v:n_v + 1, :]
        if sink is not None:
            l = l + jnp.exp2(sink - m)
        outs.append((acc[:n_v, :] * (1.0 / l), m + jnp.log2(l)))
    return outs


def _band_bias(tv_ref, head, prev_includes_window_edge):
    kr = lax.broadcasted_iota(jnp.int32, (BAND, BAND), 0)
    qc = lax.broadcasted_iota(jnp.int32, (BAND, BAND), 1)
    own, prev = [_toeplitz(tv_ref[head, which], tv_ref[head, which + 1], BAND) for which in range(2)]
    prev_mask = (qc <= kr) if prev_includes_window_edge else (qc < kr)
    return jnp.where(qc >= kr, own, NEG), jnp.where(prev_mask, prev, NEG)


def _dil_kernel(q0_ref, k0_ref, v0_ref, q1_ref, k1_ref, v1_ref, q2_ref, k2_ref, v2_ref, sg_ref, tv_ref,
                o_ref, og_ref, lg_ref, bt_ref):
    hp = pl.program_id(0)
    b = pl.program_id(1)
    s = q0_ref.shape[1]
    n_tiles = s // BAND

    @pl.when(b == 0)
    def _():
        for g in range(N_DIL):
            for sh in range(2):
                own, prev = _band_bias(tv_ref, g * N_HEADS + 2 * hp + sh, True)
                bt_ref[g, sh, 0] = own
                bt_ref[g, sh, 1] = prev

    groups = ((q0_ref, k0_ref, v0_ref), (q1_ref, k1_ref, v1_ref), (q2_ref, k2_ref, v2_ref))

    for g, (q_ref, k_ref, v_ref) in enumerate(groups):
        d = DIL_PAIRS[g][1]
        per_phase = n_tiles // d
        has_prev = per_phase > 1
        assert per_phase % TILE_GROUP == 0 or TILE_GROUP % per_phase == 0

        def body(i, c, g=g, d=d, per_phase=per_phase, has_prev=has_prev, q_ref=q_ref, k_ref=k_ref, v_ref=v_ref):
            items, dsts = [], []
            k_own = vt_own = None
            for n in range(TILE_GROUP):
                u = i * TILE_GROUP + n
                rows = pl.ds(pl.multiple_of(u * BAND, BAND), BAND)
                t = lax.rem(u, per_phase)
                r = lax.div(u, per_phase)
                qt = _unit_rms_heads(q_ref[0, rows, :].T)
                k_prev, vt_prev = k_own, vt_own
                k_own = k_ref[0, rows, :]
                vt_own = v_ref[0, rows, :].T
                if has_prev:
                    phase_start_known = TILE_GROUP % per_phase == 0
                    prev_exists = (t > 0) if (n == 0 and not phase_start_known) else None
                    if phase_start_known and n % per_phase == 0:
                        k_prev = None
                    elif n == 0:
                        prow = pl.ds(pl.multiple_of(jnp.maximum(u - 1, 0) * BAND, BAND), BAND)
                        k_prev = k_ref[0, prow, :]
                        vt_prev = v_ref[0, prow, :].T
                for sh in range(2):
                    hrows = slice(HEAD_DIM * sh, HEAD_DIM * (sh + 1))
                    blocks = [(k_own, vt_own[hrows, :], bt_ref[g, sh, 0], None)]
                    if has_prev and k_prev is not None:
                        blocks.append((k_prev, vt_prev[hrows, :], bt_ref[g, sh, 1], prev_exists))
                    items.append((_head_rows(qt, HEAD_DIM * sh), blocks, None))
                dsts.append(rows if d == 1 else pl.ds(t * (BAND * d) + r, BAND, stride=d))
            res = _band_attend(items)
            for n, dst in enumerate(dsts):
                (o0, lse0), (o1, lse1) = res[2 * n], res[2 * n + 1]
                og_ref.at[g][dst, :] = jnp.concatenate([o0, o1], axis=0).T
                lg_ref.at[g][dst, :] = jnp.concatenate(
                    [jnp.broadcast_to(lse0, (HEAD_DIM, BAND)), jnp.broadcast_to(lse1, (HEAD_DIM, BAND))], axis=0).T
            return c

        lax.fori_loop(0, n_tiles // TILE_GROUP, body, 0)

    def merge(i, c):
        rows = pl.ds(pl.multiple_of(i * ROWS, ROWS), ROWS)
        l0, l1, l2 = lg_ref[0, rows, :], lg_ref[1, rows, :], lg_ref[2, rows, :]
        m = jnp.maximum(l0, jnp.maximum(l1, l2))
        e0, e1, e2 = jnp.exp2(l0 - m), jnp.exp2(l1 - m), jnp.exp2(l2 - m)
        ob = (e0 * og_ref[0, rows, :] + e1 * og_ref[1, rows, :] + e2 * og_ref[2, rows, :]) / (e0 + e1 + e2)
        o_ref[0, rows, :] = (ob * sg_ref[0, rows, :].astype(F32)).astype(o_ref.dtype)
        return c

    lax.fori_loop(0, s // ROWS, merge, 0)


def _dilated(p, tv_b):
    b, s, _ = p.shape
    col = lambda c: pl.BlockSpec((1, s, LANE), lambda hp, bi: (bi, 0, c + hp))
    qkv = [col(COL_B + 3 * N_PAIRS * g + N_PAIRS * i) for g in range(N_DIL) for i in range(3)]
    return pl.pallas_call(
        _dil_kernel,
        grid=(N_PAIRS, b),
        in_specs=qkv + [col(COL_SGB), pl.BlockSpec(tv_b.shape, lambda hp, bi: (0, 0, 0, 0))],
        out_specs=pl.BlockSpec((1, s, LANE), lambda hp, bi: (bi, 0, hp)),
        out_shape=jax.ShapeDtypeStruct((b, s, W_BR), ACT),
        scratch_shapes=[
            pltpu.VMEM((N_DIL, s, LANE), F32),
            pltpu.VMEM((N_DIL, s, LANE), F32),
            pltpu.VMEM((N_DIL, 2, 2, BAND, BAND), F32),
        ],
        compiler_params=_cparams(2),
        name="dilated",
    )(*([p] * 10), tv_b)


def _swa_kernel(sink_ref, q_ref, sg_ref, k_ref, v_ref, tv_ref, o_ref, bt_ref, vts_ref):
    hp = pl.program_id(0)
    b = pl.program_id(1)
    n_tiles = q_ref.shape[1] // BAND
    pairs_per_kv = N_PAIRS // KV_C

    @pl.when(b == 0)
    def _():
        for sh in range(2):
            own, prev = _band_bias(tv_ref, 2 * hp + sh, False)
            bt_ref[sh, 0] = own
            bt_ref[sh, 1] = prev

    kv_lo = HEAD_DIM * (hp // pairs_per_kv)
    row = lax.broadcasted_iota(jnp.int32, (LANE, BAND), 0)
    kv_rows = (row >= kv_lo) & (row < kv_lo + HEAD_DIM)

    def kv_vt(rows, slot):
        vts_ref[slot] = v_ref[0, rows, :].T
        return vts_ref[slot, pl.ds(pl.multiple_of(kv_lo, HEAD_DIM), HEAD_DIM), :]

    def body(i, c):
        items, dsts = [], []
        zero = jnp.zeros((LANE, BAND), ACT)
        k_own = vt_own = None
        for n in range(TILE_GROUP):
            t = i * TILE_GROUP + n
            rows = pl.ds(pl.multiple_of(t * BAND, BAND), BAND)
            qt = _unit_rms_heads(q_ref[0, rows, :].T)
            k_prev, vt_prev = k_own, vt_own
            k_own = k_ref[0, rows, :]
            vt_own = kv_vt(rows, n)
            if n == 0:
                prow = pl.ds(pl.multiple_of(jnp.maximum(t - 1, 0) * BAND, BAND), BAND)
                k_prev = k_ref[0, prow, :]
                vt_prev = kv_vt(prow, TILE_GROUP)
            prev_exists = (t > 0) if n == 0 else None
            for sh in range(2):
                qh = qt[HEAD_DIM * sh:HEAD_DIM * (sh + 1), :]
                qs = jnp.where(kv_rows, jnp.concatenate([qh, qh], axis=0), zero)
                items.append((qs, [(k_own, vt_own, bt_ref[sh, 0], None), (k_prev, vt_prev, bt_ref[sh, 1], prev_exists)],
                              sink_ref[2 * hp + sh]))
            dsts.append(rows)
        res = _band_attend(items)
        for n, rows in enumerate(dsts):
            ot = jnp.concatenate([o for o, _ in res[2 * n:2 * n + 2]], axis=0).T
            o_ref[0, rows, :] = (ot * sg_ref[0, rows, :].astype(F32)).astype(o_ref.dtype)
        return c

    lax.fori_loop(0, n_tiles // TILE_GROUP, body, 0)


def _swa(p, tv_c, sinks):
    b, s, _ = p.shape
    grid_spec = pltpu.PrefetchScalarGridSpec(
        num_scalar_prefetch=1,
        grid=(N_PAIRS, b),
        in_specs=[
            pl.BlockSpec((1, s, LANE), lambda hp, bi, sk: (bi, 0, COL_C + hp)),
            pl.BlockSpec((1, s, LANE), lambda hp, bi, sk: (bi, 0, COL_C + N_PAIRS + hp)),
            pl.BlockSpec((1, s, LANE), lambda hp, bi, sk: (bi, 0, COL_C + 2 * N_PAIRS)),
            pl.BlockSpec((1, s, LANE), lambda hp, bi, sk: (bi, 0, COL_C + 2 * N_PAIRS + 1)),
            pl.BlockSpec(tv_c.shape, lambda hp, bi, sk: (0, 0, 0, 0)),
        ],
        out_specs=pl.BlockSpec((1, s, LANE), lambda hp, bi, sk: (bi, 0, hp)),
        scratch_shapes=[pltpu.VMEM((2, 2, BAND, BAND), F32),
                        pltpu.VMEM((TILE_GROUP + 1, LANE, BAND), ACT)],
    )
    return pl.pallas_call(
        _swa_kernel,
        grid_spec=grid_spec,
        out_shape=jax.ShapeDtypeStruct((b, s, W_BR), ACT),
        compiler_params=_cparams(2),
        name="swa",
    )(sinks, p, p, p, p, tv_c)


def _merge_kernel(bra_ref, brb_ref, brc_ref, g_ref, wb_ref, wo_ref, x_ref, out_ref):
    merged = None
    for i, br_ref in enumerate((bra_ref, brb_ref, brc_ref)):
        term = g_ref[:, i * D_MODEL:(i + 1) * D_MODEL].astype(F32) * _dot(br_ref[...], wb_ref[i])
        merged = term if merged is None else merged + term
    out_ref[...] = x_ref[...] + _dot(merged.astype(ACT), wo_ref[...])


def _merge(bra, brb, brc, p2, w_br, w_out, x2, layer, ts=512):
    n = x2.shape[0]
    row = lambda w: pl.BlockSpec((ts, w), lambda i: (i, 0))
    return pl.pallas_call(
        _merge_kernel,
        grid=(n // ts,),
        in_specs=[row(W_BR), row(W_BR), row(W_BR), row(GATE_W),
                  pl.BlockSpec((None,) + w_br.shape[1:], lambda i: (layer, 0, 0, 0)),
                  pl.BlockSpec((None,) + w_out.shape[1:], lambda i: (layer, 0, 0)),
                  row(D_MODEL)],
        out_specs=row(D_MODEL),
        out_shape=jax.ShapeDtypeStruct(x2.shape, F32),
        compiler_params=_cparams(1),
        name="merge",
    )(bra, brb, brc, p2, w_br, w_out, x2)


def _rel_bucket(dist):
    dist = jnp.maximum(dist, 0)
    max_exact = N_BUCKETS // 2
    log_ratio = jnp.log(jnp.maximum(dist, 1).astype(F32) / max_exact) / math.log(REL_MAX_DIST / max_exact)
    large = jnp.minimum(max_exact + (log_ratio * (N_BUCKETS - max_exact)).astype(jnp.int32), N_BUCKETS - 1)
    return jnp.where(dist < max_exact, dist, large)


def _bias_by_distance(tab, dist, chunk):
    vals = tab.astype(F32)[:, _rel_bucket(dist)] * LOG2E
    vals = jnp.pad(vals, ((0, 0), (chunk, 0)))
    return vals.reshape(tab.shape[0], -1, 1, chunk)


def _gain_rows(qk_g):
    depth = qk_g.shape[0]
    scale = HEAD_DIM ** -0.5 * LOG2E
    one = jnp.ones((depth, HEAD_DIM), F32)
    tile = lambda v, n: jnp.tile(v, (1, n))
    parts = [tile(one, N_HEADS), tile(qk_g[:, 1] * qk_g[:, 0] * scale, N_HEADS), tile(one, 2 * N_HEADS),
             tile(one, N_DIL * N_HEADS), tile(qk_g[:, 3] * qk_g[:, 2] * scale, N_DIL * N_HEADS),
             tile(one, (N_DIL + 1) * N_HEADS),
             tile(one, N_HEADS), tile(qk_g[:, 5] * qk_g[:, 4] * scale, KV_C), tile(one, KV_C + N_HEADS),
             tile(one, GATE_W // HEAD_DIM)]
    return jnp.concatenate(parts, axis=1)


def kernel(x, ln_g, w_in, qk_g, sinks, w_branch, w_out, rel_bias):
    b, s, d = x.shape
    depth = ln_g.shape[0]
    assert d == D_MODEL and s % DIL_PAIRS[-1][0] == 0 and w_in.shape[-1] == C_IN

    tv_a = _bias_by_distance(rel_bias[:N_HEADS], jnp.arange(s), MOBA_BLOCK)
    tv_b = jnp.concatenate(
        [_bias_by_distance(rel_bias[OFF_B + g * N_HEADS:OFF_B + (g + 1) * N_HEADS], dd * jnp.arange(2 * BAND), BAND)
         for g, (_, dd) in enumerate(DIL_PAIRS)], axis=0)
    tv_c = _bias_by_distance(rel_bias[OFF_C:OFF_C + N_HEADS], jnp.arange(2 * BAND), BAND)
    seg = jnp.arange(TN) // HEAD_DIM
    e_mat = (seg[:, None] == seg[None, :]).astype(ACT)
    src = jnp.asarray([c for c, _ in _LAYOUT], jnp.int32)
    fn = jnp.asarray([f for _, f in _LAYOUT[::PAIR]], jnp.int32)
    gains = _gain_rows(qk_g)

    w_in_b = w_in.astype(ACT)
    w_br_b = w_branch.astype(ACT)
    w_out_b = w_out.astype(ACT)

    for l in range(depth):
        p = _project(x, ln_g[:, None, :], w_in_b, gains[l][None, :], e_mat, src, fn, l)
        bra = _moba(p, tv_a)
        brb = _dilated(p, tv_b)
        brc = _swa(p, tv_c, sinks[l] * LOG2E)
        x2 = _merge(bra.reshape(b * s, W_BR), brb.reshape(b * s, W_BR), brc.reshape(b * s, W_BR),
                    p.reshape(b * s, P_W), w_br_b, w_out_b, x.reshape(b * s, d), l)
        x = x2.reshape(b, s, d)
    return x
```

```python
import functools
import math
import operator

import jax
import jax.numpy as jnp
from jax import lax
from jax.experimental import pallas as pl
from jax.experimental.pallas import tpu as pltpu

D_MODEL = 1024
HEAD_DIM = 64
W_BR = D_MODEL // 2
N_HEADS = W_BR // HEAD_DIM
N_PAIRS = N_HEADS // 2
MOBA_BLOCK = 256
MOBA_TOPK = 3
DIL_PAIRS = ((128, 1), (512, 4), (2048, 16))
N_DIL = len(DIL_PAIRS)
KV_C = 2
SWA_WINDOW = 128
N_BUCKETS = 32
REL_MAX_DIST = 2048
OFF_B = N_HEADS
OFF_C = N_HEADS + N_DIL * N_HEADS
EPS = 1e-6
NEG = -1e30
LOG2E = math.log2(math.e)

LANE = 128
BAND = 128
TN = 256
ROWS = 256
TILE_GROUP = 16
MOBA_GROUP = 8
SINGLE_OP_STRIDE = 4
ONES_ROWS = 16
VMEM_LIMIT = 56 * 1024 * 1024

ACT = jnp.bfloat16
F32 = jnp.float32

PLAIN, NORM, SILU, SIGMOID, KNORM_VPLAIN, NORM_D4, PLAIN_D4, NORM_D16, PLAIN_D16, PAD = range(10)
_B_GROUP = lambda g, norm, plain: [(8 + 2 * g, plain), (9 + 2 * g, plain), (14 + 2 * g, norm), (15 + 2 * g, norm),
                                   (20 + 2 * g, plain), (21 + 2 * g, plain)]
_LAYOUT = ([(j, SIGMOID) for j in range(33, 45)]
           + [(0, PLAIN), (1, PLAIN), (2, NORM), (3, NORM), (4, PLAIN), (5, PLAIN), (6, SILU), (7, SILU)]
           + _B_GROUP(0, NORM, PLAIN) + _B_GROUP(1, NORM_D4, PLAIN_D4) + _B_GROUP(2, NORM_D16, PLAIN_D16)
           + [(26, SILU), (27, SILU)]
           + [(28, PLAIN), (29, PLAIN), (31, SILU), (32, SILU), (30, KNORM_VPLAIN)])
C_IN = len(_LAYOUT) * TN
PAIR = 2
SUB = 1
_LAYOUT = _LAYOUT + [(_LAYOUT[-1][0], KNORM_VPLAIN)] * (-len(_LAYOUT) % PAIR)
_LAYOUT = _LAYOUT + [(_LAYOUT[-1][0], PAD)] * (-len(_LAYOUT) % (SUB * PAIR))
assert all(_LAYOUT[j][1] == _LAYOUT[j - j % PAIR][1] for j in range(len(_LAYOUT)))
assert [f for _, f in _LAYOUT].count(KNORM_VPLAIN) == PAIR
N_STEPS = len(_LAYOUT) // (SUB * PAIR)
P_W = len(_LAYOUT) * TN
COL_GATE, COL_A, COL_B, COL_SGB, COL_C = 0, 24, 40, 76, 80
GATE_W = 3 * D_MODEL


def _cparams(n_axes):
    return pltpu.CompilerParams(dimension_semantics=("arbitrary",) * n_axes, vmem_limit_bytes=VMEM_LIMIT)


def _dot(a, b):
    return jnp.dot(a, b, preferred_element_type=F32)


def _sigmoid(a):
    return 0.5 * jnp.tanh(0.5 * a) + 0.5


def _proj_kernel(src_ref, fn_ref, x_ref, lng_ref, *refs):
    del src_ref
    n_in = SUB * PAIR
    w_all, gain_all = refs[:n_in], refs[n_in:2 * n_in]
    e_ref, o_ref, xn_ref, acc_ref, acc2_ref = refs[2 * n_in:]
    j = pl.program_id(1)
    s = x_ref.shape[1]
    n_chunks = s // ROWS
    lanes_per_block = TN // LANE

    @pl.when(j == 0)
    def _():
        def body(i, c):
            rows = pl.ds(pl.multiple_of(i * ROWS, ROWS), ROWS)
            xv = x_ref[0, rows, :]
            ms = jnp.mean(xv * xv, axis=-1, keepdims=True)
            xn_ref[rows, :] = (xv * lax.rsqrt(ms + EPS) * lng_ref[...]).astype(ACT)
            return c
        lax.fori_loop(0, n_chunks, body, 0)

    def head_norm(a, gain):
        ss = _dot((a * a).astype(ACT), e_ref[...])
        return a * lax.rsqrt(ss * (1.0 / HEAD_DIM) + EPS) * gain

    def plain(a, gain):
        return a

    def silu(a, gain):
        return a * _sigmoid(a)

    def sigmoid(a, gain):
        return _sigmoid(a)

    def k_norm_v_plain(a, gain):
        lane = lax.broadcasted_iota(jnp.int32, a.shape, 1)
        return jnp.where(lane < KV_C * HEAD_DIM, head_norm(a, gain), a)

    def phase_rows(src_ref, slab, t, d):
        r, m0 = divmod(t * BAND, s // d)
        return src_ref.at[slab][pl.ds(m0 * d + r, BAND, stride=d), :]

    def pair_step(sub):
        w_refs, gain_refs = w_all[sub * PAIR:(sub + 1) * PAIR], gain_all[sub * PAIR:(sub + 1) * PAIR]
        fn = fn_ref[j * SUB + sub]
        col0 = sub * PAIR * TN

        def chunk(n):
            h, i = divmod(n, n_chunks)
            return _dot(xn_ref[i * ROWS:(i + 1) * ROWS, :], w_refs[h][...])

        def out_cols(h):
            return slice(col0 + h * TN, col0 + (h + 1) * TN)

        def direct(code, f, live=PAIR):
            @pl.when(fn == code)
            def _():
                pending = None
                for n in range(live * n_chunks + 1):
                    a = chunk(n) if n < live * n_chunks else None
                    if pending is not None:
                        h, i = divmod(n - 1, n_chunks)
                        o_ref[0, i * ROWS:(i + 1) * ROWS, out_cols(h)] = (
                            f(pending, gain_refs[h][...]).astype(o_ref.dtype))
                    pending = a
                if live < PAIR:
                    o_ref[0, :, col0 + live * TN:col0 + PAIR * TN] = jnp.zeros((s, (PAIR - live) * TN), o_ref.dtype)

        def phase_major(code, f, d):
            @pl.when(fn == code)
            def _():
                for n in range(PAIR * n_chunks):
                    h, i = divmod(n, n_chunks)
                    a = chunk(n)
                    for c in range(lanes_per_block):
                        acc_ref[h * lanes_per_block + c, i * ROWS:(i + 1) * ROWS, :] = a[:, c * LANE:(c + 1) * LANE]
                src_ref, stride = acc_ref, d
                if d > SINGLE_OP_STRIDE:
                    assert d == SINGLE_OP_STRIDE * SINGLE_OP_STRIDE
                    for slab in range(PAIR * lanes_per_block):
                        for t in range(s // BAND):
                            acc2_ref[slab, t * BAND:(t + 1) * BAND, :] = phase_rows(acc_ref, slab, t, SINGLE_OP_STRIDE)
                    src_ref, stride = acc2_ref, SINGLE_OP_STRIDE
                for h in range(PAIR):
                    for t in range(s // BAND):
                        a = jnp.concatenate([phase_rows(src_ref, h * lanes_per_block + c, t, stride)
                                             for c in range(lanes_per_block)], axis=1)
                        o_ref[0, t * BAND:(t + 1) * BAND, out_cols(h)] = f(a, gain_refs[h][...]).astype(o_ref.dtype)

        direct(PLAIN, plain)
        direct(NORM, head_norm)
        direct(SILU, silu)
        direct(SIGMOID, sigmoid)
        direct(KNORM_VPLAIN, k_norm_v_plain, live=1)
        direct(PAD, plain, live=0)
        phase_major(NORM_D4, head_norm, DIL_PAIRS[1][1])
        phase_major(PLAIN_D4, plain, DIL_PAIRS[1][1])
        phase_major(NORM_D16, head_norm, DIL_PAIRS[2][1])
        phase_major(PLAIN_D16, plain, DIL_PAIRS[2][1])

    for sub in range(SUB):
        pair_step(sub)


def _w_index(bi, j, src_ref, fn_ref, *, layer, slot):
    return layer, 0, src_ref[SUB * PAIR * j + slot]


def _gain_index(bi, j, src_ref, fn_ref, *, slot):
    return 0, src_ref[SUB * PAIR * j + slot]


def _project(x, ln_g, w_in, gain_row, e_mat, src, fn, layer):
    b, s, d = x.shape
    n_in = SUB * PAIR
    grid_spec = pltpu.PrefetchScalarGridSpec(
        num_scalar_prefetch=2,
        grid=(b, N_STEPS),
        in_specs=[
            pl.BlockSpec((1, s, d), lambda bi, j, src_ref, fn_ref: (bi, 0, 0)),
            pl.BlockSpec((None, 1, d), lambda bi, j, src_ref, fn_ref: (layer, 0, 0)),
        ] + [pl.BlockSpec((None, d, TN), functools.partial(_w_index, layer=layer, slot=n)) for n in range(n_in)]
          + [pl.BlockSpec((1, TN), functools.partial(_gain_index, slot=n)) for n in range(n_in)]
          + [pl.BlockSpec((TN, TN), lambda bi, j, src_ref, fn_ref: (0, 0))],
        out_specs=pl.BlockSpec((1, s, n_in * TN), lambda bi, j, src_ref, fn_ref: (bi, 0, j)),
        scratch_shapes=[pltpu.VMEM((s, d), ACT), pltpu.VMEM((PAIR * TN // LANE, s, LANE), F32),
                        pltpu.VMEM((PAIR * TN // LANE, s, LANE), F32)],
    )
    return pl.pallas_call(
        _proj_kernel,
        grid_spec=grid_spec,
        out_shape=jax.ShapeDtypeStruct((b, s, P_W), ACT),
        compiler_params=_cparams(2),
        name="proj",
    )(src, fn, x, ln_g, *([w_in] * n_in), *([gain_row] * n_in), e_mat)


def _toeplitz(lo, hi, n):
    x = jnp.concatenate([jnp.broadcast_to(lo, (n, n)), jnp.broadcast_to(hi, (n, n))], axis=1)
    return pltpu.roll(x, 0, 1, stride=1, stride_axis=0)[:, n:]


def _unit_rms_heads(qt):
    halves = []
    for sh in range(2):
        x = qt[HEAD_DIM * sh:HEAD_DIM * (sh + 1), :].astype(F32)
        halves.append(x * lax.rsqrt(jnp.mean(x * x, axis=0, keepdims=True) + EPS))
    return jnp.concatenate(halves, axis=0).astype(qt.dtype)


def _head_rows(qt, lo):
    row = lax.broadcasted_iota(jnp.int32, qt.shape, 0)
    return jnp.where((row >= lo) & (row < lo + HEAD_DIM), qt, jnp.zeros_like(qt))


def _moba_kernel(q_ref, k_ref, v_ref, sg_ref, tv_ref, o_ref,
                 bt_ref, qts_ref, vt_ref, km_ref, sel_ref, m_ref, acc_ref):
    hp = pl.program_id(0)
    b = pl.program_id(1)
    blk = MOBA_BLOCK
    n_blk = k_ref.shape[1] // blk

    def rows_of(i):
        return pl.ds(i * blk if isinstance(i, int) else pl.multiple_of(i * blk, blk), blk)

    @pl.when(b == 0)
    def _():
        kr = lax.broadcasted_iota(jnp.int32, (blk, blk), 0)
        qc = lax.broadcasted_iota(jnp.int32, (blk, blk), 1)
        for sh in range(2):
            def body(o, c, sh=sh):
                t = _toeplitz(tv_ref[2 * hp + sh, o], tv_ref[2 * hp + sh, o + 1], blk)
                bt_ref[sh, o] = jnp.where((qc >= kr) | (o > 0), t, NEG)
                return c
            lax.fori_loop(0, n_blk, body, 0)

    def prep(i, c):
        vt = v_ref[0, rows_of(i), :].T
        for sh in range(2):
            vt_ref[i, sh] = jnp.concatenate(
                [vt[HEAD_DIM * sh:HEAD_DIM * (sh + 1), :], jnp.ones((ONES_ROWS, blk), ACT)], axis=0)
        km_ref[pl.ds(i, 1), :] = jnp.mean(k_ref[0, rows_of(i), :].astype(F32), axis=0, keepdims=True)
        qt = _unit_rms_heads(q_ref[0, rows_of(i), :].T)
        for sh in range(2):
            qts_ref[i, sh] = _head_rows(qt, HEAD_DIM * sh)
        return c
    lax.fori_loop(0, n_blk, prep, 0, unroll=4)

    km = km_ref[...]
    km0 = km.astype(ACT)
    r1 = km - km0.astype(F32)
    km1 = r1.astype(ACT)
    km2 = (r1 - km1.astype(F32)).astype(ACT)
    blk_row = lax.broadcasted_iota(jnp.int32, (n_blk, blk), 0)

    ranked = [(qi, sh) for qi in range(MOBA_TOPK + 1, n_blk) for sh in range(2)]
    gates = [_dot(km0, qts_ref[qi, sh]) + _dot(km1, qts_ref[qi, sh]) + _dot(km2, qts_ref[qi, sh])
             for qi, sh in ranked]
    for (qi, sh), g in zip(ranked, gates):
        g = jnp.where(blk_row < qi, g, NEG)
        sel_rows = []
        for jb in range(qi):
            gj = g[jb:jb + 1, :]
            beats = (g > gj) | ((g == gj) & (blk_row < jb))
            cnt = jnp.sum(beats.astype(F32), axis=0, keepdims=True)
            sel_rows.append(jnp.where(cnt < MOBA_TOPK, 1.0, 0.0))
        sel_rows.append(jnp.zeros((n_blk - qi, blk), F32))
        sel_ref[qi, sh] = jnp.concatenate(sel_rows, axis=0)

    def tiles(off, qis):
        k_blks = [k_ref[0, rows_of(qi - off), :] for qi in qis]
        scores = [_dot(k_blk, qts_ref[qi, sh]) for qi, k_blk in zip(qis, k_blks) for sh in range(2)]
        work = [(qi, sh) for qi in qis for sh in range(2)]
        stats = []
        half = blk // 2
        for (qi, sh), st in zip(work, scores):
            if off == 0:
                bias = bt_ref[sh, 0]
                parts = [(st[:half, :half] + bias[:half, :half], slice(0, half)),
                         (st[:, half:] + bias[:, half:], slice(0, blk))]
                maxes = [jnp.max(s_part, axis=0, keepdims=True) for s_part, _ in parts]
                m_ref[qi, sh] = jnp.concatenate(maxes, axis=1)
                stats.append((None, [(keys, jnp.exp2(s_part - m_part).astype(ACT))
                                     for (s_part, keys), m_part in zip(parts, maxes)]))
                continue
            st = st + bt_ref[sh, off]
            tile_max = jnp.max(st, axis=0, keepdims=True)
            if qi <= MOBA_TOPK:
                m = m_ref[qi, sh]
                m_new = jnp.maximum(m, tile_max)
                p = jnp.exp2(st - m_new)
                alpha = jnp.exp2(m - m_new)
            else:
                sel = sel_ref[qi, sh, pl.ds(qi - off, 1), :] > 0.5
                m = m_ref[qi, sh]
                m_new = jnp.where(sel, jnp.maximum(m, tile_max), m)
                p = jnp.exp2(st - jnp.where(sel, m_new, -NEG))
                alpha = jnp.exp2(m - m_new)
            m_ref[qi, sh] = m_new
            stats.append((alpha, [(slice(0, blk), p.astype(ACT))]))
        pvs = [jnp.concatenate([_dot(vt_ref[qi - off, sh, :, keys], p) for keys, p in parts], axis=1)
               for (qi, sh), (_, parts) in zip(work, stats)]
        for (qi, sh), (alpha, _), pv in zip(work, stats, pvs):
            acc_ref[qi, sh] = pv if off == 0 else alpha * acc_ref[qi, sh] + pv

    for off in range(n_blk):
        qis = list(range(off, n_blk))
        for lo in range(0, len(qis), MOBA_GROUP):
            tiles(off, qis[lo:lo + MOBA_GROUP])

    def finish(qi, c):
        ot = jnp.concatenate([acc_ref[qi, sh, :HEAD_DIM, :] * (1.0 / acc_ref[qi, sh, HEAD_DIM:HEAD_DIM + 1, :])
                              for sh in range(2)], axis=0)
        o_ref[0, rows_of(qi), :] = (ot.T * sg_ref[0, rows_of(qi), :].astype(F32)).astype(o_ref.dtype)
        return c
    lax.fori_loop(0, n_blk, finish, 0, unroll=4)


def _moba(p, tv_a):
    b, s, _ = p.shape
    blk = MOBA_BLOCK
    n_blk = s // blk
    col = lambda c: pl.BlockSpec((1, s, LANE), lambda hp, bi: (bi, 0, COL_A + c * N_PAIRS + hp))
    return pl.pallas_call(
        _moba_kernel,
        grid=(N_PAIRS, b),
        in_specs=[col(0), col(1), col(2), col(3), pl.BlockSpec(tv_a.shape, lambda hp, bi: (0, 0, 0, 0))],
        out_specs=pl.BlockSpec((1, s, LANE), lambda hp, bi: (bi, 0, hp)),
        out_shape=jax.ShapeDtypeStruct((b, s, W_BR), ACT),
        scratch_shapes=[
            pltpu.VMEM((2, n_blk, blk, blk), F32),
            pltpu.VMEM((n_blk, 2, LANE, blk), ACT),
            pltpu.VMEM((n_blk, 2, HEAD_DIM + ONES_ROWS, blk), ACT),
            pltpu.VMEM((n_blk, LANE), F32),
            pltpu.VMEM((n_blk, 2, n_blk, blk), F32),
            pltpu.VMEM((n_blk, 2, 1, blk), F32),
            pltpu.VMEM((n_blk, 2, HEAD_DIM + ONES_ROWS, blk), F32),
        ],
        compiler_params=_cparams(2),
        name="moba",
    )(p, p, p, p, tv_a)


def _band_attend(items):
    def fuse(blocks):
        if len(blocks) > 1 and all(valid is None for _, _, _, valid in blocks):
            ks, vts, biases, _ = zip(*blocks)
            return [(jnp.concatenate(ks, axis=0), jnp.concatenate(vts, axis=1), jnp.concatenate(biases, axis=0), None)]
        return blocks

    items = [(qs, fuse(blocks), sink) for qs, blocks, sink in items]
    scores = [[_dot(k, qs) for k, _, _, _ in blocks] for qs, blocks, _ in items]
    probs = []
    for (_, blocks, sink), sc in zip(items, scores):
        ss = [s + bias if valid is None else jnp.where(valid, s + bias, NEG)
              for s, (_, _, bias, valid) in zip(sc, blocks)]
        m = functools.reduce(jnp.maximum, [jnp.max(s, axis=0, keepdims=True) for s in ss])
        if sink is not None:
            m = jnp.maximum(m, sink)
        probs.append((m, [jnp.exp2(s - m).astype(ACT) for s in ss]))
    outs = []
    for (_, blocks, sink), (m, ps) in zip(items, probs):
        n_v = blocks[0][1].shape[0]
        acc = functools.reduce(operator.add, [
            _dot(jnp.concatenate([vt, jnp.ones((ONES_ROWS, vt.shape[1]), ACT)], axis=0), p)
            for (_, vt, _, _), p in zip(blocks, ps)])
        l = acc[n_v:n_v + 1, :]
        if sink is not None:
            l = l + jnp.exp2(sink - m)
        outs.append((acc[:n_v, :] * (1.0 / l), m + jnp.log2(l)))
    return outs


def _band_bias(tv_ref, head, prev_includes_window_edge):
    kr = lax.broadcasted_iota(jnp.int32, (BAND, BAND), 0)
    qc = lax.broadcasted_iota(jnp.int32, (BAND, BAND), 1)
    own, prev = [_toeplitz(tv_ref[head, which], tv_ref[head, which + 1], BAND) for which in range(2)]
    prev_mask = (qc <= kr) if prev_includes_window_edge else (qc < kr)
    return jnp.where(qc >= kr, own, NEG), jnp.where(prev_mask, prev, NEG)


def _dil_kernel(q0_ref, k0_ref, v0_ref, q1_ref, k1_ref, v1_ref, q2_ref, k2_ref, v2_ref, sg_ref, tv_ref,
                o_ref, og_ref, lg_ref, bt_ref):
    hp = pl.program_id(0)
    b = pl.program_id(1)
    s = q0_ref.shape[1]
    n_tiles = s // BAND

    @pl.when(b == 0)
    def _():
        for g in range(N_DIL):
            for sh in range(2):
                own, prev = _band_bias(tv_ref, g * N_HEADS + 2 * hp + sh, True)
                bt_ref[g, sh, 0] = own
                bt_ref[g, sh, 1] = prev

    groups = ((q0_ref, k0_ref, v0_ref), (q1_ref, k1_ref, v1_ref), (q2_ref, k2_ref, v2_ref))

    for g, (q_ref, k_ref, v_ref) in enumerate(groups):
        d = DIL_PAIRS[g][1]
        per_phase = n_tiles // d
        has_prev = per_phase > 1
        assert per_phase % TILE_GROUP == 0 or TILE_GROUP % per_phase == 0

        def body(i, c, g=g, d=d, per_phase=per_phase, has_prev=has_prev, q_ref=q_ref, k_ref=k_ref, v_ref=v_ref):
            items, dsts = [], []
            k_own = vt_own = None
            for n in range(TILE_GROUP):
                u = i * TILE_GROUP + n
                rows = pl.ds(pl.multiple_of(u * BAND, BAND), BAND)
                t = lax.rem(u, per_phase)
                r = lax.div(u, per_phase)
                qt = _unit_rms_heads(q_ref[0, rows, :].T)
                k_prev, vt_prev = k_own, vt_own
                k_own = k_ref[0, rows, :]
                vt_own = v_ref[0, rows, :].T
                if has_prev:
                    phase_start_known = TILE_GROUP % per_phase == 0
                    prev_exists = (t > 0) if (n == 0 and not phase_start_known) else None
                    if phase_start_known and n % per_phase == 0:
                        k_prev = None
                    elif n == 0:
                        prow = pl.ds(pl.multiple_of(jnp.maximum(u - 1, 0) * BAND, BAND), BAND)
                        k_prev = k_ref[0, prow, :]
                        vt_prev = v_ref[0, prow, :].T
                for sh in range(2):
                    hrows = slice(HEAD_DIM * sh, HEAD_DIM * (sh + 1))
                    blocks = [(k_own, vt_own[hrows, :], bt_ref[g, sh, 0], None)]
                    if has_prev and k_prev is not None:
                        blocks.append((k_prev, vt_prev[hrows, :], bt_ref[g, sh, 1], prev_exists))
                    items.append((_head_rows(qt, HEAD_DIM * sh), blocks, None))
                dsts.append(rows if d == 1 else pl.ds(t * (BAND * d) + r, BAND, stride=d))
            res = _band_attend(items)
            for n, dst in enumerate(dsts):
                (o0, lse0), (o1, lse1) = res[2 * n], res[2 * n + 1]
                og_ref.at[g][dst, :] = jnp.concatenate([o0, o1], axis=0).T
                lg_ref.at[g][dst, :] = jnp.concatenate(
                    [jnp.broadcast_to(lse0, (HEAD_DIM, BAND)), jnp.broadcast_to(lse1, (HEAD_DIM, BAND))], axis=0).T
            return c

        lax.fori_loop(0, n_tiles // TILE_GROUP, body, 0)

    def merge(i, c):
        rows = pl.ds(pl.multiple_of(i * ROWS, ROWS), ROWS)
        l0, l1, l2 = lg_ref[0, rows, :], lg_ref[1, rows, :], lg_ref[2, rows, :]
        m = jnp.maximum(l0, jnp.maximum(l1, l2))
        e0, e1, e2 = jnp.exp2(l0 - m), jnp.exp2(l1 - m), jnp.exp2(l2 - m)
        ob = (e0 * og_ref[0, rows, :] + e1 * og_ref[1, rows, :] + e2 * og_ref[2, rows, :]) / (e0 + e1 + e2)
        o_ref[0, rows, :] = (ob * sg_ref[0, rows, :].astype(F32)).astype(o_ref.dtype)
        return c

    lax.fori_loop(0, s // ROWS, merge, 0)


def _dilated(p, tv_b):
    b, s, _ = p.shape
    col = lambda c: pl.BlockSpec((1, s, LANE), lambda hp, bi: (bi, 0, c + hp))
    qkv = [col(COL_B + 3 * N_PAIRS * g + N_PAIRS * i) for g in range(N_DIL) for i in range(3)]
    return pl.pallas_call(
        _dil_kernel,
        grid=(N_PAIRS, b),
        in_specs=qkv + [col(COL_SGB), pl.BlockSpec(tv_b.shape, lambda hp, bi: (0, 0, 0, 0))],
        out_specs=pl.BlockSpec((1, s, LANE), lambda hp, bi: (bi, 0, hp)),
        out_shape=jax.ShapeDtypeStruct((b, s, W_BR), ACT),
        scratch_shapes=[
            pltpu.VMEM((N_DIL, s, LANE), F32),
            pltpu.VMEM((N_DIL, s, LANE), F32),
            pltpu.VMEM((N_DIL, 2, 2, BAND, BAND), F32),
        ],
        compiler_params=_cparams(2),
        name="dilated",
    )(*([p] * 10), tv_b)


def _swa_kernel(sink_ref, q_ref, sg_ref, k_ref, v_ref, tv_ref, o_ref, bt_ref, vts_ref):
    hp = pl.program_id(0)
    b = pl.program_id(1)
    n_tiles = q_ref.shape[1] // BAND
    pairs_per_kv = N_PAIRS // KV_C

    @pl.when(b == 0)
    def _():
        for sh in range(2):
            own, prev = _band_bias(tv_ref, 2 * hp + sh, False)
            bt_ref[sh, 0] = own
            bt_ref[sh, 1] = prev

    kv_lo = HEAD_DIM * (hp // pairs_per_kv)
    row = lax.broadcasted_iota(jnp.int32, (LANE, BAND), 0)
    kv_rows = (row >= kv_lo) & (row < kv_lo + HEAD_DIM)

    def kv_vt(rows, slot):
        vts_ref[slot] = v_ref[0, rows, :].T
        return vts_ref[slot, pl.ds(pl.multiple_of(kv_lo, HEAD_DIM), HEAD_DIM), :]

    def body(i, c):
        items, dsts = [], []
        zero = jnp.zeros((LANE, BAND), ACT)
        k_own = vt_own = None
        for n in range(TILE_GROUP):
            t = i * TILE_GROUP + n
            rows = pl.ds(pl.multiple_of(t * BAND, BAND), BAND)
            qt = _unit_rms_heads(q_ref[0, rows, :].T)
            k_prev, vt_prev = k_own, vt_own
            k_own = k_ref[0, rows, :]
            vt_own = kv_vt(rows, n)
            if n == 0:
                prow = pl.ds(pl.multiple_of(jnp.maximum(t - 1, 0) * BAND, BAND), BAND)
                k_prev = k_ref[0, prow, :]
                vt_prev = kv_vt(prow, TILE_GROUP)
            prev_exists = (t > 0) if n == 0 else None
            for sh in range(2):
                qh = qt[HEAD_DIM * sh:HEAD_DIM * (sh + 1), :]
                qs = jnp.where(kv_rows, jnp.concatenate([qh, qh], axis=0), zero)
                items.append((qs, [(k_own, vt_own, bt_ref[sh, 0], None), (k_prev, vt_prev, bt_ref[sh, 1], prev_exists)],
                              sink_ref[2 * hp + sh]))
            dsts.append(rows)
        res = _band_attend(items)
        for n, rows in enumerate(dsts):
            ot = jnp.concatenate([o for o, _ in res[2 * n:2 * n + 2]], axis=0).T
            o_ref[0, rows, :] = (ot * sg_ref[0, rows, :].astype(F32)).astype(o_ref.dtype)
        return c

    lax.fori_loop(0, n_tiles // TILE_GROUP, body, 0)


def _swa(p, tv_c, sinks):
    b, s, _ = p.shape
    grid_spec = pltpu.PrefetchScalarGridSpec(
        num_scalar_prefetch=1,
        grid=(N_PAIRS, b),
        in_specs=[
            pl.BlockSpec((1, s, LANE), lambda hp, bi, sk: (bi, 0, COL_C + hp)),
            pl.BlockSpec((1, s, LANE), lambda hp, bi, sk: (bi, 0, COL_C + N_PAIRS + hp)),
            pl.BlockSpec((1, s, LANE), lambda hp, bi, sk: (bi, 0, COL_C + 2 * N_PAIRS)),
            pl.BlockSpec((1, s, LANE), lambda hp, bi, sk: (bi, 0, COL_C + 2 * N_PAIRS + 1)),
            pl.BlockSpec(tv_c.shape, lambda hp, bi, sk: (0, 0, 0, 0)),
        ],
        out_specs=pl.BlockSpec((1, s, LANE), lambda hp, bi, sk: (bi, 0, hp)),
        scratch_shapes=[pltpu.VMEM((2, 2, BAND, BAND), F32),
                        pltpu.VMEM((TILE_GROUP + 1, LANE, BAND), ACT)],
    )
    return pl.pallas_call(
        _swa_kernel,
        grid_spec=grid_spec,
        out_shape=jax.ShapeDtypeStruct((b, s, W_BR), ACT),
        compiler_params=_cparams(2),
        name="swa",
    )(sinks, p, p, p, p, tv_c)


def _merge_kernel(bra_ref, brb_ref, brc_ref, g_ref, wb_ref, wo_ref, x_ref, out_ref):
    merged = None
    for i, br_ref in enumerate((bra_ref, brb_ref, brc_ref)):
        term = g_ref[:, i * D_MODEL:(i + 1) * D_MODEL].astype(F32) * _dot(br_ref[...], wb_ref[i])
        merged = term if merged is None else merged + term
    out_ref[...] = x_ref[...] + _dot(merged.astype(ACT), wo_ref[...])


def _merge(bra, brb, brc, p2, w_br, w_out, x2, layer, ts=512):
    n = x2.shape[0]
    row = lambda w: pl.BlockSpec((ts, w), lambda i: (i, 0))
    return pl.pallas_call(
        _merge_kernel,
        grid=(n // ts,),
        in_specs=[row(W_BR), row(W_BR), row(W_BR), row(GATE_W),
                  pl.BlockSpec((None,) + w_br.shape[1:], lambda i: (layer, 0, 0, 0)),
                  pl.BlockSpec((None,) + w_out.shape[1:], lambda i: (layer, 0, 0)),
                  row(D_MODEL)],
        out_specs=row(D_MODEL),
        out_shape=jax.ShapeDtypeStruct(x2.shape, F32),
        compiler_params=_cparams(1),
        name="merge",
    )(bra, brb, brc, p2, w_br, w_out, x2)


def _rel_bucket(dist):
    dist = jnp.maximum(dist, 0)
    max_exact = N_BUCKETS // 2
    log_ratio = jnp.log(jnp.maximum(dist, 1).astype(F32) / max_exact) / math.log(REL_MAX_DIST / max_exact)
    large = jnp.minimum(max_exact + (log_ratio * (N_BUCKETS - max_exact)).astype(jnp.int32), N_BUCKETS - 1)
    return jnp.where(dist < max_exact, dist, large)


def _bias_by_distance(tab, dist, chunk):
    vals = tab.astype(F32)[:, _rel_bucket(dist)] * LOG2E
    vals = jnp.pad(vals, ((0, 0), (chunk, 0)))
    return vals.reshape(tab.shape[0], -1, 1, chunk)


def _gain_rows(qk_g):
    depth = qk_g.shape[0]
    scale = HEAD_DIM ** -0.5 * LOG2E
    one = jnp.ones((depth, HEAD_DIM), F32)
    tile = lambda v, n: jnp.tile(v, (1, n))
    parts = [tile(one, N_HEADS), tile(qk_g[:, 1] * qk_g[:, 0] * scale, N_HEADS), tile(one, 2 * N_HEADS),
             tile(one, N_DIL * N_HEADS), tile(qk_g[:, 3] * qk_g[:, 2] * scale, N_DIL * N_HEADS),
             tile(one, (N_DIL + 1) * N_HEADS),
             tile(one, N_HEADS), tile(qk_g[:, 5] * qk_g[:, 4] * scale, KV_C), tile(one, KV_C + N_HEADS),
             tile(one, GATE_W // HEAD_DIM)]
    return jnp.concatenate(parts, axis=1)


def kernel(x, ln_g, w_in, qk_g, sinks, w_branch, w_out, rel_bias):
    b, s, d = x.shape
    depth = ln_g.shape[0]
    assert d == D_MODEL and s % DIL_PAIRS[-1][0] == 0 and w_in.shape[-1] == C_IN

    tv_a = _bias_by_distance(rel_bias[:N_HEADS], jnp.arange(s), MOBA_BLOCK)
    tv_b = jnp.concatenate(
        [_bias_by_distance(rel_bias[OFF_B + g * N_HEADS:OFF_B + (g + 1) * N_HEADS], dd * jnp.arange(2 * BAND), BAND)
         for g, (_, dd) in enumerate(DIL_PAIRS)], axis=0)
    tv_c = _bias_by_distance(rel_bias[OFF_C:OFF_C + N_HEADS], jnp.arange(2 * BAND), BAND)
    seg = jnp.arange(TN) // HEAD_DIM
    e_mat = (seg[:, None] == seg[None, :]).astype(ACT)
    src = jnp.asarray([c for c, _ in _LAYOUT], jnp.int32)
    fn = jnp.asarray([f for _, f in _LAYOUT[::PAIR]], jnp.int32)
    gains = _gain_rows(qk_g)

    w_in_b = w_in.astype(ACT)
    w_br_b = w_branch.astype(ACT)
    w_out_b = w_out.astype(ACT)

    for l in range(depth):
        p = _project(x, ln_g[:, None, :], w_in_b, gains[l][None, :], e_mat, src, fn, l)
        bra = _moba(p, tv_a)
        brb = _dilated(p, tv_b)
        brc = _swa(p, tv_c, sinks[l] * LOG2E)
        x2 = _merge(bra.reshape(b * s, W_BR), brb.reshape(b * s, W_BR), brc.reshape(b * s, W_BR),
                    p.reshape(b * s, P_W), w_br_b, w_out_b, x.reshape(b * s, d), l)
        x = x2.reshape(b, s, d)
    return x
```

```python
import functools
import math
import operator

import jax
import jax.numpy as jnp
from jax import lax
from jax.experimental import pallas as pl
from jax.experimental.pallas import tpu as pltpu

D_MODEL = 1024
HEAD_DIM = 64
W_BR = D_MODEL // 2
N_HEADS = W_BR // HEAD_DIM
N_PAIRS = N_HEADS // 2
MOBA_BLOCK = 256
MOBA_TOPK = 3
DIL_PAIRS = ((128, 1), (512, 4), (2048, 16))
N_DIL = len(DIL_PAIRS)
KV_C = 2
SWA_WINDOW = 128
N_BUCKETS = 32
REL_MAX_DIST = 2048
OFF_B = N_HEADS
OFF_C = N_HEADS + N_DIL * N_HEADS
EPS = 1e-6
NEG = -1e30
LOG2E = math.log2(math.e)

LANE = 128
BAND = SWA_WINDOW
assert all(w == BAND * d for w, d in DIL_PAIRS)
TN = 256
ROWS = 256
TILE_GROUP = 16
MOBA_GROUP = 8
SINGLE_OP_STRIDE = 4
ONES_ROWS = 16
VMEM_LIMIT = 56 * 1024 * 1024

ACT = jnp.bfloat16
F32 = jnp.float32

PLAIN, NORM, SILU, SIGMOID, KNORM_VPLAIN, NORM_D4, PLAIN_D4, NORM_D16, PLAIN_D16 = range(9)
_B_GROUP = lambda g, norm, plain: [(8 + 2 * g, plain), (9 + 2 * g, plain), (14 + 2 * g, norm), (15 + 2 * g, norm),
                                   (20 + 2 * g, plain), (21 + 2 * g, plain)]
_LAYOUT = ([(j, SIGMOID) for j in range(33, 45)]
           + [(0, PLAIN), (1, PLAIN), (2, NORM), (3, NORM), (4, PLAIN), (5, PLAIN), (6, SILU), (7, SILU)]
           + _B_GROUP(0, NORM, PLAIN) + _B_GROUP(1, NORM_D4, PLAIN_D4) + _B_GROUP(2, NORM_D16, PLAIN_D16)
           + [(26, SILU), (27, SILU)]
           + [(28, PLAIN), (29, PLAIN), (31, SILU), (32, SILU), (30, KNORM_VPLAIN)])
C_IN = len(_LAYOUT) * TN
PAIR = 2
_LAYOUT = _LAYOUT + [(_LAYOUT[-1][0], KNORM_VPLAIN)] * (-len(_LAYOUT) % PAIR)
assert all(_LAYOUT[j][1] == _LAYOUT[j - j % PAIR][1] for j in range(len(_LAYOUT)))
assert [f for _, f in _LAYOUT].count(KNORM_VPLAIN) == PAIR
N_STEPS = len(_LAYOUT) // PAIR
P_W = len(_LAYOUT) * TN
COL_GATE, COL_A, COL_B, COL_SGB, COL_C = 0, 24, 40, 76, 80
GATE_W = 3 * D_MODEL
assert COL_GATE == 0


def _cparams(n_axes):
    return pltpu.CompilerParams(dimension_semantics=("arbitrary",) * n_axes, vmem_limit_bytes=VMEM_LIMIT)


def _dot(a, b):
    return jnp.dot(a, b, preferred_element_type=F32)


def _sigmoid(a):
    return 0.5 * jnp.tanh(0.5 * a) + 0.5


def _proj_kernel(src_ref, fn_ref, x_ref, lng_ref, *refs):
    del src_ref
    w_refs, gain_refs = refs[:PAIR], refs[PAIR:2 * PAIR]
    e_ref, o_ref, xn_ref, acc_ref, acc2_ref = refs[2 * PAIR:]
    j = pl.program_id(1)
    s = x_ref.shape[1]
    n_chunks = s // ROWS
    lanes_per_block = TN // LANE

    @pl.when(j == 0)
    def _():
        def body(i, c):
            rows = pl.ds(pl.multiple_of(i * ROWS, ROWS), ROWS)
            xv = x_ref[0, rows, :]
            ms = jnp.mean(xv * xv, axis=-1, keepdims=True)
            xn_ref[rows, :] = (xv * lax.rsqrt(ms + EPS) * lng_ref[...]).astype(ACT)
            return c
        lax.fori_loop(0, n_chunks, body, 0)

    def head_norm(a, gain):
        ss = _dot((a * a).astype(ACT), e_ref[...])
        return a * lax.rsqrt(ss * (1.0 / HEAD_DIM) + EPS) * gain

    def plain(a, gain):
        return a

    def silu(a, gain):
        return a * _sigmoid(a)

    def sigmoid(a, gain):
        return _sigmoid(a)

    def k_norm_v_plain(a, gain):
        lane = lax.broadcasted_iota(jnp.int32, a.shape, 1)
        return jnp.where(lane < KV_C * HEAD_DIM, head_norm(a, gain), a)

    def phase_rows(src_ref, slab, t, d):
        r, m0 = divmod(t * BAND, s // d)
        return src_ref.at[slab][pl.ds(m0 * d + r, BAND, stride=d), :]

    fn = fn_ref[j]

    def chunk(n):
        h, i = divmod(n, n_chunks)
        return _dot(xn_ref[i * ROWS:(i + 1) * ROWS, :], w_refs[h][...])

    def out_cols(h):
        return slice(h * TN, (h + 1) * TN)

    def direct(code, f, live=PAIR):
        @pl.when(fn == code)
        def _():
            pending = None
            for n in range(live * n_chunks + 1):
                a = chunk(n) if n < live * n_chunks else None
                if pending is not None:
                    h, i = divmod(n - 1, n_chunks)
                    o_ref[0, i * ROWS:(i + 1) * ROWS, out_cols(h)] = f(pending, gain_refs[h][...]).astype(o_ref.dtype)
                pending = a
            if live < PAIR:
                o_ref[0, :, live * TN:] = jnp.zeros((s, (PAIR - live) * TN), o_ref.dtype)

    def phase_major(code, f, d):
        @pl.when(fn == code)
        def _():
            for n in range(PAIR * n_chunks):
                h, i = divmod(n, n_chunks)
                a = chunk(n)
                for c in range(lanes_per_block):
                    acc_ref[h * lanes_per_block + c, i * ROWS:(i + 1) * ROWS, :] = a[:, c * LANE:(c + 1) * LANE]
            src_ref, stride = acc_ref, d
            if d > SINGLE_OP_STRIDE:
                assert d == SINGLE_OP_STRIDE * SINGLE_OP_STRIDE
                for slab in range(PAIR * lanes_per_block):
                    for t in range(s // BAND):
                        acc2_ref[slab, t * BAND:(t + 1) * BAND, :] = phase_rows(acc_ref, slab, t, SINGLE_OP_STRIDE)
                src_ref, stride = acc2_ref, SINGLE_OP_STRIDE
            for h in range(PAIR):
                for t in range(s // BAND):
                    a = jnp.concatenate([phase_rows(src_ref, h * lanes_per_block + c, t, stride)
                                         for c in range(lanes_per_block)], axis=1)
                    o_ref[0, t * BAND:(t + 1) * BAND, out_cols(h)] = f(a, gain_refs[h][...]).astype(o_ref.dtype)

    direct(PLAIN, plain)
    direct(NORM, head_norm)
    direct(SILU, silu)
    direct(SIGMOID, sigmoid)
    direct(KNORM_VPLAIN, k_norm_v_plain, live=1)
    phase_major(NORM_D4, head_norm, DIL_PAIRS[1][1])
    phase_major(PLAIN_D4, plain, DIL_PAIRS[1][1])
    phase_major(NORM_D16, head_norm, DIL_PAIRS[2][1])
    phase_major(PLAIN_D16, plain, DIL_PAIRS[2][1])


def _w_index(bi, j, src_ref, fn_ref, *, layer, slot):
    return layer, 0, src_ref[PAIR * j + slot]


def _gain_index(bi, j, src_ref, fn_ref, *, slot):
    return 0, src_ref[PAIR * j + slot]


def _project(x, ln_g, w_in, gain_row, e_mat, src, fn, layer):
    b, s, d = x.shape
    n_in = PAIR
    grid_spec = pltpu.PrefetchScalarGridSpec(
        num_scalar_prefetch=2,
        grid=(b, N_STEPS),
        in_specs=[
            pl.BlockSpec((1, s, d), lambda bi, j, src_ref, fn_ref: (bi, 0, 0)),
            pl.BlockSpec((None, 1, d), lambda bi, j, src_ref, fn_ref: (layer, 0, 0)),
        ] + [pl.BlockSpec((None, d, TN), functools.partial(_w_index, layer=layer, slot=n)) for n in range(n_in)]
          + [pl.BlockSpec((1, TN), functools.partial(_gain_index, slot=n)) for n in range(n_in)]
          + [pl.BlockSpec((TN, TN), lambda bi, j, src_ref, fn_ref: (0, 0))],
        out_specs=pl.BlockSpec((1, s, n_in * TN), lambda bi, j, src_ref, fn_ref: (bi, 0, j)),
        scratch_shapes=[pltpu.VMEM((s, d), ACT), pltpu.VMEM((PAIR * TN // LANE, s, LANE), F32),
                        pltpu.VMEM((PAIR * TN // LANE, s, LANE), F32)],
    )
    return pl.pallas_call(
        _proj_kernel,
        grid_spec=grid_spec,
        out_shape=jax.ShapeDtypeStruct((b, s, P_W), ACT),
        compiler_params=_cparams(2),
        name="proj",
    )(src, fn, x, ln_g, *([w_in] * n_in), *([gain_row] * n_in), e_mat)


def _toeplitz(lo, hi, n):
    x = jnp.concatenate([jnp.broadcast_to(lo, (n, n)), jnp.broadcast_to(hi, (n, n))], axis=1)
    return pltpu.roll(x, 0, 1, stride=1, stride_axis=0)[:, n:]


def _unit_rms_heads(qt):
    halves = []
    for sh in range(2):
        x = qt[HEAD_DIM * sh:HEAD_DIM * (sh + 1), :].astype(F32)
        halves.append(x * lax.rsqrt(jnp.mean(x * x, axis=0, keepdims=True) + EPS))
    return jnp.concatenate(halves, axis=0).astype(qt.dtype)


def _head_rows(qt, lo):
    row = lax.broadcasted_iota(jnp.int32, qt.shape, 0)
    return jnp.where((row >= lo) & (row < lo + HEAD_DIM), qt, jnp.zeros_like(qt))


def _moba_kernel(q_ref, k_ref, v_ref, sg_ref, tv_ref, o_ref,
                 bt_ref, qts_ref, vt_ref, km_ref, sel_ref, m_ref, acc_ref):
    hp = pl.program_id(0)
    b = pl.program_id(1)
    blk = MOBA_BLOCK
    n_blk = k_ref.shape[1] // blk

    def rows_of(i):
        return pl.ds(i * blk if isinstance(i, int) else pl.multiple_of(i * blk, blk), blk)

    @pl.when(b == 0)
    def _():
        kr = lax.broadcasted_iota(jnp.int32, (blk, blk), 0)
        qc = lax.broadcasted_iota(jnp.int32, (blk, blk), 1)
        for sh in range(2):
            def body(o, c, sh=sh):
                t = _toeplitz(tv_ref[2 * hp + sh, o], tv_ref[2 * hp + sh, o + 1], blk)
                bt_ref[sh, o] = jnp.where((qc >= kr) | (o > 0), t, NEG)
                return c
            lax.fori_loop(0, n_blk, body, 0)

    def prep(i, c):
        vt = v_ref[0, rows_of(i), :].T
        for sh in range(2):
            vt_ref[i, sh] = jnp.concatenate(
                [vt[HEAD_DIM * sh:HEAD_DIM * (sh + 1), :], jnp.ones((ONES_ROWS, blk), ACT)], axis=0)
        km_ref[pl.ds(i, 1), :] = jnp.mean(k_ref[0, rows_of(i), :].astype(F32), axis=0, keepdims=True)
        qt = _unit_rms_heads(q_ref[0, rows_of(i), :].T)
        for sh in range(2):
            qts_ref[i, sh] = _head_rows(qt, HEAD_DIM * sh)
        return c
    lax.fori_loop(0, n_blk, prep, 0, unroll=4)

    km = km_ref[...]
    km0 = km.astype(ACT)
    r1 = km - km0.astype(F32)
    km1 = r1.astype(ACT)
    km2 = (r1 - km1.astype(F32)).astype(ACT)
    blk_row = lax.broadcasted_iota(jnp.int32, (n_blk, blk), 0)

    ranked = [(qi, sh) for qi in range(MOBA_TOPK + 1, n_blk) for sh in range(2)]
    gates = [_dot(km0, qts_ref[qi, sh]) + _dot(km1, qts_ref[qi, sh]) + _dot(km2, qts_ref[qi, sh])
             for qi, sh in ranked]
    for (qi, sh), g in zip(ranked, gates):
        g = jnp.where(blk_row < qi, g, NEG)
        sel_rows = []
        for jb in range(qi):
            gj = g[jb:jb + 1, :]
            beats = (g > gj) | ((g == gj) & (blk_row < jb))
            cnt = jnp.sum(beats.astype(F32), axis=0, keepdims=True)
            sel_rows.append(jnp.where(cnt < MOBA_TOPK, 1.0, 0.0))
        sel_rows.append(jnp.zeros((n_blk - qi, blk), F32))
        sel_ref[qi, sh] = jnp.concatenate(sel_rows, axis=0)

    def tiles(off, qis):
        k_blks = [k_ref[0, rows_of(qi - off), :] for qi in qis]
        scores = [_dot(k_blk, qts_ref[qi, sh]) for qi, k_blk in zip(qis, k_blks) for sh in range(2)]
        work = [(qi, sh) for qi in qis for sh in range(2)]
        stats = []
        half = blk // 2
        for (qi, sh), st in zip(work, scores):
            if off == 0:
                bias = bt_ref[sh, 0]
                parts = [(st[:half, :half] + bias[:half, :half], slice(0, half)),
                         (st[:, half:] + bias[:, half:], slice(0, blk))]
                maxes = [jnp.max(s_part, axis=0, keepdims=True) for s_part, _ in parts]
                m_ref[qi, sh] = jnp.concatenate(maxes, axis=1)
                stats.append((None, [(keys, jnp.exp2(s_part - m_part).astype(ACT))
                                     for (s_part, keys), m_part in zip(parts, maxes)]))
                continue
            st = st + bt_ref[sh, off]
            tile_max = jnp.max(st, axis=0, keepdims=True)
            if qi <= MOBA_TOPK:
                m = m_ref[qi, sh]
                m_new = jnp.maximum(m, tile_max)
                p = jnp.exp2(st - m_new)
                alpha = jnp.exp2(m - m_new)
            else:
                sel = sel_ref[qi, sh, pl.ds(qi - off, 1), :] > 0.5
                m = m_ref[qi, sh]
                m_new = jnp.where(sel, jnp.maximum(m, tile_max), m)
                p = jnp.exp2(st - jnp.where(sel, m_new, -NEG))
                alpha = jnp.exp2(m - m_new)
            m_ref[qi, sh] = m_new
            stats.append((alpha, [(slice(0, blk), p.astype(ACT))]))
        pvs = [jnp.concatenate([_dot(vt_ref[qi - off, sh, :, keys], p) for keys, p in parts], axis=1)
               for (qi, sh), (_, parts) in zip(work, stats)]
        for (qi, sh), (alpha, _), pv in zip(work, stats, pvs):
            acc_ref[qi, sh] = pv if off == 0 else alpha * acc_ref[qi, sh] + pv

    for off in range(n_blk):
        qis = list(range(off, n_blk))
        for lo in range(0, len(qis), MOBA_GROUP):
            tiles(off, qis[lo:lo + MOBA_GROUP])

    def finish(qi, c):
        ot = jnp.concatenate([acc_ref[qi, sh, :HEAD_DIM, :] * (1.0 / acc_ref[qi, sh, HEAD_DIM:HEAD_DIM + 1, :])
                              for sh in range(2)], axis=0)
        o_ref[0, rows_of(qi), :] = (ot.T * sg_ref[0, rows_of(qi), :].astype(F32)).astype(o_ref.dtype)
        return c
    lax.fori_loop(0, n_blk, finish, 0, unroll=4)


def _moba(p, tv_a):
    b, s, _ = p.shape
    blk = MOBA_BLOCK
    n_blk = s // blk
    col = lambda c: pl.BlockSpec((1, s, LANE), lambda hp, bi: (bi, 0, COL_A + c * N_PAIRS + hp))
    return pl.pallas_call(
        _moba_kernel,
        grid=(N_PAIRS, b),
        in_specs=[col(0), col(1), col(2), col(3), pl.BlockSpec(tv_a.shape, lambda hp, bi: (0, 0, 0, 0))],
        out_specs=pl.BlockSpec((1, s, LANE), lambda hp, bi: (bi, 0, hp)),
        out_shape=jax.ShapeDtypeStruct((b, s, W_BR), ACT),
        scratch_shapes=[
            pltpu.VMEM((2, n_blk, blk, blk), F32),
            pltpu.VMEM((n_blk, 2, LANE, blk), ACT),
            pltpu.VMEM((n_blk, 2, HEAD_DIM + ONES_ROWS, blk), ACT),
            pltpu.VMEM((n_blk, LANE), F32),
            pltpu.VMEM((n_blk, 2, n_blk, blk), F32),
            pltpu.VMEM((n_blk, 2, 1, blk), F32),
            pltpu.VMEM((n_blk, 2, HEAD_DIM + ONES_ROWS, blk), F32),
        ],
        compiler_params=_cparams(2),
        name="moba",
    )(p, p, p, p, tv_a)


def _band_attend(items):
    def fuse(blocks):
        if len(blocks) > 1 and all(valid is None for _, _, _, valid in blocks):
            ks, vts, biases, _ = zip(*blocks)
            return [(jnp.concatenate(ks, axis=0), jnp.concatenate(vts, axis=1), jnp.concatenate(biases, axis=0), None)]
        return blocks

    items = [(qs, fuse(blocks), sink) for qs, blocks, sink in items]
    scores = [[_dot(k, qs) for k, _, _, _ in blocks] for qs, blocks, _ in items]
    probs = []
    for (_, blocks, sink), sc in zip(items, scores):
        ss = [s + bias if valid is None else jnp.where(valid, s + bias, NEG)
              for s, (_, _, bias, valid) in zip(sc, blocks)]
        m = functools.reduce(jnp.maximum, [jnp.max(s, axis=0, keepdims=True) for s in ss])
        if sink is not None:
            m = jnp.maximum(m, sink)
        probs.append((m, [jnp.exp2(s - m).astype(ACT) for s in ss]))
    outs = []
    for (_, blocks, sink), (m, ps) in zip(items, probs):
        n_v = blocks[0][1].shape[0]
        acc = functools.reduce(operator.add, [
            _dot(jnp.concatenate([vt, jnp.ones((ONES_ROWS, vt.shape[1]), ACT)], axis=0), p)
            for (_, vt, _, _), p in zip(blocks, ps)])
        l = acc[n_v:n_v + 1, :]
        if sink is not None:
            l = l + jnp.exp2(sink - m)
        outs.append((acc[:n_v, :] * (1.0 / l), m + jnp.log2(l)))
    return outs


def _band_bias(tv_ref, head, prev_includes_window_edge):
    kr = lax.broadcasted_iota(jnp.int32, (BAND, BAND), 0)
    qc = lax.broadcasted_iota(jnp.int32, (BAND, BAND), 1)
    own, prev = [_toeplitz(tv_ref[head, which], tv_ref[head, which + 1], BAND) for which in range(2)]
    prev_mask = (qc <= kr) if prev_includes_window_edge else (qc < kr)
    return jnp.where(qc >= kr, own, NEG), jnp.where(prev_mask, prev, NEG)


def _dil_kernel(q0_ref, k0_ref, v0_ref, q1_ref, k1_ref, v1_ref, q2_ref, k2_ref, v2_ref, sg_ref, tv_ref,
                o_ref, og_ref, lg_ref, bt_ref):
    hp = pl.program_id(0)
    b = pl.program_id(1)
    s = q0_ref.shape[1]
    n_tiles = s // BAND

    @pl.when(b == 0)
    def _():
        for g in range(N_DIL):
            for sh in range(2):
                own, prev = _band_bias(tv_ref, g * N_HEADS + 2 * hp + sh, True)
                bt_ref[g, sh, 0] = own
                bt_ref[g, sh, 1] = prev

    groups = ((q0_ref, k0_ref, v0_ref), (q1_ref, k1_ref, v1_ref), (q2_ref, k2_ref, v2_ref))

    for g, (q_ref, k_ref, v_ref) in enumerate(groups):
        d = DIL_PAIRS[g][1]
        per_phase = n_tiles // d
        has_prev = per_phase > 1
        assert per_phase % TILE_GROUP == 0 or TILE_GROUP % per_phase == 0

        def body(i, c, g=g, d=d, per_phase=per_phase, has_prev=has_prev, q_ref=q_ref, k_ref=k_ref, v_ref=v_ref):
            items, dsts = [], []
            k_own = vt_own = None
            for n in range(TILE_GROUP):
                u = i * TILE_GROUP + n
                rows = pl.ds(pl.multiple_of(u * BAND, BAND), BAND)
                t = lax.rem(u, per_phase)
                r = lax.div(u, per_phase)
                qt = _unit_rms_heads(q_ref[0, rows, :].T)
                k_prev, vt_prev = k_own, vt_own
                k_own = k_ref[0, rows, :]
                vt_own = v_ref[0, rows, :].T
                if has_prev:
                    phase_start_known = TILE_GROUP % per_phase == 0
                    prev_exists = (t > 0) if (n == 0 and not phase_start_known) else None
                    if phase_start_known and n % per_phase == 0:
                        k_prev = None
                    elif n == 0:
                        prow = pl.ds(pl.multiple_of(jnp.maximum(u - 1, 0) * BAND, BAND), BAND)
                        k_prev = k_ref[0, prow, :]
                        vt_prev = v_ref[0, prow, :].T
                for sh in range(2):
                    hrows = slice(HEAD_DIM * sh, HEAD_DIM * (sh + 1))
                    blocks = [(k_own, vt_own[hrows, :], bt_ref[g, sh, 0], None)]
                    if has_prev and k_prev is not None:
                        blocks.append((k_prev, vt_prev[hrows, :], bt_ref[g, sh, 1], prev_exists))
                    items.append((_head_rows(qt, HEAD_DIM * sh), blocks, None))
                dsts.append(rows if d == 1 else pl.ds(t * (BAND * d) + r, BAND, stride=d))
            res = _band_attend(items)
            for n, dst in enumerate(dsts):
                (o0, lse0), (o1, lse1) = res[2 * n], res[2 * n + 1]
                og_ref.at[g][dst, :] = jnp.concatenate([o0, o1], axis=0).T
                lg_ref.at[g][dst, :] = jnp.concatenate(
                    [jnp.broadcast_to(lse0, (HEAD_DIM, BAND)), jnp.broadcast_to(lse1, (HEAD_DIM, BAND))], axis=0).T
            return c

        lax.fori_loop(0, n_tiles // TILE_GROUP, body, 0)

    def merge(i, c):
        rows = pl.ds(pl.multiple_of(i * ROWS, ROWS), ROWS)
        l0, l1, l2 = lg_ref[0, rows, :], lg_ref[1, rows, :], lg_ref[2, rows, :]
        m = jnp.maximum(l0, jnp.maximum(l1, l2))
        e0, e1, e2 = jnp.exp2(l0 - m), jnp.exp2(l1 - m), jnp.exp2(l2 - m)
        ob = (e0 * og_ref[0, rows, :] + e1 * og_ref[1, rows, :] + e2 * og_ref[2, rows, :]) / (e0 + e1 + e2)
        o_ref[0, rows, :] = (ob * sg_ref[0, rows, :].astype(F32)).astype(o_ref.dtype)
        return c

    lax.fori_loop(0, s // ROWS, merge, 0)


def _dilated(p, tv_b):
    b, s, _ = p.shape
    col = lambda c: pl.BlockSpec((1, s, LANE), lambda hp, bi: (bi, 0, c + hp))
    qkv = [col(COL_B + 3 * N_PAIRS * g + N_PAIRS * i) for g in range(N_DIL) for i in range(3)]
    return pl.pallas_call(
        _dil_kernel,
        grid=(N_PAIRS, b),
        in_specs=qkv + [col(COL_SGB), pl.BlockSpec(tv_b.shape, lambda hp, bi: (0, 0, 0, 0))],
        out_specs=pl.BlockSpec((1, s, LANE), lambda hp, bi: (bi, 0, hp)),
        out_shape=jax.ShapeDtypeStruct((b, s, W_BR), ACT),
        scratch_shapes=[
            pltpu.VMEM((N_DIL, s, LANE), F32),
            pltpu.VMEM((N_DIL, s, LANE), F32),
            pltpu.VMEM((N_DIL, 2, 2, BAND, BAND), F32),
        ],
        compiler_params=_cparams(2),
        name="dilated",
    )(*([p] * 10), tv_b)


def _swa_kernel(sink_ref, q_ref, sg_ref, k_ref, v_ref, tv_ref, o_ref, bt_ref, vts_ref):
    hp = pl.program_id(0)
    b = pl.program_id(1)
    n_tiles = q_ref.shape[1] // BAND
    pairs_per_kv = N_PAIRS // KV_C

    @pl.when(b == 0)
    def _():
        for sh in range(2):
            own, prev = _band_bias(tv_ref, 2 * hp + sh, False)
            bt_ref[sh, 0] = own
            bt_ref[sh, 1] = prev

    kv_lo = HEAD_DIM * (hp // pairs_per_kv)
    row = lax.broadcasted_iota(jnp.int32, (LANE, BAND), 0)
    kv_rows = (row >= kv_lo) & (row < kv_lo + HEAD_DIM)

    def kv_vt(rows, slot):
        vts_ref[slot] = v_ref[0, rows, :].T
        return vts_ref[slot, pl.ds(pl.multiple_of(kv_lo, HEAD_DIM), HEAD_DIM), :]

    def body(i, c):
        items, dsts = [], []
        zero = jnp.zeros((LANE, BAND), ACT)
        k_own = vt_own = None
        for n in range(TILE_GROUP):
            t = i * TILE_GROUP + n
            rows = pl.ds(pl.multiple_of(t * BAND, BAND), BAND)
            qt = _unit_rms_heads(q_ref[0, rows, :].T)
            k_prev, vt_prev = k_own, vt_own
            k_own = k_ref[0, rows, :]
            vt_own = kv_vt(rows, n)
            if n == 0:
                prow = pl.ds(pl.multiple_of(jnp.maximum(t - 1, 0) * BAND, BAND), BAND)
                k_prev = k_ref[0, prow, :]
                vt_prev = kv_vt(prow, TILE_GROUP)
            prev_exists = (t > 0) if n == 0 else None
            for sh in range(2):
                qh = qt[HEAD_DIM * sh:HEAD_DIM * (sh + 1), :]
                qs = jnp.where(kv_rows, jnp.concatenate([qh, qh], axis=0), zero)
                items.append((qs, [(k_own, vt_own, bt_ref[sh, 0], None), (k_prev, vt_prev, bt_ref[sh, 1], prev_exists)],
                              sink_ref[2 * hp + sh]))
            dsts.append(rows)
        res = _band_attend(items)
        for n, rows in enumerate(dsts):
            ot = jnp.concatenate([o for o, _ in res[2 * n:2 * n + 2]], axis=0).T
            o_ref[0, rows, :] = (ot * sg_ref[0, rows, :].astype(F32)).astype(o_ref.dtype)
        return c

    lax.fori_loop(0, n_tiles // TILE_GROUP, body, 0)


def _swa(p, tv_c, sinks):
    b, s, _ = p.shape
    grid_spec = pltpu.PrefetchScalarGridSpec(
        num_scalar_prefetch=1,
        grid=(N_PAIRS, b),
        in_specs=[
            pl.BlockSpec((1, s, LANE), lambda hp, bi, sk: (bi, 0, COL_C + hp)),
            pl.BlockSpec((1, s, LANE), lambda hp, bi, sk: (bi, 0, COL_C + N_PAIRS + hp)),
            pl.BlockSpec((1, s, LANE), lambda hp, bi, sk: (bi, 0, COL_C + 2 * N_PAIRS)),
            pl.BlockSpec((1, s, LANE), lambda hp, bi, sk: (bi, 0, COL_C + 2 * N_PAIRS + 1)),
            pl.BlockSpec(tv_c.shape, lambda hp, bi, sk: (0, 0, 0, 0)),
        ],
        out_specs=pl.BlockSpec((1, s, LANE), lambda hp, bi, sk: (bi, 0, hp)),
        scratch_shapes=[pltpu.VMEM((2, 2, BAND, BAND), F32),
                        pltpu.VMEM((TILE_GROUP + 1, LANE, BAND), ACT)],
    )
    return pl.pallas_call(
        _swa_kernel,
        grid_spec=grid_spec,
        out_shape=jax.ShapeDtypeStruct((b, s, W_BR), ACT),
        compiler_params=_cparams(2),
        name="swa",
    )(sinks, p, p, p, p, tv_c)


def _merge_kernel(bra_ref, brb_ref, brc_ref, g_ref, wb_ref, wo_ref, x_ref, out_ref):
    merged = None
    for i, br_ref in enumerate((bra_ref, brb_ref, brc_ref)):
        term = g_ref[:, i * D_MODEL:(i + 1) * D_MODEL].astype(F32) * _dot(br_ref[...], wb_ref[i])
        merged = term if merged is None else merged + term
    out_ref[...] = x_ref[...] + _dot(merged.astype(ACT), wo_ref[...])


def _merge(bra, brb, brc, p2, w_br, w_out, x2, layer, ts=512):
    n = x2.shape[0]
    row = lambda w: pl.BlockSpec((ts, w), lambda i: (i, 0))
    return pl.pallas_call(
        _merge_kernel,
        grid=(n // ts,),
        in_specs=[row(W_BR), row(W_BR), row(W_BR), row(GATE_W),
                  pl.BlockSpec((None,) + w_br.shape[1:], lambda i: (layer, 0, 0, 0)),
                  pl.BlockSpec((None,) + w_out.shape[1:], lambda i: (layer, 0, 0)),
                  row(D_MODEL)],
        out_specs=row(D_MODEL),
        out_shape=jax.ShapeDtypeStruct(x2.shape, F32),
        compiler_params=_cparams(1),
        name="merge",
    )(bra, brb, brc, p2, w_br, w_out, x2)


def _rel_bucket(dist):
    dist = jnp.maximum(dist, 0)
    max_exact = N_BUCKETS // 2
    log_ratio = jnp.log(jnp.maximum(dist, 1).astype(F32) / max_exact) / math.log(REL_MAX_DIST / max_exact)
    large = jnp.minimum(max_exact + (log_ratio * (N_BUCKETS - max_exact)).astype(jnp.int32), N_BUCKETS - 1)
    return jnp.where(dist < max_exact, dist, large)


def _bias_by_distance(tab, dist, chunk):
    vals = tab.astype(F32)[:, _rel_bucket(dist)] * LOG2E
    vals = jnp.pad(vals, ((0, 0), (chunk, 0)))
    return vals.reshape(tab.shape[0], -1, 1, chunk)


def _gain_rows(qk_g):
    depth = qk_g.shape[0]
    scale = HEAD_DIM ** -0.5 * LOG2E
    one = jnp.ones((depth, HEAD_DIM), F32)
    tile = lambda v, n: jnp.tile(v, (1, n))
    parts = [tile(one, N_HEADS), tile(qk_g[:, 1] * qk_g[:, 0] * scale, N_HEADS), tile(one, 2 * N_HEADS),
             tile(one, N_DIL * N_HEADS), tile(qk_g[:, 3] * qk_g[:, 2] * scale, N_DIL * N_HEADS),
             tile(one, (N_DIL + 1) * N_HEADS),
             tile(one, N_HEADS), tile(qk_g[:, 5] * qk_g[:, 4] * scale, KV_C), tile(one, KV_C + N_HEADS),
             tile(one, GATE_W // HEAD_DIM)]
    return jnp.concatenate(parts, axis=1)


def kernel(x, ln_g, w_in, qk_g, sinks, w_branch, w_out, rel_bias):
    b, s, d = x.shape
    depth = ln_g.shape[0]
    assert d == D_MODEL and s % DIL_PAIRS[-1][0] == 0 and w_in.shape[-1] == C_IN

    tv_a = _bias_by_distance(rel_bias[:N_HEADS], jnp.arange(s), MOBA_BLOCK)
    tv_b = jnp.concatenate(
        [_bias_by_distance(rel_bias[OFF_B + g * N_HEADS:OFF_B + (g + 1) * N_HEADS], dd * jnp.arange(2 * BAND), BAND)
         for g, (_, dd) in enumerate(DIL_PAIRS)], axis=0)
    tv_c = _bias_by_distance(rel_bias[OFF_C:OFF_C + N_HEADS], jnp.arange(2 * BAND), BAND)
    seg = jnp.arange(TN) // HEAD_DIM
    e_mat = (seg[:, None] == seg[None, :]).astype(ACT)
    src = jnp.asarray([c for c, _ in _LAYOUT], jnp.int32)
    fn = jnp.asarray([f for _, f in _LAYOUT[::PAIR]], jnp.int32)
    gains = _gain_rows(qk_g)

    w_in_b = w_in.astype(ACT)
    w_br_b = w_branch.astype(ACT)
    w_out_b = w_out.astype(ACT)

    for l in range(depth):
        p = _project(x, ln_g[:, None, :], w_in_b, gains[l][None, :], e_mat, src, fn, l)
        bra = _moba(p, tv_a)
        brb = _dilated(p, tv_b)
        brc = _swa(p, tv_c, sinks[l] * LOG2E)
        x2 = _merge(bra.reshape(b * s, W_BR), brb.reshape(b * s, W_BR), brc.reshape(b * s, W_BR),
                    p.reshape(b * s, P_W), w_br_b, w_out_b, x.reshape(b * s, d), l)
        x = x2.reshape(b, s, d)
    return x
```

```python
import functools
import math
import operator

import jax
import jax.numpy as jnp
from jax import lax
from jax.experimental import pallas as pl
from jax.experimental.pallas import tpu as pltpu

D_MODEL = 1024
HEAD_DIM = 64
W_BR = D_MODEL // 2
N_HEADS = W_BR // HEAD_DIM
N_PAIRS = N_HEADS // 2
MOBA_BLOCK = 256
MOBA_TOPK = 3
DIL_PAIRS = ((128, 1), (512, 4), (2048, 16))
N_DIL = len(DIL_PAIRS)
KV_C = 2
SWA_WINDOW = 128
N_BUCKETS = 32
REL_MAX_DIST = 2048
OFF_B = N_HEADS
OFF_C = N_HEADS + N_DIL * N_HEADS
EPS = 1e-6
NEG = -1e30
LOG2E = math.log2(math.e)

LANE = 128
BAND = SWA_WINDOW
assert all(w == BAND * d for w, d in DIL_PAIRS)
TN = 256
ROWS = 256
TILE_GROUP = 16
MOBA_GROUP = 8
SINGLE_OP_STRIDE = 4
ONES_ROWS = 16
VMEM_LIMIT = 56 * 1024 * 1024

ACT = jnp.bfloat16
F32 = jnp.float32

PLAIN, NORM, SILU, SIGMOID, KNORM_VPLAIN, NORM_D4, PLAIN_D4, NORM_D16, PLAIN_D16 = range(9)
_B_GROUP = lambda g, norm, plain: [(8 + 2 * g, plain), (9 + 2 * g, plain), (14 + 2 * g, norm), (15 + 2 * g, norm),
                                   (20 + 2 * g, plain), (21 + 2 * g, plain)]
_LAYOUT = ([(j, SIGMOID) for j in range(33, 45)]
           + [(0, PLAIN), (1, PLAIN), (2, NORM), (3, NORM), (4, PLAIN), (5, PLAIN), (6, SILU), (7, SILU)]
           + _B_GROUP(0, NORM, PLAIN) + _B_GROUP(1, NORM_D4, PLAIN_D4) + _B_GROUP(2, NORM_D16, PLAIN_D16)
           + [(26, SILU), (27, SILU)]
           + [(28, PLAIN), (29, PLAIN), (31, SILU), (32, SILU), (30, KNORM_VPLAIN)])
C_IN = len(_LAYOUT) * TN
PAIR = 2
_LAYOUT = _LAYOUT + [(_LAYOUT[-1][0], KNORM_VPLAIN)] * (-len(_LAYOUT) % PAIR)
assert all(_LAYOUT[j][1] == _LAYOUT[j - j % PAIR][1] for j in range(len(_LAYOUT)))
assert [f for _, f in _LAYOUT].count(KNORM_VPLAIN) == PAIR
N_STEPS = len(_LAYOUT) // PAIR
P_W = len(_LAYOUT) * TN
COL_GATE, COL_A, COL_B, COL_SGB, COL_C = 0, 24, 40, 76, 80
GATE_W = 3 * D_MODEL
assert COL_GATE == 0


def _cparams(n_axes):
    return pltpu.CompilerParams(dimension_semantics=("arbitrary",) * n_axes, vmem_limit_bytes=VMEM_LIMIT)


def _dot(a, b):
    return jnp.dot(a, b, preferred_element_type=F32)


def _sigmoid(a):
    return 0.5 * jnp.tanh(0.5 * a) + 0.5


def _proj_kernel(src_ref, fn_ref, x_ref, lng_ref, *refs):
    del src_ref
    w_refs, gain_refs = refs[:PAIR], refs[PAIR:2 * PAIR]
    e_ref, o_ref, xn_ref, acc_ref, acc2_ref = refs[2 * PAIR:]
    j = pl.program_id(1)
    s = x_ref.shape[1]
    n_chunks = s // ROWS
    lanes_per_block = TN // LANE

    @pl.when(j == 0)
    def _():
        def body(i, c):
            rows = pl.ds(pl.multiple_of(i * ROWS, ROWS), ROWS)
            xv = x_ref[0, rows, :]
            ms = jnp.mean(xv * xv, axis=-1, keepdims=True)
            xn_ref[rows, :] = (xv * lax.rsqrt(ms + EPS) * lng_ref[...]).astype(ACT)
            return c
        lax.fori_loop(0, n_chunks, body, 0)

    def head_norm(a, gain):
        ss = _dot((a * a).astype(ACT), e_ref[...])
        return a * lax.rsqrt(ss * (1.0 / HEAD_DIM) + EPS) * gain

    def plain(a, gain):
        return a

    def silu(a, gain):
        return a * _sigmoid(a)

    def sigmoid(a, gain):
        return _sigmoid(a)

    def k_norm_v_plain(a, gain):
        lane = lax.broadcasted_iota(jnp.int32, a.shape, 1)
        return jnp.where(lane < KV_C * HEAD_DIM, head_norm(a, gain), a)

    def phase_rows(src_ref, slab, t, d):
        r, m0 = divmod(t * BAND, s // d)
        return src_ref.at[slab][pl.ds(m0 * d + r, BAND, stride=d), :]

    fn = fn_ref[j]

    def chunk(n):
        h, i = divmod(n, n_chunks)
        return _dot(xn_ref[i * ROWS:(i + 1) * ROWS, :], w_refs[h][...])

    def out_cols(h):
        return slice(h * TN, (h + 1) * TN)

    def direct(code, f, live=PAIR):
        @pl.when(fn == code)
        def _():
            pending = None
            for n in range(live * n_chunks + 1):
                a = chunk(n) if n < live * n_chunks else None
                if pending is not None:
                    h, i = divmod(n - 1, n_chunks)
                    o_ref[0, i * ROWS:(i + 1) * ROWS, out_cols(h)] = f(pending, gain_refs[h][...]).astype(o_ref.dtype)
                pending = a
            if live < PAIR:
                o_ref[0, :, live * TN:] = jnp.zeros((s, (PAIR - live) * TN), o_ref.dtype)

    def phase_major(code, f, d):
        @pl.when(fn == code)
        def _():
            for n in range(PAIR * n_chunks):
                h, i = divmod(n, n_chunks)
                a = chunk(n)
                for c in range(lanes_per_block):
                    acc_ref[h * lanes_per_block + c, i * ROWS:(i + 1) * ROWS, :] = a[:, c * LANE:(c + 1) * LANE]
            src_ref, stride = acc_ref, d
            if d > SINGLE_OP_STRIDE:
                assert d == SINGLE_OP_STRIDE * SINGLE_OP_STRIDE
                for slab in range(PAIR * lanes_per_block):
                    for t in range(s // BAND):
                        acc2_ref[slab, t * BAND:(t + 1) * BAND, :] = phase_rows(acc_ref, slab, t, SINGLE_OP_STRIDE)
                src_ref, stride = acc2_ref, SINGLE_OP_STRIDE
            for h in range(PAIR):
                for t in range(s // BAND):
                    a = jnp.concatenate([phase_rows(src_ref, h * lanes_per_block + c, t, stride)
                                         for c in range(lanes_per_block)], axis=1)
                    o_ref[0, t * BAND:(t + 1) * BAND, out_cols(h)] = f(a, gain_refs[h][...]).astype(o_ref.dtype)

    direct(PLAIN, plain)
    direct(NORM, head_norm)
    direct(SILU, silu)
    direct(SIGMOID, sigmoid)
    direct(KNORM_VPLAIN, k_norm_v_plain, live=1)
    phase_major(NORM_D4, head_norm, DIL_PAIRS[1][1])
    phase_major(PLAIN_D4, plain, DIL_PAIRS[1][1])
    phase_major(NORM_D16, head_norm, DIL_PAIRS[2][1])
    phase_major(PLAIN_D16, plain, DIL_PAIRS[2][1])


def _w_index(bi, j, src_ref, fn_ref, *, layer, slot):
    return layer, 0, src_ref[PAIR * j + slot]


def _gain_index(bi, j, src_ref, fn_ref, *, slot):
    return 0, src_ref[PAIR * j + slot]


def _project(x, ln_g, w_in, gain_row, e_mat, src, fn, layer):
    b, s, d = x.shape
    n_in = PAIR
    grid_spec = pltpu.PrefetchScalarGridSpec(
        num_scalar_prefetch=2,
        grid=(b, N_STEPS),
        in_specs=[
            pl.BlockSpec((1, s, d), lambda bi, j, src_ref, fn_ref: (bi, 0, 0)),
            pl.BlockSpec((None, 1, d), lambda bi, j, src_ref, fn_ref: (layer, 0, 0)),
        ] + [pl.BlockSpec((None, d, TN), functools.partial(_w_index, layer=layer, slot=n)) for n in range(n_in)]
          + [pl.BlockSpec((1, TN), functools.partial(_gain_index, slot=n)) for n in range(n_in)]
          + [pl.BlockSpec((TN, TN), lambda bi, j, src_ref, fn_ref: (0, 0))],
        out_specs=pl.BlockSpec((1, s, n_in * TN), lambda bi, j, src_ref, fn_ref: (bi, 0, j)),
        scratch_shapes=[pltpu.VMEM((s, d), ACT), pltpu.VMEM((PAIR * TN // LANE, s, LANE), F32),
                        pltpu.VMEM((PAIR * TN // LANE, s, LANE), F32)],
    )
    return pl.pallas_call(
        _proj_kernel,
        grid_spec=grid_spec,
        out_shape=jax.ShapeDtypeStruct((b, s, P_W), ACT),
        compiler_params=_cparams(2),
        name="proj",
    )(src, fn, x, ln_g, *([w_in] * n_in), *([gain_row] * n_in), e_mat)


def _toeplitz(lo, hi, n):
    x = jnp.concatenate([jnp.broadcast_to(lo, (n, n)), jnp.broadcast_to(hi, (n, n))], axis=1)
    return pltpu.roll(x, 0, 1, stride=1, stride_axis=0)[:, n:]


def _unit_rms_heads(qt):
    halves = []
    for sh in range(2):
        x = qt[HEAD_DIM * sh:HEAD_DIM * (sh + 1), :].astype(F32)
        halves.append(x * lax.rsqrt(jnp.mean(x * x, axis=0, keepdims=True) + EPS))
    return jnp.concatenate(halves, axis=0).astype(qt.dtype)


def _head_rows(qt, lo):
    row = lax.broadcasted_iota(jnp.int32, qt.shape, 0)
    return jnp.where((row >= lo) & (row < lo + HEAD_DIM), qt, jnp.zeros_like(qt))


def _moba_kernel(q_ref, k_ref, v_ref, sg_ref, tv_ref, o_ref,
                 bt_ref, qts_ref, vt_ref, km_ref, sel_ref, m_ref, acc_ref):
    hp = pl.program_id(0)
    b = pl.program_id(1)
    blk = MOBA_BLOCK
    n_blk = k_ref.shape[1] // blk

    def rows_of(i):
        return pl.ds(i * blk if isinstance(i, int) else pl.multiple_of(i * blk, blk), blk)

    @pl.when(b == 0)
    def _():
        kr = lax.broadcasted_iota(jnp.int32, (blk, blk), 0)
        qc = lax.broadcasted_iota(jnp.int32, (blk, blk), 1)
        for sh in range(2):
            def body(o, c, sh=sh):
                t = _toeplitz(tv_ref[2 * hp + sh, o], tv_ref[2 * hp + sh, o + 1], blk)
                bt_ref[sh, o] = jnp.where((qc >= kr) | (o > 0), t, NEG)
                return c
            lax.fori_loop(0, n_blk, body, 0)

    def prep(i, c):
        vt = v_ref[0, rows_of(i), :].T
        for sh in range(2):
            vt_ref[i, sh] = jnp.concatenate(
                [vt[HEAD_DIM * sh:HEAD_DIM * (sh + 1), :], jnp.ones((ONES_ROWS, blk), ACT)], axis=0)
        km_ref[pl.ds(i, 1), :] = jnp.mean(k_ref[0, rows_of(i), :].astype(F32), axis=0, keepdims=True)
        qt = _unit_rms_heads(q_ref[0, rows_of(i), :].T)
        for sh in range(2):
            qts_ref[i, sh] = _head_rows(qt, HEAD_DIM * sh)
        return c
    lax.fori_loop(0, n_blk, prep, 0, unroll=4)

    km = km_ref[...]
    km0 = km.astype(ACT)
    r1 = km - km0.astype(F32)
    km1 = r1.astype(ACT)
    km2 = (r1 - km1.astype(F32)).astype(ACT)
    blk_row = lax.broadcasted_iota(jnp.int32, (n_blk, blk), 0)

    ranked = [(qi, sh) for qi in range(MOBA_TOPK + 1, n_blk) for sh in range(2)]
    gates = [_dot(km0, qts_ref[qi, sh]) + _dot(km1, qts_ref[qi, sh]) + _dot(km2, qts_ref[qi, sh])
             for qi, sh in ranked]
    for (qi, sh), g in zip(ranked, gates):
        g = jnp.where(blk_row < qi, g, NEG)
        sel_rows = []
        for jb in range(qi):
            gj = g[jb:jb + 1, :]
            beats = (g > gj) | ((g == gj) & (blk_row < jb))
            cnt = jnp.sum(beats.astype(F32), axis=0, keepdims=True)
            sel_rows.append(jnp.where(cnt < MOBA_TOPK, 1.0, 0.0))
        sel_rows.append(jnp.zeros((n_blk - qi, blk), F32))
        sel_ref[qi, sh] = jnp.concatenate(sel_rows, axis=0)

    def tiles(off, qis):
        k_blks = [k_ref[0, rows_of(qi - off), :] for qi in qis]
        scores = [_dot(k_blk, qts_ref[qi, sh]) for qi, k_blk in zip(qis, k_blks) for sh in range(2)]
        work = [(qi, sh) for qi in qis for sh in range(2)]
        stats = []
        half = blk // 2
        for (qi, sh), st in zip(work, scores):
            if off == 0:
                bias = bt_ref[sh, 0]
                parts = [(st[:half, :half] + bias[:half, :half], slice(0, half)),
                         (st[:, half:] + bias[:, half:], slice(0, blk))]
                maxes = [jnp.max(s_part, axis=0, keepdims=True) for s_part, _ in parts]
                m_ref[qi, sh] = jnp.concatenate(maxes, axis=1)
                stats.append((None, [(keys, jnp.exp2(s_part - m_part).astype(ACT))
                                     for (s_part, keys), m_part in zip(parts, maxes)]))
                continue
            st = st + bt_ref[sh, off]
            tile_max = jnp.max(st, axis=0, keepdims=True)
            if qi <= MOBA_TOPK:
                m = m_ref[qi, sh]
                m_new = jnp.maximum(m, tile_max)
                p = jnp.exp2(st - m_new)
                alpha = jnp.exp2(m - m_new)
            else:
                sel = sel_ref[qi, sh, pl.ds(qi - off, 1), :] > 0.5
                m = m_ref[qi, sh]
                m_new = jnp.where(sel, jnp.maximum(m, tile_max), m)
                p = jnp.exp2(st - jnp.where(sel, m_new, -NEG))
                alpha = jnp.exp2(m - m_new)
            m_ref[qi, sh] = m_new
            stats.append((alpha, [(slice(0, blk), p.astype(ACT))]))
        pvs = [jnp.concatenate([_dot(vt_ref[qi - off, sh, :, keys], p) for keys, p in parts], axis=1)
               for (qi, sh), (_, parts) in zip(work, stats)]
        for (qi, sh), (alpha, _), pv in zip(work, stats, pvs):
            acc_ref[qi, sh] = pv if off == 0 else alpha * acc_ref[qi, sh] + pv

    def finish(qi):
        ot = jnp.concatenate([acc_ref[qi, sh, :HEAD_DIM, :] * (1.0 / acc_ref[qi, sh, HEAD_DIM:HEAD_DIM + 1, :])
                              for sh in range(2)], axis=0)
        o_ref[0, rows_of(qi), :] = (ot.T * sg_ref[0, rows_of(qi), :].astype(F32)).astype(o_ref.dtype)

    for off in range(n_blk):
        qis = list(range(off, n_blk))
        for lo in range(0, len(qis), MOBA_GROUP):
            tiles(off, qis[lo:lo + MOBA_GROUP])
        finish(off)


def _moba(p, tv_a):
    b, s, _ = p.shape
    blk = MOBA_BLOCK
    n_blk = s // blk
    col = lambda c: pl.BlockSpec((1, s, LANE), lambda hp, bi: (bi, 0, COL_A + c * N_PAIRS + hp))
    return pl.pallas_call(
        _moba_kernel,
        grid=(N_PAIRS, b),
        in_specs=[col(0), col(1), col(2), col(3), pl.BlockSpec(tv_a.shape, lambda hp, bi: (0, 0, 0, 0))],
        out_specs=pl.BlockSpec((1, s, LANE), lambda hp, bi: (bi, 0, hp)),
        out_shape=jax.ShapeDtypeStruct((b, s, W_BR), ACT),
        scratch_shapes=[
            pltpu.VMEM((2, n_blk, blk, blk), F32),
            pltpu.VMEM((n_blk, 2, LANE, blk), ACT),
            pltpu.VMEM((n_blk, 2, HEAD_DIM + ONES_ROWS, blk), ACT),
            pltpu.VMEM((n_blk, LANE), F32),
            pltpu.VMEM((n_blk, 2, n_blk, blk), F32),
            pltpu.VMEM((n_blk, 2, 1, blk), F32),
            pltpu.VMEM((n_blk, 2, HEAD_DIM + ONES_ROWS, blk), F32),
        ],
        compiler_params=_cparams(2),
        name="moba",
    )(p, p, p, p, tv_a)


def _band_attend(items):
    def fuse(blocks):
        if len(blocks) > 1 and all(valid is None for _, _, _, valid in blocks):
            ks, vts, biases, _ = zip(*blocks)
            return [(jnp.concatenate(ks, axis=0), jnp.concatenate(vts, axis=1), jnp.concatenate(biases, axis=0), None)]
        return blocks

    items = [(qs, fuse(blocks), sink) for qs, blocks, sink in items]
    scores = [[_dot(k, qs) for k, _, _, _ in blocks] for qs, blocks, _ in items]
    probs = []
    for (_, blocks, sink), sc in zip(items, scores):
        ss = [s + bias if valid is None else jnp.where(valid, s + bias, NEG)
              for s, (_, _, bias, valid) in zip(sc, blocks)]
        m = functools.reduce(jnp.maximum, [jnp.max(s, axis=0, keepdims=True) for s in ss])
        if sink is not None:
            m = jnp.maximum(m, sink)
        probs.append((m, [jnp.exp2(s - m).astype(ACT) for s in ss]))
    outs = []
    for (_, blocks, sink), (m, ps) in zip(items, probs):
        n_v = blocks[0][1].shape[0]
        acc = functools.reduce(operator.add, [
            _dot(jnp.concatenate([vt, jnp.ones((ONES_ROWS, vt.shape[1]), ACT)], axis=0), p)
            for (_, vt, _, _), p in zip(blocks, ps)])
        l = acc[n_v:n_v + 1, :]
        if sink is not None:
            l = l + jnp.exp2(sink - m)
        outs.append((acc[:n_v, :] * (1.0 / l), m + jnp.log2(l)))
    return outs


def _band_bias(tv_ref, head, prev_includes_window_edge):
    kr = lax.broadcasted_iota(jnp.int32, (BAND, BAND), 0)
    qc = lax.broadcasted_iota(jnp.int32, (BAND, BAND), 1)
    own, prev = [_toeplitz(tv_ref[head, which], tv_ref[head, which + 1], BAND) for which in range(2)]
    prev_mask = (qc <= kr) if prev_includes_window_edge else (qc < kr)
    return jnp.where(qc >= kr, own, NEG), jnp.where(prev_mask, prev, NEG)


def _dil_kernel(q0_ref, k0_ref, v0_ref, q1_ref, k1_ref, v1_ref, q2_ref, k2_ref, v2_ref, sg_ref, tv_ref,
                o_ref, og_ref, lg_ref, bt_ref):
    hp = pl.program_id(0)
    b = pl.program_id(1)
    s = q0_ref.shape[1]
    n_tiles = s // BAND

    @pl.when(b == 0)
    def _():
        for g in range(N_DIL):
            for sh in range(2):
                own, prev = _band_bias(tv_ref, g * N_HEADS + 2 * hp + sh, True)
                bt_ref[g, sh, 0] = own
                bt_ref[g, sh, 1] = prev

    groups = ((q0_ref, k0_ref, v0_ref), (q1_ref, k1_ref, v1_ref), (q2_ref, k2_ref, v2_ref))

    def mix(rows, o0, l0):
        l1, l2 = lg_ref[0, rows, :], lg_ref[1, rows, :]
        m = jnp.maximum(l0, jnp.maximum(l1, l2))
        e0, e1, e2 = jnp.exp2(l0 - m), jnp.exp2(l1 - m), jnp.exp2(l2 - m)
        ob = (e0 * o0 + e1 * og_ref[0, rows, :] + e2 * og_ref[1, rows, :]) / (e0 + e1 + e2)
        o_ref[0, rows, :] = (ob * sg_ref[0, rows, :].astype(F32)).astype(o_ref.dtype)

    for g in (*range(1, N_DIL), 0):
        q_ref, k_ref, v_ref = groups[g]
        d = DIL_PAIRS[g][1]
        assert (d == 1) == (g == 0)
        per_phase = n_tiles // d
        has_prev = per_phase > 1
        assert per_phase % TILE_GROUP == 0 or TILE_GROUP % per_phase == 0

        def body(i, c, g=g, d=d, per_phase=per_phase, has_prev=has_prev, q_ref=q_ref, k_ref=k_ref, v_ref=v_ref):
            items, dsts = [], []
            k_own = vt_own = None
            for n in range(TILE_GROUP):
                u = i * TILE_GROUP + n
                rows = pl.ds(pl.multiple_of(u * BAND, BAND), BAND)
                t = lax.rem(u, per_phase)
                r = lax.div(u, per_phase)
                qt = _unit_rms_heads(q_ref[0, rows, :].T)
                k_prev, vt_prev = k_own, vt_own
                k_own = k_ref[0, rows, :]
                vt_own = v_ref[0, rows, :].T
                if has_prev:
                    phase_start_known = TILE_GROUP % per_phase == 0
                    prev_exists = (t > 0) if (n == 0 and not phase_start_known) else None
                    if phase_start_known and n % per_phase == 0:
                        k_prev = None
                    elif n == 0:
                        prow = pl.ds(pl.multiple_of(jnp.maximum(u - 1, 0) * BAND, BAND), BAND)
                        k_prev = k_ref[0, prow, :]
                        vt_prev = v_ref[0, prow, :].T
                for sh in range(2):
                    hrows = slice(HEAD_DIM * sh, HEAD_DIM * (sh + 1))
                    blocks = [(k_own, vt_own[hrows, :], bt_ref[g, sh, 0], None)]
                    if has_prev and k_prev is not None:
                        blocks.append((k_prev, vt_prev[hrows, :], bt_ref[g, sh, 1], prev_exists))
                    items.append((_head_rows(qt, HEAD_DIM * sh), blocks, None))
                dsts.append(rows if d == 1 else pl.ds(t * (BAND * d) + r, BAND, stride=d))
            res = _band_attend(items)
            for n, dst in enumerate(dsts):
                (o0, lse0), (o1, lse1) = res[2 * n], res[2 * n + 1]
                out = jnp.concatenate([o0, o1], axis=0).T
                lse = jnp.concatenate(
                    [jnp.broadcast_to(lse0, (HEAD_DIM, BAND)), jnp.broadcast_to(lse1, (HEAD_DIM, BAND))], axis=0).T
                if g == 0:
                    mix(dst, out, lse)
                else:
                    og_ref.at[g - 1][dst, :] = out
                    lg_ref.at[g - 1][dst, :] = lse
            return c

        lax.fori_loop(0, n_tiles // TILE_GROUP, body, 0)


def _dilated(p, tv_b):
    b, s, _ = p.shape
    col = lambda c: pl.BlockSpec((1, s, LANE), lambda hp, bi: (bi, 0, c + hp))
    qkv = [col(COL_B + 3 * N_PAIRS * g + N_PAIRS * i) for g in range(N_DIL) for i in range(3)]
    return pl.pallas_call(
        _dil_kernel,
        grid=(N_PAIRS, b),
        in_specs=qkv + [col(COL_SGB), pl.BlockSpec(tv_b.shape, lambda hp, bi: (0, 0, 0, 0))],
        out_specs=pl.BlockSpec((1, s, LANE), lambda hp, bi: (bi, 0, hp)),
        out_shape=jax.ShapeDtypeStruct((b, s, W_BR), ACT),
        scratch_shapes=[
            pltpu.VMEM((N_DIL - 1, s, LANE), F32),
            pltpu.VMEM((N_DIL - 1, s, LANE), F32),
            pltpu.VMEM((N_DIL, 2, 2, BAND, BAND), F32),
        ],
        compiler_params=_cparams(2),
        name="dilated",
    )(*([p] * 10), tv_b)


def _swa_kernel(sink_ref, q_ref, sg_ref, k_ref, v_ref, tv_ref, o_ref, bt_ref, vts_ref):
    hp = pl.program_id(0)
    b = pl.program_id(1)
    n_tiles = q_ref.shape[1] // BAND
    pairs_per_kv = N_PAIRS // KV_C

    @pl.when(b == 0)
    def _():
        for sh in range(2):
            own, prev = _band_bias(tv_ref, 2 * hp + sh, False)
            bt_ref[sh, 0] = own
            bt_ref[sh, 1] = prev

    kv_lo = HEAD_DIM * (hp // pairs_per_kv)
    row = lax.broadcasted_iota(jnp.int32, (LANE, BAND), 0)
    kv_rows = (row >= kv_lo) & (row < kv_lo + HEAD_DIM)

    def kv_vt(rows, slot):
        vts_ref[slot] = v_ref[0, rows, :].T
        return vts_ref[slot, pl.ds(pl.multiple_of(kv_lo, HEAD_DIM), HEAD_DIM), :]

    def body(i, c):
        items, dsts = [], []
        zero = jnp.zeros((LANE, BAND), ACT)
        k_own = vt_own = None
        for n in range(TILE_GROUP):
            t = i * TILE_GROUP + n
            rows = pl.ds(pl.multiple_of(t * BAND, BAND), BAND)
            qt = _unit_rms_heads(q_ref[0, rows, :].T)
            k_prev, vt_prev = k_own, vt_own
            k_own = k_ref[0, rows, :]
            vt_own = kv_vt(rows, n)
            if n == 0:
                prow = pl.ds(pl.multiple_of(jnp.maximum(t - 1, 0) * BAND, BAND), BAND)
                k_prev = k_ref[0, prow, :]
                vt_prev = kv_vt(prow, TILE_GROUP)
            prev_exists = (t > 0) if n == 0 else None
            for sh in range(2):
                qh = qt[HEAD_DIM * sh:HEAD_DIM * (sh + 1), :]
                qs = jnp.where(kv_rows, jnp.concatenate([qh, qh], axis=0), zero)
                items.append((qs, [(k_own, vt_own, bt_ref[sh, 0], None), (k_prev, vt_prev, bt_ref[sh, 1], prev_exists)],
                              sink_ref[2 * hp + sh]))
            dsts.append(rows)
        res = _band_attend(items)
        for n, rows in enumerate(dsts):
            ot = jnp.concatenate([o for o, _ in res[2 * n:2 * n + 2]], axis=0).T
            o_ref[0, rows, :] = (ot * sg_ref[0, rows, :].astype(F32)).astype(o_ref.dtype)
        return c

    lax.fori_loop(0, n_tiles // TILE_GROUP, body, 0)


def _swa(p, tv_c, sinks):
    b, s, _ = p.shape
    grid_spec = pltpu.PrefetchScalarGridSpec(
        num_scalar_prefetch=1,
        grid=(N_PAIRS, b),
        in_specs=[
            pl.BlockSpec((1, s, LANE), lambda hp, bi, sk: (bi, 0, COL_C + hp)),
            pl.BlockSpec((1, s, LANE), lambda hp, bi, sk: (bi, 0, COL_C + N_PAIRS + hp)),
            pl.BlockSpec((1, s, LANE), lambda hp, bi, sk: (bi, 0, COL_C + 2 * N_PAIRS)),
            pl.BlockSpec((1, s, LANE), lambda hp, bi, sk: (bi, 0, COL_C + 2 * N_PAIRS + 1)),
            pl.BlockSpec(tv_c.shape, lambda hp, bi, sk: (0, 0, 0, 0)),
        ],
        out_specs=pl.BlockSpec((1, s, LANE), lambda hp, bi, sk: (bi, 0, hp)),
        scratch_shapes=[pltpu.VMEM((2, 2, BAND, BAND), F32),
                        pltpu.VMEM((TILE_GROUP + 1, LANE, BAND), ACT)],
    )
    return pl.pallas_call(
        _swa_kernel,
        grid_spec=grid_spec,
        out_shape=jax.ShapeDtypeStruct((b, s, W_BR), ACT),
        compiler_params=_cparams(2),
        name="swa",
    )(sinks, p, p, p, p, tv_c)


def _merge_kernel(bra_ref, brb_ref, brc_ref, g_ref, wb_ref, wo_ref, x_ref, out_ref):
    merged = None
    for i, br_ref in enumerate((bra_ref, brb_ref, brc_ref)):
        term = g_ref[:, i * D_MODEL:(i + 1) * D_MODEL].astype(F32) * _dot(br_ref[...], wb_ref[i])
        merged = term if merged is None else merged + term
    out_ref[...] = x_ref[...] + _dot(merged.astype(ACT), wo_ref[...])


def _merge(bra, brb, brc, p2, w_br, w_out, x2, layer, ts=512):
    n = x2.shape[0]
    row = lambda w: pl.BlockSpec((ts, w), lambda i: (i, 0))
    return pl.pallas_call(
        _merge_kernel,
        grid=(n // ts,),
        in_specs=[row(W_BR), row(W_BR), row(W_BR), row(GATE_W),
                  pl.BlockSpec((None,) + w_br.shape[1:], lambda i: (layer, 0, 0, 0)),
                  pl.BlockSpec((None,) + w_out.shape[1:], lambda i: (layer, 0, 0)),
                  row(D_MODEL)],
        out_specs=row(D_MODEL),
        out_shape=jax.ShapeDtypeStruct(x2.shape, F32),
        compiler_params=_cparams(1),
        name="merge",
    )(bra, brb, brc, p2, w_br, w_out, x2)


def _rel_bucket(dist):
    dist = jnp.maximum(dist, 0)
    max_exact = N_BUCKETS // 2
    log_ratio = jnp.log(jnp.maximum(dist, 1).astype(F32) / max_exact) / math.log(REL_MAX_DIST / max_exact)
    large = jnp.minimum(max_exact + (log_ratio * (N_BUCKETS - max_exact)).astype(jnp.int32), N_BUCKETS - 1)
    return jnp.where(dist < max_exact, dist, large)


def _bias_by_distance(tab, dist, chunk):
    vals = tab.astype(F32)[:, _rel_bucket(dist)] * LOG2E
    vals = jnp.pad(vals, ((0, 0), (chunk, 0)))
    return vals.reshape(tab.shape[0], -1, 1, chunk)


def _gain_rows(qk_g):
    depth = qk_g.shape[0]
    scale = HEAD_DIM ** -0.5 * LOG2E
    one = jnp.ones((depth, HEAD_DIM), F32)
    tile = lambda v, n: jnp.tile(v, (1, n))
    parts = [tile(one, N_HEADS), tile(qk_g[:, 1] * qk_g[:, 0] * scale, N_HEADS), tile(one, 2 * N_HEADS),
             tile(one, N_DIL * N_HEADS), tile(qk_g[:, 3] * qk_g[:, 2] * scale, N_DIL * N_HEADS),
             tile(one, (N_DIL + 1) * N_HEADS),
             tile(one, N_HEADS), tile(qk_g[:, 5] * qk_g[:, 4] * scale, KV_C), tile(one, KV_C + N_HEADS),
             tile(one, GATE_W // HEAD_DIM)]
    return jnp.concatenate(parts, axis=1)


def kernel(x, ln_g, w_in, qk_g, sinks, w_branch, w_out, rel_bias):
    b, s, d = x.shape
    depth = ln_g.shape[0]
    assert d == D_MODEL and s % DIL_PAIRS[-1][0] == 0 and w_in.shape[-1] == C_IN

    tv_a = _bias_by_distance(rel_bias[:N_HEADS], jnp.arange(s), MOBA_BLOCK)
    tv_b = jnp.concatenate(
        [_bias_by_distance(rel_bias[OFF_B + g * N_HEADS:OFF_B + (g + 1) * N_HEADS], dd * jnp.arange(2 * BAND), BAND)
         for g, (_, dd) in enumerate(DIL_PAIRS)], axis=0)
    tv_c = _bias_by_distance(rel_bias[OFF_C:OFF_C + N_HEADS], jnp.arange(2 * BAND), BAND)
    seg = jnp.arange(TN) // HEAD_DIM
    e_mat = (seg[:, None] == seg[None, :]).astype(ACT)
    src = jnp.asarray([c for c, _ in _LAYOUT], jnp.int32)
    fn = jnp.asarray([f for _, f in _LAYOUT[::PAIR]], jnp.int32)
    gains = _gain_rows(qk_g)

    w_in_b = w_in.astype(ACT)
    w_br_b = w_branch.astype(ACT)
    w_out_b = w_out.astype(ACT)

    for l in range(depth):
        p = _project(x, ln_g[:, None, :], w_in_b, gains[l][None, :], e_mat, src, fn, l)
        bra = _moba(p, tv_a)
        brb = _dilated(p, tv_b)
        brc = _swa(p, tv_c, sinks[l] * LOG2E)
        x2 = _merge(bra.reshape(b * s, W_BR), brb.reshape(b * s, W_BR), brc.reshape(b * s, W_BR),
                    p.reshape(b * s, P_W), w_br_b, w_out_b, x.reshape(b * s, d), l)
        x = x2.reshape(b, s, d)
    return x
```

```python
import functools
import math
import operator

import jax
import jax.numpy as jnp
from jax import lax
from jax.experimental import pallas as pl
from jax.experimental.pallas import tpu as pltpu

D_MODEL = 1024
HEAD_DIM = 64
W_BR = D_MODEL // 2
N_HEADS = W_BR // HEAD_DIM
N_PAIRS = N_HEADS // 2
MOBA_BLOCK = 256
MOBA_TOPK = 3
DIL_PAIRS = ((128, 1), (512, 4), (2048, 16))
N_DIL = len(DIL_PAIRS)
KV_C = 2
SWA_WINDOW = 128
N_BUCKETS = 32
REL_MAX_DIST = 2048
OFF_B = N_HEADS
OFF_C = N_HEADS + N_DIL * N_HEADS
EPS = 1e-6
NEG = -1e30
LOG2E = math.log2(math.e)

LANE = 128
BAND = SWA_WINDOW
assert all(w == BAND * d for w, d in DIL_PAIRS)
TN = 256
ROWS = 256
TILE_GROUP = 16
MOBA_GROUP = 8
SINGLE_OP_STRIDE = 4
ONES_ROWS = 16
VMEM_LIMIT = 56 * 1024 * 1024

ACT = jnp.bfloat16
F32 = jnp.float32

PLAIN, NORM, SILU, SIGMOID, KNORM_VPLAIN, NORM_D4, PLAIN_D4, NORM_D16, PLAIN_D16 = range(9)
_B_GROUP = lambda g, norm, plain: [(8 + 2 * g, plain), (9 + 2 * g, plain), (14 + 2 * g, norm), (15 + 2 * g, norm),
                                   (20 + 2 * g, plain), (21 + 2 * g, plain)]
_LAYOUT = ([(j, SIGMOID) for j in range(33, 45)]
           + [(0, PLAIN), (1, PLAIN), (2, NORM), (3, NORM), (4, PLAIN), (5, PLAIN), (6, SILU), (7, SILU)]
           + _B_GROUP(0, NORM, PLAIN) + _B_GROUP(1, NORM_D4, PLAIN_D4) + _B_GROUP(2, NORM_D16, PLAIN_D16)
           + [(26, SILU), (27, SILU)]
           + [(28, PLAIN), (29, PLAIN), (31, SILU), (32, SILU), (30, KNORM_VPLAIN)])
C_IN = len(_LAYOUT) * TN
PAIR = 2
_LAYOUT = _LAYOUT + [(_LAYOUT[-1][0], KNORM_VPLAIN)] * (-len(_LAYOUT) % PAIR)
assert all(_LAYOUT[j][1] == _LAYOUT[j - j % PAIR][1] for j in range(len(_LAYOUT)))
assert [f for _, f in _LAYOUT].count(KNORM_VPLAIN) == PAIR
N_STEPS = len(_LAYOUT) // PAIR
P_W = len(_LAYOUT) * TN
COL_GATE, COL_A, COL_B, COL_SGB, COL_C = 0, 24, 40, 76, 80
GATE_W = 3 * D_MODEL
assert COL_GATE == 0


def _cparams(n_axes):
    return pltpu.CompilerParams(dimension_semantics=("arbitrary",) * n_axes, vmem_limit_bytes=VMEM_LIMIT)


def _dot(a, b):
    return jnp.dot(a, b, preferred_element_type=F32)


def _sigmoid(a):
    return 0.5 * jnp.tanh(0.5 * a) + 0.5


def _proj_kernel(src_ref, fn_ref, x_ref, lng_ref, *refs):
    del src_ref
    w_refs, gain_refs = refs[:PAIR], refs[PAIR:2 * PAIR]
    e_ref, o_ref, xn_ref, acc_ref, acc2_ref = refs[2 * PAIR:]
    j = pl.program_id(1)
    s = x_ref.shape[1]
    n_chunks = s // ROWS
    lanes_per_block = TN // LANE

    @pl.when(j == 0)
    def _():
        def body(i, c):
            rows = pl.ds(pl.multiple_of(i * ROWS, ROWS), ROWS)
            xv = x_ref[0, rows, :]
            ms = jnp.mean(xv * xv, axis=-1, keepdims=True)
            xn_ref[rows, :] = (xv * lax.rsqrt(ms + EPS) * lng_ref[...]).astype(ACT)
            return c
        lax.fori_loop(0, n_chunks, body, 0)

    def head_norm(a, gain):
        ss = _dot((a * a).astype(ACT), e_ref[...])
        return a * lax.rsqrt(ss * (1.0 / HEAD_DIM) + EPS) * gain

    def plain(a, gain):
        return a

    def silu(a, gain):
        return a * _sigmoid(a)

    def sigmoid(a, gain):
        return _sigmoid(a)

    def k_norm_v_plain(a, gain):
        lane = lax.broadcasted_iota(jnp.int32, a.shape, 1)
        return jnp.where(lane < KV_C * HEAD_DIM, head_norm(a, gain), a)

    def phase_rows(src_ref, slab, t, d):
        r, m0 = divmod(t * BAND, s // d)
        return src_ref.at[slab][pl.ds(m0 * d + r, BAND, stride=d), :]

    fn = fn_ref[j]

    def chunk(n):
        h, i = divmod(n, n_chunks)
        return _dot(xn_ref[i * ROWS:(i + 1) * ROWS, :], w_refs[h][...])

    def out_cols(h):
        return slice(h * TN, (h + 1) * TN)

    def direct(code, f, live=PAIR):
        @pl.when(fn == code)
        def _():
            pending = None
            for n in range(live * n_chunks + 1):
                a = chunk(n) if n < live * n_chunks else None
                if pending is not None:
                    h, i = divmod(n - 1, n_chunks)
                    o_ref[0, i * ROWS:(i + 1) * ROWS, out_cols(h)] = f(pending, gain_refs[h][...]).astype(o_ref.dtype)
                pending = a
            if live < PAIR:
                o_ref[0, :, live * TN:] = jnp.zeros((s, (PAIR - live) * TN), o_ref.dtype)

    def phase_major(code, f, d):
        @pl.when(fn == code)
        def _():
            for n in range(PAIR * n_chunks):
                h, i = divmod(n, n_chunks)
                a = chunk(n)
                for c in range(lanes_per_block):
                    acc_ref[h * lanes_per_block + c, i * ROWS:(i + 1) * ROWS, :] = a[:, c * LANE:(c + 1) * LANE]
            src_ref, stride = acc_ref, d
            if d > SINGLE_OP_STRIDE:
                assert d == SINGLE_OP_STRIDE * SINGLE_OP_STRIDE
                for slab in range(PAIR * lanes_per_block):
                    for t in range(s // BAND):
                        acc2_ref[slab, t * BAND:(t + 1) * BAND, :] = phase_rows(acc_ref, slab, t, SINGLE_OP_STRIDE)
                src_ref, stride = acc2_ref, SINGLE_OP_STRIDE
            for h in range(PAIR):
                for t in range(s // BAND):
                    a = jnp.concatenate([phase_rows(src_ref, h * lanes_per_block + c, t, stride)
                                         for c in range(lanes_per_block)], axis=1)
                    o_ref[0, t * BAND:(t + 1) * BAND, out_cols(h)] = f(a, gain_refs[h][...]).astype(o_ref.dtype)

    direct(PLAIN, plain)
    direct(NORM, head_norm)
    direct(SILU, silu)
    direct(SIGMOID, sigmoid)
    direct(KNORM_VPLAIN, k_norm_v_plain, live=1)
    phase_major(NORM_D4, head_norm, DIL_PAIRS[1][1])
    phase_major(PLAIN_D4, plain, DIL_PAIRS[1][1])
    phase_major(NORM_D16, head_norm, DIL_PAIRS[2][1])
    phase_major(PLAIN_D16, plain, DIL_PAIRS[2][1])


def _w_index(bi, j, src_ref, fn_ref, *, layer, slot):
    return layer, 0, src_ref[PAIR * j + slot]


def _gain_index(bi, j, src_ref, fn_ref, *, slot):
    return 0, src_ref[PAIR * j + slot]


def _project(x, ln_g, w_in, gain_row, e_mat, src, fn, layer):
    b, s, d = x.shape
    n_in = PAIR
    grid_spec = pltpu.PrefetchScalarGridSpec(
        num_scalar_prefetch=2,
        grid=(b, N_STEPS),
        in_specs=[
            pl.BlockSpec((1, s, d), lambda bi, j, src_ref, fn_ref: (bi, 0, 0)),
            pl.BlockSpec((None, 1, d), lambda bi, j, src_ref, fn_ref: (layer, 0, 0)),
        ] + [pl.BlockSpec((None, d, TN), functools.partial(_w_index, layer=layer, slot=n)) for n in range(n_in)]
          + [pl.BlockSpec((1, TN), functools.partial(_gain_index, slot=n)) for n in range(n_in)]
          + [pl.BlockSpec((TN, TN), lambda bi, j, src_ref, fn_ref: (0, 0))],
        out_specs=pl.BlockSpec((1, s, n_in * TN), lambda bi, j, src_ref, fn_ref: (bi, 0, j)),
        scratch_shapes=[pltpu.VMEM((s, d), ACT), pltpu.VMEM((PAIR * TN // LANE, s, LANE), F32),
                        pltpu.VMEM((PAIR * TN // LANE, s, LANE), F32)],
    )
    return pl.pallas_call(
        _proj_kernel,
        grid_spec=grid_spec,
        out_shape=jax.ShapeDtypeStruct((b, s, P_W), ACT),
        compiler_params=_cparams(2),
        name="proj",
    )(src, fn, x, ln_g, *([w_in] * n_in), *([gain_row] * n_in), e_mat)


def _toeplitz(lo, hi, n):
    x = jnp.concatenate([jnp.broadcast_to(lo, (n, n)), jnp.broadcast_to(hi, (n, n))], axis=1)
    return pltpu.roll(x, 0, 1, stride=1, stride_axis=0)[:, n:]


def _unit_rms_heads(qt):
    halves = []
    for sh in range(2):
        x = qt[HEAD_DIM * sh:HEAD_DIM * (sh + 1), :].astype(F32)
        halves.append(x * lax.rsqrt(jnp.mean(x * x, axis=0, keepdims=True) + EPS))
    return jnp.concatenate(halves, axis=0).astype(qt.dtype)


def _head_rows(qt, lo):
    row = lax.broadcasted_iota(jnp.int32, qt.shape, 0)
    return jnp.where((row >= lo) & (row < lo + HEAD_DIM), qt, jnp.zeros_like(qt))


def _moba_kernel(q_ref, k_ref, v_ref, sg_ref, tv_ref, o_ref,
                 bt_ref, qts_ref, vt_ref, km_ref, sel_ref, m_ref, acc_ref):
    hp = pl.program_id(0)
    b = pl.program_id(1)
    blk = MOBA_BLOCK
    n_blk = k_ref.shape[1] // blk

    def rows_of(i):
        return pl.ds(i * blk if isinstance(i, int) else pl.multiple_of(i * blk, blk), blk)

    @pl.when(b == 0)
    def _():
        kr = lax.broadcasted_iota(jnp.int32, (blk, blk), 0)
        qc = lax.broadcasted_iota(jnp.int32, (blk, blk), 1)
        for sh in range(2):
            def body(o, c, sh=sh):
                t = _toeplitz(tv_ref[2 * hp + sh, o], tv_ref[2 * hp + sh, o + 1], blk)
                bt_ref[sh, o] = jnp.where((qc >= kr) | (o > 0), t, NEG)
                return c
            lax.fori_loop(0, n_blk, body, 0)

    for i in range(n_blk):
        vt = v_ref[0, rows_of(i), :].T
        for sh in range(2):
            vt_ref[i, sh] = jnp.concatenate(
                [vt[HEAD_DIM * sh:HEAD_DIM * (sh + 1), :], jnp.ones((ONES_ROWS, blk), ACT)], axis=0)
        km_ref[i:i + 1, :] = jnp.mean(k_ref[0, rows_of(i), :].astype(F32), axis=0, keepdims=True)
        qt = _unit_rms_heads(q_ref[0, rows_of(i), :].T)
        for sh in range(2):
            qts_ref[i, sh] = _head_rows(qt, HEAD_DIM * sh)

    km = km_ref[...]
    km0 = km.astype(ACT)
    r1 = km - km0.astype(F32)
    km1 = r1.astype(ACT)
    km2 = (r1 - km1.astype(F32)).astype(ACT)
    blk_row = lax.broadcasted_iota(jnp.int32, (n_blk, blk), 0)

    ranked = [(qi, sh) for qi in range(MOBA_TOPK + 1, n_blk) for sh in range(2)]
    gates = [_dot(km0, qts_ref[qi, sh]) + _dot(km1, qts_ref[qi, sh]) + _dot(km2, qts_ref[qi, sh])
             for qi, sh in ranked]
    for (qi, sh), g in zip(ranked, gates):
        g = jnp.where(blk_row < qi, g, NEG)
        sel_rows = []
        for jb in range(qi):
            gj = g[jb:jb + 1, :]
            beats = (g > gj) | ((g == gj) & (blk_row < jb))
            cnt = jnp.sum(beats.astype(F32), axis=0, keepdims=True)
            sel_rows.append(jnp.where(cnt < MOBA_TOPK, 1.0, 0.0))
        sel_rows.append(jnp.zeros((n_blk - qi, blk), F32))
        sel_ref[qi, sh] = jnp.concatenate(sel_rows, axis=0)

    def tiles(off, qis):
        k_blks = [k_ref[0, rows_of(qi - off), :] for qi in qis]
        scores = [_dot(k_blk, qts_ref[qi, sh]) for qi, k_blk in zip(qis, k_blks) for sh in range(2)]
        work = [(qi, sh) for qi in qis for sh in range(2)]
        stats = []
        half = blk // 2
        for (qi, sh), st in zip(work, scores):
            if off == 0:
                bias = bt_ref[sh, 0]
                parts = [(st[:half, :half] + bias[:half, :half], slice(0, half)),
                         (st[:, half:] + bias[:, half:], slice(0, blk))]
                maxes = [jnp.max(s_part, axis=0, keepdims=True) for s_part, _ in parts]
                m_ref[qi, sh] = jnp.concatenate(maxes, axis=1)
                stats.append((None, [(keys, jnp.exp2(s_part - m_part).astype(ACT))
                                     for (s_part, keys), m_part in zip(parts, maxes)]))
                continue
            st = st + bt_ref[sh, off]
            tile_max = jnp.max(st, axis=0, keepdims=True)
            if qi <= MOBA_TOPK:
                m = m_ref[qi, sh]
                m_new = jnp.maximum(m, tile_max)
                p = jnp.exp2(st - m_new)
                alpha = jnp.exp2(m - m_new)
            else:
                sel = sel_ref[qi, sh, pl.ds(qi - off, 1), :] > 0.5
                m = m_ref[qi, sh]
                m_new = jnp.where(sel, jnp.maximum(m, tile_max), m)
                p = jnp.exp2(st - jnp.where(sel, m_new, -NEG))
                alpha = jnp.exp2(m - m_new)
            m_ref[qi, sh] = m_new
            stats.append((alpha, [(slice(0, blk), p.astype(ACT))]))
        pvs = [jnp.concatenate([_dot(vt_ref[qi - off, sh, :, keys], p) for keys, p in parts], axis=1)
               for (qi, sh), (_, parts) in zip(work, stats)]
        for (qi, sh), (alpha, _), pv in zip(work, stats, pvs):
            acc_ref[qi, sh] = pv if off == 0 else alpha * acc_ref[qi, sh] + pv

    def finish(qi):
        ot = jnp.concatenate([acc_ref[qi, sh, :HEAD_DIM, :] * (1.0 / acc_ref[qi, sh, HEAD_DIM:HEAD_DIM + 1, :])
                              for sh in range(2)], axis=0)
        o_ref[0, rows_of(qi), :] = (ot.T * sg_ref[0, rows_of(qi), :].astype(F32)).astype(o_ref.dtype)

    for off in range(n_blk):
        qis = list(range(off, n_blk))
        for lo in range(0, len(qis), MOBA_GROUP):
            tiles(off, qis[lo:lo + MOBA_GROUP])
        finish(off)


def _moba(p, tv_a):
    b, s, _ = p.shape
    blk = MOBA_BLOCK
    n_blk = s // blk
    col = lambda c: pl.BlockSpec((1, s, LANE), lambda hp, bi: (bi, 0, COL_A + c * N_PAIRS + hp))
    return pl.pallas_call(
        _moba_kernel,
        grid=(N_PAIRS, b),
        in_specs=[col(0), col(1), col(2), col(3), pl.BlockSpec(tv_a.shape, lambda hp, bi: (0, 0, 0, 0))],
        out_specs=pl.BlockSpec((1, s, LANE), lambda hp, bi: (bi, 0, hp)),
        out_shape=jax.ShapeDtypeStruct((b, s, W_BR), ACT),
        scratch_shapes=[
            pltpu.VMEM((2, n_blk, blk, blk), F32),
            pltpu.VMEM((n_blk, 2, LANE, blk), ACT),
            pltpu.VMEM((n_blk, 2, HEAD_DIM + ONES_ROWS, blk), ACT),
            pltpu.VMEM((n_blk, LANE), F32),
            pltpu.VMEM((n_blk, 2, n_blk, blk), F32),
            pltpu.VMEM((n_blk, 2, 1, blk), F32),
            pltpu.VMEM((n_blk, 2, HEAD_DIM + ONES_ROWS, blk), F32),
        ],
        compiler_params=_cparams(2),
        name="moba",
    )(p, p, p, p, tv_a)


def _band_attend(items):
    def fuse(blocks):
        if len(blocks) > 1 and all(valid is None for _, _, _, valid in blocks):
            ks, vts, biases, _ = zip(*blocks)
            return [(jnp.concatenate(ks, axis=0), jnp.concatenate(vts, axis=1), jnp.concatenate(biases, axis=0), None)]
        return blocks

    items = [(qs, fuse(blocks), sink) for qs, blocks, sink in items]
    scores = [[_dot(k, qs) for k, _, _, _ in blocks] for qs, blocks, _ in items]
    probs = []
    for (_, blocks, sink), sc in zip(items, scores):
        ss = [s + bias if valid is None else jnp.where(valid, s + bias, NEG)
              for s, (_, _, bias, valid) in zip(sc, blocks)]
        m = functools.reduce(jnp.maximum, [jnp.max(s, axis=0, keepdims=True) for s in ss])
        if sink is not None:
            m = jnp.maximum(m, sink)
        probs.append((m, [jnp.exp2(s - m).astype(ACT) for s in ss]))
    outs = []
    for (_, blocks, sink), (m, ps) in zip(items, probs):
        n_v = blocks[0][1].shape[0]
        acc = functools.reduce(operator.add, [
            _dot(jnp.concatenate([vt, jnp.ones((ONES_ROWS, vt.shape[1]), ACT)], axis=0), p)
            for (_, vt, _, _), p in zip(blocks, ps)])
        l = acc[n_v:n_v + 1, :]
        if sink is not None:
            l = l + jnp.exp2(sink - m)
        outs.append((acc[:n_v, :] * (1.0 / l), m + jnp.log2(l)))
    return outs


def _band_bias(tv_ref, head, prev_includes_window_edge):
    kr = lax.broadcasted_iota(jnp.int32, (BAND, BAND), 0)
    qc = lax.broadcasted_iota(jnp.int32, (BAND, BAND), 1)
    own, prev = [_toeplitz(tv_ref[head, which], tv_ref[head, which + 1], BAND) for which in range(2)]
    prev_mask = (qc <= kr) if prev_includes_window_edge else (qc < kr)
    return jnp.where(qc >= kr, own, NEG), jnp.where(prev_mask, prev, NEG)


def _dil_kernel(q0_ref, k0_ref, v0_ref, q1_ref, k1_ref, v1_ref, q2_ref, k2_ref, v2_ref, sg_ref, tv_ref,
                o_ref, og_ref, lg_ref, bt_ref):
    hp = pl.program_id(0)
    b = pl.program_id(1)
    s = q0_ref.shape[1]
    n_tiles = s // BAND

    @pl.when(b == 0)
    def _():
        for g in range(N_DIL):
            for sh in range(2):
                own, prev = _band_bias(tv_ref, g * N_HEADS + 2 * hp + sh, True)
                bt_ref[g, sh, 0] = own
                bt_ref[g, sh, 1] = prev

    groups = ((q0_ref, k0_ref, v0_ref), (q1_ref, k1_ref, v1_ref), (q2_ref, k2_ref, v2_ref))

    def mix(rows, o0, l0):
        l1, l2 = lg_ref[0, rows, :], lg_ref[1, rows, :]
        m = jnp.maximum(l0, jnp.maximum(l1, l2))
        e0, e1, e2 = jnp.exp2(l0 - m), jnp.exp2(l1 - m), jnp.exp2(l2 - m)
        ob = (e0 * o0 + e1 * og_ref[0, rows, :] + e2 * og_ref[1, rows, :]) / (e0 + e1 + e2)
        o_ref[0, rows, :] = (ob * sg_ref[0, rows, :].astype(F32)).astype(o_ref.dtype)

    for g in (*range(1, N_DIL), 0):
        q_ref, k_ref, v_ref = groups[g]
        d = DIL_PAIRS[g][1]
        assert (d == 1) == (g == 0)
        per_phase = n_tiles // d
        has_prev = per_phase > 1
        assert per_phase % TILE_GROUP == 0 or TILE_GROUP % per_phase == 0

        def body(i, c, g=g, d=d, per_phase=per_phase, has_prev=has_prev, q_ref=q_ref, k_ref=k_ref, v_ref=v_ref):
            items, dsts = [], []
            k_own = vt_own = None
            for n in range(TILE_GROUP):
                u = i * TILE_GROUP + n
                rows = pl.ds(pl.multiple_of(u * BAND, BAND), BAND)
                t = lax.rem(u, per_phase)
                r = lax.div(u, per_phase)
                qt = _unit_rms_heads(q_ref[0, rows, :].T)
                k_prev, vt_prev = k_own, vt_own
                k_own = k_ref[0, rows, :]
                vt_own = v_ref[0, rows, :].T
                if has_prev:
                    phase_start_known = TILE_GROUP % per_phase == 0
                    prev_exists = (t > 0) if (n == 0 and not phase_start_known) else None
                    if phase_start_known and n % per_phase == 0:
                        k_prev = None
                    elif n == 0:
                        prow = pl.ds(pl.multiple_of(jnp.maximum(u - 1, 0) * BAND, BAND), BAND)
                        k_prev = k_ref[0, prow, :]
                        vt_prev = v_ref[0, prow, :].T
                for sh in range(2):
                    hrows = slice(HEAD_DIM * sh, HEAD_DIM * (sh + 1))
                    blocks = [(k_own, vt_own[hrows, :], bt_ref[g, sh, 0], None)]
                    if has_prev and k_prev is not None:
                        blocks.append((k_prev, vt_prev[hrows, :], bt_ref[g, sh, 1], prev_exists))
                    items.append((_head_rows(qt, HEAD_DIM * sh), blocks, None))
                dsts.append(rows if d == 1 else pl.ds(t * (BAND * d) + r, BAND, stride=d))
            res = _band_attend(items)
            for n, dst in enumerate(dsts):
                (o0, lse0), (o1, lse1) = res[2 * n], res[2 * n + 1]
                out = jnp.concatenate([o0, o1], axis=0).T
                lse = jnp.concatenate(
                    [jnp.broadcast_to(lse0, (HEAD_DIM, BAND)), jnp.broadcast_to(lse1, (HEAD_DIM, BAND))], axis=0).T
                if g == 0:
                    mix(dst, out, lse)
                else:
                    og_ref.at[g - 1][dst, :] = out
                    lg_ref.at[g - 1][dst, :] = lse
            return c

        lax.fori_loop(0, n_tiles // TILE_GROUP, body, 0)


def _dilated(p, tv_b):
    b, s, _ = p.shape
    col = lambda c: pl.BlockSpec((1, s, LANE), lambda hp, bi: (bi, 0, c + hp))
    qkv = [col(COL_B + 3 * N_PAIRS * g + N_PAIRS * i) for g in range(N_DIL) for i in range(3)]
    return pl.pallas_call(
        _dil_kernel,
        grid=(N_PAIRS, b),
        in_specs=qkv + [col(COL_SGB), pl.BlockSpec(tv_b.shape, lambda hp, bi: (0, 0, 0, 0))],
        out_specs=pl.BlockSpec((1, s, LANE), lambda hp, bi: (bi, 0, hp)),
        out_shape=jax.ShapeDtypeStruct((b, s, W_BR), ACT),
        scratch_shapes=[
            pltpu.VMEM((N_DIL - 1, s, LANE), F32),
            pltpu.VMEM((N_DIL - 1, s, LANE), F32),
            pltpu.VMEM((N_DIL, 2, 2, BAND, BAND), F32),
        ],
        compiler_params=_cparams(2),
        name="dilated",
    )(*([p] * 10), tv_b)


def _swa_kernel(sink_ref, q_ref, sg_ref, k_ref, v_ref, tv_ref, o_ref, bt_ref, vts_ref):
    hp = pl.program_id(0)
    b = pl.program_id(1)
    n_tiles = q_ref.shape[1] // BAND
    pairs_per_kv = N_PAIRS // KV_C

    @pl.when(b == 0)
    def _():
        for sh in range(2):
            own, prev = _band_bias(tv_ref, 2 * hp + sh, False)
            bt_ref[sh, 0] = own
            bt_ref[sh, 1] = prev

    kv_lo = HEAD_DIM * (hp // pairs_per_kv)
    row = lax.broadcasted_iota(jnp.int32, (LANE, BAND), 0)
    kv_rows = (row >= kv_lo) & (row < kv_lo + HEAD_DIM)

    def kv_vt(rows, slot):
        vts_ref[slot] = v_ref[0, rows, :].T
        return vts_ref[slot, pl.ds(pl.multiple_of(kv_lo, HEAD_DIM), HEAD_DIM), :]

    def body(i, c):
        items, dsts = [], []
        zero = jnp.zeros((LANE, BAND), ACT)
        k_own = vt_own = None
        for n in range(TILE_GROUP):
            t = i * TILE_GROUP + n
            rows = pl.ds(pl.multiple_of(t * BAND, BAND), BAND)
            qt = _unit_rms_heads(q_ref[0, rows, :].T)
            k_prev, vt_prev = k_own, vt_own
            k_own = k_ref[0, rows, :]
            vt_own = kv_vt(rows, n)
            if n == 0:
                prow = pl.ds(pl.multiple_of(jnp.maximum(t - 1, 0) * BAND, BAND), BAND)
                k_prev = k_ref[0, prow, :]
                vt_prev = kv_vt(prow, TILE_GROUP)
            prev_exists = (t > 0) if n == 0 else None
            for sh in range(2):
                qh = qt[HEAD_DIM * sh:HEAD_DIM * (sh + 1), :]
                qs = jnp.where(kv_rows, jnp.concatenate([qh, qh], axis=0), zero)
                items.append((qs, [(k_own, vt_own, bt_ref[sh, 0], None), (k_prev, vt_prev, bt_ref[sh, 1], prev_exists)],
                              sink_ref[2 * hp + sh]))
            dsts.append(rows)
        res = _band_attend(items)
        for n, rows in enumerate(dsts):
            ot = jnp.concatenate([o for o, _ in res[2 * n:2 * n + 2]], axis=0).T
            o_ref[0, rows, :] = (ot * sg_ref[0, rows, :].astype(F32)).astype(o_ref.dtype)
        return c

    lax.fori_loop(0, n_tiles // TILE_GROUP, body, 0)


def _swa(p, tv_c, sinks):
    b, s, _ = p.shape
    grid_spec = pltpu.PrefetchScalarGridSpec(
        num_scalar_prefetch=1,
        grid=(N_PAIRS, b),
        in_specs=[
            pl.BlockSpec((1, s, LANE), lambda hp, bi, sk: (bi, 0, COL_C + hp)),
            pl.BlockSpec((1, s, LANE), lambda hp, bi, sk: (bi, 0, COL_C + N_PAIRS + hp)),
            pl.BlockSpec((1, s, LANE), lambda hp, bi, sk: (bi, 0, COL_C + 2 * N_PAIRS)),
            pl.BlockSpec((1, s, LANE), lambda hp, bi, sk: (bi, 0, COL_C + 2 * N_PAIRS + 1)),
            pl.BlockSpec(tv_c.shape, lambda hp, bi, sk: (0, 0, 0, 0)),
        ],
        out_specs=pl.BlockSpec((1, s, LANE), lambda hp, bi, sk: (bi, 0, hp)),
        scratch_shapes=[pltpu.VMEM((2, 2, BAND, BAND), F32),
                        pltpu.VMEM((TILE_GROUP + 1, LANE, BAND), ACT)],
    )
    return pl.pallas_call(
        _swa_kernel,
        grid_spec=grid_spec,
        out_shape=jax.ShapeDtypeStruct((b, s, W_BR), ACT),
        compiler_params=_cparams(2),
        name="swa",
    )(sinks, p, p, p, p, tv_c)


def _merge_kernel(bra_ref, brb_ref, brc_ref, g_ref, wb_ref, wo_ref, x_ref, out_ref):
    merged = None
    for i, br_ref in enumerate((bra_ref, brb_ref, brc_ref)):
        term = g_ref[:, i * D_MODEL:(i + 1) * D_MODEL].astype(F32) * _dot(br_ref[...], wb_ref[i])
        merged = term if merged is None else merged + term
    out_ref[...] = x_ref[...] + _dot(merged.astype(ACT), wo_ref[...])


def _merge(bra, brb, brc, p2, w_br, w_out, x2, layer, ts=512):
    n = x2.shape[0]
    row = lambda w: pl.BlockSpec((ts, w), lambda i: (i, 0))
    return pl.pallas_call(
        _merge_kernel,
        grid=(n // ts,),
        in_specs=[row(W_BR), row(W_BR), row(W_BR), row(GATE_W),
                  pl.BlockSpec((None,) + w_br.shape[1:], lambda i: (layer, 0, 0, 0)),
                  pl.BlockSpec((None,) + w_out.shape[1:], lambda i: (layer, 0, 0)),
                  row(D_MODEL)],
        out_specs=row(D_MODEL),
        out_shape=jax.ShapeDtypeStruct(x2.shape, F32),
        compiler_params=_cparams(1),
        name="merge",
    )(bra, brb, brc, p2, w_br, w_out, x2)


def _rel_bucket(dist):
    dist = jnp.maximum(dist, 0)
    max_exact = N_BUCKETS // 2
    log_ratio = jnp.log(jnp.maximum(dist, 1).astype(F32) / max_exact) / math.log(REL_MAX_DIST / max_exact)
    large = jnp.minimum(max_exact + (log_ratio * (N_BUCKETS - max_exact)).astype(jnp.int32), N_BUCKETS - 1)
    return jnp.where(dist < max_exact, dist, large)


def _bias_by_distance(tab, dist, chunk):
    vals = tab.astype(F32)[:, _rel_bucket(dist)] * LOG2E
    vals = jnp.pad(vals, ((0, 0), (chunk, 0)))
    return vals.reshape(tab.shape[0], -1, 1, chunk)


def _gain_rows(qk_g):
    depth = qk_g.shape[0]
    scale = HEAD_DIM ** -0.5 * LOG2E
    one = jnp.ones((depth, HEAD_DIM), F32)
    tile = lambda v, n: jnp.tile(v, (1, n))
    parts = [tile(one, N_HEADS), tile(qk_g[:, 1] * qk_g[:, 0] * scale, N_HEADS), tile(one, 2 * N_HEADS),
             tile(one, N_DIL * N_HEADS), tile(qk_g[:, 3] * qk_g[:, 2] * scale, N_DIL * N_HEADS),
             tile(one, (N_DIL + 1) * N_HEADS),
             tile(one, N_HEADS), tile(qk_g[:, 5] * qk_g[:, 4] * scale, KV_C), tile(one, KV_C + N_HEADS),
             tile(one, GATE_W // HEAD_DIM)]
    return jnp.concatenate(parts, axis=1)


def kernel(x, ln_g, w_in, qk_g, sinks, w_branch, w_out, rel_bias):
    b, s, d = x.shape
    depth = ln_g.shape[0]
    assert d == D_MODEL and s % DIL_PAIRS[-1][0] == 0 and w_in.shape[-1] == C_IN

    tv_a = _bias_by_distance(rel_bias[:N_HEADS], jnp.arange(s), MOBA_BLOCK)
    tv_b = jnp.concatenate(
        [_bias_by_distance(rel_bias[OFF_B + g * N_HEADS:OFF_B + (g + 1) * N_HEADS], dd * jnp.arange(2 * BAND), BAND)
         for g, (_, dd) in enumerate(DIL_PAIRS)], axis=0)
    tv_c = _bias_by_distance(rel_bias[OFF_C:OFF_C + N_HEADS], jnp.arange(2 * BAND), BAND)
    seg = jnp.arange(TN) // HEAD_DIM
    e_mat = (seg[:, None] == seg[None, :]).astype(ACT)
    src = jnp.asarray([c for c, _ in _LAYOUT], jnp.int32)
    fn = jnp.asarray([f for _, f in _LAYOUT[::PAIR]], jnp.int32)
    gains = _gain_rows(qk_g)

    w_in_b = w_in.astype(ACT)
    w_br_b = w_branch.astype(ACT)
    w_out_b = w_out.astype(ACT)

    for l in range(depth):
        p = _project(x, ln_g[:, None, :], w_in_b, gains[l][None, :], e_mat, src, fn, l)
        bra = _moba(p, tv_a)
        brb = _dilated(p, tv_b)
        brc = _swa(p, tv_c, sinks[l] * LOG2E)
        x2 = _merge(bra.reshape(b * s, W_BR), brb.reshape(b * s, W_BR), brc.reshape(b * s, W_BR),
                    p.reshape(b * s, P_W), w_br_b, w_out_b, x.reshape(b * s, d), l)
        x = x2.reshape(b, s, d)
    return x
```

```python
import functools
import math
import operator

import jax
import jax.numpy as jnp
from jax import lax
from jax.experimental import pallas as pl
from jax.experimental.pallas import tpu as pltpu

D_MODEL = 1024
HEAD_DIM = 64
W_BR = D_MODEL // 2
N_HEADS = W_BR // HEAD_DIM
N_PAIRS = N_HEADS // 2
MOBA_BLOCK = 256
MOBA_TOPK = 3
DIL_PAIRS = ((128, 1), (512, 4), (2048, 16))
N_DIL = len(DIL_PAIRS)
KV_C = 2
SWA_WINDOW = 128
N_BUCKETS = 32
REL_MAX_DIST = 2048
OFF_B = N_HEADS
OFF_C = N_HEADS + N_DIL * N_HEADS
EPS = 1e-6
NEG = -1e30
LOG2E = math.log2(math.e)

LANE = 128
BAND = SWA_WINDOW
assert all(w == BAND * d for w, d in DIL_PAIRS)
TN = 256
ROWS = 256
TILE_GROUP = 16
MOBA_GROUP = 8
EPILOGUE_LAG = 2
SINGLE_OP_STRIDE = 4
ONES_ROWS = 16
VMEM_LIMIT = 56 * 1024 * 1024

ACT = jnp.bfloat16
F32 = jnp.float32

PLAIN, NORM, SILU, SIGMOID, KNORM_VPLAIN, NORM_D4, PLAIN_D4, NORM_D16, PLAIN_D16 = range(9)
_B_GROUP = lambda g, norm, plain: [(8 + 2 * g, plain), (9 + 2 * g, plain), (14 + 2 * g, norm), (15 + 2 * g, norm),
                                   (20 + 2 * g, plain), (21 + 2 * g, plain)]
_LAYOUT = ([(j, SIGMOID) for j in range(33, 45)]
           + [(0, PLAIN), (1, PLAIN), (2, NORM), (3, NORM), (4, PLAIN), (5, PLAIN), (6, SILU), (7, SILU)]
           + _B_GROUP(0, NORM, PLAIN) + _B_GROUP(1, NORM_D4, PLAIN_D4) + _B_GROUP(2, NORM_D16, PLAIN_D16)
           + [(26, SILU), (27, SILU)]
           + [(28, PLAIN), (29, PLAIN), (31, SILU), (32, SILU), (30, KNORM_VPLAIN)])
C_IN = len(_LAYOUT) * TN
PAIR = 2
_LAYOUT = _LAYOUT + [(_LAYOUT[-1][0], KNORM_VPLAIN)] * (-len(_LAYOUT) % PAIR)
assert all(_LAYOUT[j][1] == _LAYOUT[j - j % PAIR][1] for j in range(len(_LAYOUT)))
assert [f for _, f in _LAYOUT].count(KNORM_VPLAIN) == PAIR
N_STEPS = len(_LAYOUT) // PAIR
P_W = len(_LAYOUT) * TN
COL_GATE, COL_A, COL_B, COL_SGB, COL_C = 0, 24, 40, 76, 80
GATE_W = 3 * D_MODEL
assert COL_GATE == 0


def _cparams(n_axes):
    return pltpu.CompilerParams(dimension_semantics=("arbitrary",) * n_axes, vmem_limit_bytes=VMEM_LIMIT)


def _dot(a, b):
    return jnp.dot(a, b, preferred_element_type=F32)


def _sigmoid(a):
    return 0.5 * jnp.tanh(0.5 * a) + 0.5


def _proj_kernel(src_ref, fn_ref, x_ref, lng_ref, *refs):
    del src_ref
    w_refs, gain_refs = refs[:PAIR], refs[PAIR:2 * PAIR]
    e_ref, o_ref, xn_ref, acc_ref, acc2_ref = refs[2 * PAIR:]
    j = pl.program_id(1)
    s = x_ref.shape[1]
    n_chunks = s // ROWS
    lanes_per_block = TN // LANE

    @pl.when(j == 0)
    def _():
        def body(i, c):
            rows = pl.ds(pl.multiple_of(i * ROWS, ROWS), ROWS)
            xv = x_ref[0, rows, :]
            ms = jnp.mean(xv * xv, axis=-1, keepdims=True)
            xn_ref[rows, :] = (xv * lax.rsqrt(ms + EPS) * lng_ref[...]).astype(ACT)
            return c
        lax.fori_loop(0, n_chunks, body, 0)

    def mean_square(a):
        return _dot((a * a).astype(ACT), e_ref[...])

    def head_norm(a, ss, gain):
        return a * lax.rsqrt(ss * (1.0 / HEAD_DIM) + EPS) * gain

    def k_norm_v_plain(a, ss, gain):
        lane = lax.broadcasted_iota(jnp.int32, a.shape, 1)
        return jnp.where(lane < KV_C * HEAD_DIM, head_norm(a, ss, gain), a)

    plain = (None, lambda a, _, gain: a)
    silu = (None, lambda a, _, gain: a * _sigmoid(a))
    sigmoid = (None, lambda a, _, gain: _sigmoid(a))
    norm = (mean_square, head_norm)
    knorm_vplain = (mean_square, k_norm_v_plain)

    def pipeline(n_items, per_block, produce, epilogue, store):
        pre, post = epilogue
        tiles = {}
        for n in range(n_items + EPILOGUE_LAG):
            if n < n_items:
                tiles[n] = produce(n)
            if n >= EPILOGUE_LAG:
                m = n - EPILOGUE_LAG
                tile = tiles.pop(m)
                store(m, post(tile, pre(tile) if pre is not None else None, gain_refs[m // per_block][...]))

    def phase_rows(src_ref, slab, t, d):
        r, m0 = divmod(t * BAND, s // d)
        return src_ref.at[slab][pl.ds(m0 * d + r, BAND, stride=d), :]

    fn = fn_ref[j]

    def chunk(n):
        h, i = divmod(n, n_chunks)
        return _dot(xn_ref[i * ROWS:(i + 1) * ROWS, :], w_refs[h][...])

    def out_cols(h):
        return slice(h * TN, (h + 1) * TN)

    def direct(code, epilogue, live=PAIR):
        @pl.when(fn == code)
        def _():
            def store(n, val):
                h, i = divmod(n, n_chunks)
                o_ref[0, i * ROWS:(i + 1) * ROWS, out_cols(h)] = val.astype(o_ref.dtype)

            pipeline(live * n_chunks, n_chunks, chunk, epilogue, store)
            if live < PAIR:
                o_ref[0, :, live * TN:] = jnp.zeros((s, (PAIR - live) * TN), o_ref.dtype)

    def phase_major(code, epilogue, d):
        @pl.when(fn == code)
        def _():
            def stage(n, val):
                h, i = divmod(n, n_chunks)
                for c in range(lanes_per_block):
                    acc_ref[h * lanes_per_block + c, i * ROWS:(i + 1) * ROWS, :] = val[:, c * LANE:(c + 1) * LANE]

            pipeline(PAIR * n_chunks, n_chunks, chunk, epilogue, stage)
            src_ref, stride = acc_ref, d
            if d > SINGLE_OP_STRIDE:
                assert d == SINGLE_OP_STRIDE * SINGLE_OP_STRIDE
                for slab in range(PAIR * lanes_per_block):
                    for t in range(s // BAND):
                        acc2_ref[slab, t * BAND:(t + 1) * BAND, :] = phase_rows(acc_ref, slab, t, SINGLE_OP_STRIDE)
                src_ref, stride = acc2_ref, SINGLE_OP_STRIDE
            for h in range(PAIR):
                for t in range(s // BAND):
                    a = jnp.concatenate([phase_rows(src_ref, h * lanes_per_block + c, t, stride)
                                         for c in range(lanes_per_block)], axis=1)
                    o_ref[0, t * BAND:(t + 1) * BAND, out_cols(h)] = a.astype(o_ref.dtype)

    direct(PLAIN, plain)
    direct(NORM, norm)
    direct(SILU, silu)
    direct(SIGMOID, sigmoid)
    direct(KNORM_VPLAIN, knorm_vplain, live=1)
    phase_major(NORM_D4, norm, DIL_PAIRS[1][1])
    phase_major(PLAIN_D4, plain, DIL_PAIRS[1][1])
    phase_major(NORM_D16, norm, DIL_PAIRS[2][1])
    phase_major(PLAIN_D16, plain, DIL_PAIRS[2][1])


def _w_index(bi, j, src_ref, fn_ref, *, layer, slot):
    return layer, 0, src_ref[PAIR * j + slot]


def _gain_index(bi, j, src_ref, fn_ref, *, slot):
    return 0, src_ref[PAIR * j + slot]


def _project(x, ln_g, w_in, gain_row, e_mat, src, fn, layer):
    b, s, d = x.shape
    n_in = PAIR
    grid_spec = pltpu.PrefetchScalarGridSpec(
        num_scalar_prefetch=2,
        grid=(b, N_STEPS),
        in_specs=[
            pl.BlockSpec((1, s, d), lambda bi, j, src_ref, fn_ref: (bi, 0, 0)),
            pl.BlockSpec((None, 1, d), lambda bi, j, src_ref, fn_ref: (layer, 0, 0)),
        ] + [pl.BlockSpec((None, d, TN), functools.partial(_w_index, layer=layer, slot=n)) for n in range(n_in)]
          + [pl.BlockSpec((1, TN), functools.partial(_gain_index, slot=n)) for n in range(n_in)]
          + [pl.BlockSpec((TN, TN), lambda bi, j, src_ref, fn_ref: (0, 0))],
        out_specs=pl.BlockSpec((1, s, n_in * TN), lambda bi, j, src_ref, fn_ref: (bi, 0, j)),
        scratch_shapes=[pltpu.VMEM((s, d), ACT), pltpu.VMEM((PAIR * TN // LANE, s, LANE), F32),
                        pltpu.VMEM((PAIR * TN // LANE, s, LANE), F32)],
    )
    return pl.pallas_call(
        _proj_kernel,
        grid_spec=grid_spec,
        out_shape=jax.ShapeDtypeStruct((b, s, P_W), ACT),
        compiler_params=_cparams(2),
        name="proj",
    )(src, fn, x, ln_g, *([w_in] * n_in), *([gain_row] * n_in), e_mat)


def _toeplitz(lo, hi, n):
    x = jnp.concatenate([jnp.broadcast_to(lo, (n, n)), jnp.broadcast_to(hi, (n, n))], axis=1)
    return pltpu.roll(x, 0, 1, stride=1, stride_axis=0)[:, n:]


def _unit_rms_heads(qt):
    halves = []
    for sh in range(2):
        x = qt[HEAD_DIM * sh:HEAD_DIM * (sh + 1), :].astype(F32)
        halves.append(x * lax.rsqrt(jnp.mean(x * x, axis=0, keepdims=True) + EPS))
    return jnp.concatenate(halves, axis=0).astype(qt.dtype)


def _head_rows(qt, lo):
    row = lax.broadcasted_iota(jnp.int32, qt.shape, 0)
    return jnp.where((row >= lo) & (row < lo + HEAD_DIM), qt, jnp.zeros_like(qt))


def _moba_kernel(q_ref, k_ref, v_ref, sg_ref, tv_ref, o_ref,
                 bt_ref, qts_ref, vt_ref, km_ref, sel_ref, m_ref, acc_ref):
    hp = pl.program_id(0)
    b = pl.program_id(1)
    blk = MOBA_BLOCK
    n_blk = k_ref.shape[1] // blk

    def rows_of(i):
        return pl.ds(i * blk if isinstance(i, int) else pl.multiple_of(i * blk, blk), blk)

    @pl.when(b == 0)
    def _():
        kr = lax.broadcasted_iota(jnp.int32, (blk, blk), 0)
        qc = lax.broadcasted_iota(jnp.int32, (blk, blk), 1)
        for sh in range(2):
            def body(o, c, sh=sh):
                t = _toeplitz(tv_ref[2 * hp + sh, o], tv_ref[2 * hp + sh, o + 1], blk)
                bt_ref[sh, o] = jnp.where((qc >= kr) | (o > 0), t, NEG)
                return c
            lax.fori_loop(0, n_blk, body, 0)

    for i in range(n_blk):
        vt = v_ref[0, rows_of(i), :].T
        for sh in range(2):
            vt_ref[i, sh] = jnp.concatenate(
                [vt[HEAD_DIM * sh:HEAD_DIM * (sh + 1), :], jnp.ones((ONES_ROWS, blk), ACT)], axis=0)
        km_ref[i:i + 1, :] = jnp.mean(k_ref[0, rows_of(i), :].astype(F32), axis=0, keepdims=True)
        qt = _unit_rms_heads(q_ref[0, rows_of(i), :].T)
        for sh in range(2):
            qts_ref[i, sh] = _head_rows(qt, HEAD_DIM * sh)

    km = km_ref[...]
    km0 = km.astype(ACT)
    r1 = km - km0.astype(F32)
    km1 = r1.astype(ACT)
    km2 = (r1 - km1.astype(F32)).astype(ACT)
    blk_row = lax.broadcasted_iota(jnp.int32, (n_blk, blk), 0)

    ranked = [(qi, sh) for qi in range(MOBA_TOPK + 1, n_blk) for sh in range(2)]
    gates = [_dot(km0, qts_ref[qi, sh]) + _dot(km1, qts_ref[qi, sh]) + _dot(km2, qts_ref[qi, sh])
             for qi, sh in ranked]
    for (qi, sh), g in zip(ranked, gates):
        g = jnp.where(blk_row < qi, g, NEG)
        sel_rows = []
        for jb in range(qi):
            gj = g[jb:jb + 1, :]
            beats = (g > gj) | ((g == gj) & (blk_row < jb))
            cnt = jnp.sum(beats.astype(F32), axis=0, keepdims=True)
            sel_rows.append(jnp.where(cnt < MOBA_TOPK, 1.0, 0.0))
        sel_rows.append(jnp.zeros((n_blk - qi, blk), F32))
        sel_ref[qi, sh] = jnp.concatenate(sel_rows, axis=0)

    def tiles(off, qis):
        k_blks = [k_ref[0, rows_of(qi - off), :] for qi in qis]
        scores = [_dot(k_blk, qts_ref[qi, sh]) for qi, k_blk in zip(qis, k_blks) for sh in range(2)]
        work = [(qi, sh) for qi in qis for sh in range(2)]
        stats = []
        half = blk // 2
        for (qi, sh), st in zip(work, scores):
            if off == 0:
                bias = bt_ref[sh, 0]
                parts = [(st[:half, :half] + bias[:half, :half], slice(0, half)),
                         (st[:, half:] + bias[:, half:], slice(0, blk))]
                maxes = [jnp.max(s_part, axis=0, keepdims=True) for s_part, _ in parts]
                m_ref[qi, sh] = jnp.concatenate(maxes, axis=1)
                stats.append((None, [(keys, jnp.exp2(s_part - m_part).astype(ACT))
                                     for (s_part, keys), m_part in zip(parts, maxes)]))
                continue
            st = st + bt_ref[sh, off]
            tile_max = jnp.max(st, axis=0, keepdims=True)
            if qi <= MOBA_TOPK:
                m = m_ref[qi, sh]
                m_new = jnp.maximum(m, tile_max)
                p = jnp.exp2(st - m_new)
                alpha = jnp.exp2(m - m_new)
            else:
                sel = sel_ref[qi, sh, pl.ds(qi - off, 1), :] > 0.5
                m = m_ref[qi, sh]
                m_new = jnp.where(sel, jnp.maximum(m, tile_max), m)
                p = jnp.exp2(st - jnp.where(sel, m_new, -NEG))
                alpha = jnp.exp2(m - m_new)
            m_ref[qi, sh] = m_new
            stats.append((alpha, [(slice(0, blk), p.astype(ACT))]))
        pvs = [jnp.concatenate([_dot(vt_ref[qi - off, sh, :, keys], p) for keys, p in parts], axis=1)
               for (qi, sh), (_, parts) in zip(work, stats)]
        for (qi, sh), (alpha, _), pv in zip(work, stats, pvs):
            acc_ref[qi, sh] = pv if off == 0 else alpha * acc_ref[qi, sh] + pv

    def finish(qi):
        ot = jnp.concatenate([acc_ref[qi, sh, :HEAD_DIM, :] * (1.0 / acc_ref[qi, sh, HEAD_DIM:HEAD_DIM + 1, :])
                              for sh in range(2)], axis=0)
        o_ref[0, rows_of(qi), :] = (ot.T * sg_ref[0, rows_of(qi), :].astype(F32)).astype(o_ref.dtype)

    for off in range(n_blk):
        qis = list(range(off, n_blk))
        for lo in range(0, len(qis), MOBA_GROUP):
            tiles(off, qis[lo:lo + MOBA_GROUP])
        finish(off)


def _moba(p, tv_a):
    b, s, _ = p.shape
    blk = MOBA_BLOCK
    n_blk = s // blk
    col = lambda c: pl.BlockSpec((1, s, LANE), lambda hp, bi: (bi, 0, COL_A + c * N_PAIRS + hp))
    return pl.pallas_call(
        _moba_kernel,
        grid=(N_PAIRS, b),
        in_specs=[col(0), col(1), col(2), col(3), pl.BlockSpec(tv_a.shape, lambda hp, bi: (0, 0, 0, 0))],
        out_specs=pl.BlockSpec((1, s, LANE), lambda hp, bi: (bi, 0, hp)),
        out_shape=jax.ShapeDtypeStruct((b, s, W_BR), ACT),
        scratch_shapes=[
            pltpu.VMEM((2, n_blk, blk, blk), F32),
            pltpu.VMEM((n_blk, 2, LANE, blk), ACT),
            pltpu.VMEM((n_blk, 2, HEAD_DIM + ONES_ROWS, blk), ACT),
            pltpu.VMEM((n_blk, LANE), F32),
            pltpu.VMEM((n_blk, 2, n_blk, blk), F32),
            pltpu.VMEM((n_blk, 2, 1, blk), F32),
            pltpu.VMEM((n_blk, 2, HEAD_DIM + ONES_ROWS, blk), F32),
        ],
        compiler_params=_cparams(2),
        name="moba",
    )(p, p, p, p, tv_a)


def _band_attend(items):
    def fuse(blocks):
        if len(blocks) > 1 and all(valid is None for _, _, _, valid in blocks):
            ks, vts, biases, _ = zip(*blocks)
            return [(jnp.concatenate(ks, axis=0), jnp.concatenate(vts, axis=1), jnp.concatenate(biases, axis=0), None)]
        return blocks

    items = [(qs, fuse(blocks), sink) for qs, blocks, sink in items]
    scores = [[_dot(k, qs) for k, _, _, _ in blocks] for qs, blocks, _ in items]
    probs = []
    for (_, blocks, sink), sc in zip(items, scores):
        ss = [s + bias if valid is None else jnp.where(valid, s + bias, NEG)
              for s, (_, _, bias, valid) in zip(sc, blocks)]
        m = functools.reduce(jnp.maximum, [jnp.max(s, axis=0, keepdims=True) for s in ss])
        if sink is not None:
            m = jnp.maximum(m, sink)
        probs.append((m, [jnp.exp2(s - m).astype(ACT) for s in ss]))
    outs = []
    for (_, blocks, sink), (m, ps) in zip(items, probs):
        n_v = blocks[0][1].shape[0]
        acc = functools.reduce(operator.add, [
            _dot(jnp.concatenate([vt, jnp.ones((ONES_ROWS, vt.shape[1]), ACT)], axis=0), p)
            for (_, vt, _, _), p in zip(blocks, ps)])
        l = acc[n_v:n_v + 1, :]
        if sink is not None:
            l = l + jnp.exp2(sink - m)
        outs.append((acc[:n_v, :] * (1.0 / l), m + jnp.log2(l)))
    return outs


def _band_bias(tv_ref, head, prev_includes_window_edge):
    kr = lax.broadcasted_iota(jnp.int32, (BAND, BAND), 0)
    qc = lax.broadcasted_iota(jnp.int32, (BAND, BAND), 1)
    own, prev = [_toeplitz(tv_ref[head, which], tv_ref[head, which + 1], BAND) for which in range(2)]
    prev_mask = (qc <= kr) if prev_includes_window_edge else (qc < kr)
    return jnp.where(qc >= kr, own, NEG), jnp.where(prev_mask, prev, NEG)


def _dil_kernel(q0_ref, k0_ref, v0_ref, q1_ref, k1_ref, v1_ref, q2_ref, k2_ref, v2_ref, sg_ref, tv_ref,
                o_ref, og_ref, lg_ref, bt_ref):
    hp = pl.program_id(0)
    b = pl.program_id(1)
    s = q0_ref.shape[1]
    n_tiles = s // BAND

    @pl.when(b == 0)
    def _():
        for g in range(N_DIL):
            for sh in range(2):
                own, prev = _band_bias(tv_ref, g * N_HEADS + 2 * hp + sh, True)
                bt_ref[g, sh, 0] = own
                bt_ref[g, sh, 1] = prev

    groups = ((q0_ref, k0_ref, v0_ref), (q1_ref, k1_ref, v1_ref), (q2_ref, k2_ref, v2_ref))

    def mix(rows, o0, l0):
        l1, l2 = lg_ref[0, rows, :], lg_ref[1, rows, :]
        m = jnp.maximum(l0, jnp.maximum(l1, l2))
        e0, e1, e2 = jnp.exp2(l0 - m), jnp.exp2(l1 - m), jnp.exp2(l2 - m)
        ob = (e0 * o0 + e1 * og_ref[0, rows, :] + e2 * og_ref[1, rows, :]) / (e0 + e1 + e2)
        o_ref[0, rows, :] = (ob * sg_ref[0, rows, :].astype(F32)).astype(o_ref.dtype)

    for g in (*range(1, N_DIL), 0):
        q_ref, k_ref, v_ref = groups[g]
        d = DIL_PAIRS[g][1]
        assert (d == 1) == (g == 0)
        per_phase = n_tiles // d
        has_prev = per_phase > 1
        assert per_phase % TILE_GROUP == 0 or TILE_GROUP % per_phase == 0

        def body(i, c, g=g, d=d, per_phase=per_phase, has_prev=has_prev, q_ref=q_ref, k_ref=k_ref, v_ref=v_ref):
            items, dsts = [], []
            k_own = vt_own = None
            for n in range(TILE_GROUP):
                u = i * TILE_GROUP + n
                rows = pl.ds(pl.multiple_of(u * BAND, BAND), BAND)
                t = lax.rem(u, per_phase)
                r = lax.div(u, per_phase)
                qt = _unit_rms_heads(q_ref[0, rows, :].T)
                k_prev, vt_prev = k_own, vt_own
                k_own = k_ref[0, rows, :]
                vt_own = v_ref[0, rows, :].T
                if has_prev:
                    phase_start_known = TILE_GROUP % per_phase == 0
                    prev_exists = (t > 0) if (n == 0 and not phase_start_known) else None
                    if phase_start_known and n % per_phase == 0:
                        k_prev = None
                    elif n == 0:
                        prow = pl.ds(pl.multiple_of(jnp.maximum(u - 1, 0) * BAND, BAND), BAND)
                        k_prev = k_ref[0, prow, :]
                        vt_prev = v_ref[0, prow, :].T
                for sh in range(2):
                    hrows = slice(HEAD_DIM * sh, HEAD_DIM * (sh + 1))
                    blocks = [(k_own, vt_own[hrows, :], bt_ref[g, sh, 0], None)]
                    if has_prev and k_prev is not None:
                        blocks.append((k_prev, vt_prev[hrows, :], bt_ref[g, sh, 1], prev_exists))
                    items.append((_head_rows(qt, HEAD_DIM * sh), blocks, None))
                dsts.append(rows if d == 1 else pl.ds(t * (BAND * d) + r, BAND, stride=d))
            res = _band_attend(items)
            for n, dst in enumerate(dsts):
                (o0, lse0), (o1, lse1) = res[2 * n], res[2 * n + 1]
                out = jnp.concatenate([o0, o1], axis=0).T
                lse = jnp.concatenate(
                    [jnp.broadcast_to(lse0, (HEAD_DIM, BAND)), jnp.broadcast_to(lse1, (HEAD_DIM, BAND))], axis=0).T
                if g == 0:
                    mix(dst, out, lse)
                else:
                    og_ref.at[g - 1][dst, :] = out
                    lg_ref.at[g - 1][dst, :] = lse
            return c

        lax.fori_loop(0, n_tiles // TILE_GROUP, body, 0)


def _dilated(p, tv_b):
    b, s, _ = p.shape
    col = lambda c: pl.BlockSpec((1, s, LANE), lambda hp, bi: (bi, 0, c + hp))
    qkv = [col(COL_B + 3 * N_PAIRS * g + N_PAIRS * i) for g in range(N_DIL) for i in range(3)]
    return pl.pallas_call(
        _dil_kernel,
        grid=(N_PAIRS, b),
        in_specs=qkv + [col(COL_SGB), pl.BlockSpec(tv_b.shape, lambda hp, bi: (0, 0, 0, 0))],
        out_specs=pl.BlockSpec((1, s, LANE), lambda hp, bi: (bi, 0, hp)),
        out_shape=jax.ShapeDtypeStruct((b, s, W_BR), ACT),
        scratch_shapes=[
            pltpu.VMEM((N_DIL - 1, s, LANE), F32),
            pltpu.VMEM((N_DIL - 1, s, LANE), F32),
            pltpu.VMEM((N_DIL, 2, 2, BAND, BAND), F32),
        ],
        compiler_params=_cparams(2),
        name="dilated",
    )(*([p] * 10), tv_b)


def _swa_kernel(sink_ref, q_ref, sg_ref, k_ref, v_ref, tv_ref, o_ref, bt_ref, vts_ref):
    hp = pl.program_id(0)
    b = pl.program_id(1)
    n_tiles = q_ref.shape[1] // BAND
    pairs_per_kv = N_PAIRS // KV_C

    @pl.when(b == 0)
    def _():
        for sh in range(2):
            own, prev = _band_bias(tv_ref, 2 * hp + sh, False)
            bt_ref[sh, 0] = own
            bt_ref[sh, 1] = prev

    kv_lo = HEAD_DIM * (hp // pairs_per_kv)
    row = lax.broadcasted_iota(jnp.int32, (LANE, BAND), 0)
    kv_rows = (row >= kv_lo) & (row < kv_lo + HEAD_DIM)

    def kv_vt(rows, slot):
        vts_ref[slot] = v_ref[0, rows, :].T
        return vts_ref[slot, pl.ds(pl.multiple_of(kv_lo, HEAD_DIM), HEAD_DIM), :]

    def body(i, c):
        items, dsts = [], []
        zero = jnp.zeros((LANE, BAND), ACT)
        k_own = vt_own = None
        for n in range(TILE_GROUP):
            t = i * TILE_GROUP + n
            rows = pl.ds(pl.multiple_of(t * BAND, BAND), BAND)
            qt = _unit_rms_heads(q_ref[0, rows, :].T)
            k_prev, vt_prev = k_own, vt_own
            k_own = k_ref[0, rows, :]
            vt_own = kv_vt(rows, n)
            if n == 0:
                prow = pl.ds(pl.multiple_of(jnp.maximum(t - 1, 0) * BAND, BAND), BAND)
                k_prev = k_ref[0, prow, :]
                vt_prev = kv_vt(prow, TILE_GROUP)
            prev_exists = (t > 0) if n == 0 else None
            for sh in range(2):
                qh = qt[HEAD_DIM * sh:HEAD_DIM * (sh + 1), :]
                qs = jnp.where(kv_rows, jnp.concatenate([qh, qh], axis=0), zero)
                items.append((qs, [(k_own, vt_own, bt_ref[sh, 0], None), (k_prev, vt_prev, bt_ref[sh, 1], prev_exists)],
                              sink_ref[2 * hp + sh]))
            dsts.append(rows)
        res = _band_attend(items)
        for n, rows in enumerate(dsts):
            ot = jnp.concatenate([o for o, _ in res[2 * n:2 * n + 2]], axis=0).T
            o_ref[0, rows, :] = (ot * sg_ref[0, rows, :].astype(F32)).astype(o_ref.dtype)
        return c

    lax.fori_loop(0, n_tiles // TILE_GROUP, body, 0)


def _swa(p, tv_c, sinks):
    b, s, _ = p.shape
    grid_spec = pltpu.PrefetchScalarGridSpec(
        num_scalar_prefetch=1,
        grid=(N_PAIRS, b),
        in_specs=[
            pl.BlockSpec((1, s, LANE), lambda hp, bi, sk: (bi, 0, COL_C + hp)),
            pl.BlockSpec((1, s, LANE), lambda hp, bi, sk: (bi, 0, COL_C + N_PAIRS + hp)),
            pl.BlockSpec((1, s, LANE), lambda hp, bi, sk: (bi, 0, COL_C + 2 * N_PAIRS)),
            pl.BlockSpec((1, s, LANE), lambda hp, bi, sk: (bi, 0, COL_C + 2 * N_PAIRS + 1)),
            pl.BlockSpec(tv_c.shape, lambda hp, bi, sk: (0, 0, 0, 0)),
        ],
        out_specs=pl.BlockSpec((1, s, LANE), lambda hp, bi, sk: (bi, 0, hp)),
        scratch_shapes=[pltpu.VMEM((2, 2, BAND, BAND), F32),
                        pltpu.VMEM((TILE_GROUP + 1, LANE, BAND), ACT)],
    )
    return pl.pallas_call(
        _swa_kernel,
        grid_spec=grid_spec,
        out_shape=jax.ShapeDtypeStruct((b, s, W_BR), ACT),
        compiler_params=_cparams(2),
        name="swa",
    )(sinks, p, p, p, p, tv_c)


def _merge_kernel(bra_ref, brb_ref, brc_ref, g_ref, wb_ref, wo_ref, x_ref, out_ref):
    merged = None
    for i, br_ref in enumerate((bra_ref, brb_ref, brc_ref)):
        term = g_ref[:, i * D_MODEL:(i + 1) * D_MODEL].astype(F32) * _dot(br_ref[...], wb_ref[i])
        merged = term if merged is None else merged + term
    out_ref[...] = x_ref[...] + _dot(merged.astype(ACT), wo_ref[...])


def _merge(bra, brb, brc, p2, w_br, w_out, x2, layer, ts=512):
    n = x2.shape[0]
    row = lambda w: pl.BlockSpec((ts, w), lambda i: (i, 0))
    return pl.pallas_call(
        _merge_kernel,
        grid=(n // ts,),
        in_specs=[row(W_BR), row(W_BR), row(W_BR), row(GATE_W),
                  pl.BlockSpec((None,) + w_br.shape[1:], lambda i: (layer, 0, 0, 0)),
                  pl.BlockSpec((None,) + w_out.shape[1:], lambda i: (layer, 0, 0)),
                  row(D_MODEL)],
        out_specs=row(D_MODEL),
        out_shape=jax.ShapeDtypeStruct(x2.shape, F32),
        compiler_params=_cparams(1),
        name="merge",
    )(bra, brb, brc, p2, w_br, w_out, x2)


def _rel_bucket(dist):
    dist = jnp.maximum(dist, 0)
    max_exact = N_BUCKETS // 2
    log_ratio = jnp.log(jnp.maximum(dist, 1).astype(F32) / max_exact) / math.log(REL_MAX_DIST / max_exact)
    large = jnp.minimum(max_exact + (log_ratio * (N_BUCKETS - max_exact)).astype(jnp.int32), N_BUCKETS - 1)
    return jnp.where(dist < max_exact, dist, large)


def _bias_by_distance(tab, dist, chunk):
    vals = tab.astype(F32)[:, _rel_bucket(dist)] * LOG2E
    vals = jnp.pad(vals, ((0, 0), (chunk, 0)))
    return vals.reshape(tab.shape[0], -1, 1, chunk)


def _gain_rows(qk_g):
    depth = qk_g.shape[0]
    scale = HEAD_DIM ** -0.5 * LOG2E
    one = jnp.ones((depth, HEAD_DIM), F32)
    tile = lambda v, n: jnp.tile(v, (1, n))
    parts = [tile(one, N_HEADS), tile(qk_g[:, 1] * qk_g[:, 0] * scale, N_HEADS), tile(one, 2 * N_HEADS),
             tile(one, N_DIL * N_HEADS), tile(qk_g[:, 3] * qk_g[:, 2] * scale, N_DIL * N_HEADS),
             tile(one, (N_DIL + 1) * N_HEADS),
             tile(one, N_HEADS), tile(qk_g[:, 5] * qk_g[:, 4] * scale, KV_C), tile(one, KV_C + N_HEADS),
             tile(one, GATE_W // HEAD_DIM)]
    return jnp.concatenate(parts, axis=1)


def kernel(x, ln_g, w_in, qk_g, sinks, w_branch, w_out, rel_bias):
    b, s, d = x.shape
    depth = ln_g.shape[0]
    assert d == D_MODEL and s % DIL_PAIRS[-1][0] == 0 and w_in.shape[-1] == C_IN

    tv_a = _bias_by_distance(rel_bias[:N_HEADS], jnp.arange(s), MOBA_BLOCK)
    tv_b = jnp.concatenate(
        [_bias_by_distance(rel_bias[OFF_B + g * N_HEADS:OFF_B + (g + 1) * N_HEADS], dd * jnp.arange(2 * BAND), BAND)
         for g, (_, dd) in enumerate(DIL_PAIRS)], axis=0)
    tv_c = _bias_by_distance(rel_bias[OFF_C:OFF_C + N_HEADS], jnp.arange(2 * BAND), BAND)
    seg = jnp.arange(TN) // HEAD_DIM
    e_mat = (seg[:, None] == seg[None, :]).astype(ACT)
    src = jnp.asarray([c for c, _ in _LAYOUT], jnp.int32)
    fn = jnp.asarray([f for _, f in _LAYOUT[::PAIR]], jnp.int32)
    gains = _gain_rows(qk_g)

    w_in_b = w_in.astype(ACT)
    w_br_b = w_branch.astype(ACT)
    w_out_b = w_out.astype(ACT)

    for l in range(depth):
        p = _project(x, ln_g[:, None, :], w_in_b, gains[l][None, :], e_mat, src, fn, l)
        bra = _moba(p, tv_a)
        brb = _dilated(p, tv_b)
        brc = _swa(p, tv_c, sinks[l] * LOG2E)
        x2 = _merge(bra.reshape(b * s, W_BR), brb.reshape(b * s, W_BR), brc.reshape(b * s, W_BR),
                    p.reshape(b * s, P_W), w_br_b, w_out_b, x.reshape(b * s, d), l)
        x = x2.reshape(b, s, d)
    return x
```

```python
import functools
import math
import operator

import jax
import jax.numpy as jnp
from jax import lax
from jax.experimental import pallas as pl
from jax.experimental.pallas import tpu as pltpu

D_MODEL = 1024
HEAD_DIM = 64
W_BR = D_MODEL // 2
N_HEADS = W_BR // HEAD_DIM
N_PAIRS = N_HEADS // 2
MOBA_BLOCK = 256
MOBA_TOPK = 3
DIL_PAIRS = ((128, 1), (512, 4), (2048, 16))
N_DIL = len(DIL_PAIRS)
KV_C = 2
SWA_WINDOW = 128
N_BUCKETS = 32
REL_MAX_DIST = 2048
OFF_B = N_HEADS
OFF_C = N_HEADS + N_DIL * N_HEADS
EPS = 1e-6
NEG = -1e30
LOG2E = math.log2(math.e)

LANE = 128
BAND = SWA_WINDOW
assert all(w == BAND * d for w, d in DIL_PAIRS)
TN = 256
ROWS = 256
TILE_GROUP = 16
MOBA_GROUP = 8
EPILOGUE_LAG = 2
SINGLE_OP_STRIDE = 4
ONES_ROWS = 16
VMEM_LIMIT = 56 * 1024 * 1024

ACT = jnp.bfloat16
F32 = jnp.float32

PLAIN, NORM, SILU, SIGMOID, KNORM_VPLAIN, NORM_D4, PLAIN_D4, NORM_D16, PLAIN_D16 = range(9)
_B_GROUP = lambda g, norm, plain: [(8 + 2 * g, plain), (9 + 2 * g, plain), (14 + 2 * g, norm), (15 + 2 * g, norm),
                                   (20 + 2 * g, plain), (21 + 2 * g, plain)]
_LAYOUT = ([(j, SIGMOID) for j in range(33, 45)]
           + [(0, PLAIN), (1, PLAIN), (2, NORM), (3, NORM), (4, PLAIN), (5, PLAIN), (6, SILU), (7, SILU)]
           + _B_GROUP(0, NORM, PLAIN) + _B_GROUP(1, NORM_D4, PLAIN_D4) + _B_GROUP(2, NORM_D16, PLAIN_D16)
           + [(26, SILU), (27, SILU)]
           + [(28, PLAIN), (29, PLAIN), (31, SILU), (32, SILU), (30, KNORM_VPLAIN)])
C_IN = len(_LAYOUT) * TN
PAIR = 2
_LAYOUT = _LAYOUT + [(_LAYOUT[-1][0], KNORM_VPLAIN)] * (-len(_LAYOUT) % PAIR)
assert all(_LAYOUT[j][1] == _LAYOUT[j - j % PAIR][1] for j in range(len(_LAYOUT)))
assert [f for _, f in _LAYOUT].count(KNORM_VPLAIN) == PAIR
N_STEPS = len(_LAYOUT) // PAIR
P_W = len(_LAYOUT) * TN
COL_GATE, COL_A, COL_B, COL_SGB, COL_C = 0, 24, 40, 76, 80
GATE_W = 3 * D_MODEL
assert COL_GATE == 0


def _cparams(n_axes):
    return pltpu.CompilerParams(dimension_semantics=("arbitrary",) * n_axes, vmem_limit_bytes=VMEM_LIMIT)


def _dot(a, b):
    return jnp.dot(a, b, preferred_element_type=F32)


def _sigmoid(a):
    return 0.5 * jnp.tanh(0.5 * a) + 0.5


def _proj_kernel(src_ref, fn_ref, x_ref, lng_ref, *refs):
    del src_ref
    w_refs, gain_refs = refs[:PAIR], refs[PAIR:2 * PAIR]
    e_ref, o_ref, xn_ref, acc_ref, acc2_ref = refs[2 * PAIR:]
    j = pl.program_id(1)
    s = x_ref.shape[1]
    n_chunks = s // ROWS
    lanes_per_block = TN // LANE

    @pl.when(j == 0)
    def _():
        def body(i, c):
            rows = pl.ds(pl.multiple_of(i * ROWS, ROWS), ROWS)
            xv = x_ref[0, rows, :]
            ms = jnp.mean(xv * xv, axis=-1, keepdims=True)
            xn_ref[rows, :] = (xv * lax.rsqrt(ms + EPS) * lng_ref[...]).astype(ACT)
            return c
        lax.fori_loop(0, n_chunks, body, 0)

    def mean_square(a):
        return _dot((a * a).astype(ACT), e_ref[...])

    def head_norm(a, ss, gain):
        return a * lax.rsqrt(ss * (1.0 / HEAD_DIM) + EPS) * gain

    def k_norm_v_plain(a, ss, gain):
        lane = lax.broadcasted_iota(jnp.int32, a.shape, 1)
        return jnp.where(lane < KV_C * HEAD_DIM, head_norm(a, ss, gain), a)

    plain = (None, lambda a, _, gain: a)
    silu = (None, lambda a, _, gain: a * _sigmoid(a))
    sigmoid = (None, lambda a, _, gain: _sigmoid(a))
    norm = (mean_square, head_norm)
    knorm_vplain = (mean_square, k_norm_v_plain)

    def pipeline(n_items, per_block, produce, epilogue, store):
        pre, post = epilogue
        tiles = {}
        for n in range(n_items + EPILOGUE_LAG):
            if n < n_items:
                tiles[n] = produce(n)
            if n >= EPILOGUE_LAG:
                m = n - EPILOGUE_LAG
                tile = tiles.pop(m)
                store(m, post(tile, pre(tile) if pre is not None else None, gain_refs[m // per_block][...]))

    def phase_rows(src_ref, slab, t, d):
        r, m0 = divmod(t * BAND, s // d)
        return src_ref.at[slab][pl.ds(m0 * d + r, BAND, stride=d), :]

    fn = fn_ref[j]

    def chunk(n):
        h, i = divmod(n, n_chunks)
        return _dot(xn_ref[i * ROWS:(i + 1) * ROWS, :], w_refs[h][...])

    def out_cols(h):
        return slice(h * TN, (h + 1) * TN)

    def direct(code, epilogue, live=PAIR):
        @pl.when(fn == code)
        def _():
            def store(n, val):
                h, i = divmod(n, n_chunks)
                o_ref[0, i * ROWS:(i + 1) * ROWS, out_cols(h)] = val.astype(o_ref.dtype)

            pipeline(live * n_chunks, n_chunks, chunk, epilogue, store)
            if live < PAIR:
                o_ref[0, :, live * TN:] = jnp.zeros((s, (PAIR - live) * TN), o_ref.dtype)

    def phase_major(code, epilogue, d):
        @pl.when(fn == code)
        def _():
            def stage(n, val):
                h, i = divmod(n, n_chunks)
                for c in range(lanes_per_block):
                    acc_ref[h * lanes_per_block + c, i * ROWS:(i + 1) * ROWS, :] = val[:, c * LANE:(c + 1) * LANE]

            pipeline(PAIR * n_chunks, n_chunks, chunk, epilogue, stage)
            src_ref, stride = acc_ref, d
            if d > SINGLE_OP_STRIDE:
                assert d == SINGLE_OP_STRIDE * SINGLE_OP_STRIDE
                for slab in range(PAIR * lanes_per_block):
                    for t in range(s // BAND):
                        acc2_ref[slab, t * BAND:(t + 1) * BAND, :] = phase_rows(acc_ref, slab, t, SINGLE_OP_STRIDE)
                src_ref, stride = acc2_ref, SINGLE_OP_STRIDE
            for h in range(PAIR):
                for t in range(s // BAND):
                    a = jnp.concatenate([phase_rows(src_ref, h * lanes_per_block + c, t, stride)
                                         for c in range(lanes_per_block)], axis=1)
                    o_ref[0, t * BAND:(t + 1) * BAND, out_cols(h)] = a.astype(o_ref.dtype)

    direct(PLAIN, plain)
    direct(NORM, norm)
    direct(SILU, silu)
    direct(SIGMOID, sigmoid)
    direct(KNORM_VPLAIN, knorm_vplain, live=1)
    phase_major(NORM_D4, norm, DIL_PAIRS[1][1])
    phase_major(PLAIN_D4, plain, DIL_PAIRS[1][1])
    phase_major(NORM_D16, norm, DIL_PAIRS[2][1])
    phase_major(PLAIN_D16, plain, DIL_PAIRS[2][1])


def _w_index(bi, j, src_ref, fn_ref, *, layer, slot):
    return layer, 0, src_ref[PAIR * j + slot]


def _gain_index(bi, j, src_ref, fn_ref, *, slot):
    return 0, src_ref[PAIR * j + slot]


def _project(x, ln_g, w_in, gain_row, e_mat, src, fn, layer):
    b, s, d = x.shape
    n_in = PAIR
    grid_spec = pltpu.PrefetchScalarGridSpec(
        num_scalar_prefetch=2,
        grid=(b, N_STEPS),
        in_specs=[
            pl.BlockSpec((1, s, d), lambda bi, j, src_ref, fn_ref: (bi, 0, 0)),
            pl.BlockSpec((None, 1, d), lambda bi, j, src_ref, fn_ref: (layer, 0, 0)),
        ] + [pl.BlockSpec((None, d, TN), functools.partial(_w_index, layer=layer, slot=n)) for n in range(n_in)]
          + [pl.BlockSpec((1, TN), functools.partial(_gain_index, slot=n)) for n in range(n_in)]
          + [pl.BlockSpec((TN, TN), lambda bi, j, src_ref, fn_ref: (0, 0))],
        out_specs=pl.BlockSpec((1, s, n_in * TN), lambda bi, j, src_ref, fn_ref: (bi, 0, j)),
        scratch_shapes=[pltpu.VMEM((s, d), ACT), pltpu.VMEM((PAIR * TN // LANE, s, LANE), F32),
                        pltpu.VMEM((PAIR * TN // LANE, s, LANE), F32)],
    )
    return pl.pallas_call(
        _proj_kernel,
        grid_spec=grid_spec,
        out_shape=jax.ShapeDtypeStruct((b, s, P_W), ACT),
        compiler_params=_cparams(2),
        name="proj",
    )(src, fn, x, ln_g, *([w_in] * n_in), *([gain_row] * n_in), e_mat)


def _toeplitz(lo, hi, n):
    x = jnp.concatenate([jnp.broadcast_to(lo, (n, n)), jnp.broadcast_to(hi, (n, n))], axis=1)
    return pltpu.roll(x, 0, 1, stride=1, stride_axis=0)[:, n:]


def _unit_rms_heads(qt):
    halves = []
    for sh in range(2):
        x = qt[HEAD_DIM * sh:HEAD_DIM * (sh + 1), :].astype(F32)
        halves.append(x * lax.rsqrt(jnp.mean(x * x, axis=0, keepdims=True) + EPS))
    return jnp.concatenate(halves, axis=0).astype(qt.dtype)


def _head_rows(qt, lo):
    row = lax.broadcasted_iota(jnp.int32, qt.shape, 0)
    return jnp.where((row >= lo) & (row < lo + HEAD_DIM), qt, jnp.zeros_like(qt))


def _moba_kernel(q_ref, k_ref, v_ref, sg_ref, tv_ref, o_ref,
                 bt_ref, qts_ref, vt_ref, km_ref, sel_ref, m_ref, acc_ref):
    hp = pl.program_id(0)
    b = pl.program_id(1)
    blk = MOBA_BLOCK
    n_blk = k_ref.shape[1] // blk

    def rows_of(i):
        return pl.ds(i * blk if isinstance(i, int) else pl.multiple_of(i * blk, blk), blk)

    @pl.when(b == 0)
    def _():
        kr = lax.broadcasted_iota(jnp.int32, (blk, blk), 0)
        qc = lax.broadcasted_iota(jnp.int32, (blk, blk), 1)
        for sh in range(2):
            def body(o, c, sh=sh):
                t = _toeplitz(tv_ref[2 * hp + sh, o], tv_ref[2 * hp + sh, o + 1], blk)
                bt_ref[sh, o] = jnp.where((qc >= kr) | (o > 0), t, NEG)
                return c
            lax.fori_loop(0, n_blk, body, 0)

    for i in range(n_blk):
        vt = v_ref[0, rows_of(i), :].T
        for sh in range(2):
            vt_ref[i, sh] = jnp.concatenate(
                [vt[HEAD_DIM * sh:HEAD_DIM * (sh + 1), :], jnp.ones((ONES_ROWS, blk), ACT)], axis=0)
        km_ref[i:i + 1, :] = jnp.mean(k_ref[0, rows_of(i), :].astype(F32), axis=0, keepdims=True)
        qt = _unit_rms_heads(q_ref[0, rows_of(i), :].T)
        for sh in range(2):
            qts_ref[i, sh] = _head_rows(qt, HEAD_DIM * sh)

    km = km_ref[...]
    km0 = km.astype(ACT).astype(F32)
    km1 = (km - km0).astype(ACT).astype(F32)
    km2 = km - km0 - km1
    km_terms = jnp.concatenate([km0, km1, km2], axis=0).astype(ACT)
    blk_row = lax.broadcasted_iota(jnp.int32, (n_blk, blk), 0)

    ranked = [(qi, sh) for qi in range(MOBA_TOPK + 1, n_blk) for sh in range(2)]
    gates = [_dot(km_terms, qts_ref[qi, sh]) for qi, sh in ranked]
    for (qi, sh), g3 in zip(ranked, gates):
        g = g3[:n_blk] + g3[n_blk:2 * n_blk] + g3[2 * n_blk:]
        g = jnp.where(blk_row < qi, g, NEG)
        sel_rows = []
        for jb in range(qi):
            gj = g[jb:jb + 1, :]
            beats = (g > gj) | ((g == gj) & (blk_row < jb))
            cnt = jnp.sum(beats.astype(F32), axis=0, keepdims=True)
            sel_rows.append(jnp.where(cnt < MOBA_TOPK, 1.0, 0.0))
        sel_rows.append(jnp.zeros((n_blk - qi, blk), F32))
        sel_ref[qi, sh] = jnp.concatenate(sel_rows, axis=0)

    def tiles(off, qis):
        k_blks = [k_ref[0, rows_of(qi - off), :] for qi in qis]
        scores = [_dot(k_blk, qts_ref[qi, sh]) for qi, k_blk in zip(qis, k_blks) for sh in range(2)]
        work = [(qi, sh) for qi in qis for sh in range(2)]
        stats = []
        half = blk // 2
        for (qi, sh), st in zip(work, scores):
            if off == 0:
                bias = bt_ref[sh, 0]
                parts = [(st[:half, :half] + bias[:half, :half], slice(0, half)),
                         (st[:, half:] + bias[:, half:], slice(0, blk))]
                maxes = [jnp.max(s_part, axis=0, keepdims=True) for s_part, _ in parts]
                m_ref[qi, sh] = jnp.concatenate(maxes, axis=1)
                stats.append((None, [(keys, jnp.exp2(s_part - m_part).astype(ACT))
                                     for (s_part, keys), m_part in zip(parts, maxes)]))
                continue
            st = st + bt_ref[sh, off]
            tile_max = jnp.max(st, axis=0, keepdims=True)
            if qi <= MOBA_TOPK:
                m = m_ref[qi, sh]
                m_new = jnp.maximum(m, tile_max)
                p = jnp.exp2(st - m_new)
                alpha = jnp.exp2(m - m_new)
            else:
                sel = sel_ref[qi, sh, pl.ds(qi - off, 1), :] > 0.5
                m = m_ref[qi, sh]
                m_new = jnp.where(sel, jnp.maximum(m, tile_max), m)
                p = jnp.exp2(st - jnp.where(sel, m_new, -NEG))
                alpha = jnp.exp2(m - m_new)
            m_ref[qi, sh] = m_new
            stats.append((alpha, [(slice(0, blk), p.astype(ACT))]))
        pvs = [jnp.concatenate([_dot(vt_ref[qi - off, sh, :, keys], p) for keys, p in parts], axis=1)
               for (qi, sh), (_, parts) in zip(work, stats)]
        for (qi, sh), (alpha, _), pv in zip(work, stats, pvs):
            acc_ref[qi, sh] = pv if off == 0 else alpha * acc_ref[qi, sh] + pv

    def finish(qi):
        ot = jnp.concatenate([acc_ref[qi, sh, :HEAD_DIM, :] * (1.0 / acc_ref[qi, sh, HEAD_DIM:HEAD_DIM + 1, :])
                              for sh in range(2)], axis=0)
        o_ref[0, rows_of(qi), :] = (ot.T * sg_ref[0, rows_of(qi), :].astype(F32)).astype(o_ref.dtype)

    for off in range(n_blk):
        qis = list(range(off, n_blk))
        for lo in range(0, len(qis), MOBA_GROUP):
            tiles(off, qis[lo:lo + MOBA_GROUP])
        finish(off)


def _moba(p, tv_a):
    b, s, _ = p.shape
    blk = MOBA_BLOCK
    n_blk = s // blk
    col = lambda c: pl.BlockSpec((1, s, LANE), lambda hp, bi: (bi, 0, COL_A + c * N_PAIRS + hp))
    return pl.pallas_call(
        _moba_kernel,
        grid=(N_PAIRS, b),
        in_specs=[col(0), col(1), col(2), col(3), pl.BlockSpec(tv_a.shape, lambda hp, bi: (0, 0, 0, 0))],
        out_specs=pl.BlockSpec((1, s, LANE), lambda hp, bi: (bi, 0, hp)),
        out_shape=jax.ShapeDtypeStruct((b, s, W_BR), ACT),
        scratch_shapes=[
            pltpu.VMEM((2, n_blk, blk, blk), F32),
            pltpu.VMEM((n_blk, 2, LANE, blk), ACT),
            pltpu.VMEM((n_blk, 2, HEAD_DIM + ONES_ROWS, blk), ACT),
            pltpu.VMEM((n_blk, LANE), F32),
            pltpu.VMEM((n_blk, 2, n_blk, blk), F32),
            pltpu.VMEM((n_blk, 2, 1, blk), F32),
            pltpu.VMEM((n_blk, 2, HEAD_DIM + ONES_ROWS, blk), F32),
        ],
        compiler_params=_cparams(2),
        name="moba",
    )(p, p, p, p, tv_a)


def _band_attend(items):
    def fuse(blocks):
        if len(blocks) > 1 and all(valid is None for _, _, _, valid in blocks):
            ks, vts, biases, _ = zip(*blocks)
            return [(jnp.concatenate(ks, axis=0), jnp.concatenate(vts, axis=1), jnp.concatenate(biases, axis=0), None)]
        return blocks

    items = [(qs, fuse(blocks), sink) for qs, blocks, sink in items]

    probs = []
    for qs, blocks, sink in items:
        sc = [_dot(k, qs) for k, _, _, _ in blocks]
        ss = [s + bias if valid is None else jnp.where(valid, s + bias, NEG)
              for s, (_, _, bias, valid) in zip(sc, blocks)]
        m = functools.reduce(jnp.maximum, [jnp.max(s, axis=0, keepdims=True) for s in ss])
        if sink is not None:
            m = jnp.maximum(m, sink)
        probs.append((m, [jnp.exp2(s - m).astype(ACT) for s in ss]))
    outs = []
    for (_, blocks, sink), (m, ps) in zip(items, probs):
        n_v = blocks[0][1].shape[0]
        acc = functools.reduce(operator.add, [
            _dot(jnp.concatenate([vt, jnp.ones((ONES_ROWS, vt.shape[1]), ACT)], axis=0), p)
            for (_, vt, _, _), p in zip(blocks, ps)])
        l = acc[n_v:n_v + 1, :]
        if sink is not None:
            l = l + jnp.exp2(sink - m)
        outs.append((acc[:n_v, :] * (1.0 / l), m + jnp.log2(l)))
    return outs


def _band_bias(tv_ref, head, prev_includes_window_edge):
    kr = lax.broadcasted_iota(jnp.int32, (BAND, BAND), 0)
    qc = lax.broadcasted_iota(jnp.int32, (BAND, BAND), 1)
    own, prev = [_toeplitz(tv_ref[head, which], tv_ref[head, which + 1], BAND) for which in range(2)]
    prev_mask = (qc <= kr) if prev_includes_window_edge else (qc < kr)
    return jnp.where(qc >= kr, own, NEG), jnp.where(prev_mask, prev, NEG)


def _dil_kernel(q0_ref, k0_ref, v0_ref, q1_ref, k1_ref, v1_ref, q2_ref, k2_ref, v2_ref, sg_ref, tv_ref,
                o_ref, og_ref, lg_ref, bt_ref):
    hp = pl.program_id(0)
    b = pl.program_id(1)
    s = q0_ref.shape[1]
    n_tiles = s // BAND

    @pl.when(b == 0)
    def _():
        for g in range(N_DIL):
            for sh in range(2):
                own, prev = _band_bias(tv_ref, g * N_HEADS + 2 * hp + sh, True)
                bt_ref[g, sh, 0] = own
                bt_ref[g, sh, 1] = prev

    groups = ((q0_ref, k0_ref, v0_ref), (q1_ref, k1_ref, v1_ref), (q2_ref, k2_ref, v2_ref))

    def mix(rows, o0, l0):
        l1, l2 = lg_ref[0, rows, :], lg_ref[1, rows, :]
        m = jnp.maximum(l0, jnp.maximum(l1, l2))
        e0, e1, e2 = jnp.exp2(l0 - m), jnp.exp2(l1 - m), jnp.exp2(l2 - m)
        ob = (e0 * o0 + e1 * og_ref[0, rows, :] + e2 * og_ref[1, rows, :]) / (e0 + e1 + e2)
        o_ref[0, rows, :] = (ob * sg_ref[0, rows, :].astype(F32)).astype(o_ref.dtype)

    for g in (*range(1, N_DIL), 0):
        q_ref, k_ref, v_ref = groups[g]
        d = DIL_PAIRS[g][1]
        assert (d == 1) == (g == 0)
        per_phase = n_tiles // d
        has_prev = per_phase > 1
        assert per_phase % TILE_GROUP == 0 or TILE_GROUP % per_phase == 0

        def body(i, c, g=g, d=d, per_phase=per_phase, has_prev=has_prev, q_ref=q_ref, k_ref=k_ref, v_ref=v_ref):
            items, dsts = [], []
            k_own = vt_own = None
            for n in range(TILE_GROUP):
                u = i * TILE_GROUP + n
                rows = pl.ds(pl.multiple_of(u * BAND, BAND), BAND)
                t = lax.rem(u, per_phase)
                r = lax.div(u, per_phase)
                qt = _unit_rms_heads(q_ref[0, rows, :].T)
                k_prev, vt_prev = k_own, vt_own
                k_own = k_ref[0, rows, :]
                vt_own = v_ref[0, rows, :].T
                if has_prev:
                    phase_start_known = TILE_GROUP % per_phase == 0
                    prev_exists = (t > 0) if (n == 0 and not phase_start_known) else None
                    if phase_start_known and n % per_phase == 0:
                        k_prev = None
                    elif n == 0:
                        prow = pl.ds(pl.multiple_of(jnp.maximum(u - 1, 0) * BAND, BAND), BAND)
                        k_prev = k_ref[0, prow, :]
                        vt_prev = v_ref[0, prow, :].T
                for sh in range(2):
                    hrows = slice(HEAD_DIM * sh, HEAD_DIM * (sh + 1))
                    blocks = [(k_own, vt_own[hrows, :], bt_ref[g, sh, 0], None)]
                    if has_prev and k_prev is not None:
                        blocks.append((k_prev, vt_prev[hrows, :], bt_ref[g, sh, 1], prev_exists))
                    items.append((_head_rows(qt, HEAD_DIM * sh), blocks, None))
                dsts.append(rows if d == 1 else pl.ds(t * (BAND * d) + r, BAND, stride=d))
            res = _band_attend(items)
            for n, dst in enumerate(dsts):
                (o0, lse0), (o1, lse1) = res[2 * n], res[2 * n + 1]
                out = jnp.concatenate([o0, o1], axis=0).T
                lse = jnp.concatenate(
                    [jnp.broadcast_to(lse0, (HEAD_DIM, BAND)), jnp.broadcast_to(lse1, (HEAD_DIM, BAND))], axis=0).T
                if g == 0:
                    mix(dst, out, lse)
                else:
                    og_ref.at[g - 1][dst, :] = out
                    lg_ref.at[g - 1][dst, :] = lse
            return c

        lax.fori_loop(0, n_tiles // TILE_GROUP, body, 0)


def _dilated(p, tv_b):
    b, s, _ = p.shape
    col = lambda c: pl.BlockSpec((1, s, LANE), lambda hp, bi: (bi, 0, c + hp))
    qkv = [col(COL_B + 3 * N_PAIRS * g + N_PAIRS * i) for g in range(N_DIL) for i in range(3)]
    return pl.pallas_call(
        _dil_kernel,
        grid=(N_PAIRS, b),
        in_specs=qkv + [col(COL_SGB), pl.BlockSpec(tv_b.shape, lambda hp, bi: (0, 0, 0, 0))],
        out_specs=pl.BlockSpec((1, s, LANE), lambda hp, bi: (bi, 0, hp)),
        out_shape=jax.ShapeDtypeStruct((b, s, W_BR), ACT),
        scratch_shapes=[
            pltpu.VMEM((N_DIL - 1, s, LANE), F32),
            pltpu.VMEM((N_DIL - 1, s, LANE), F32),
            pltpu.VMEM((N_DIL, 2, 2, BAND, BAND), F32),
        ],
        compiler_params=_cparams(2),
        name="dilated",
    )(*([p] * 10), tv_b)


def _swa_kernel(sink_ref, q_ref, sg_ref, k_ref, v_ref, tv_ref, o_ref, bt_ref, vts_ref):
    hp = pl.program_id(0)
    b = pl.program_id(1)
    n_tiles = q_ref.shape[1] // BAND
    pairs_per_kv = N_PAIRS // KV_C

    @pl.when(b == 0)
    def _():
        for sh in range(2):
            own, prev = _band_bias(tv_ref, 2 * hp + sh, False)
            bt_ref[sh, 0] = own
            bt_ref[sh, 1] = prev

    kv_lo = HEAD_DIM * (hp // pairs_per_kv)
    row = lax.broadcasted_iota(jnp.int32, (LANE, BAND), 0)
    kv_rows = (row >= kv_lo) & (row < kv_lo + HEAD_DIM)

    def kv_vt(rows, slot):
        vts_ref[slot] = v_ref[0, rows, :].T
        return vts_ref[slot, pl.ds(pl.multiple_of(kv_lo, HEAD_DIM), HEAD_DIM), :]

    def body(i, c):
        items, dsts = [], []
        zero = jnp.zeros((LANE, BAND), ACT)
        k_own = vt_own = None
        for n in range(TILE_GROUP):
            t = i * TILE_GROUP + n
            rows = pl.ds(pl.multiple_of(t * BAND, BAND), BAND)
            qt = _unit_rms_heads(q_ref[0, rows, :].T)
            k_prev, vt_prev = k_own, vt_own
            k_own = k_ref[0, rows, :]
            vt_own = kv_vt(rows, n)
            if n == 0:
                prow = pl.ds(pl.multiple_of(jnp.maximum(t - 1, 0) * BAND, BAND), BAND)
                k_prev = k_ref[0, prow, :]
                vt_prev = kv_vt(prow, TILE_GROUP)
            prev_exists = (t > 0) if n == 0 else None
            for sh in range(2):
                qh = qt[HEAD_DIM * sh:HEAD_DIM * (sh + 1), :]
                qs = jnp.where(kv_rows, jnp.concatenate([qh, qh], axis=0), zero)
                items.append((qs, [(k_own, vt_own, bt_ref[sh, 0], None), (k_prev, vt_prev, bt_ref[sh, 1], prev_exists)],
                              sink_ref[2 * hp + sh]))
            dsts.append(rows)
        res = _band_attend(items)
        for n, rows in enumerate(dsts):
            ot = jnp.concatenate([o for o, _ in res[2 * n:2 * n + 2]], axis=0).T
            o_ref[0, rows, :] = (ot * sg_ref[0, rows, :].astype(F32)).astype(o_ref.dtype)
        return c

    lax.fori_loop(0, n_tiles // TILE_GROUP, body, 0)


def _swa(p, tv_c, sinks):
    b, s, _ = p.shape
    grid_spec = pltpu.PrefetchScalarGridSpec(
        num_scalar_prefetch=1,
        grid=(N_PAIRS, b),
        in_specs=[
            pl.BlockSpec((1, s, LANE), lambda hp, bi, sk: (bi, 0, COL_C + hp)),
            pl.BlockSpec((1, s, LANE), lambda hp, bi, sk: (bi, 0, COL_C + N_PAIRS + hp)),
            pl.BlockSpec((1, s, LANE), lambda hp, bi, sk: (bi, 0, COL_C + 2 * N_PAIRS)),
            pl.BlockSpec((1, s, LANE), lambda hp, bi, sk: (bi, 0, COL_C + 2 * N_PAIRS + 1)),
            pl.BlockSpec(tv_c.shape, lambda hp, bi, sk: (0, 0, 0, 0)),
        ],
        out_specs=pl.BlockSpec((1, s, LANE), lambda hp, bi, sk: (bi, 0, hp)),
        scratch_shapes=[pltpu.VMEM((2, 2, BAND, BAND), F32),
                        pltpu.VMEM((TILE_GROUP + 1, LANE, BAND), ACT)],
    )
    return pl.pallas_call(
        _swa_kernel,
        grid_spec=grid_spec,
        out_shape=jax.ShapeDtypeStruct((b, s, W_BR), ACT),
        compiler_params=_cparams(2),
        name="swa",
    )(sinks, p, p, p, p, tv_c)


def _merge_kernel(bra_ref, brb_ref, brc_ref, g_ref, wb_ref, wo_ref, x_ref, out_ref):
    merged = None
    for i, br_ref in enumerate((bra_ref, brb_ref, brc_ref)):
        term = g_ref[:, i * D_MODEL:(i + 1) * D_MODEL].astype(F32) * _dot(br_ref[...], wb_ref[i])
        merged = term if merged is None else merged + term
    out_ref[...] = x_ref[...] + _dot(merged.astype(ACT), wo_ref[...])


def _merge(bra, brb, brc, p2, w_br, w_out, x2, layer, ts=512):
    n = x2.shape[0]
    row = lambda w: pl.BlockSpec((ts, w), lambda i: (i, 0))
    return pl.pallas_call(
        _merge_kernel,
        grid=(n // ts,),
        in_specs=[row(W_BR), row(W_BR), row(W_BR), row(GATE_W),
                  pl.BlockSpec((None,) + w_br.shape[1:], lambda i: (layer, 0, 0, 0)),
                  pl.BlockSpec((None,) + w_out.shape[1:], lambda i: (layer, 0, 0)),
                  row(D_MODEL)],
        out_specs=row(D_MODEL),
        out_shape=jax.ShapeDtypeStruct(x2.shape, F32),
        compiler_params=_cparams(1),
        name="merge",
    )(bra, brb, brc, p2, w_br, w_out, x2)


def _rel_bucket(dist):
    dist = jnp.maximum(dist, 0)
    max_exact = N_BUCKETS // 2
    log_ratio = jnp.log(jnp.maximum(dist, 1).astype(F32) / max_exact) / math.log(REL_MAX_DIST / max_exact)
    large = jnp.minimum(max_exact + (log_ratio * (N_BUCKETS - max_exact)).astype(jnp.int32), N_BUCKETS - 1)
    return jnp.where(dist < max_exact, dist, large)


def _bias_by_distance(tab, dist, chunk):
    vals = tab.astype(F32)[:, _rel_bucket(dist)] * LOG2E
    vals = jnp.pad(vals, ((0, 0), (chunk, 0)))
    return vals.reshape(tab.shape[0], -1, 1, chunk)


def _gain_rows(qk_g):
    depth = qk_g.shape[0]
    scale = HEAD_DIM ** -0.5 * LOG2E
    one = jnp.ones((depth, HEAD_DIM), F32)
    tile = lambda v, n: jnp.tile(v, (1, n))
    parts = [tile(one, N_HEADS), tile(qk_g[:, 1] * qk_g[:, 0] * scale, N_HEADS), tile(one, 2 * N_HEADS),
             tile(one, N_DIL * N_HEADS), tile(qk_g[:, 3] * qk_g[:, 2] * scale, N_DIL * N_HEADS),
             tile(one, (N_DIL + 1) * N_HEADS),
             tile(one, N_HEADS), tile(qk_g[:, 5] * qk_g[:, 4] * scale, KV_C), tile(one, KV_C + N_HEADS),
             tile(one, GATE_W // HEAD_DIM)]
    return jnp.concatenate(parts, axis=1)


def kernel(x, ln_g, w_in, qk_g, sinks, w_branch, w_out, rel_bias):
    b, s, d = x.shape
    depth = ln_g.shape[0]
    assert d == D_MODEL and s % DIL_PAIRS[-1][0] == 0 and w_in.shape[-1] == C_IN

    tv_a = _bias_by_distance(rel_bias[:N_HEADS], jnp.arange(s), MOBA_BLOCK)
    tv_b = jnp.concatenate(
        [_bias_by_distance(rel_bias[OFF_B + g * N_HEADS:OFF_B + (g + 1) * N_HEADS], dd * jnp.arange(2 * BAND), BAND)
         for g, (_, dd) in enumerate(DIL_PAIRS)], axis=0)
    tv_c = _bias_by_distance(rel_bias[OFF_C:OFF_C + N_HEADS], jnp.arange(2 * BAND), BAND)
    seg = jnp.arange(TN) // HEAD_DIM
    e_mat = (seg[:, None] == seg[None, :]).astype(ACT)
    src = jnp.asarray([c for c, _ in _LAYOUT], jnp.int32)
    fn = jnp.asarray([f for _, f in _LAYOUT[::PAIR]], jnp.int32)
    gains = _gain_rows(qk_g)

    w_in_b = w_in.astype(ACT)
    w_br_b = w_branch.astype(ACT)
    w_out_b = w_out.astype(ACT)

    for l in range(depth):
        p = _project(x, ln_g[:, None, :], w_in_b, gains[l][None, :], e_mat, src, fn, l)
        bra = _moba(p, tv_a)
        brb = _dilated(p, tv_b)
        brc = _swa(p, tv_c, sinks[l] * LOG2E)
        x2 = _merge(bra.reshape(b * s, W_BR), brb.reshape(b * s, W_BR), brc.reshape(b * s, W_BR),
                    p.reshape(b * s, P_W), w_br_b, w_out_b, x.reshape(b * s, d), l)
        x = x2.reshape(b, s, d)
    return x
```

```python
import functools
import math
import operator

import jax
import jax.numpy as jnp
from jax import lax
from jax.experimental import pallas as pl
from jax.experimental.pallas import tpu as pltpu

D_MODEL = 1024
HEAD_DIM = 64
W_BR = D_MODEL // 2
N_HEADS = W_BR // HEAD_DIM
N_PAIRS = N_HEADS // 2
MOBA_BLOCK = 256
MOBA_TOPK = 3
DIL_PAIRS = ((128, 1), (512, 4), (2048, 16))
N_DIL = len(DIL_PAIRS)
KV_C = 2
SWA_WINDOW = 128
N_BUCKETS = 32
REL_MAX_DIST = 2048
OFF_B = N_HEADS
OFF_C = N_HEADS + N_DIL * N_HEADS
EPS = 1e-6
NEG = -1e30
LOG2E = math.log2(math.e)

LANE = 128
BAND = SWA_WINDOW
assert all(w == BAND * d for w, d in DIL_PAIRS)
TN = 256
ROWS = 256
TILE_GROUP = 16
MOBA_GROUP = 8
EPILOGUE_LAG = 2
SINGLE_OP_STRIDE = 4
ONES_ROWS = 16
VMEM_LIMIT = 56 * 1024 * 1024

ACT = jnp.bfloat16
F32 = jnp.float32

PLAIN, NORM, SILU, SIGMOID, KNORM_VPLAIN, NORM_D4, PLAIN_D4, NORM_D16, PLAIN_D16 = range(9)
_B_GROUP = lambda g, norm, plain: [(8 + 2 * g, plain), (9 + 2 * g, plain), (14 + 2 * g, norm), (15 + 2 * g, norm),
                                   (20 + 2 * g, plain), (21 + 2 * g, plain)]
_LAYOUT = ([(j, SIGMOID) for j in range(33, 45)]
           + [(0, PLAIN), (1, PLAIN), (2, NORM), (3, NORM), (4, PLAIN), (5, PLAIN), (6, SILU), (7, SILU)]
           + _B_GROUP(0, NORM, PLAIN) + _B_GROUP(1, NORM_D4, PLAIN_D4) + _B_GROUP(2, NORM_D16, PLAIN_D16)
           + [(26, SILU), (27, SILU)]
           + [(28, PLAIN), (29, PLAIN), (31, SILU), (32, SILU), (30, KNORM_VPLAIN)])
C_IN = len(_LAYOUT) * TN
PAIR = 2
_LAYOUT = _LAYOUT + [(_LAYOUT[-1][0], KNORM_VPLAIN)] * (-len(_LAYOUT) % PAIR)
assert all(_LAYOUT[j][1] == _LAYOUT[j - j % PAIR][1] for j in range(len(_LAYOUT)))
assert [f for _, f in _LAYOUT].count(KNORM_VPLAIN) == PAIR
N_STEPS = len(_LAYOUT) // PAIR
P_W = len(_LAYOUT) * TN
COL_GATE, COL_A, COL_B, COL_SGB, COL_C = 0, 24, 40, 76, 80
GATE_W = 3 * D_MODEL
assert COL_GATE == 0


def _cparams(n_axes):
    return pltpu.CompilerParams(dimension_semantics=("arbitrary",) * n_axes, vmem_limit_bytes=VMEM_LIMIT)


def _dot(a, b):
    return jnp.dot(a, b, preferred_element_type=F32)


def _sigmoid(a):
    return 0.5 * jnp.tanh(0.5 * a) + 0.5


def _proj_kernel(src_ref, fn_ref, x_ref, lng_ref, *refs):
    del src_ref
    w_refs, gain_refs = refs[:PAIR], refs[PAIR:2 * PAIR]
    e_ref, o_ref, xn_ref, acc_ref, acc2_ref = refs[2 * PAIR:]
    j = pl.program_id(1)
    s = x_ref.shape[1]
    n_chunks = s // ROWS
    lanes_per_block = TN // LANE

    @pl.when(j == 0)
    def _():
        def body(i, c):
            rows = pl.ds(pl.multiple_of(i * ROWS, ROWS), ROWS)
            xv = x_ref[0, rows, :]
            ms = jnp.mean(xv * xv, axis=-1, keepdims=True)
            xn_ref[rows, :] = (xv * lax.rsqrt(ms + EPS) * lng_ref[...]).astype(ACT)
            return c
        lax.fori_loop(0, n_chunks, body, 0)

    def mean_square(a):
        return _dot((a * a).astype(ACT), e_ref[...])

    def head_norm(a, ss, gain):
        return a * lax.rsqrt(ss * (1.0 / HEAD_DIM) + EPS) * gain

    def k_norm_v_plain(a, ss, gain):
        lane = lax.broadcasted_iota(jnp.int32, a.shape, 1)
        return jnp.where(lane < KV_C * HEAD_DIM, head_norm(a, ss, gain), a)

    plain = (None, lambda a, _, gain: a)
    silu = (None, lambda a, _, gain: a * _sigmoid(a))
    sigmoid = (None, lambda a, _, gain: _sigmoid(a))
    norm = (mean_square, head_norm)
    knorm_vplain = (mean_square, k_norm_v_plain)

    def pipeline(n_items, per_block, produce, epilogue, store):
        pre, post = epilogue
        tiles = {}
        for n in range(n_items + EPILOGUE_LAG):
            if n < n_items:
                tiles[n] = produce(n)
            if n >= EPILOGUE_LAG:
                m = n - EPILOGUE_LAG
                tile = tiles.pop(m)
                store(m, post(tile, pre(tile) if pre is not None else None, gain_refs[m // per_block][...]))

    def phase_rows(src_ref, slab, t, d):
        r, m0 = divmod(t * BAND, s // d)
        return src_ref.at[slab][pl.ds(m0 * d + r, BAND, stride=d), :]

    fn = fn_ref[j]

    def chunk(n):
        h, i = divmod(n, n_chunks)
        return _dot(xn_ref[i * ROWS:(i + 1) * ROWS, :], w_refs[h][...])

    def out_cols(h):
        return slice(h * TN, (h + 1) * TN)

    def direct(code, epilogue, live=PAIR):
        @pl.when(fn == code)
        def _():
            def store(n, val):
                h, i = divmod(n, n_chunks)
                o_ref[0, i * ROWS:(i + 1) * ROWS, out_cols(h)] = val.astype(o_ref.dtype)

            pipeline(live * n_chunks, n_chunks, chunk, epilogue, store)
            if live < PAIR:
                o_ref[0, :, live * TN:] = jnp.zeros((s, (PAIR - live) * TN), o_ref.dtype)

    def phase_major(code, epilogue, d):
        @pl.when(fn == code)
        def _():
            def stage(n, val):
                h, i = divmod(n, n_chunks)
                for c in range(lanes_per_block):
                    acc_ref[h * lanes_per_block + c, i * ROWS:(i + 1) * ROWS, :] = val[:, c * LANE:(c + 1) * LANE]

            pipeline(PAIR * n_chunks, n_chunks, chunk, epilogue, stage)
            src_ref, stride = acc_ref, d
            if d > SINGLE_OP_STRIDE:
                assert d == SINGLE_OP_STRIDE * SINGLE_OP_STRIDE
                for slab in range(PAIR * lanes_per_block):
                    for t in range(s // BAND):
                        acc2_ref[slab, t * BAND:(t + 1) * BAND, :] = phase_rows(acc_ref, slab, t, SINGLE_OP_STRIDE)
                src_ref, stride = acc2_ref, SINGLE_OP_STRIDE
            for h in range(PAIR):
                for t in range(s // BAND):
                    a = jnp.concatenate([phase_rows(src_ref, h * lanes_per_block + c, t, stride)
                                         for c in range(lanes_per_block)], axis=1)
                    o_ref[0, t * BAND:(t + 1) * BAND, out_cols(h)] = a.astype(o_ref.dtype)

    direct(PLAIN, plain)
    direct(NORM, norm)
    direct(SILU, silu)
    direct(SIGMOID, sigmoid)
    direct(KNORM_VPLAIN, knorm_vplain, live=1)
    phase_major(NORM_D4, norm, DIL_PAIRS[1][1])
    phase_major(PLAIN_D4, plain, DIL_PAIRS[1][1])
    phase_major(NORM_D16, norm, DIL_PAIRS[2][1])
    phase_major(PLAIN_D16, plain, DIL_PAIRS[2][1])


def _w_index(bi, j, src_ref, fn_ref, *, layer, slot):
    return layer, 0, src_ref[PAIR * j + slot]


def _gain_index(bi, j, src_ref, fn_ref, *, slot):
    return 0, src_ref[PAIR * j + slot]


def _project(x, ln_g, w_in, gain_row, e_mat, src, fn, layer):
    b, s, d = x.shape
    n_in = PAIR
    grid_spec = pltpu.PrefetchScalarGridSpec(
        num_scalar_prefetch=2,
        grid=(b, N_STEPS),
        in_specs=[
            pl.BlockSpec((1, s, d), lambda bi, j, src_ref, fn_ref: (bi, 0, 0)),
            pl.BlockSpec((None, 1, d), lambda bi, j, src_ref, fn_ref: (layer, 0, 0)),
        ] + [pl.BlockSpec((None, d, TN), functools.partial(_w_index, layer=layer, slot=n)) for n in range(n_in)]
          + [pl.BlockSpec((1, TN), functools.partial(_gain_index, slot=n)) for n in range(n_in)]
          + [pl.BlockSpec((TN, TN), lambda bi, j, src_ref, fn_ref: (0, 0))],
        out_specs=pl.BlockSpec((1, s, n_in * TN), lambda bi, j, src_ref, fn_ref: (bi, 0, j)),
        scratch_shapes=[pltpu.VMEM((s, d), ACT), pltpu.VMEM((PAIR * TN // LANE, s, LANE), F32),
                        pltpu.VMEM((PAIR * TN // LANE, s, LANE), F32)],
    )
    return pl.pallas_call(
        _proj_kernel,
        grid_spec=grid_spec,
        out_shape=jax.ShapeDtypeStruct((b, s, P_W), ACT),
        compiler_params=_cparams(2),
        name="proj",
    )(src, fn, x, ln_g, *([w_in] * n_in), *([gain_row] * n_in), e_mat)


def _toeplitz(lo, hi, n):
    x = jnp.concatenate([jnp.broadcast_to(lo, (n, n)), jnp.broadcast_to(hi, (n, n))], axis=1)
    return pltpu.roll(x, 0, 1, stride=1, stride_axis=0)[:, n:]


def _unit_rms_heads(qt):
    halves = []
    for sh in range(2):
        x = qt[HEAD_DIM * sh:HEAD_DIM * (sh + 1), :].astype(F32)
        halves.append(x * lax.rsqrt(jnp.mean(x * x, axis=0, keepdims=True) + EPS))
    return jnp.concatenate(halves, axis=0).astype(qt.dtype)


def _head_rows(qt, lo):
    row = lax.broadcasted_iota(jnp.int32, qt.shape, 0)
    return jnp.where((row >= lo) & (row < lo + HEAD_DIM), qt, jnp.zeros_like(qt))


def _moba_kernel(q_ref, k_ref, v_ref, sg_ref, tv_ref, o_ref,
                 bt_ref, qts_ref, vt_ref, km_ref, sel_ref, m_ref, acc_ref):
    hp = pl.program_id(0)
    b = pl.program_id(1)
    blk = MOBA_BLOCK
    n_blk = k_ref.shape[1] // blk

    def rows_of(i):
        return pl.ds(i * blk if isinstance(i, int) else pl.multiple_of(i * blk, blk), blk)

    @pl.when(b == 0)
    def _():
        kr = lax.broadcasted_iota(jnp.int32, (blk, blk), 0)
        qc = lax.broadcasted_iota(jnp.int32, (blk, blk), 1)
        for sh in range(2):
            def body(o, c, sh=sh):
                t = _toeplitz(tv_ref[2 * hp + sh, o], tv_ref[2 * hp + sh, o + 1], blk)
                bt_ref[sh, o] = jnp.where((qc >= kr) | (o > 0), t, NEG)
                return c
            lax.fori_loop(0, n_blk, body, 0)

    for i in range(n_blk):
        vt = v_ref[0, rows_of(i), :].T
        for sh in range(2):
            vt_ref[i, sh] = jnp.concatenate(
                [vt[HEAD_DIM * sh:HEAD_DIM * (sh + 1), :], jnp.ones((ONES_ROWS, blk), ACT)], axis=0)
        km_ref[i:i + 1, :] = jnp.mean(k_ref[0, rows_of(i), :].astype(F32), axis=0, keepdims=True)
        qt = _unit_rms_heads(q_ref[0, rows_of(i), :].T)
        for sh in range(2):
            qts_ref[i, sh] = _head_rows(qt, HEAD_DIM * sh)

    km = km_ref[...]
    km0 = km.astype(ACT).astype(F32)
    km1 = (km - km0).astype(ACT).astype(F32)
    km2 = km - km0 - km1
    km_terms = jnp.concatenate([km0, km1, km2], axis=0).astype(ACT)
    blk_row = lax.broadcasted_iota(jnp.int32, (n_blk, blk), 0)

    ranked = [(qi, sh) for qi in range(MOBA_TOPK + 1, n_blk) for sh in range(2)]
    gates = [_dot(km_terms, qts_ref[qi, sh]) for qi, sh in ranked]
    for (qi, sh), g3 in zip(ranked, gates):
        g = g3[:n_blk] + g3[n_blk:2 * n_blk] + g3[2 * n_blk:]
        g = jnp.where(blk_row < qi, g, NEG)
        sel_rows = []
        for jb in range(qi):
            gj = g[jb:jb + 1, :]
            beats = (g > gj) | ((g == gj) & (blk_row < jb))
            cnt = jnp.sum(beats.astype(F32), axis=0, keepdims=True)
            sel_rows.append(jnp.where(cnt < MOBA_TOPK, 1.0, 0.0))
        sel_rows.append(jnp.zeros((n_blk - qi, blk), F32))
        sel_ref[qi, sh] = jnp.concatenate(sel_rows, axis=0)

    def tiles(off, qis):
        k_blks = [k_ref[0, rows_of(qi - off), :] for qi in qis]
        scores = [_dot(k_blk, qts_ref[qi, sh]) for qi, k_blk in zip(qis, k_blks) for sh in range(2)]
        work = [(qi, sh) for qi in qis for sh in range(2)]
        stats = []
        half = blk // 2
        for (qi, sh), st in zip(work, scores):
            if off == 0:
                bias = bt_ref[sh, 0]
                parts = [(st[:half, :half] + bias[:half, :half], slice(0, half)),
                         (st[:, half:] + bias[:, half:], slice(0, blk))]
                maxes = [jnp.max(s_part, axis=0, keepdims=True) for s_part, _ in parts]
                m_ref[qi, sh] = jnp.concatenate(maxes, axis=1)
                stats.append((None, [(keys, jnp.exp2(s_part - m_part).astype(ACT))
                                     for (s_part, keys), m_part in zip(parts, maxes)]))
                continue
            st = st + bt_ref[sh, off]
            tile_max = jnp.max(st, axis=0, keepdims=True)
            if qi <= MOBA_TOPK:
                m = m_ref[qi, sh]
                m_new = jnp.maximum(m, tile_max)
                p = jnp.exp2(st - m_new)
                alpha = jnp.exp2(m - m_new)
            else:
                sel = sel_ref[qi, sh, pl.ds(qi - off, 1), :] > 0.5
                m = m_ref[qi, sh]
                m_new = jnp.where(sel, jnp.maximum(m, tile_max), m)
                p = jnp.exp2(st - jnp.where(sel, m_new, -NEG))
                alpha = jnp.exp2(m - m_new)
            m_ref[qi, sh] = m_new
            stats.append((alpha, [(slice(0, blk), p.astype(ACT))]))
        pvs = [jnp.concatenate([_dot(vt_ref[qi - off, sh, :, keys], p) for keys, p in parts], axis=1)
               for (qi, sh), (_, parts) in zip(work, stats)]
        for (qi, sh), (alpha, _), pv in zip(work, stats, pvs):
            acc_ref[qi, sh] = pv if off == 0 else alpha * acc_ref[qi, sh] + pv

    def finish(qi):
        ot = jnp.concatenate([acc_ref[qi, sh, :HEAD_DIM, :] * (1.0 / acc_ref[qi, sh, HEAD_DIM:HEAD_DIM + 1, :])
                              for sh in range(2)], axis=0)
        o_ref[0, rows_of(qi), :] = (ot.T * sg_ref[0, rows_of(qi), :].astype(F32)).astype(o_ref.dtype)

    for off in range(n_blk):
        qis = list(range(off, n_blk))
        for lo in range(0, len(qis), MOBA_GROUP):
            tiles(off, qis[lo:lo + MOBA_GROUP])
        finish(off)


def _moba(p, tv_a):
    b, s, _ = p.shape
    blk = MOBA_BLOCK
    n_blk = s // blk
    col = lambda c: pl.BlockSpec((1, s, LANE), lambda hp, bi: (bi, 0, COL_A + c * N_PAIRS + hp))
    return pl.pallas_call(
        _moba_kernel,
        grid=(N_PAIRS, b),
        in_specs=[col(0), col(1), col(2), col(3), pl.BlockSpec(tv_a.shape, lambda hp, bi: (0, 0, 0, 0))],
        out_specs=pl.BlockSpec((1, s, LANE), lambda hp, bi: (bi, 0, hp)),
        out_shape=jax.ShapeDtypeStruct((b, s, W_BR), ACT),
        scratch_shapes=[
            pltpu.VMEM((2, n_blk, blk, blk), F32),
            pltpu.VMEM((n_blk, 2, LANE, blk), ACT),
            pltpu.VMEM((n_blk, 2, HEAD_DIM + ONES_ROWS, blk), ACT),
            pltpu.VMEM((n_blk, LANE), F32),
            pltpu.VMEM((n_blk, 2, n_blk, blk), F32),
            pltpu.VMEM((n_blk, 2, 1, blk), F32),
            pltpu.VMEM((n_blk, 2, HEAD_DIM + ONES_ROWS, blk), F32),
        ],
        compiler_params=_cparams(2),
        name="moba",
    )(p, p, p, p, tv_a)


def _band_attend(items):
    def fuse(blocks):
        if len(blocks) > 1 and all(valid is None for _, _, _, valid in blocks):
            ks, vts, biases, _ = zip(*blocks)
            return [(jnp.concatenate(ks, axis=0), jnp.concatenate(vts, axis=1), jnp.concatenate(biases, axis=0), None)]
        return blocks

    items = [(qs, fuse(blocks), sink) for qs, blocks, sink in items]

    probs = []
    for qs, blocks, sink in items:
        sc = [_dot(k, qs) for k, _, _, _ in blocks]
        ss = [s + bias if valid is None else jnp.where(valid, s + bias, NEG)
              for s, (_, _, bias, valid) in zip(sc, blocks)]
        m = functools.reduce(jnp.maximum, [jnp.max(s, axis=0, keepdims=True) for s in ss])
        if sink is not None:
            m = jnp.maximum(m, sink)
        probs.append((m, [jnp.exp2(s - m).astype(ACT) for s in ss]))
    outs = []
    for (_, blocks, sink), (m, ps) in zip(items, probs):
        n_v = blocks[0][1].shape[0]
        acc = functools.reduce(operator.add, [
            _dot(jnp.concatenate([vt, jnp.ones((ONES_ROWS, vt.shape[1]), ACT)], axis=0), p)
            for (_, vt, _, _), p in zip(blocks, ps)])
        l = acc[n_v:n_v + 1, :]
        if sink is not None:
            l = l + jnp.exp2(sink - m)
        outs.append((acc[:n_v, :] * (1.0 / l), m + jnp.log2(l)))
    return outs


def _band_bias(tv_ref, head, prev_includes_window_edge):
    kr = lax.broadcasted_iota(jnp.int32, (BAND, BAND), 0)
    qc = lax.broadcasted_iota(jnp.int32, (BAND, BAND), 1)
    own, prev = [_toeplitz(tv_ref[head, which], tv_ref[head, which + 1], BAND) for which in range(2)]
    prev_mask = (qc <= kr) if prev_includes_window_edge else (qc < kr)
    return jnp.where(qc >= kr, own, NEG), jnp.where(prev_mask, prev, NEG)


def _dil_kernel(q0_ref, k0_ref, v0_ref, q1_ref, k1_ref, v1_ref, q2_ref, k2_ref, v2_ref, sg_ref, tv_ref,
                o_ref, og_ref, lg_ref, bt_ref):
    hp = pl.program_id(0)
    b = pl.program_id(1)
    s = q0_ref.shape[1]
    n_tiles = s // BAND

    @pl.when(b == 0)
    def _():
        for g in range(N_DIL):
            for sh in range(2):
                own, prev = _band_bias(tv_ref, g * N_HEADS + 2 * hp + sh, True)
                bt_ref[g, sh, 0] = own
                bt_ref[g, sh, 1] = prev

    groups = ((q0_ref, k0_ref, v0_ref), (q1_ref, k1_ref, v1_ref), (q2_ref, k2_ref, v2_ref))

    def mix(rows, o0, l0):
        l1, l2 = lg_ref[0, rows, :], lg_ref[1, rows, :]
        m = jnp.maximum(l0, jnp.maximum(l1, l2))
        e0, e1, e2 = jnp.exp2(l0 - m), jnp.exp2(l1 - m), jnp.exp2(l2 - m)
        ob = (e0 * o0 + e1 * og_ref[0, rows, :] + e2 * og_ref[1, rows, :]) / (e0 + e1 + e2)
        o_ref[0, rows, :] = (ob * sg_ref[0, rows, :].astype(F32)).astype(o_ref.dtype)

    for g in (*range(1, N_DIL), 0):
        q_ref, k_ref, v_ref = groups[g]
        d = DIL_PAIRS[g][1]
        assert (d == 1) == (g == 0)
        per_phase = n_tiles // d
        has_prev = per_phase > 1
        assert per_phase % TILE_GROUP == 0 or TILE_GROUP % per_phase == 0

        def body(i, c, g=g, d=d, per_phase=per_phase, has_prev=has_prev, q_ref=q_ref, k_ref=k_ref, v_ref=v_ref):
            items, dsts = [], []
            k_own = vt_own = None
            for n in range(TILE_GROUP):
                u = i * TILE_GROUP + n
                rows = pl.ds(pl.multiple_of(u * BAND, BAND), BAND)
                t = lax.rem(u, per_phase)
                r = lax.div(u, per_phase)
                qt = _unit_rms_heads(q_ref[0, rows, :].T)
                k_prev, vt_prev = k_own, vt_own
                k_own = k_ref[0, rows, :]
                vt_own = v_ref[0, rows, :].T
                if has_prev:
                    phase_start_known = TILE_GROUP % per_phase == 0
                    prev_exists = (t > 0) if (n == 0 and not phase_start_known) else None
                    if phase_start_known and n % per_phase == 0:
                        k_prev = None
                    elif n == 0:
                        prow = pl.ds(pl.multiple_of(jnp.maximum(u - 1, 0) * BAND, BAND), BAND)
                        k_prev = k_ref[0, prow, :]
                        vt_prev = v_ref[0, prow, :].T
                for sh in range(2):
                    hrows = slice(HEAD_DIM * sh, HEAD_DIM * (sh + 1))
                    blocks = [(k_own, vt_own[hrows, :], bt_ref[g, sh, 0], None)]
                    if has_prev and k_prev is not None:
                        blocks.append((k_prev, vt_prev[hrows, :], bt_ref[g, sh, 1], prev_exists))
                    items.append((_head_rows(qt, HEAD_DIM * sh), blocks, None))
                dsts.append(rows if d == 1 else pl.ds(t * (BAND * d) + r, BAND, stride=d))
            res = _band_attend(items)
            for n, dst in enumerate(dsts):
                (o0, lse0), (o1, lse1) = res[2 * n], res[2 * n + 1]
                out = jnp.concatenate([o0, o1], axis=0).T
                lse = jnp.concatenate(
                    [jnp.broadcast_to(lse0, (HEAD_DIM, BAND)), jnp.broadcast_to(lse1, (HEAD_DIM, BAND))], axis=0).T
                if g == 0:
                    mix(dst, out, lse)
                else:
                    og_ref.at[g - 1][dst, :] = out
                    lg_ref.at[g - 1][dst, :] = lse
            return c

        lax.fori_loop(0, n_tiles // TILE_GROUP, body, 0)


def _dilated(p, tv_b):
    b, s, _ = p.shape
    col = lambda c: pl.BlockSpec((1, s, LANE), lambda hp, bi: (bi, 0, c + hp))
    qkv = [col(COL_B + 3 * N_PAIRS * g + N_PAIRS * i) for g in range(N_DIL) for i in range(3)]
    return pl.pallas_call(
        _dil_kernel,
        grid=(N_PAIRS, b),
        in_specs=qkv + [col(COL_SGB), pl.BlockSpec(tv_b.shape, lambda hp, bi: (0, 0, 0, 0))],
        out_specs=pl.BlockSpec((1, s, LANE), lambda hp, bi: (bi, 0, hp)),
        out_shape=jax.ShapeDtypeStruct((b, s, W_BR), ACT),
        scratch_shapes=[
            pltpu.VMEM((N_DIL - 1, s, LANE), F32),
            pltpu.VMEM((N_DIL - 1, s, LANE), F32),
            pltpu.VMEM((N_DIL, 2, 2, BAND, BAND), F32),
        ],
        compiler_params=_cparams(2),
        name="dilated",
    )(*([p] * 10), tv_b)


def _swa_kernel(sink_ref, q_ref, sg_ref, k_ref, v_ref, tv_ref, o_ref, bt_ref, vts_ref):
    hp = pl.program_id(0)
    b = pl.program_id(1)
    n_tiles = q_ref.shape[1] // BAND
    pairs_per_kv = N_PAIRS // KV_C

    @pl.when(b == 0)
    def _():
        for sh in range(2):
            own, prev = _band_bias(tv_ref, 2 * hp + sh, False)
            bt_ref[sh, 0] = own
            bt_ref[sh, 1] = prev

    kv_lo = HEAD_DIM * (hp // pairs_per_kv)
    row = lax.broadcasted_iota(jnp.int32, (LANE, BAND), 0)
    kv_rows = (row >= kv_lo) & (row < kv_lo + HEAD_DIM)

    def kv_vt(rows, slot):
        vts_ref[slot] = v_ref[0, rows, :].T
        return vts_ref[slot, pl.ds(pl.multiple_of(kv_lo, HEAD_DIM), HEAD_DIM), :]

    def body(i, c):
        items, dsts = [], []
        zero = jnp.zeros((LANE, BAND), ACT)
        k_own = vt_own = None
        for n in range(TILE_GROUP):
            t = i * TILE_GROUP + n
            rows = pl.ds(pl.multiple_of(t * BAND, BAND), BAND)
            qt = _unit_rms_heads(q_ref[0, rows, :].T)
            k_prev, vt_prev = k_own, vt_own
            k_own = k_ref[0, rows, :]
            vt_own = kv_vt(rows, n)
            if n == 0:
                prow = pl.ds(pl.multiple_of(jnp.maximum(t - 1, 0) * BAND, BAND), BAND)
                k_prev = k_ref[0, prow, :]
                vt_prev = kv_vt(prow, TILE_GROUP)
            prev_exists = (t > 0) if n == 0 else None
            for sh in range(2):
                qh = qt[HEAD_DIM * sh:HEAD_DIM * (sh + 1), :]
                qs = jnp.where(kv_rows, jnp.concatenate([qh, qh], axis=0), zero)
                items.append((qs, [(k_own, vt_own, bt_ref[sh, 0], None), (k_prev, vt_prev, bt_ref[sh, 1], prev_exists)],
                              sink_ref[2 * hp + sh]))
            dsts.append(rows)
        res = _band_attend(items)
        for n, rows in enumerate(dsts):
            ot = jnp.concatenate([o for o, _ in res[2 * n:2 * n + 2]], axis=0).T
            o_ref[0, rows, :] = (ot * sg_ref[0, rows, :].astype(F32)).astype(o_ref.dtype)
        return c

    lax.fori_loop(0, n_tiles // TILE_GROUP, body, 0)


def _swa(p, tv_c, sinks):
    b, s, _ = p.shape
    grid_spec = pltpu.PrefetchScalarGridSpec(
        num_scalar_prefetch=1,
        grid=(N_PAIRS, b),
        in_specs=[
            pl.BlockSpec((1, s, LANE), lambda hp, bi, sk: (bi, 0, COL_C + hp)),
            pl.BlockSpec((1, s, LANE), lambda hp, bi, sk: (bi, 0, COL_C + N_PAIRS + hp)),
            pl.BlockSpec((1, s, LANE), lambda hp, bi, sk: (bi, 0, COL_C + 2 * N_PAIRS)),
            pl.BlockSpec((1, s, LANE), lambda hp, bi, sk: (bi, 0, COL_C + 2 * N_PAIRS + 1)),
            pl.BlockSpec(tv_c.shape, lambda hp, bi, sk: (0, 0, 0, 0)),
        ],
        out_specs=pl.BlockSpec((1, s, LANE), lambda hp, bi, sk: (bi, 0, hp)),
        scratch_shapes=[pltpu.VMEM((2, 2, BAND, BAND), F32),
                        pltpu.VMEM((TILE_GROUP + 1, LANE, BAND), ACT)],
    )
    return pl.pallas_call(
        _swa_kernel,
        grid_spec=grid_spec,
        out_shape=jax.ShapeDtypeStruct((b, s, W_BR), ACT),
        compiler_params=_cparams(2),
        name="swa",
    )(sinks, p, p, p, p, tv_c)


def _merge_kernel(bra_ref, brb_ref, brc_ref, g_ref, wb_ref, wo_ref, x_ref, out_ref):
    merged = None
    for i, br_ref in enumerate((bra_ref, brb_ref, brc_ref)):
        term = g_ref[:, i * D_MODEL:(i + 1) * D_MODEL].astype(F32) * _dot(br_ref[...], wb_ref[i])
        merged = term if merged is None else merged + term
    out_ref[...] = x_ref[...] + _dot(merged.astype(ACT), wo_ref[...])


def _merge(bra, brb, brc, p2, w_br, w_out, x2, layer, ts=1024):
    n = x2.shape[0]
    row = lambda w: pl.BlockSpec((ts, w), lambda i: (i, 0))
    return pl.pallas_call(
        _merge_kernel,
        grid=(n // ts,),
        in_specs=[row(W_BR), row(W_BR), row(W_BR), row(GATE_W),
                  pl.BlockSpec((None,) + w_br.shape[1:], lambda i: (layer, 0, 0, 0)),
                  pl.BlockSpec((None,) + w_out.shape[1:], lambda i: (layer, 0, 0)),
                  row(D_MODEL)],
        out_specs=row(D_MODEL),
        out_shape=jax.ShapeDtypeStruct(x2.shape, F32),
        compiler_params=_cparams(1),
        name="merge",
    )(bra, brb, brc, p2, w_br, w_out, x2)


def _rel_bucket(dist):
    dist = jnp.maximum(dist, 0)
    max_exact = N_BUCKETS // 2
    log_ratio = jnp.log(jnp.maximum(dist, 1).astype(F32) / max_exact) / math.log(REL_MAX_DIST / max_exact)
    large = jnp.minimum(max_exact + (log_ratio * (N_BUCKETS - max_exact)).astype(jnp.int32), N_BUCKETS - 1)
    return jnp.where(dist < max_exact, dist, large)


def _bias_by_distance(tab, dist, chunk):
    vals = tab.astype(F32)[:, _rel_bucket(dist)] * LOG2E
    vals = jnp.pad(vals, ((0, 0), (chunk, 0)))
    return vals.reshape(tab.shape[0], -1, 1, chunk)


def _gain_rows(qk_g):
    depth = qk_g.shape[0]
    scale = HEAD_DIM ** -0.5 * LOG2E
    one = jnp.ones((depth, HEAD_DIM), F32)
    tile = lambda v, n: jnp.tile(v, (1, n))
    parts = [tile(one, N_HEADS), tile(qk_g[:, 1] * qk_g[:, 0] * scale, N_HEADS), tile(one, 2 * N_HEADS),
             tile(one, N_DIL * N_HEADS), tile(qk_g[:, 3] * qk_g[:, 2] * scale, N_DIL * N_HEADS),
             tile(one, (N_DIL + 1) * N_HEADS),
             tile(one, N_HEADS), tile(qk_g[:, 5] * qk_g[:, 4] * scale, KV_C), tile(one, KV_C + N_HEADS),
             tile(one, GATE_W // HEAD_DIM)]
    return jnp.concatenate(parts, axis=1)


def kernel(x, ln_g, w_in, qk_g, sinks, w_branch, w_out, rel_bias):
    b, s, d = x.shape
    depth = ln_g.shape[0]
    assert d == D_MODEL and s % DIL_PAIRS[-1][0] == 0 and w_in.shape[-1] == C_IN

    tv_a = _bias_by_distance(rel_bias[:N_HEADS], jnp.arange(s), MOBA_BLOCK)
    tv_b = jnp.concatenate(
        [_bias_by_distance(rel_bias[OFF_B + g * N_HEADS:OFF_B + (g + 1) * N_HEADS], dd * jnp.arange(2 * BAND), BAND)
         for g, (_, dd) in enumerate(DIL_PAIRS)], axis=0)
    tv_c = _bias_by_distance(rel_bias[OFF_C:OFF_C + N_HEADS], jnp.arange(2 * BAND), BAND)
    seg = jnp.arange(TN) // HEAD_DIM
    e_mat = (seg[:, None] == seg[None, :]).astype(ACT)
    src = jnp.asarray([c for c, _ in _LAYOUT], jnp.int32)
    fn = jnp.asarray([f for _, f in _LAYOUT[::PAIR]], jnp.int32)
    gains = _gain_rows(qk_g)

    w_in_b = w_in.astype(ACT)
    w_br_b = w_branch.astype(ACT)
    w_out_b = w_out.astype(ACT)

    for l in range(depth):
        p = _project(x, ln_g[:, None, :], w_in_b, gains[l][None, :], e_mat, src, fn, l)
        bra = _moba(p, tv_a)
        brb = _dilated(p, tv_b)
        brc = _swa(p, tv_c, sinks[l] * LOG2E)
        x2 = _merge(bra.reshape(b * s, W_BR), brb.reshape(b * s, W_BR), brc.reshape(b * s, W_BR),
                    p.reshape(b * s, P_W), w_br_b, w_out_b, x.reshape(b * s, d), l)
        x = x2.reshape(b, s, d)
    return x
```

```python
import functools
import math
import operator

import jax
import jax.numpy as jnp
from jax import lax
from jax.experimental import pallas as pl
from jax.experimental.pallas import tpu as pltpu

D_MODEL = 1024
HEAD_DIM = 64
W_BR = D_MODEL // 2
N_HEADS = W_BR // HEAD_DIM
N_PAIRS = N_HEADS // 2
MOBA_BLOCK = 256
MOBA_TOPK = 3
DIL_PAIRS = ((128, 1), (512, 4), (2048, 16))
N_DIL = len(DIL_PAIRS)
KV_C = 2
SWA_WINDOW = 128
N_BUCKETS = 32
REL_MAX_DIST = 2048
OFF_B = N_HEADS
OFF_C = N_HEADS + N_DIL * N_HEADS
EPS = 1e-6
NEG = -1e30
LOG2E = math.log2(math.e)

LANE = 128
BAND = SWA_WINDOW
assert all(w == BAND * d for w, d in DIL_PAIRS)
TN = 256
ROWS = 256
TILE_GROUP = 16
MOBA_GROUP = 8
EPILOGUE_LAG = 2
SINGLE_OP_STRIDE = 4
ONES_ROWS = 16
VMEM_LIMIT = 56 * 1024 * 1024

ACT = jnp.bfloat16
F32 = jnp.float32

PLAIN, NORM, SILU, SIGMOID, KNORM_VPLAIN, NORM_D4, PLAIN_D4, NORM_D16, PLAIN_D16 = range(9)
_B_GROUP = lambda g, norm, plain: [(8 + 2 * g, plain), (9 + 2 * g, plain), (14 + 2 * g, norm), (15 + 2 * g, norm),
                                   (20 + 2 * g, plain), (21 + 2 * g, plain)]
_LAYOUT = ([(j, SIGMOID) for j in range(33, 45)]
           + [(0, PLAIN), (1, PLAIN), (2, NORM), (3, NORM), (4, PLAIN), (5, PLAIN), (6, SILU), (7, SILU)]
           + _B_GROUP(0, NORM, PLAIN) + _B_GROUP(1, NORM_D4, PLAIN_D4) + _B_GROUP(2, NORM_D16, PLAIN_D16)
           + [(26, SILU), (27, SILU)]
           + [(28, PLAIN), (29, PLAIN), (31, SILU), (32, SILU), (30, KNORM_VPLAIN)])
C_IN = len(_LAYOUT) * TN
PAIR = 2
_LAYOUT = _LAYOUT + [(_LAYOUT[-1][0], KNORM_VPLAIN)] * (-len(_LAYOUT) % PAIR)
assert all(_LAYOUT[j][1] == _LAYOUT[j - j % PAIR][1] for j in range(len(_LAYOUT)))
assert [f for _, f in _LAYOUT].count(KNORM_VPLAIN) == PAIR
N_STEPS = len(_LAYOUT) // PAIR
P_W = len(_LAYOUT) * TN
COL_GATE, COL_A, COL_B, COL_SGB, COL_C = 0, 24, 40, 76, 80
GATE_W = 3 * D_MODEL
assert COL_GATE == 0


def _cparams(n_axes):
    return pltpu.CompilerParams(dimension_semantics=("arbitrary",) * n_axes, vmem_limit_bytes=VMEM_LIMIT)


def _dot(a, b):
    return jnp.dot(a, b, preferred_element_type=F32)


def _sigmoid(a):
    return 0.5 * jnp.tanh(0.5 * a) + 0.5


def _proj_kernel(src_ref, fn_ref, x_ref, lng_ref, *refs):
    del src_ref
    w_refs, gain_refs = refs[:PAIR], refs[PAIR:2 * PAIR]
    e_ref, o_ref, xn_ref, acc_ref, acc2_ref = refs[2 * PAIR:]
    j = pl.program_id(1)
    s = x_ref.shape[1]
    n_chunks = s // ROWS
    lanes_per_block = TN // LANE

    @pl.when(j == 0)
    def _():
        def body(i, c):
            rows = pl.ds(pl.multiple_of(i * ROWS, ROWS), ROWS)
            xv = x_ref[0, rows, :]
            ms = jnp.mean(xv * xv, axis=-1, keepdims=True)
            xn_ref[rows, :] = (xv * lax.rsqrt(ms + EPS) * lng_ref[...]).astype(ACT)
            return c
        lax.fori_loop(0, n_chunks, body, 0)

    def mean_square(a):
        return _dot((a * a).astype(ACT), e_ref[...])

    def head_norm(a, ss, gain):
        return a * lax.rsqrt(ss * (1.0 / HEAD_DIM) + EPS) * gain

    def k_norm_v_plain(a, ss, gain):
        lane = lax.broadcasted_iota(jnp.int32, a.shape, 1)
        return jnp.where(lane < KV_C * HEAD_DIM, head_norm(a, ss, gain), a)

    plain = (None, lambda a, _, gain: a)
    silu = (None, lambda a, _, gain: a * _sigmoid(a))
    sigmoid = (None, lambda a, _, gain: _sigmoid(a))
    norm = (mean_square, head_norm)
    knorm_vplain = (mean_square, k_norm_v_plain)

    def pipeline(n_items, per_block, produce, epilogue, store):
        pre, post = epilogue
        tiles = {}
        for n in range(n_items + EPILOGUE_LAG):
            if n < n_items:
                tiles[n] = produce(n)
            if n >= EPILOGUE_LAG:
                m = n - EPILOGUE_LAG
                tile = tiles.pop(m)
                store(m, post(tile, pre(tile) if pre is not None else None, gain_refs[m // per_block][...]))

    def phase_rows(src_ref, slab, t, d):
        r, m0 = divmod(t * BAND, s // d)
        return src_ref.at[slab][pl.ds(m0 * d + r, BAND, stride=d), :]

    fn = fn_ref[j]

    def chunk(n):
        h, i = divmod(n, n_chunks)
        return _dot(xn_ref[i * ROWS:(i + 1) * ROWS, :], w_refs[h][...])

    def out_cols(h):
        return slice(h * TN, (h + 1) * TN)

    def direct(code, epilogue, live=PAIR):
        @pl.when(fn == code)
        def _():
            def store(n, val):
                h, i = divmod(n, n_chunks)
                o_ref[0, i * ROWS:(i + 1) * ROWS, out_cols(h)] = val.astype(o_ref.dtype)

            pipeline(live * n_chunks, n_chunks, chunk, epilogue, store)
            if live < PAIR:
                o_ref[0, :, live * TN:] = jnp.zeros((s, (PAIR - live) * TN), o_ref.dtype)

    def phase_major(code, epilogue, d):
        @pl.when(fn == code)
        def _():
            def stage(n, val):
                h, i = divmod(n, n_chunks)
                for c in range(lanes_per_block):
                    acc_ref[h * lanes_per_block + c, i * ROWS:(i + 1) * ROWS, :] = val[:, c * LANE:(c + 1) * LANE]

            pipeline(PAIR * n_chunks, n_chunks, chunk, epilogue, stage)
            src_ref, stride = acc_ref, d
            if d > SINGLE_OP_STRIDE:
                assert d == SINGLE_OP_STRIDE * SINGLE_OP_STRIDE
                for slab in range(PAIR * lanes_per_block):
                    for t in range(s // BAND):
                        acc2_ref[slab, t * BAND:(t + 1) * BAND, :] = phase_rows(acc_ref, slab, t, SINGLE_OP_STRIDE)
                src_ref, stride = acc2_ref, SINGLE_OP_STRIDE
            for h in range(PAIR):
                for t in range(s // BAND):
                    a = jnp.concatenate([phase_rows(src_ref, h * lanes_per_block + c, t, stride)
                                         for c in range(lanes_per_block)], axis=1)
                    o_ref[0, t * BAND:(t + 1) * BAND, out_cols(h)] = a.astype(o_ref.dtype)

    direct(PLAIN, plain)
    direct(NORM, norm)
    direct(SILU, silu)
    direct(SIGMOID, sigmoid)
    direct(KNORM_VPLAIN, knorm_vplain, live=1)
    phase_major(NORM_D4, norm, DIL_PAIRS[1][1])
    phase_major(PLAIN_D4, plain, DIL_PAIRS[1][1])
    phase_major(NORM_D16, norm, DIL_PAIRS[2][1])
    phase_major(PLAIN_D16, plain, DIL_PAIRS[2][1])


def _w_index(bi, j, src_ref, fn_ref, *, layer, slot):
    return layer, 0, src_ref[PAIR * j + slot]


def _gain_index(bi, j, src_ref, fn_ref, *, slot):
    return 0, src_ref[PAIR * j + slot]


def _project(x, ln_g, w_in, gain_row, e_mat, src, fn, layer):
    b, s, d = x.shape
    n_in = PAIR
    grid_spec = pltpu.PrefetchScalarGridSpec(
        num_scalar_prefetch=2,
        grid=(b, N_STEPS),
        in_specs=[
            pl.BlockSpec((1, s, d), lambda bi, j, src_ref, fn_ref: (jnp.minimum(bi + jnp.minimum(j, 1), b - 1), 0, 0)),
            pl.BlockSpec((None, 1, d), lambda bi, j, src_ref, fn_ref: (layer, 0, 0)),
        ] + [pl.BlockSpec((None, d, TN), functools.partial(_w_index, layer=layer, slot=n)) for n in range(n_in)]
          + [pl.BlockSpec((1, TN), functools.partial(_gain_index, slot=n)) for n in range(n_in)]
          + [pl.BlockSpec((TN, TN), lambda bi, j, src_ref, fn_ref: (0, 0))],
        out_specs=pl.BlockSpec((1, s, n_in * TN), lambda bi, j, src_ref, fn_ref: (bi, 0, j)),
        scratch_shapes=[pltpu.VMEM((s, d), ACT), pltpu.VMEM((PAIR * TN // LANE, s, LANE), F32),
                        pltpu.VMEM((PAIR * TN // LANE, s, LANE), F32)],
    )
    return pl.pallas_call(
        _proj_kernel,
        grid_spec=grid_spec,
        out_shape=jax.ShapeDtypeStruct((b, s, P_W), ACT),
        compiler_params=_cparams(2),
        name="proj",
    )(src, fn, x, ln_g, *([w_in] * n_in), *([gain_row] * n_in), e_mat)


def _toeplitz(lo, hi, n):
    x = jnp.concatenate([jnp.broadcast_to(lo, (n, n)), jnp.broadcast_to(hi, (n, n))], axis=1)
    return pltpu.roll(x, 0, 1, stride=1, stride_axis=0)[:, n:]


def _unit_rms_heads(qt):
    halves = []
    for sh in range(2):
        x = qt[HEAD_DIM * sh:HEAD_DIM * (sh + 1), :].astype(F32)
        halves.append(x * lax.rsqrt(jnp.mean(x * x, axis=0, keepdims=True) + EPS))
    return jnp.concatenate(halves, axis=0).astype(qt.dtype)


def _head_rows(qt, lo):
    row = lax.broadcasted_iota(jnp.int32, qt.shape, 0)
    return jnp.where((row >= lo) & (row < lo + HEAD_DIM), qt, jnp.zeros_like(qt))


def _moba_kernel(q_ref, k_ref, v_ref, sg_ref, tv_ref, o_ref,
                 bt_ref, qts_ref, vt_ref, km_ref, sel_ref, m_ref, acc_ref):
    hp = pl.program_id(0)
    b = pl.program_id(1)
    blk = MOBA_BLOCK
    n_blk = k_ref.shape[1] // blk

    def rows_of(i):
        return pl.ds(i * blk if isinstance(i, int) else pl.multiple_of(i * blk, blk), blk)

    @pl.when(b == 0)
    def _():
        kr = lax.broadcasted_iota(jnp.int32, (blk, blk), 0)
        qc = lax.broadcasted_iota(jnp.int32, (blk, blk), 1)
        for sh in range(2):
            def body(o, c, sh=sh):
                t = _toeplitz(tv_ref[2 * hp + sh, o], tv_ref[2 * hp + sh, o + 1], blk)
                bt_ref[sh, o] = jnp.where((qc >= kr) | (o > 0), t, NEG)
                return c
            lax.fori_loop(0, n_blk, body, 0)

    for i in range(n_blk):
        vt = v_ref[0, rows_of(i), :].T
        for sh in range(2):
            vt_ref[i, sh] = jnp.concatenate(
                [vt[HEAD_DIM * sh:HEAD_DIM * (sh + 1), :], jnp.ones((ONES_ROWS, blk), ACT)], axis=0)
        km_ref[i:i + 1, :] = jnp.mean(k_ref[0, rows_of(i), :].astype(F32), axis=0, keepdims=True)
        qt = _unit_rms_heads(q_ref[0, rows_of(i), :].T)
        for sh in range(2):
            qts_ref[i, sh] = _head_rows(qt, HEAD_DIM * sh)

    km = km_ref[...]
    km0 = km.astype(ACT).astype(F32)
    km1 = (km - km0).astype(ACT).astype(F32)
    km2 = km - km0 - km1
    km_terms = jnp.concatenate([km0, km1, km2], axis=0).astype(ACT)
    blk_row = lax.broadcasted_iota(jnp.int32, (n_blk, blk), 0)

    ranked = [(qi, sh) for qi in range(MOBA_TOPK + 1, n_blk) for sh in range(2)]
    gates = [_dot(km_terms, qts_ref[qi, sh]) for qi, sh in ranked]
    for (qi, sh), g3 in zip(ranked, gates):
        g = g3[:n_blk] + g3[n_blk:2 * n_blk] + g3[2 * n_blk:]
        g = jnp.where(blk_row < qi, g, NEG)
        sel_rows = []
        for jb in range(qi):
            gj = g[jb:jb + 1, :]
            beats = (g > gj) | ((g == gj) & (blk_row < jb))
            cnt = jnp.sum(beats.astype(F32), axis=0, keepdims=True)
            sel_rows.append(jnp.where(cnt < MOBA_TOPK, 1.0, 0.0))
        sel_rows.append(jnp.zeros((n_blk - qi, blk), F32))
        sel_ref[qi, sh] = jnp.concatenate(sel_rows, axis=0)

    def tiles(off, qis):
        k_blks = [k_ref[0, rows_of(qi - off), :] for qi in qis]
        scores = [_dot(k_blk, qts_ref[qi, sh]) for qi, k_blk in zip(qis, k_blks) for sh in range(2)]
        work = [(qi, sh) for qi in qis for sh in range(2)]
        stats = []
        half = blk // 2
        for (qi, sh), st in zip(work, scores):
            if off == 0:
                bias = bt_ref[sh, 0]
                parts = [(st[:half, :half] + bias[:half, :half], slice(0, half)),
                         (st[:, half:] + bias[:, half:], slice(0, blk))]
                maxes = [jnp.max(s_part, axis=0, keepdims=True) for s_part, _ in parts]
                m_ref[qi, sh] = jnp.concatenate(maxes, axis=1)
                stats.append((None, [(keys, jnp.exp2(s_part - m_part).astype(ACT))
                                     for (s_part, keys), m_part in zip(parts, maxes)]))
                continue
            st = st + bt_ref[sh, off]
            tile_max = jnp.max(st, axis=0, keepdims=True)
            if qi <= MOBA_TOPK:
                m = m_ref[qi, sh]
                m_new = jnp.maximum(m, tile_max)
                p = jnp.exp2(st - m_new)
                alpha = jnp.exp2(m - m_new)
            else:
                sel = sel_ref[qi, sh, pl.ds(qi - off, 1), :] > 0.5
                m = m_ref[qi, sh]
                m_new = jnp.where(sel, jnp.maximum(m, tile_max), m)
                p = jnp.exp2(st - jnp.where(sel, m_new, -NEG))
                alpha = jnp.exp2(m - m_new)
            m_ref[qi, sh] = m_new
            stats.append((alpha, [(slice(0, blk), p.astype(ACT))]))
        pvs = [jnp.concatenate([_dot(vt_ref[qi - off, sh, :, keys], p) for keys, p in parts], axis=1)
               for (qi, sh), (_, parts) in zip(work, stats)]
        for (qi, sh), (alpha, _), pv in zip(work, stats, pvs):
            acc_ref[qi, sh] = pv if off == 0 else alpha * acc_ref[qi, sh] + pv

    def finish(qi):
        ot = jnp.concatenate([acc_ref[qi, sh, :HEAD_DIM, :] * (1.0 / acc_ref[qi, sh, HEAD_DIM:HEAD_DIM + 1, :])
                              for sh in range(2)], axis=0)
        o_ref[0, rows_of(qi), :] = (ot.T * sg_ref[0, rows_of(qi), :].astype(F32)).astype(o_ref.dtype)

    for off in range(n_blk):
        qis = list(range(off, n_blk))
        for lo in range(0, len(qis), MOBA_GROUP):
            tiles(off, qis[lo:lo + MOBA_GROUP])
        finish(off)


def _moba(p, tv_a):
    b, s, _ = p.shape
    blk = MOBA_BLOCK
    n_blk = s // blk
    col = lambda c: pl.BlockSpec((1, s, LANE), lambda hp, bi: (bi, 0, COL_A + c * N_PAIRS + hp))
    return pl.pallas_call(
        _moba_kernel,
        grid=(N_PAIRS, b),
        in_specs=[col(0), col(1), col(2), col(3), pl.BlockSpec(tv_a.shape, lambda hp, bi: (0, 0, 0, 0))],
        out_specs=pl.BlockSpec((1, s, LANE), lambda hp, bi: (bi, 0, hp)),
        out_shape=jax.ShapeDtypeStruct((b, s, W_BR), ACT),
        scratch_shapes=[
            pltpu.VMEM((2, n_blk, blk, blk), F32),
            pltpu.VMEM((n_blk, 2, LANE, blk), ACT),
            pltpu.VMEM((n_blk, 2, HEAD_DIM + ONES_ROWS, blk), ACT),
            pltpu.VMEM((n_blk, LANE), F32),
            pltpu.VMEM((n_blk, 2, n_blk, blk), F32),
            pltpu.VMEM((n_blk, 2, 1, blk), F32),
            pltpu.VMEM((n_blk, 2, HEAD_DIM + ONES_ROWS, blk), F32),
        ],
        compiler_params=_cparams(2),
        name="moba",
    )(p, p, p, p, tv_a)


def _band_attend(items):
    def fuse(blocks):
        if len(blocks) > 1 and all(valid is None for _, _, _, valid in blocks):
            ks, vts, biases, _ = zip(*blocks)
            return [(jnp.concatenate(ks, axis=0), jnp.concatenate(vts, axis=1), jnp.concatenate(biases, axis=0), None)]
        return blocks

    items = [(qs, fuse(blocks), sink) for qs, blocks, sink in items]

    probs = []
    for qs, blocks, sink in items:
        sc = [_dot(k, qs) for k, _, _, _ in blocks]
        ss = [s + bias if valid is None else jnp.where(valid, s + bias, NEG)
              for s, (_, _, bias, valid) in zip(sc, blocks)]
        m = functools.reduce(jnp.maximum, [jnp.max(s, axis=0, keepdims=True) for s in ss])
        if sink is not None:
            m = jnp.maximum(m, sink)
        probs.append((m, [jnp.exp2(s - m).astype(ACT) for s in ss]))
    outs = []
    for (_, blocks, sink), (m, ps) in zip(items, probs):
        n_v = blocks[0][1].shape[0]
        acc = functools.reduce(operator.add, [
            _dot(jnp.concatenate([vt, jnp.ones((ONES_ROWS, vt.shape[1]), ACT)], axis=0), p)
            for (_, vt, _, _), p in zip(blocks, ps)])
        l = acc[n_v:n_v + 1, :]
        if sink is not None:
            l = l + jnp.exp2(sink - m)
        outs.append((acc[:n_v, :] * (1.0 / l), m + jnp.log2(l)))
    return outs


def _band_bias(tv_ref, head, prev_includes_window_edge):
    kr = lax.broadcasted_iota(jnp.int32, (BAND, BAND), 0)
    qc = lax.broadcasted_iota(jnp.int32, (BAND, BAND), 1)
    own, prev = [_toeplitz(tv_ref[head, which], tv_ref[head, which + 1], BAND) for which in range(2)]
    prev_mask = (qc <= kr) if prev_includes_window_edge else (qc < kr)
    return jnp.where(qc >= kr, own, NEG), jnp.where(prev_mask, prev, NEG)


def _dil_kernel(q0_ref, k0_ref, v0_ref, q1_ref, k1_ref, v1_ref, q2_ref, k2_ref, v2_ref, sg_ref, tv_ref,
                o_ref, og_ref, lg_ref, bt_ref):
    hp = pl.program_id(0)
    b = pl.program_id(1)
    s = q0_ref.shape[1]
    n_tiles = s // BAND

    @pl.when(b == 0)
    def _():
        for g in range(N_DIL):
            for sh in range(2):
                own, prev = _band_bias(tv_ref, g * N_HEADS + 2 * hp + sh, True)
                bt_ref[g, sh, 0] = own
                bt_ref[g, sh, 1] = prev

    groups = ((q0_ref, k0_ref, v0_ref), (q1_ref, k1_ref, v1_ref), (q2_ref, k2_ref, v2_ref))

    def mix(rows, o0, l0):
        l1, l2 = lg_ref[0, rows, :], lg_ref[1, rows, :]
        m = jnp.maximum(l0, jnp.maximum(l1, l2))
        e0, e1, e2 = jnp.exp2(l0 - m), jnp.exp2(l1 - m), jnp.exp2(l2 - m)
        ob = (e0 * o0 + e1 * og_ref[0, rows, :] + e2 * og_ref[1, rows, :]) / (e0 + e1 + e2)
        o_ref[0, rows, :] = (ob * sg_ref[0, rows, :].astype(F32)).astype(o_ref.dtype)

    for g in (*range(1, N_DIL), 0):
        q_ref, k_ref, v_ref = groups[g]
        d = DIL_PAIRS[g][1]
        assert (d == 1) == (g == 0)
        per_phase = n_tiles // d
        has_prev = per_phase > 1
        assert per_phase % TILE_GROUP == 0 or TILE_GROUP % per_phase == 0

        def body(i, c, g=g, d=d, per_phase=per_phase, has_prev=has_prev, q_ref=q_ref, k_ref=k_ref, v_ref=v_ref):
            items, dsts = [], []
            k_own = vt_own = None
            for n in range(TILE_GROUP):
                u = i * TILE_GROUP + n
                rows = pl.ds(pl.multiple_of(u * BAND, BAND), BAND)
                t = lax.rem(u, per_phase)
                r = lax.div(u, per_phase)
                qt = _unit_rms_heads(q_ref[0, rows, :].T)
                k_prev, vt_prev = k_own, vt_own
                k_own = k_ref[0, rows, :]
                vt_own = v_ref[0, rows, :].T
                if has_prev:
                    phase_start_known = TILE_GROUP % per_phase == 0
                    prev_exists = (t > 0) if (n == 0 and not phase_start_known) else None
                    if phase_start_known and n % per_phase == 0:
                        k_prev = None
                    elif n == 0:
                        prow = pl.ds(pl.multiple_of(jnp.maximum(u - 1, 0) * BAND, BAND), BAND)
                        k_prev = k_ref[0, prow, :]
                        vt_prev = v_ref[0, prow, :].T
                for sh in range(2):
                    hrows = slice(HEAD_DIM * sh, HEAD_DIM * (sh + 1))
                    blocks = [(k_own, vt_own[hrows, :], bt_ref[g, sh, 0], None)]
                    if has_prev and k_prev is not None:
                        blocks.append((k_prev, vt_prev[hrows, :], bt_ref[g, sh, 1], prev_exists))
                    items.append((_head_rows(qt, HEAD_DIM * sh), blocks, None))
                dsts.append(rows if d == 1 else pl.ds(t * (BAND * d) + r, BAND, stride=d))
            res = _band_attend(items)
            for n, dst in enumerate(dsts):
                (o0, lse0), (o1, lse1) = res[2 * n], res[2 * n + 1]
                out = jnp.concatenate([o0, o1], axis=0).T
                lse = jnp.concatenate(
                    [jnp.broadcast_to(lse0, (HEAD_DIM, BAND)), jnp.broadcast_to(lse1, (HEAD_DIM, BAND))], axis=0).T
                if g == 0:
                    mix(dst, out, lse)
                else:
                    og_ref.at[g - 1][dst, :] = out
                    lg_ref.at[g - 1][dst, :] = lse
            return c

        lax.fori_loop(0, n_tiles // TILE_GROUP, body, 0)


def _dilated(p, tv_b):
    b, s, _ = p.shape
    col = lambda c: pl.BlockSpec((1, s, LANE), lambda hp, bi: (bi, 0, c + hp))
    qkv = [col(COL_B + 3 * N_PAIRS * g + N_PAIRS * i) for g in range(N_DIL) for i in range(3)]
    return pl.pallas_call(
        _dil_kernel,
        grid=(N_PAIRS, b),
        in_specs=qkv + [col(COL_SGB), pl.BlockSpec(tv_b.shape, lambda hp, bi: (0, 0, 0, 0))],
        out_specs=pl.BlockSpec((1, s, LANE), lambda hp, bi: (bi, 0, hp)),
        out_shape=jax.ShapeDtypeStruct((b, s, W_BR), ACT),
        scratch_shapes=[
            pltpu.VMEM((N_DIL - 1, s, LANE), F32),
            pltpu.VMEM((N_DIL - 1, s, LANE), F32),
            pltpu.VMEM((N_DIL, 2, 2, BAND, BAND), F32),
        ],
        compiler_params=_cparams(2),
        name="dilated",
    )(*([p] * 10), tv_b)


def _swa_kernel(sink_ref, q_ref, sg_ref, k_ref, v_ref, tv_ref, o_ref, bt_ref, vts_ref):
    hp = pl.program_id(0)
    b = pl.program_id(1)
    n_tiles = q_ref.shape[1] // BAND
    pairs_per_kv = N_PAIRS // KV_C

    @pl.when(b == 0)
    def _():
        for sh in range(2):
            own, prev = _band_bias(tv_ref, 2 * hp + sh, False)
            bt_ref[sh, 0] = own
            bt_ref[sh, 1] = prev

    kv_lo = HEAD_DIM * (hp // pairs_per_kv)
    row = lax.broadcasted_iota(jnp.int32, (LANE, BAND), 0)
    kv_rows = (row >= kv_lo) & (row < kv_lo + HEAD_DIM)

    def kv_vt(rows, slot):
        vts_ref[slot] = v_ref[0, rows, :].T
        return vts_ref[slot, pl.ds(pl.multiple_of(kv_lo, HEAD_DIM), HEAD_DIM), :]

    def body(i, c):
        items, dsts = [], []
        zero = jnp.zeros((LANE, BAND), ACT)
        k_own = vt_own = None
        for n in range(TILE_GROUP):
            t = i * TILE_GROUP + n
            rows = pl.ds(pl.multiple_of(t * BAND, BAND), BAND)
            qt = _unit_rms_heads(q_ref[0, rows, :].T)
            k_prev, vt_prev = k_own, vt_own
            k_own = k_ref[0, rows, :]
            vt_own = kv_vt(rows, n)
            if n == 0:
                prow = pl.ds(pl.multiple_of(jnp.maximum(t - 1, 0) * BAND, BAND), BAND)
                k_prev = k_ref[0, prow, :]
                vt_prev = kv_vt(prow, TILE_GROUP)
            prev_exists = (t > 0) if n == 0 else None
            for sh in range(2):
                qh = qt[HEAD_DIM * sh:HEAD_DIM * (sh + 1), :]
                qs = jnp.where(kv_rows, jnp.concatenate([qh, qh], axis=0), zero)
                items.append((qs, [(k_own, vt_own, bt_ref[sh, 0], None), (k_prev, vt_prev, bt_ref[sh, 1], prev_exists)],
                              sink_ref[2 * hp + sh]))
            dsts.append(rows)
        res = _band_attend(items)
        for n, rows in enumerate(dsts):
            ot = jnp.concatenate([o for o, _ in res[2 * n:2 * n + 2]], axis=0).T
            o_ref[0, rows, :] = (ot * sg_ref[0, rows, :].astype(F32)).astype(o_ref.dtype)
        return c

    lax.fori_loop(0, n_tiles // TILE_GROUP, body, 0)


def _swa(p, tv_c, sinks):
    b, s, _ = p.shape
    grid_spec = pltpu.PrefetchScalarGridSpec(
        num_scalar_prefetch=1,
        grid=(N_PAIRS, b),
        in_specs=[
            pl.BlockSpec((1, s, LANE), lambda hp, bi, sk: (bi, 0, COL_C + hp)),
            pl.BlockSpec((1, s, LANE), lambda hp, bi, sk: (bi, 0, COL_C + N_PAIRS + hp)),
            pl.BlockSpec((1, s, LANE), lambda hp, bi, sk: (bi, 0, COL_C + 2 * N_PAIRS)),
            pl.BlockSpec((1, s, LANE), lambda hp, bi, sk: (bi, 0, COL_C + 2 * N_PAIRS + 1)),
            pl.BlockSpec(tv_c.shape, lambda hp, bi, sk: (0, 0, 0, 0)),
        ],
        out_specs=pl.BlockSpec((1, s, LANE), lambda hp, bi, sk: (bi, 0, hp)),
        scratch_shapes=[pltpu.VMEM((2, 2, BAND, BAND), F32),
                        pltpu.VMEM((TILE_GROUP + 1, LANE, BAND), ACT)],
    )
    return pl.pallas_call(
        _swa_kernel,
        grid_spec=grid_spec,
        out_shape=jax.ShapeDtypeStruct((b, s, W_BR), ACT),
        compiler_params=_cparams(2),
        name="swa",
    )(sinks, p, p, p, p, tv_c)


def _merge_kernel(bra_ref, brb_ref, brc_ref, g_ref, wb_ref, wo_ref, x_ref, out_ref):
    merged = None
    for i, br_ref in enumerate((bra_ref, brb_ref, brc_ref)):
        term = g_ref[:, i * D_MODEL:(i + 1) * D_MODEL].astype(F32) * _dot(br_ref[...], wb_ref[i])
        merged = term if merged is None else merged + term
    out_ref[...] = x_ref[...] + _dot(merged.astype(ACT), wo_ref[...])


def _merge(bra, brb, brc, p2, w_br, w_out, x2, layer, ts=1024):
    n = x2.shape[0]
    row = lambda w: pl.BlockSpec((ts, w), lambda i: (i, 0))
    return pl.pallas_call(
        _merge_kernel,
        grid=(n // ts,),
        in_specs=[row(W_BR), row(W_BR), row(W_BR), row(GATE_W),
                  pl.BlockSpec((None,) + w_br.shape[1:], lambda i: (layer, 0, 0, 0)),
                  pl.BlockSpec((None,) + w_out.shape[1:], lambda i: (layer, 0, 0)),
                  row(D_MODEL)],
        out_specs=row(D_MODEL),
        out_shape=jax.ShapeDtypeStruct(x2.shape, F32),
        compiler_params=_cparams(1),
        name="merge",
    )(bra, brb, brc, p2, w_br, w_out, x2)


def _rel_bucket(dist):
    dist = jnp.maximum(dist, 0)
    max_exact = N_BUCKETS // 2
    log_ratio = jnp.log(jnp.maximum(dist, 1).astype(F32) / max_exact) / math.log(REL_MAX_DIST / max_exact)
    large = jnp.minimum(max_exact + (log_ratio * (N_BUCKETS - max_exact)).astype(jnp.int32), N_BUCKETS - 1)
    return jnp.where(dist < max_exact, dist, large)


def _bias_by_distance(tab, dist, chunk):
    vals = tab.astype(F32)[:, _rel_bucket(dist)] * LOG2E
    vals = jnp.pad(vals, ((0, 0), (chunk, 0)))
    return vals.reshape(tab.shape[0], -1, 1, chunk)


def _gain_rows(qk_g):
    depth = qk_g.shape[0]
    scale = HEAD_DIM ** -0.5 * LOG2E
    one = jnp.ones((depth, HEAD_DIM), F32)
    tile = lambda v, n: jnp.tile(v, (1, n))
    parts = [tile(one, N_HEADS), tile(qk_g[:, 1] * qk_g[:, 0] * scale, N_HEADS), tile(one, 2 * N_HEADS),
             tile(one, N_DIL * N_HEADS), tile(qk_g[:, 3] * qk_g[:, 2] * scale, N_DIL * N_HEADS),
             tile(one, (N_DIL + 1) * N_HEADS),
             tile(one, N_HEADS), tile(qk_g[:, 5] * qk_g[:, 4] * scale, KV_C), tile(one, KV_C + N_HEADS),
             tile(one, GATE_W // HEAD_DIM)]
    return jnp.concatenate(parts, axis=1)


def kernel(x, ln_g, w_in, qk_g, sinks, w_branch, w_out, rel_bias):
    b, s, d = x.shape
    depth = ln_g.shape[0]
    assert d == D_MODEL and s % DIL_PAIRS[-1][0] == 0 and w_in.shape[-1] == C_IN

    tv_a = _bias_by_distance(rel_bias[:N_HEADS], jnp.arange(s), MOBA_BLOCK)
    tv_b = jnp.concatenate(
        [_bias_by_distance(rel_bias[OFF_B + g * N_HEADS:OFF_B + (g + 1) * N_HEADS], dd * jnp.arange(2 * BAND), BAND)
         for g, (_, dd) in enumerate(DIL_PAIRS)], axis=0)
    tv_c = _bias_by_distance(rel_bias[OFF_C:OFF_C + N_HEADS], jnp.arange(2 * BAND), BAND)
    seg = jnp.arange(TN) // HEAD_DIM
    e_mat = (seg[:, None] == seg[None, :]).astype(ACT)
    src = jnp.asarray([c for c, _ in _LAYOUT], jnp.int32)
    fn = jnp.asarray([f for _, f in _LAYOUT[::PAIR]], jnp.int32)
    gains = _gain_rows(qk_g)

    w_in_b = w_in.astype(ACT)
    w_br_b = w_branch.astype(ACT)
    w_out_b = w_out.astype(ACT)

    for l in range(depth):
        p = _project(x, ln_g[:, None, :], w_in_b, gains[l][None, :], e_mat, src, fn, l)
        bra = _moba(p, tv_a)
        brb = _dilated(p, tv_b)
        brc = _swa(p, tv_c, sinks[l] * LOG2E)
        x2 = _merge(bra.reshape(b * s, W_BR), brb.reshape(b * s, W_BR), brc.reshape(b * s, W_BR),
                    p.reshape(b * s, P_W), w_br_b, w_out_b, x.reshape(b * s, d), l)
        x = x2.reshape(b, s, d)
    return x
```

```python
import functools
import math
import operator

import jax
import jax.numpy as jnp
from jax import lax
from jax.experimental import pallas as pl
from jax.experimental.pallas import tpu as pltpu

D_MODEL = 1024
HEAD_DIM = 64
W_BR = D_MODEL // 2
N_HEADS = W_BR // HEAD_DIM
N_PAIRS = N_HEADS // 2
MOBA_BLOCK = 256
MOBA_TOPK = 3
DIL_PAIRS = ((128, 1), (512, 4), (2048, 16))
N_DIL = len(DIL_PAIRS)
KV_C = 2
SWA_WINDOW = 128
N_BUCKETS = 32
REL_MAX_DIST = 2048
OFF_B = N_HEADS
OFF_C = N_HEADS + N_DIL * N_HEADS
EPS = 1e-6
NEG = -1e30
LOG2E = math.log2(math.e)

LANE = 128
BAND = SWA_WINDOW
assert all(w == BAND * d for w, d in DIL_PAIRS)
TN = 256
ROWS = 256
TILE_GROUP = 16
MOBA_GROUP = 8
EPILOGUE_LAG = 2
SINGLE_OP_STRIDE = 4
ONES_ROWS = 16
VMEM_LIMIT = 56 * 1024 * 1024

ACT = jnp.bfloat16
F32 = jnp.float32

PLAIN, NORM, SILU, SIGMOID, KNORM_VPLAIN, NORM_D4, PLAIN_D4, NORM_D16, PLAIN_D16, SIGMOID_FIRST = range(10)
_B_GROUP = lambda g, norm, plain: [(8 + 2 * g, plain), (9 + 2 * g, plain), (14 + 2 * g, norm), (15 + 2 * g, norm),
                                   (20 + 2 * g, plain), (21 + 2 * g, plain)]
_LAYOUT = ([(33, SIGMOID_FIRST), (34, SIGMOID_FIRST)] + [(j, SIGMOID) for j in range(35, 45)]
           + [(0, PLAIN), (1, PLAIN), (2, NORM), (3, NORM), (4, PLAIN), (5, PLAIN), (6, SILU), (7, SILU)]
           + _B_GROUP(0, NORM, PLAIN) + _B_GROUP(1, NORM_D4, PLAIN_D4) + _B_GROUP(2, NORM_D16, PLAIN_D16)
           + [(26, SILU), (27, SILU)]
           + [(28, PLAIN), (29, PLAIN), (31, SILU), (32, SILU), (30, KNORM_VPLAIN)])
C_IN = len(_LAYOUT) * TN
PAIR = 2
_LAYOUT = _LAYOUT + [(_LAYOUT[-1][0], KNORM_VPLAIN)] * (-len(_LAYOUT) % PAIR)
assert all(_LAYOUT[j][1] == _LAYOUT[j - j % PAIR][1] for j in range(len(_LAYOUT)))
assert [f for _, f in _LAYOUT].count(KNORM_VPLAIN) == PAIR
N_STEPS = len(_LAYOUT) // PAIR
P_W = len(_LAYOUT) * TN
COL_GATE, COL_A, COL_B, COL_SGB, COL_C = 0, 24, 40, 76, 80
GATE_W = 3 * D_MODEL
assert COL_GATE == 0


def _cparams(n_axes):
    return pltpu.CompilerParams(dimension_semantics=("arbitrary",) * n_axes, vmem_limit_bytes=VMEM_LIMIT)


def _dot(a, b):
    return jnp.dot(a, b, preferred_element_type=F32)


def _sigmoid(a):
    return 0.5 * jnp.tanh(0.5 * a) + 0.5


def _proj_kernel(src_ref, fn_ref, x_ref, lng_ref, *refs):
    del src_ref
    w_refs, gain_refs = refs[:PAIR], refs[PAIR:2 * PAIR]
    e_ref, o_ref, xn_ref, acc_ref, acc2_ref = refs[2 * PAIR:]
    j = pl.program_id(1)
    s = x_ref.shape[1]
    n_chunks = s // ROWS
    lanes_per_block = TN // LANE

    def normalise_rows():
        for i in range(n_chunks):
            xv = x_ref[0, i * ROWS:(i + 1) * ROWS, :]
            ms = jnp.mean(xv * xv, axis=-1, keepdims=True)
            xn_ref[i * ROWS:(i + 1) * ROWS, :] = (xv * lax.rsqrt(ms + EPS) * lng_ref[...]).astype(ACT)

    def mean_square(a):
        return _dot((a * a).astype(ACT), e_ref[...])

    def head_norm(a, ss, gain):
        return a * lax.rsqrt(ss * (1.0 / HEAD_DIM) + EPS) * gain

    def k_norm_v_plain(a, ss, gain):
        lane = lax.broadcasted_iota(jnp.int32, a.shape, 1)
        return jnp.where(lane < KV_C * HEAD_DIM, head_norm(a, ss, gain), a)

    plain = (None, lambda a, _, gain: a)
    silu = (None, lambda a, _, gain: a * _sigmoid(a))
    sigmoid = (None, lambda a, _, gain: _sigmoid(a))
    norm = (mean_square, head_norm)
    knorm_vplain = (mean_square, k_norm_v_plain)

    def pipeline(n_items, per_block, produce, epilogue, store):
        pre, post = epilogue
        tiles = {}
        for n in range(n_items + EPILOGUE_LAG):
            if n < n_items:
                tiles[n] = produce(n)
            if n >= EPILOGUE_LAG:
                m = n - EPILOGUE_LAG
                tile = tiles.pop(m)
                store(m, post(tile, pre(tile) if pre is not None else None, gain_refs[m // per_block][...]))

    def phase_rows(src_ref, slab, t, d):
        r, m0 = divmod(t * BAND, s // d)
        return src_ref.at[slab][pl.ds(m0 * d + r, BAND, stride=d), :]

    fn = fn_ref[j]

    def chunk(n):
        h, i = divmod(n, n_chunks)
        return _dot(xn_ref[i * ROWS:(i + 1) * ROWS, :], w_refs[h][...])

    def out_cols(h):
        return slice(h * TN, (h + 1) * TN)

    def direct(code, epilogue, live=PAIR, first=False):
        @pl.when(fn == code)
        def _():
            if first:
                normalise_rows()

            def store(n, val):
                h, i = divmod(n, n_chunks)
                o_ref[0, i * ROWS:(i + 1) * ROWS, out_cols(h)] = val.astype(o_ref.dtype)

            pipeline(live * n_chunks, n_chunks, chunk, epilogue, store)
            if live < PAIR:
                o_ref[0, :, live * TN:] = jnp.zeros((s, (PAIR - live) * TN), o_ref.dtype)

    def phase_major(code, epilogue, d):
        @pl.when(fn == code)
        def _():
            def stage(n, val):
                h, i = divmod(n, n_chunks)
                for c in range(lanes_per_block):
                    acc_ref[h * lanes_per_block + c, i * ROWS:(i + 1) * ROWS, :] = val[:, c * LANE:(c + 1) * LANE]

            pipeline(PAIR * n_chunks, n_chunks, chunk, epilogue, stage)
            src_ref, stride = acc_ref, d
            if d > SINGLE_OP_STRIDE:
                assert d == SINGLE_OP_STRIDE * SINGLE_OP_STRIDE
                for slab in range(PAIR * lanes_per_block):
                    for t in range(s // BAND):
                        acc2_ref[slab, t * BAND:(t + 1) * BAND, :] = phase_rows(acc_ref, slab, t, SINGLE_OP_STRIDE)
                src_ref, stride = acc2_ref, SINGLE_OP_STRIDE
            for h in range(PAIR):
                for t in range(s // BAND):
                    a = jnp.concatenate([phase_rows(src_ref, h * lanes_per_block + c, t, stride)
                                         for c in range(lanes_per_block)], axis=1)
                    o_ref[0, t * BAND:(t + 1) * BAND, out_cols(h)] = a.astype(o_ref.dtype)

    direct(PLAIN, plain)
    direct(NORM, norm)
    direct(SILU, silu)
    direct(SIGMOID, sigmoid)
    direct(SIGMOID_FIRST, sigmoid, first=True)
    direct(KNORM_VPLAIN, knorm_vplain, live=1)
    phase_major(NORM_D4, norm, DIL_PAIRS[1][1])
    phase_major(PLAIN_D4, plain, DIL_PAIRS[1][1])
    phase_major(NORM_D16, norm, DIL_PAIRS[2][1])
    phase_major(PLAIN_D16, plain, DIL_PAIRS[2][1])


def _w_index(bi, j, src_ref, fn_ref, *, layer, slot):
    return layer, 0, src_ref[PAIR * j + slot]


def _gain_index(bi, j, src_ref, fn_ref, *, slot):
    return 0, src_ref[PAIR * j + slot]


def _project(x, ln_g, w_in, gain_row, e_mat, src, fn, layer):
    b, s, d = x.shape
    n_in = PAIR
    grid_spec = pltpu.PrefetchScalarGridSpec(
        num_scalar_prefetch=2,
        grid=(b, N_STEPS),
        in_specs=[
            pl.BlockSpec((1, s, d), lambda bi, j, src_ref, fn_ref: (jnp.minimum(bi + jnp.minimum(j, 1), b - 1), 0, 0)),
            pl.BlockSpec((None, 1, d), lambda bi, j, src_ref, fn_ref: (layer, 0, 0)),
        ] + [pl.BlockSpec((None, d, TN), functools.partial(_w_index, layer=layer, slot=n)) for n in range(n_in)]
          + [pl.BlockSpec((1, TN), functools.partial(_gain_index, slot=n)) for n in range(n_in)]
          + [pl.BlockSpec((TN, TN), lambda bi, j, src_ref, fn_ref: (0, 0))],
        out_specs=pl.BlockSpec((1, s, n_in * TN), lambda bi, j, src_ref, fn_ref: (bi, 0, j)),
        scratch_shapes=[pltpu.VMEM((s, d), ACT), pltpu.VMEM((PAIR * TN // LANE, s, LANE), F32),
                        pltpu.VMEM((PAIR * TN // LANE, s, LANE), F32)],
    )
    return pl.pallas_call(
        _proj_kernel,
        grid_spec=grid_spec,
        out_shape=jax.ShapeDtypeStruct((b, s, P_W), ACT),
        compiler_params=_cparams(2),
        name="proj",
    )(src, fn, x, ln_g, *([w_in] * n_in), *([gain_row] * n_in), e_mat)


def _toeplitz(lo, hi, n):
    x = jnp.concatenate([jnp.broadcast_to(lo, (n, n)), jnp.broadcast_to(hi, (n, n))], axis=1)
    return pltpu.roll(x, 0, 1, stride=1, stride_axis=0)[:, n:]


def _unit_rms_heads(qt):
    halves = []
    for sh in range(2):
        x = qt[HEAD_DIM * sh:HEAD_DIM * (sh + 1), :].astype(F32)
        halves.append(x * lax.rsqrt(jnp.mean(x * x, axis=0, keepdims=True) + EPS))
    return jnp.concatenate(halves, axis=0).astype(qt.dtype)


def _head_rows(qt, lo):
    row = lax.broadcasted_iota(jnp.int32, qt.shape, 0)
    return jnp.where((row >= lo) & (row < lo + HEAD_DIM), qt, jnp.zeros_like(qt))


def _moba_kernel(q_ref, k_ref, v_ref, sg_ref, tv_ref, o_ref,
                 bt_ref, qts_ref, vt_ref, km_ref, sel_ref, m_ref, acc_ref):
    hp = pl.program_id(0)
    b = pl.program_id(1)
    blk = MOBA_BLOCK
    n_blk = k_ref.shape[1] // blk

    def rows_of(i):
        return pl.ds(i * blk if isinstance(i, int) else pl.multiple_of(i * blk, blk), blk)

    @pl.when(b == 0)
    def _():
        kr = lax.broadcasted_iota(jnp.int32, (blk, blk), 0)
        qc = lax.broadcasted_iota(jnp.int32, (blk, blk), 1)
        for sh in range(2):
            def body(o, c, sh=sh):
                t = _toeplitz(tv_ref[2 * hp + sh, o], tv_ref[2 * hp + sh, o + 1], blk)
                bt_ref[sh, o] = jnp.where((qc >= kr) | (o > 0), t, NEG)
                return c
            lax.fori_loop(0, n_blk, body, 0)

    for i in range(n_blk):
        vt = v_ref[0, rows_of(i), :].T
        for sh in range(2):
            vt_ref[i, sh] = jnp.concatenate(
                [vt[HEAD_DIM * sh:HEAD_DIM * (sh + 1), :], jnp.ones((ONES_ROWS, blk), ACT)], axis=0)
        km_ref[i:i + 1, :] = jnp.mean(k_ref[0, rows_of(i), :].astype(F32), axis=0, keepdims=True)
        qt = _unit_rms_heads(q_ref[0, rows_of(i), :].T)
        for sh in range(2):
            qts_ref[i, sh] = _head_rows(qt, HEAD_DIM * sh)

    km = km_ref[...]
    km0 = km.astype(ACT).astype(F32)
    km1 = (km - km0).astype(ACT).astype(F32)
    km2 = km - km0 - km1
    km_terms = jnp.concatenate([km0, km1, km2], axis=0).astype(ACT)
    blk_row = lax.broadcasted_iota(jnp.int32, (n_blk, blk), 0)

    ranked = [(qi, sh) for qi in range(MOBA_TOPK + 1, n_blk) for sh in range(2)]
    gates = [_dot(km_terms, qts_ref[qi, sh]) for qi, sh in ranked]
    for (qi, sh), g3 in zip(ranked, gates):
        g = g3[:n_blk] + g3[n_blk:2 * n_blk] + g3[2 * n_blk:]
        g = jnp.where(blk_row < qi, g, NEG)
        sel_rows = []
        for jb in range(qi):
            gj = g[jb:jb + 1, :]
            beats = (g > gj) | ((g == gj) & (blk_row < jb))
            cnt = jnp.sum(beats.astype(F32), axis=0, keepdims=True)
            sel_rows.append(jnp.where(cnt < MOBA_TOPK, 1.0, 0.0))
        sel_rows.append(jnp.zeros((n_blk - qi, blk), F32))
        sel_ref[qi, sh] = jnp.concatenate(sel_rows, axis=0)

    def tiles(off, qis):
        k_blks = [k_ref[0, rows_of(qi - off), :] for qi in qis]
        scores = [_dot(k_blk, qts_ref[qi, sh]) for qi, k_blk in zip(qis, k_blks) for sh in range(2)]
        work = [(qi, sh) for qi in qis for sh in range(2)]
        stats = []
        half = blk // 2
        for (qi, sh), st in zip(work, scores):
            if off == 0:
                bias = bt_ref[sh, 0]
                parts = [(st[:half, :half] + bias[:half, :half], slice(0, half)),
                         (st[:, half:] + bias[:, half:], slice(0, blk))]
                maxes = [jnp.max(s_part, axis=0, keepdims=True) for s_part, _ in parts]
                m_ref[qi, sh] = jnp.concatenate(maxes, axis=1)
                stats.append((None, [(keys, jnp.exp2(s_part - m_part).astype(ACT))
                                     for (s_part, keys), m_part in zip(parts, maxes)]))
                continue
            st = st + bt_ref[sh, off]
            tile_max = jnp.max(st, axis=0, keepdims=True)
            if qi <= MOBA_TOPK:
                m = m_ref[qi, sh]
                m_new = jnp.maximum(m, tile_max)
                p = jnp.exp2(st - m_new)
                alpha = jnp.exp2(m - m_new)
            else:
                sel = sel_ref[qi, sh, pl.ds(qi - off, 1), :] > 0.5
                m = m_ref[qi, sh]
                m_new = jnp.where(sel, jnp.maximum(m, tile_max), m)
                p = jnp.exp2(st - jnp.where(sel, m_new, -NEG))
                alpha = jnp.exp2(m - m_new)
            m_ref[qi, sh] = m_new
            stats.append((alpha, [(slice(0, blk), p.astype(ACT))]))
        pvs = [jnp.concatenate([_dot(vt_ref[qi - off, sh, :, keys], p) for keys, p in parts], axis=1)
               for (qi, sh), (_, parts) in zip(work, stats)]
        for (qi, sh), (alpha, _), pv in zip(work, stats, pvs):
            acc_ref[qi, sh] = pv if off == 0 else alpha * acc_ref[qi, sh] + pv

    def finish(qi):
        ot = jnp.concatenate([acc_ref[qi, sh, :HEAD_DIM, :] * (1.0 / acc_ref[qi, sh, HEAD_DIM:HEAD_DIM + 1, :])
                              for sh in range(2)], axis=0)
        o_ref[0, rows_of(qi), :] = (ot.T * sg_ref[0, rows_of(qi), :].astype(F32)).astype(o_ref.dtype)

    for off in range(n_blk):
        qis = list(range(off, n_blk))
        for lo in range(0, len(qis), MOBA_GROUP):
            tiles(off, qis[lo:lo + MOBA_GROUP])
        finish(off)


def _moba(p, tv_a):
    b, s, _ = p.shape
    blk = MOBA_BLOCK
    n_blk = s // blk
    col = lambda c: pl.BlockSpec((1, s, LANE), lambda hp, bi: (bi, 0, COL_A + c * N_PAIRS + hp))
    return pl.pallas_call(
        _moba_kernel,
        grid=(N_PAIRS, b),
        in_specs=[col(0), col(1), col(2), col(3), pl.BlockSpec(tv_a.shape, lambda hp, bi: (0, 0, 0, 0))],
        out_specs=pl.BlockSpec((1, s, LANE), lambda hp, bi: (bi, 0, hp)),
        out_shape=jax.ShapeDtypeStruct((b, s, W_BR), ACT),
        scratch_shapes=[
            pltpu.VMEM((2, n_blk, blk, blk), F32),
            pltpu.VMEM((n_blk, 2, LANE, blk), ACT),
            pltpu.VMEM((n_blk, 2, HEAD_DIM + ONES_ROWS, blk), ACT),
            pltpu.VMEM((n_blk, LANE), F32),
            pltpu.VMEM((n_blk, 2, n_blk, blk), F32),
            pltpu.VMEM((n_blk, 2, 1, blk), F32),
            pltpu.VMEM((n_blk, 2, HEAD_DIM + ONES_ROWS, blk), F32),
        ],
        compiler_params=_cparams(2),
        name="moba",
    )(p, p, p, p, tv_a)


def _band_attend(items):
    def fuse(blocks):
        if len(blocks) > 1 and all(valid is None for _, _, _, valid in blocks):
            ks, vts, biases, _ = zip(*blocks)
            return [(jnp.concatenate(ks, axis=0), jnp.concatenate(vts, axis=1), jnp.concatenate(biases, axis=0), None)]
        return blocks

    items = [(qs, fuse(blocks), sink) for qs, blocks, sink in items]

    probs = []
    for qs, blocks, sink in items:
        sc = [_dot(k, qs) for k, _, _, _ in blocks]
        ss = [s + bias if valid is None else jnp.where(valid, s + bias, NEG)
              for s, (_, _, bias, valid) in zip(sc, blocks)]
        m = functools.reduce(jnp.maximum, [jnp.max(s, axis=0, keepdims=True) for s in ss])
        if sink is not None:
            m = jnp.maximum(m, sink)
        probs.append((m, [jnp.exp2(s - m).astype(ACT) for s in ss]))
    outs = []
    for (_, blocks, sink), (m, ps) in zip(items, probs):
        n_v = blocks[0][1].shape[0]
        acc = functools.reduce(operator.add, [
            _dot(jnp.concatenate([vt, jnp.ones((ONES_ROWS, vt.shape[1]), ACT)], axis=0), p)
            for (_, vt, _, _), p in zip(blocks, ps)])
        l = acc[n_v:n_v + 1, :]
        if sink is not None:
            l = l + jnp.exp2(sink - m)
        outs.append((acc[:n_v, :] * (1.0 / l), m + jnp.log2(l)))
    return outs


def _band_bias(tv_ref, head, prev_includes_window_edge):
    kr = lax.broadcasted_iota(jnp.int32, (BAND, BAND), 0)
    qc = lax.broadcasted_iota(jnp.int32, (BAND, BAND), 1)
    own, prev = [_toeplitz(tv_ref[head, which], tv_ref[head, which + 1], BAND) for which in range(2)]
    prev_mask = (qc <= kr) if prev_includes_window_edge else (qc < kr)
    return jnp.where(qc >= kr, own, NEG), jnp.where(prev_mask, prev, NEG)


def _dil_kernel(q0_ref, k0_ref, v0_ref, q1_ref, k1_ref, v1_ref, q2_ref, k2_ref, v2_ref, sg_ref, tv_ref,
                o_ref, og_ref, lg_ref, bt_ref):
    hp = pl.program_id(0)
    b = pl.program_id(1)
    s = q0_ref.shape[1]
    n_tiles = s // BAND

    @pl.when(b == 0)
    def _():
        for g in range(N_DIL):
            for sh in range(2):
                own, prev = _band_bias(tv_ref, g * N_HEADS + 2 * hp + sh, True)
                bt_ref[g, sh, 0] = own
                bt_ref[g, sh, 1] = prev

    groups = ((q0_ref, k0_ref, v0_ref), (q1_ref, k1_ref, v1_ref), (q2_ref, k2_ref, v2_ref))

    def mix(rows, o0, l0):
        l1, l2 = lg_ref[0, rows, :], lg_ref[1, rows, :]
        m = jnp.maximum(l0, jnp.maximum(l1, l2))
        e0, e1, e2 = jnp.exp2(l0 - m), jnp.exp2(l1 - m), jnp.exp2(l2 - m)
        ob = (e0 * o0 + e1 * og_ref[0, rows, :] + e2 * og_ref[1, rows, :]) / (e0 + e1 + e2)
        o_ref[0, rows, :] = (ob * sg_ref[0, rows, :].astype(F32)).astype(o_ref.dtype)

    for g in (*range(1, N_DIL), 0):
        q_ref, k_ref, v_ref = groups[g]
        d = DIL_PAIRS[g][1]
        assert (d == 1) == (g == 0)
        per_phase = n_tiles // d
        has_prev = per_phase > 1
        assert per_phase % TILE_GROUP == 0 or TILE_GROUP % per_phase == 0

        def body(i, c, g=g, d=d, per_phase=per_phase, has_prev=has_prev, q_ref=q_ref, k_ref=k_ref, v_ref=v_ref):
            items, dsts = [], []
            k_own = vt_own = None
            for n in range(TILE_GROUP):
                u = i * TILE_GROUP + n
                rows = pl.ds(pl.multiple_of(u * BAND, BAND), BAND)
                t = lax.rem(u, per_phase)
                r = lax.div(u, per_phase)
                qt = _unit_rms_heads(q_ref[0, rows, :].T)
                k_prev, vt_prev = k_own, vt_own
                k_own = k_ref[0, rows, :]
                vt_own = v_ref[0, rows, :].T
                if has_prev:
                    phase_start_known = TILE_GROUP % per_phase == 0
                    prev_exists = (t > 0) if (n == 0 and not phase_start_known) else None
                    if phase_start_known and n % per_phase == 0:
                        k_prev = None
                    elif n == 0:
                        prow = pl.ds(pl.multiple_of(jnp.maximum(u - 1, 0) * BAND, BAND), BAND)
                        k_prev = k_ref[0, prow, :]
                        vt_prev = v_ref[0, prow, :].T
                for sh in range(2):
                    hrows = slice(HEAD_DIM * sh, HEAD_DIM * (sh + 1))
                    blocks = [(k_own, vt_own[hrows, :], bt_ref[g, sh, 0], None)]
                    if has_prev and k_prev is not None:
                        blocks.append((k_prev, vt_prev[hrows, :], bt_ref[g, sh, 1], prev_exists))
                    items.append((_head_rows(qt, HEAD_DIM * sh), blocks, None))
                dsts.append(rows if d == 1 else pl.ds(t * (BAND * d) + r, BAND, stride=d))
            res = _band_attend(items)
            for n, dst in enumerate(dsts):
                (o0, lse0), (o1, lse1) = res[2 * n], res[2 * n + 1]
                out = jnp.concatenate([o0, o1], axis=0).T
                lse = jnp.concatenate(
                    [jnp.broadcast_to(lse0, (HEAD_DIM, BAND)), jnp.broadcast_to(lse1, (HEAD_DIM, BAND))], axis=0).T
                if g == 0:
                    mix(dst, out, lse)
                else:
                    og_ref.at[g - 1][dst, :] = out
                    lg_ref.at[g - 1][dst, :] = lse
            return c

        lax.fori_loop(0, n_tiles // TILE_GROUP, body, 0)


def _dilated(p, tv_b):
    b, s, _ = p.shape
    col = lambda c: pl.BlockSpec((1, s, LANE), lambda hp, bi: (bi, 0, c + hp))
    qkv = [col(COL_B + 3 * N_PAIRS * g + N_PAIRS * i) for g in range(N_DIL) for i in range(3)]
    return pl.pallas_call(
        _dil_kernel,
        grid=(N_PAIRS, b),
        in_specs=qkv + [col(COL_SGB), pl.BlockSpec(tv_b.shape, lambda hp, bi: (0, 0, 0, 0))],
        out_specs=pl.BlockSpec((1, s, LANE), lambda hp, bi: (bi, 0, hp)),
        out_shape=jax.ShapeDtypeStruct((b, s, W_BR), ACT),
        scratch_shapes=[
            pltpu.VMEM((N_DIL - 1, s, LANE), F32),
            pltpu.VMEM((N_DIL - 1, s, LANE), F32),
            pltpu.VMEM((N_DIL, 2, 2, BAND, BAND), F32),
        ],
        compiler_params=_cparams(2),
        name="dilated",
    )(*([p] * 10), tv_b)


def _swa_kernel(sink_ref, q_ref, sg_ref, k_ref, v_ref, tv_ref, o_ref, bt_ref, vts_ref):
    hp = pl.program_id(0)
    b = pl.program_id(1)
    n_tiles = q_ref.shape[1] // BAND
    pairs_per_kv = N_PAIRS // KV_C

    @pl.when(b == 0)
    def _():
        for sh in range(2):
            own, prev = _band_bias(tv_ref, 2 * hp + sh, False)
            bt_ref[sh, 0] = own
            bt_ref[sh, 1] = prev

    kv_lo = HEAD_DIM * (hp // pairs_per_kv)
    row = lax.broadcasted_iota(jnp.int32, (LANE, BAND), 0)
    kv_rows = (row >= kv_lo) & (row < kv_lo + HEAD_DIM)

    def kv_vt(rows, slot):
        vts_ref[slot] = v_ref[0, rows, :].T
        return vts_ref[slot, pl.ds(pl.multiple_of(kv_lo, HEAD_DIM), HEAD_DIM), :]

    def body(i, c):
        items, dsts = [], []
        zero = jnp.zeros((LANE, BAND), ACT)
        k_own = vt_own = None
        for n in range(TILE_GROUP):
            t = i * TILE_GROUP + n
            rows = pl.ds(pl.multiple_of(t * BAND, BAND), BAND)
            qt = _unit_rms_heads(q_ref[0, rows, :].T)
            k_prev, vt_prev = k_own, vt_own
            k_own = k_ref[0, rows, :]
            vt_own = kv_vt(rows, n)
            if n == 0:
                prow = pl.ds(pl.multiple_of(jnp.maximum(t - 1, 0) * BAND, BAND), BAND)
                k_prev = k_ref[0, prow, :]
                vt_prev = kv_vt(prow, TILE_GROUP)
            prev_exists = (t > 0) if n == 0 else None
            for sh in range(2):
                qh = qt[HEAD_DIM * sh:HEAD_DIM * (sh + 1), :]
                qs = jnp.where(kv_rows, jnp.concatenate([qh, qh], axis=0), zero)
                items.append((qs, [(k_own, vt_own, bt_ref[sh, 0], None), (k_prev, vt_prev, bt_ref[sh, 1], prev_exists)],
                              sink_ref[2 * hp + sh]))
            dsts.append(rows)
        res = _band_attend(items)
        for n, rows in enumerate(dsts):
            ot = jnp.concatenate([o for o, _ in res[2 * n:2 * n + 2]], axis=0).T
            o_ref[0, rows, :] = (ot * sg_ref[0, rows, :].astype(F32)).astype(o_ref.dtype)
        return c

    lax.fori_loop(0, n_tiles // TILE_GROUP, body, 0)


def _swa(p, tv_c, sinks):
    b, s, _ = p.shape
    grid_spec = pltpu.PrefetchScalarGridSpec(
        num_scalar_prefetch=1,
        grid=(N_PAIRS, b),
        in_specs=[
            pl.BlockSpec((1, s, LANE), lambda hp, bi, sk: (bi, 0, COL_C + hp)),
            pl.BlockSpec((1, s, LANE), lambda hp, bi, sk: (bi, 0, COL_C + N_PAIRS + hp)),
            pl.BlockSpec((1, s, LANE), lambda hp, bi, sk: (bi, 0, COL_C + 2 * N_PAIRS)),
            pl.BlockSpec((1, s, LANE), lambda hp, bi, sk: (bi, 0, COL_C + 2 * N_PAIRS + 1)),
            pl.BlockSpec(tv_c.shape, lambda hp, bi, sk: (0, 0, 0, 0)),
        ],
        out_specs=pl.BlockSpec((1, s, LANE), lambda hp, bi, sk: (bi, 0, hp)),
        scratch_shapes=[pltpu.VMEM((2, 2, BAND, BAND), F32),
                        pltpu.VMEM((TILE_GROUP + 1, LANE, BAND), ACT)],
    )
    return pl.pallas_call(
        _swa_kernel,
        grid_spec=grid_spec,
        out_shape=jax.ShapeDtypeStruct((b, s, W_BR), ACT),
        compiler_params=_cparams(2),
        name="swa",
    )(sinks, p, p, p, p, tv_c)


def _merge_kernel(bra_ref, brb_ref, brc_ref, g_ref, wb_ref, wo_ref, x_ref, out_ref):
    merged = None
    for i, br_ref in enumerate((bra_ref, brb_ref, brc_ref)):
        term = g_ref[:, i * D_MODEL:(i + 1) * D_MODEL].astype(F32) * _dot(br_ref[...], wb_ref[i])
        merged = term if merged is None else merged + term
    out_ref[...] = x_ref[...] + _dot(merged.astype(ACT), wo_ref[...])


def _merge(bra, brb, brc, p2, w_br, w_out, x2, layer, ts=1024):
    n = x2.shape[0]
    row = lambda w: pl.BlockSpec((ts, w), lambda i: (i, 0))
    return pl.pallas_call(
        _merge_kernel,
        grid=(n // ts,),
        in_specs=[row(W_BR), row(W_BR), row(W_BR), row(GATE_W),
                  pl.BlockSpec((None,) + w_br.shape[1:], lambda i: (layer, 0, 0, 0)),
                  pl.BlockSpec((None,) + w_out.shape[1:], lambda i: (layer, 0, 0)),
                  row(D_MODEL)],
        out_specs=row(D_MODEL),
        out_shape=jax.ShapeDtypeStruct(x2.shape, F32),
        compiler_params=_cparams(1),
        name="merge",
    )(bra, brb, brc, p2, w_br, w_out, x2)


def _rel_bucket(dist):
    dist = jnp.maximum(dist, 0)
    max_exact = N_BUCKETS // 2
    log_ratio = jnp.log(jnp.maximum(dist, 1).astype(F32) / max_exact) / math.log(REL_MAX_DIST / max_exact)
    large = jnp.minimum(max_exact + (log_ratio * (N_BUCKETS - max_exact)).astype(jnp.int32), N_BUCKETS - 1)
    return jnp.where(dist < max_exact, dist, large)


def _bias_by_distance(tab, dist, chunk):
    vals = tab.astype(F32)[:, _rel_bucket(dist)] * LOG2E
    vals = jnp.pad(vals, ((0, 0), (chunk, 0)))
    return vals.reshape(tab.shape[0], -1, 1, chunk)


def _gain_rows(qk_g):
    depth = qk_g.shape[0]
    scale = HEAD_DIM ** -0.5 * LOG2E
    one = jnp.ones((depth, HEAD_DIM), F32)
    tile = lambda v, n: jnp.tile(v, (1, n))
    parts = [tile(one, N_HEADS), tile(qk_g[:, 1] * qk_g[:, 0] * scale, N_HEADS), tile(one, 2 * N_HEADS),
             tile(one, N_DIL * N_HEADS), tile(qk_g[:, 3] * qk_g[:, 2] * scale, N_DIL * N_HEADS),
             tile(one, (N_DIL + 1) * N_HEADS),
             tile(one, N_HEADS), tile(qk_g[:, 5] * qk_g[:, 4] * scale, KV_C), tile(one, KV_C + N_HEADS),
             tile(one, GATE_W // HEAD_DIM)]
    return jnp.concatenate(parts, axis=1)


def kernel(x, ln_g, w_in, qk_g, sinks, w_branch, w_out, rel_bias):
    b, s, d = x.shape
    depth = ln_g.shape[0]
    assert d == D_MODEL and s % DIL_PAIRS[-1][0] == 0 and w_in.shape[-1] == C_IN

    tv_a = _bias_by_distance(rel_bias[:N_HEADS], jnp.arange(s), MOBA_BLOCK)
    tv_b = jnp.concatenate(
        [_bias_by_distance(rel_bias[OFF_B + g * N_HEADS:OFF_B + (g + 1) * N_HEADS], dd * jnp.arange(2 * BAND), BAND)
         for g, (_, dd) in enumerate(DIL_PAIRS)], axis=0)
    tv_c = _bias_by_distance(rel_bias[OFF_C:OFF_C + N_HEADS], jnp.arange(2 * BAND), BAND)
    seg = jnp.arange(TN) // HEAD_DIM
    e_mat = (seg[:, None] == seg[None, :]).astype(ACT)
    src = jnp.asarray([c for c, _ in _LAYOUT], jnp.int32)
    fn = jnp.asarray([f for _, f in _LAYOUT[::PAIR]], jnp.int32)
    gains = _gain_rows(qk_g)

    w_in_b = w_in.astype(ACT)
    w_br_b = w_branch.astype(ACT)
    w_out_b = w_out.astype(ACT)

    for l in range(depth):
        p = _project(x, ln_g[:, None, :], w_in_b, gains[l][None, :], e_mat, src, fn, l)
        bra = _moba(p, tv_a)
        brb = _dilated(p, tv_b)
        brc = _swa(p, tv_c, sinks[l] * LOG2E)
        x2 = _merge(bra.reshape(b * s, W_BR), brb.reshape(b * s, W_BR), brc.reshape(b * s, W_BR),
                    p.reshape(b * s, P_W), w_br_b, w_out_b, x.reshape(b * s, d), l)
        x = x2.reshape(b, s, d)
    return x
```

```python
import functools
import math
import operator

import jax
import jax.numpy as jnp
from jax import lax
from jax.experimental import pallas as pl
from jax.experimental.pallas import tpu as pltpu

D_MODEL = 1024
HEAD_DIM = 64
W_BR = D_MODEL // 2
N_HEADS = W_BR // HEAD_DIM
N_PAIRS = N_HEADS // 2
MOBA_BLOCK = 256
MOBA_TOPK = 3
DIL_PAIRS = ((128, 1), (512, 4), (2048, 16))
N_DIL = len(DIL_PAIRS)
KV_C = 2
SWA_WINDOW = 128
N_BUCKETS = 32
REL_MAX_DIST = 2048
OFF_B = N_HEADS
OFF_C = N_HEADS + N_DIL * N_HEADS
EPS = 1e-6
NEG = -1e30
LOG2E = math.log2(math.e)

LANE = 128
BAND = SWA_WINDOW
assert all(w == BAND * d for w, d in DIL_PAIRS)
TN = 256
ROWS = 256
TILE_GROUP = 16
MOBA_GROUP = 8
EPILOGUE_LAG = 2
SINGLE_OP_STRIDE = 4
ONES_ROWS = 16
VMEM_LIMIT = 56 * 1024 * 1024

ACT = jnp.bfloat16
F32 = jnp.float32

PLAIN, NORM, SILU, SIGMOID, KNORM_VPLAIN, NORM_D4, PLAIN_D4, NORM_D16, PLAIN_D16, SIGMOID_FIRST = range(10)
_B_GROUP = lambda g, norm, plain: [(8 + 2 * g, plain), (9 + 2 * g, plain), (14 + 2 * g, norm), (15 + 2 * g, norm),
                                   (20 + 2 * g, plain), (21 + 2 * g, plain)]
_LAYOUT = ([(33, SIGMOID_FIRST), (34, SIGMOID_FIRST)] + [(j, SIGMOID) for j in range(35, 45)]
           + [(0, PLAIN), (1, PLAIN), (2, NORM), (3, NORM), (4, PLAIN), (5, PLAIN), (6, SILU), (7, SILU)]
           + _B_GROUP(0, NORM, PLAIN) + _B_GROUP(1, NORM_D4, PLAIN_D4) + _B_GROUP(2, NORM_D16, PLAIN_D16)
           + [(26, SILU), (27, SILU)]
           + [(28, PLAIN), (29, PLAIN), (31, SILU), (32, SILU), (30, KNORM_VPLAIN)])
C_IN = len(_LAYOUT) * TN
PAIR = 2
_LAYOUT = _LAYOUT + [(_LAYOUT[-1][0], KNORM_VPLAIN)] * (-len(_LAYOUT) % PAIR)
assert all(_LAYOUT[j][1] == _LAYOUT[j - j % PAIR][1] for j in range(len(_LAYOUT)))
assert [f for _, f in _LAYOUT].count(KNORM_VPLAIN) == PAIR
N_STEPS = len(_LAYOUT) // PAIR
P_W = len(_LAYOUT) * TN
COL_GATE, COL_A, COL_B, COL_SGB, COL_C = 0, 24, 40, 76, 80
GATE_W = 3 * D_MODEL
assert COL_GATE == 0


def _cparams(n_axes):
    return pltpu.CompilerParams(dimension_semantics=("arbitrary",) * n_axes, vmem_limit_bytes=VMEM_LIMIT)


def _dot(a, b):
    return jnp.dot(a, b, preferred_element_type=F32)


def _sigmoid(a):
    return 0.5 * jnp.tanh(0.5 * a) + 0.5


def _proj_kernel(src_ref, fn_ref, x_ref, lng_ref, *refs):
    del src_ref
    w_refs, gain_refs = refs[:PAIR], refs[PAIR:2 * PAIR]
    e_ref, o_ref, xn_ref, acc_ref, acc2_ref = refs[2 * PAIR:]
    j = pl.program_id(1)
    s = x_ref.shape[1]
    n_chunks = s // ROWS
    lanes_per_block = TN // LANE

    def normalise_rows():
        for i in range(n_chunks):
            xv = x_ref[0, i * ROWS:(i + 1) * ROWS, :]
            ms = jnp.mean(xv * xv, axis=-1, keepdims=True)
            xn_ref[i * ROWS:(i + 1) * ROWS, :] = (xv * lax.rsqrt(ms + EPS) * lng_ref[...]).astype(ACT)

    def mean_square(a):
        return _dot((a * a).astype(ACT), e_ref[...])

    def head_norm(a, ss, gain):
        return a * lax.rsqrt(ss * (1.0 / HEAD_DIM) + EPS) * gain

    def k_norm_v_plain(a, ss, gain):
        lane = lax.broadcasted_iota(jnp.int32, a.shape, 1)
        return jnp.where(lane < KV_C * HEAD_DIM, head_norm(a, ss, gain), a)

    plain = (None, lambda a, _, gain: a)
    silu = (None, lambda a, _, gain: a * _sigmoid(a))
    sigmoid = (None, lambda a, _, gain: _sigmoid(a))
    norm = (mean_square, head_norm)
    knorm_vplain = (mean_square, k_norm_v_plain)

    def pipeline(n_items, per_block, produce, epilogue, store):
        pre, post = epilogue
        tiles = {}
        for n in range(n_items + EPILOGUE_LAG):
            if n < n_items:
                tiles[n] = produce(n)
            if n >= EPILOGUE_LAG:
                m = n - EPILOGUE_LAG
                tile = tiles.pop(m)
                store(m, post(tile, pre(tile) if pre is not None else None, gain_refs[m // per_block][...]))

    def phase_rows(src_ref, slab, t, d):
        r, m0 = divmod(t * BAND, s // d)
        return src_ref.at[slab][pl.ds(m0 * d + r, BAND, stride=d), :]

    fn = fn_ref[j]

    def chunk(n):
        h, i = divmod(n, n_chunks)
        return _dot(xn_ref[i * ROWS:(i + 1) * ROWS, :], w_refs[h][...])

    def out_cols(h):
        return slice(h * TN, (h + 1) * TN)

    def direct(code, epilogue, live=PAIR, first=False):
        @pl.when(fn == code)
        def _():
            if first:
                normalise_rows()

            def store(n, val):
                h, i = divmod(n, n_chunks)
                o_ref[0, i * ROWS:(i + 1) * ROWS, out_cols(h)] = val.astype(o_ref.dtype)

            pipeline(live * n_chunks, n_chunks, chunk, epilogue, store)
            if live < PAIR:
                o_ref[0, :, live * TN:] = jnp.zeros((s, (PAIR - live) * TN), o_ref.dtype)

    def phase_major(code, epilogue, d):
        @pl.when(fn == code)
        def _():
            def stage(n, val):
                h, i = divmod(n, n_chunks)
                for c in range(lanes_per_block):
                    acc_ref[h * lanes_per_block + c, i * ROWS:(i + 1) * ROWS, :] = val[:, c * LANE:(c + 1) * LANE]

            pipeline(PAIR * n_chunks, n_chunks, chunk, epilogue, stage)
            src_ref, stride = acc_ref, d
            if d > SINGLE_OP_STRIDE:
                assert d == SINGLE_OP_STRIDE * SINGLE_OP_STRIDE
                for slab in range(PAIR * lanes_per_block):
                    for t in range(s // BAND):
                        acc2_ref[slab, t * BAND:(t + 1) * BAND, :] = phase_rows(acc_ref, slab, t, SINGLE_OP_STRIDE)
                src_ref, stride = acc2_ref, SINGLE_OP_STRIDE
            for h in range(PAIR):
                for t in range(s // BAND):
                    a = jnp.concatenate([phase_rows(src_ref, h * lanes_per_block + c, t, stride)
                                         for c in range(lanes_per_block)], axis=1)
                    o_ref[0, t * BAND:(t + 1) * BAND, out_cols(h)] = a.astype(o_ref.dtype)

    direct(PLAIN, plain)
    direct(NORM, norm)
    direct(SILU, silu)
    direct(SIGMOID, sigmoid)
    direct(SIGMOID_FIRST, sigmoid, first=True)
    direct(KNORM_VPLAIN, knorm_vplain, live=1)
    phase_major(NORM_D4, norm, DIL_PAIRS[1][1])
    phase_major(PLAIN_D4, plain, DIL_PAIRS[1][1])
    phase_major(NORM_D16, norm, DIL_PAIRS[2][1])
    phase_major(PLAIN_D16, plain, DIL_PAIRS[2][1])


def _w_index(bi, j, src_ref, fn_ref, *, layer, slot):
    return layer, 0, src_ref[PAIR * j + slot]


def _gain_index(bi, j, src_ref, fn_ref, *, slot):
    return 0, src_ref[PAIR * j + slot]


def _project(x, ln_g, w_in, gain_row, e_mat, src, fn, layer):
    b, s, d = x.shape
    n_in = PAIR
    grid_spec = pltpu.PrefetchScalarGridSpec(
        num_scalar_prefetch=2,
        grid=(b, N_STEPS),
        in_specs=[
            pl.BlockSpec((1, s, d), lambda bi, j, src_ref, fn_ref: (jnp.minimum(bi + jnp.minimum(j, 1), b - 1), 0, 0)),
            pl.BlockSpec((None, 1, d), lambda bi, j, src_ref, fn_ref: (layer, 0, 0)),
        ] + [pl.BlockSpec((None, d, TN), functools.partial(_w_index, layer=layer, slot=n)) for n in range(n_in)]
          + [pl.BlockSpec((1, TN), functools.partial(_gain_index, slot=n)) for n in range(n_in)]
          + [pl.BlockSpec((TN, TN), lambda bi, j, src_ref, fn_ref: (0, 0))],
        out_specs=pl.BlockSpec((1, s, n_in * TN), lambda bi, j, src_ref, fn_ref: (bi, 0, j)),
        scratch_shapes=[pltpu.VMEM((s, d), ACT), pltpu.VMEM((PAIR * TN // LANE, s, LANE), F32),
                        pltpu.VMEM((PAIR * TN // LANE, s, LANE), F32)],
    )
    return pl.pallas_call(
        _proj_kernel,
        grid_spec=grid_spec,
        out_shape=jax.ShapeDtypeStruct((b, s, P_W), ACT),
        compiler_params=_cparams(2),
        name="proj",
    )(src, fn, x, ln_g, *([w_in] * n_in), *([gain_row] * n_in), e_mat)


def _toeplitz(lo, hi, n):
    x = jnp.concatenate([jnp.broadcast_to(lo, (n, n)), jnp.broadcast_to(hi, (n, n))], axis=1)
    return pltpu.roll(x, 0, 1, stride=1, stride_axis=0)[:, n:]


def _unit_rms_heads(qt):
    halves = []
    for sh in range(2):
        x = qt[HEAD_DIM * sh:HEAD_DIM * (sh + 1), :].astype(F32)
        halves.append(x * lax.rsqrt(jnp.mean(x * x, axis=0, keepdims=True) + EPS))
    return jnp.concatenate(halves, axis=0).astype(qt.dtype)


def _head_rows(qt, lo):
    row = lax.broadcasted_iota(jnp.int32, qt.shape, 0)
    return jnp.where((row >= lo) & (row < lo + HEAD_DIM), qt, jnp.zeros_like(qt))


def _moba_bias_kernel(tv_ref, o_ref):
    h = pl.program_id(0)
    blk = MOBA_BLOCK
    kr = lax.broadcasted_iota(jnp.int32, (blk, blk), 0)
    qc = lax.broadcasted_iota(jnp.int32, (blk, blk), 1)

    def body(o, c):
        t = _toeplitz(tv_ref[h, o], tv_ref[h, o + 1], blk)
        o_ref[0, o] = jnp.where((qc >= kr) | (o > 0), t, NEG)
        return c
    lax.fori_loop(0, o_ref.shape[1], body, 0)


def _moba_bias(tv_a):
    n_heads, n_chunks = tv_a.shape[:2]
    blk = MOBA_BLOCK
    return pl.pallas_call(
        _moba_bias_kernel,
        grid=(n_heads,),
        in_specs=[pl.BlockSpec(tv_a.shape, lambda h: (0, 0, 0, 0))],
        out_specs=pl.BlockSpec((1, n_chunks - 1, blk, blk), lambda h: (h, 0, 0, 0)),
        out_shape=jax.ShapeDtypeStruct((n_heads, n_chunks - 1, blk, blk), F32),
        compiler_params=_cparams(1),
        name="moba_bias",
    )(tv_a)


def _moba_kernel(q_ref, k_ref, v_ref, sg_ref, bt_ref, o_ref, qts_ref, vt_ref, km_ref, sel_ref, m_ref, acc_ref):
    blk = MOBA_BLOCK
    n_blk = k_ref.shape[1] // blk

    def rows_of(i):
        return pl.ds(i * blk if isinstance(i, int) else pl.multiple_of(i * blk, blk), blk)

    for i in range(n_blk):
        vt = v_ref[0, rows_of(i), :].T
        for sh in range(2):
            vt_ref[i, sh] = jnp.concatenate(
                [vt[HEAD_DIM * sh:HEAD_DIM * (sh + 1), :], jnp.ones((ONES_ROWS, blk), ACT)], axis=0)
        km_ref[i:i + 1, :] = jnp.mean(k_ref[0, rows_of(i), :].astype(F32), axis=0, keepdims=True)
        qt = _unit_rms_heads(q_ref[0, rows_of(i), :].T)
        for sh in range(2):
            qts_ref[i, sh] = _head_rows(qt, HEAD_DIM * sh)

    km = km_ref[...]
    km0 = km.astype(ACT).astype(F32)
    km1 = (km - km0).astype(ACT).astype(F32)
    km2 = km - km0 - km1
    km_terms = jnp.concatenate([km0, km1, km2], axis=0).astype(ACT)
    blk_row = lax.broadcasted_iota(jnp.int32, (n_blk, blk), 0)

    ranked = [(qi, sh) for qi in range(MOBA_TOPK + 1, n_blk) for sh in range(2)]
    gates = [_dot(km_terms, qts_ref[qi, sh]) for qi, sh in ranked]
    for (qi, sh), g3 in zip(ranked, gates):
        g = g3[:n_blk] + g3[n_blk:2 * n_blk] + g3[2 * n_blk:]
        g = jnp.where(blk_row < qi, g, NEG)
        sel_rows = []
        for jb in range(qi):
            gj = g[jb:jb + 1, :]
            beats = (g > gj) | ((g == gj) & (blk_row < jb))
            cnt = jnp.sum(beats.astype(F32), axis=0, keepdims=True)
            sel_rows.append(jnp.where(cnt < MOBA_TOPK, 1.0, 0.0))
        sel_rows.append(jnp.zeros((n_blk - qi, blk), F32))
        sel_ref[qi, sh] = jnp.concatenate(sel_rows, axis=0)

    def tiles(off, qis):
        k_blks = [k_ref[0, rows_of(qi - off), :] for qi in qis]
        scores = [_dot(k_blk, qts_ref[qi, sh]) for qi, k_blk in zip(qis, k_blks) for sh in range(2)]
        work = [(qi, sh) for qi in qis for sh in range(2)]
        stats = []
        half = blk // 2
        for (qi, sh), st in zip(work, scores):
            if off == 0:
                bias = bt_ref[sh, 0]
                parts = [(st[:half, :half] + bias[:half, :half], slice(0, half)),
                         (st[:, half:] + bias[:, half:], slice(0, blk))]
                maxes = [jnp.max(s_part, axis=0, keepdims=True) for s_part, _ in parts]
                m_ref[qi, sh] = jnp.concatenate(maxes, axis=1)
                stats.append((None, [(keys, jnp.exp2(s_part - m_part).astype(ACT))
                                     for (s_part, keys), m_part in zip(parts, maxes)]))
                continue
            st = st + bt_ref[sh, off]
            tile_max = jnp.max(st, axis=0, keepdims=True)
            if qi <= MOBA_TOPK:
                m = m_ref[qi, sh]
                m_new = jnp.maximum(m, tile_max)
                p = jnp.exp2(st - m_new)
                alpha = jnp.exp2(m - m_new)
            else:
                sel = sel_ref[qi, sh, pl.ds(qi - off, 1), :] > 0.5
                m = m_ref[qi, sh]
                m_new = jnp.where(sel, jnp.maximum(m, tile_max), m)
                p = jnp.exp2(st - jnp.where(sel, m_new, -NEG))
                alpha = jnp.exp2(m - m_new)
            m_ref[qi, sh] = m_new
            stats.append((alpha, [(slice(0, blk), p.astype(ACT))]))
        pvs = [jnp.concatenate([_dot(vt_ref[qi - off, sh, :, keys], p) for keys, p in parts], axis=1)
               for (qi, sh), (_, parts) in zip(work, stats)]
        for (qi, sh), (alpha, _), pv in zip(work, stats, pvs):
            acc_ref[qi, sh] = pv if off == 0 else alpha * acc_ref[qi, sh] + pv

    def finish(qi):
        ot = jnp.concatenate([acc_ref[qi, sh, :HEAD_DIM, :] * (1.0 / acc_ref[qi, sh, HEAD_DIM:HEAD_DIM + 1, :])
                              for sh in range(2)], axis=0)
        o_ref[0, rows_of(qi), :] = (ot.T * sg_ref[0, rows_of(qi), :].astype(F32)).astype(o_ref.dtype)

    for off in range(n_blk):
        qis = list(range(off, n_blk))
        for lo in range(0, len(qis), MOBA_GROUP):
            tiles(off, qis[lo:lo + MOBA_GROUP])
        finish(off)


def _moba(p, bias_a):
    b, s, _ = p.shape
    blk = MOBA_BLOCK
    n_blk = s // blk
    col = lambda c: pl.BlockSpec((1, s, LANE), lambda hp, bi: (bi, 0, COL_A + c * N_PAIRS + hp))
    return pl.pallas_call(
        _moba_kernel,
        grid=(N_PAIRS, b),
        in_specs=[col(0), col(1), col(2), col(3),
                  pl.BlockSpec((2, n_blk, blk, blk), lambda hp, bi: (hp, 0, 0, 0))],
        out_specs=pl.BlockSpec((1, s, LANE), lambda hp, bi: (bi, 0, hp)),
        out_shape=jax.ShapeDtypeStruct((b, s, W_BR), ACT),
        scratch_shapes=[
            pltpu.VMEM((n_blk, 2, LANE, blk), ACT),
            pltpu.VMEM((n_blk, 2, HEAD_DIM + ONES_ROWS, blk), ACT),
            pltpu.VMEM((n_blk, LANE), F32),
            pltpu.VMEM((n_blk, 2, n_blk, blk), F32),
            pltpu.VMEM((n_blk, 2, 1, blk), F32),
            pltpu.VMEM((n_blk, 2, HEAD_DIM + ONES_ROWS, blk), F32),
        ],
        compiler_params=_cparams(2),
        name="moba",
    )(p, p, p, p, bias_a)


def _band_attend(items):
    def fuse(blocks):
        if len(blocks) > 1 and all(valid is None for _, _, _, valid in blocks):
            ks, vts, biases, _ = zip(*blocks)
            return [(jnp.concatenate(ks, axis=0), jnp.concatenate(vts, axis=1), jnp.concatenate(biases, axis=0), None)]
        return blocks

    items = [(qs, fuse(blocks), sink) for qs, blocks, sink in items]

    probs = []
    for qs, blocks, sink in items:
        sc = [_dot(k, qs) for k, _, _, _ in blocks]
        ss = [s + bias if valid is None else jnp.where(valid, s + bias, NEG)
              for s, (_, _, bias, valid) in zip(sc, blocks)]
        m = functools.reduce(jnp.maximum, [jnp.max(s, axis=0, keepdims=True) for s in ss])
        if sink is not None:
            m = jnp.maximum(m, sink)
        probs.append((m, [jnp.exp2(s - m).astype(ACT) for s in ss]))
    outs = []
    for (_, blocks, sink), (m, ps) in zip(items, probs):
        n_v = blocks[0][1].shape[0]
        acc = functools.reduce(operator.add, [
            _dot(jnp.concatenate([vt, jnp.ones((ONES_ROWS, vt.shape[1]), ACT)], axis=0), p)
            for (_, vt, _, _), p in zip(blocks, ps)])
        l = acc[n_v:n_v + 1, :]
        if sink is not None:
            l = l + jnp.exp2(sink - m)
        outs.append((acc[:n_v, :] * (1.0 / l), m + jnp.log2(l)))
    return outs


def _band_bias(tv_ref, head, prev_includes_window_edge):
    kr = lax.broadcasted_iota(jnp.int32, (BAND, BAND), 0)
    qc = lax.broadcasted_iota(jnp.int32, (BAND, BAND), 1)
    own, prev = [_toeplitz(tv_ref[head, which], tv_ref[head, which + 1], BAND) for which in range(2)]
    prev_mask = (qc <= kr) if prev_includes_window_edge else (qc < kr)
    return jnp.where(qc >= kr, own, NEG), jnp.where(prev_mask, prev, NEG)


def _dil_kernel(q0_ref, k0_ref, v0_ref, q1_ref, k1_ref, v1_ref, q2_ref, k2_ref, v2_ref, sg_ref, tv_ref,
                o_ref, og_ref, lg_ref, bt_ref):
    hp = pl.program_id(0)
    b = pl.program_id(1)
    s = q0_ref.shape[1]
    n_tiles = s // BAND

    @pl.when(b == 0)
    def _():
        for g in range(N_DIL):
            for sh in range(2):
                own, prev = _band_bias(tv_ref, g * N_HEADS + 2 * hp + sh, True)
                bt_ref[g, sh, 0] = own
                bt_ref[g, sh, 1] = prev

    groups = ((q0_ref, k0_ref, v0_ref), (q1_ref, k1_ref, v1_ref), (q2_ref, k2_ref, v2_ref))

    def mix(rows, o0, l0):
        l1, l2 = lg_ref[0, rows, :], lg_ref[1, rows, :]
        m = jnp.maximum(l0, jnp.maximum(l1, l2))
        e0, e1, e2 = jnp.exp2(l0 - m), jnp.exp2(l1 - m), jnp.exp2(l2 - m)
        ob = (e0 * o0 + e1 * og_ref[0, rows, :] + e2 * og_ref[1, rows, :]) / (e0 + e1 + e2)
        o_ref[0, rows, :] = (ob * sg_ref[0, rows, :].astype(F32)).astype(o_ref.dtype)

    for g in (*range(1, N_DIL), 0):
        q_ref, k_ref, v_ref = groups[g]
        d = DIL_PAIRS[g][1]
        assert (d == 1) == (g == 0)
        per_phase = n_tiles // d
        has_prev = per_phase > 1
        assert per_phase % TILE_GROUP == 0 or TILE_GROUP % per_phase == 0

        def body(i, c, g=g, d=d, per_phase=per_phase, has_prev=has_prev, q_ref=q_ref, k_ref=k_ref, v_ref=v_ref):
            items, dsts = [], []
            k_own = vt_own = None
            for n in range(TILE_GROUP):
                u = i * TILE_GROUP + n
                rows = pl.ds(pl.multiple_of(u * BAND, BAND), BAND)
                t = lax.rem(u, per_phase)
                r = lax.div(u, per_phase)
                qt = _unit_rms_heads(q_ref[0, rows, :].T)
                k_prev, vt_prev = k_own, vt_own
                k_own = k_ref[0, rows, :]
                vt_own = v_ref[0, rows, :].T
                if has_prev:
                    phase_start_known = TILE_GROUP % per_phase == 0
                    prev_exists = (t > 0) if (n == 0 and not phase_start_known) else None
                    if phase_start_known and n % per_phase == 0:
                        k_prev = None
                    elif n == 0:
                        prow = pl.ds(pl.multiple_of(jnp.maximum(u - 1, 0) * BAND, BAND), BAND)
                        k_prev = k_ref[0, prow, :]
                        vt_prev = v_ref[0, prow, :].T
                for sh in range(2):
                    hrows = slice(HEAD_DIM * sh, HEAD_DIM * (sh + 1))
                    blocks = [(k_own, vt_own[hrows, :], bt_ref[g, sh, 0], None)]
                    if has_prev and k_prev is not None:
                        blocks.append((k_prev, vt_prev[hrows, :], bt_ref[g, sh, 1], prev_exists))
                    items.append((_head_rows(qt, HEAD_DIM * sh), blocks, None))
                dsts.append(rows if d == 1 else pl.ds(t * (BAND * d) + r, BAND, stride=d))
            res = _band_attend(items)
            for n, dst in enumerate(dsts):
                (o0, lse0), (o1, lse1) = res[2 * n], res[2 * n + 1]
                out = jnp.concatenate([o0, o1], axis=0).T
                lse = jnp.concatenate(
                    [jnp.broadcast_to(lse0, (HEAD_DIM, BAND)), jnp.broadcast_to(lse1, (HEAD_DIM, BAND))], axis=0).T
                if g == 0:
                    mix(dst, out, lse)
                else:
                    og_ref.at[g - 1][dst, :] = out
                    lg_ref.at[g - 1][dst, :] = lse
            return c

        lax.fori_loop(0, n_tiles // TILE_GROUP, body, 0)


def _dilated(p, tv_b):
    b, s, _ = p.shape
    col = lambda c: pl.BlockSpec((1, s, LANE), lambda hp, bi: (bi, 0, c + hp))
    qkv = [col(COL_B + 3 * N_PAIRS * g + N_PAIRS * i) for g in range(N_DIL) for i in range(3)]
    return pl.pallas_call(
        _dil_kernel,
        grid=(N_PAIRS, b),
        in_specs=qkv + [col(COL_SGB), pl.BlockSpec(tv_b.shape, lambda hp, bi: (0, 0, 0, 0))],
        out_specs=pl.BlockSpec((1, s, LANE), lambda hp, bi: (bi, 0, hp)),
        out_shape=jax.ShapeDtypeStruct((b, s, W_BR), ACT),
        scratch_shapes=[
            pltpu.VMEM((N_DIL - 1, s, LANE), F32),
            pltpu.VMEM((N_DIL - 1, s, LANE), F32),
            pltpu.VMEM((N_DIL, 2, 2, BAND, BAND), F32),
        ],
        compiler_params=_cparams(2),
        name="dilated",
    )(*([p] * 10), tv_b)


def _swa_kernel(sink_ref, q_ref, sg_ref, k_ref, v_ref, tv_ref, o_ref, bt_ref, vts_ref):
    hp = pl.program_id(0)
    b = pl.program_id(1)
    n_tiles = q_ref.shape[1] // BAND
    pairs_per_kv = N_PAIRS // KV_C

    @pl.when(b == 0)
    def _():
        for sh in range(2):
            own, prev = _band_bias(tv_ref, 2 * hp + sh, False)
            bt_ref[sh, 0] = own
            bt_ref[sh, 1] = prev

    kv_lo = HEAD_DIM * (hp // pairs_per_kv)
    row = lax.broadcasted_iota(jnp.int32, (LANE, BAND), 0)
    kv_rows = (row >= kv_lo) & (row < kv_lo + HEAD_DIM)

    def kv_vt(rows, slot):
        vts_ref[slot] = v_ref[0, rows, :].T
        return vts_ref[slot, pl.ds(pl.multiple_of(kv_lo, HEAD_DIM), HEAD_DIM), :]

    def body(i, c):
        items, dsts = [], []
        zero = jnp.zeros((LANE, BAND), ACT)
        k_own = vt_own = None
        for n in range(TILE_GROUP):
            t = i * TILE_GROUP + n
            rows = pl.ds(pl.multiple_of(t * BAND, BAND), BAND)
            qt = _unit_rms_heads(q_ref[0, rows, :].T)
            k_prev, vt_prev = k_own, vt_own
            k_own = k_ref[0, rows, :]
            vt_own = kv_vt(rows, n)
            if n == 0:
                prow = pl.ds(pl.multiple_of(jnp.maximum(t - 1, 0) * BAND, BAND), BAND)
                k_prev = k_ref[0, prow, :]
                vt_prev = kv_vt(prow, TILE_GROUP)
            prev_exists = (t > 0) if n == 0 else None
            for sh in range(2):
                qh = qt[HEAD_DIM * sh:HEAD_DIM * (sh + 1), :]
                qs = jnp.where(kv_rows, jnp.concatenate([qh, qh], axis=0), zero)
                items.append((qs, [(k_own, vt_own, bt_ref[sh, 0], None), (k_prev, vt_prev, bt_ref[sh, 1], prev_exists)],
                              sink_ref[2 * hp + sh]))
            dsts.append(rows)
        res = _band_attend(items)
        for n, rows in enumerate(dsts):
            ot = jnp.concatenate([o for o, _ in res[2 * n:2 * n + 2]], axis=0).T
            o_ref[0, rows, :] = (ot * sg_ref[0, rows, :].astype(F32)).astype(o_ref.dtype)
        return c

    lax.fori_loop(0, n_tiles // TILE_GROUP, body, 0)


def _swa(p, tv_c, sinks):
    b, s, _ = p.shape
    grid_spec = pltpu.PrefetchScalarGridSpec(
        num_scalar_prefetch=1,
        grid=(N_PAIRS, b),
        in_specs=[
            pl.BlockSpec((1, s, LANE), lambda hp, bi, sk: (bi, 0, COL_C + hp)),
            pl.BlockSpec((1, s, LANE), lambda hp, bi, sk: (bi, 0, COL_C + N_PAIRS + hp)),
            pl.BlockSpec((1, s, LANE), lambda hp, bi, sk: (bi, 0, COL_C + 2 * N_PAIRS)),
            pl.BlockSpec((1, s, LANE), lambda hp, bi, sk: (bi, 0, COL_C + 2 * N_PAIRS + 1)),
            pl.BlockSpec(tv_c.shape, lambda hp, bi, sk: (0, 0, 0, 0)),
        ],
        out_specs=pl.BlockSpec((1, s, LANE), lambda hp, bi, sk: (bi, 0, hp)),
        scratch_shapes=[pltpu.VMEM((2, 2, BAND, BAND), F32),
                        pltpu.VMEM((TILE_GROUP + 1, LANE, BAND), ACT)],
    )
    return pl.pallas_call(
        _swa_kernel,
        grid_spec=grid_spec,
        out_shape=jax.ShapeDtypeStruct((b, s, W_BR), ACT),
        compiler_params=_cparams(2),
        name="swa",
    )(sinks, p, p, p, p, tv_c)


def _merge_kernel(bra_ref, brb_ref, brc_ref, g_ref, wb_ref, wo_ref, x_ref, out_ref):
    merged = None
    for i, br_ref in enumerate((bra_ref, brb_ref, brc_ref)):
        term = g_ref[:, i * D_MODEL:(i + 1) * D_MODEL].astype(F32) * _dot(br_ref[...], wb_ref[i])
        merged = term if merged is None else merged + term
    out_ref[...] = x_ref[...] + _dot(merged.astype(ACT), wo_ref[...])


def _merge(bra, brb, brc, p2, w_br, w_out, x2, layer, ts=1024):
    n = x2.shape[0]
    row = lambda w: pl.BlockSpec((ts, w), lambda i: (i, 0))
    return pl.pallas_call(
        _merge_kernel,
        grid=(n // ts,),
        in_specs=[row(W_BR), row(W_BR), row(W_BR), row(GATE_W),
                  pl.BlockSpec((None,) + w_br.shape[1:], lambda i: (layer, 0, 0, 0)),
                  pl.BlockSpec((None,) + w_out.shape[1:], lambda i: (layer, 0, 0)),
                  row(D_MODEL)],
        out_specs=row(D_MODEL),
        out_shape=jax.ShapeDtypeStruct(x2.shape, F32),
        compiler_params=_cparams(1),
        name="merge",
    )(bra, brb, brc, p2, w_br, w_out, x2)


def _rel_bucket(dist):
    dist = jnp.maximum(dist, 0)
    max_exact = N_BUCKETS // 2
    log_ratio = jnp.log(jnp.maximum(dist, 1).astype(F32) / max_exact) / math.log(REL_MAX_DIST / max_exact)
    large = jnp.minimum(max_exact + (log_ratio * (N_BUCKETS - max_exact)).astype(jnp.int32), N_BUCKETS - 1)
    return jnp.where(dist < max_exact, dist, large)


def _bias_by_distance(tab, dist, chunk):
    vals = tab.astype(F32)[:, _rel_bucket(dist)] * LOG2E
    vals = jnp.pad(vals, ((0, 0), (chunk, 0)))
    return vals.reshape(tab.shape[0], -1, 1, chunk)


def _gain_rows(qk_g):
    depth = qk_g.shape[0]
    scale = HEAD_DIM ** -0.5 * LOG2E
    one = jnp.ones((depth, HEAD_DIM), F32)
    tile = lambda v, n: jnp.tile(v, (1, n))
    parts = [tile(one, N_HEADS), tile(qk_g[:, 1] * qk_g[:, 0] * scale, N_HEADS), tile(one, 2 * N_HEADS),
             tile(one, N_DIL * N_HEADS), tile(qk_g[:, 3] * qk_g[:, 2] * scale, N_DIL * N_HEADS),
             tile(one, (N_DIL + 1) * N_HEADS),
             tile(one, N_HEADS), tile(qk_g[:, 5] * qk_g[:, 4] * scale, KV_C), tile(one, KV_C + N_HEADS),
             tile(one, GATE_W // HEAD_DIM)]
    return jnp.concatenate(parts, axis=1)


def kernel(x, ln_g, w_in, qk_g, sinks, w_branch, w_out, rel_bias):
    b, s, d = x.shape
    depth = ln_g.shape[0]
    assert d == D_MODEL and s % DIL_PAIRS[-1][0] == 0 and w_in.shape[-1] == C_IN

    bias_a = _moba_bias(_bias_by_distance(rel_bias[:N_HEADS], jnp.arange(s), MOBA_BLOCK))
    tv_b = jnp.concatenate(
        [_bias_by_distance(rel_bias[OFF_B + g * N_HEADS:OFF_B + (g + 1) * N_HEADS], dd * jnp.arange(2 * BAND), BAND)
         for g, (_, dd) in enumerate(DIL_PAIRS)], axis=0)
    tv_c = _bias_by_distance(rel_bias[OFF_C:OFF_C + N_HEADS], jnp.arange(2 * BAND), BAND)
    seg = jnp.arange(TN) // HEAD_DIM
    e_mat = (seg[:, None] == seg[None, :]).astype(ACT)
    src = jnp.asarray([c for c, _ in _LAYOUT], jnp.int32)
    fn = jnp.asarray([f for _, f in _LAYOUT[::PAIR]], jnp.int32)
    gains = _gain_rows(qk_g)

    w_in_b = w_in.astype(ACT)
    w_br_b = w_branch.astype(ACT)
    w_out_b = w_out.astype(ACT)

    for l in range(depth):
        p = _project(x, ln_g[:, None, :], w_in_b, gains[l][None, :], e_mat, src, fn, l)
        bra = _moba(p, bias_a)
        brb = _dilated(p, tv_b)
        brc = _swa(p, tv_c, sinks[l] * LOG2E)
        x2 = _merge(bra.reshape(b * s, W_BR), brb.reshape(b * s, W_BR), brc.reshape(b * s, W_BR),
                    p.reshape(b * s, P_W), w_br_b, w_out_b, x.reshape(b * s, d), l)
        x = x2.reshape(b, s, d)
    return x
```

```python
import functools
import math
import operator

import jax
import jax.numpy as jnp
from jax import lax
from jax.experimental import pallas as pl
from jax.experimental.pallas import tpu as pltpu

D_MODEL = 1024
HEAD_DIM = 64
W_BR = D_MODEL // 2
N_HEADS = W_BR // HEAD_DIM
N_PAIRS = N_HEADS // 2
MOBA_BLOCK = 256
MOBA_TOPK = 3
DIL_PAIRS = ((128, 1), (512, 4), (2048, 16))
N_DIL = len(DIL_PAIRS)
KV_C = 2
SWA_WINDOW = 128
N_BUCKETS = 32
REL_MAX_DIST = 2048
OFF_B = N_HEADS
OFF_C = N_HEADS + N_DIL * N_HEADS
EPS = 1e-6
NEG = -1e30
LOG2E = math.log2(math.e)

LANE = 128
BAND = SWA_WINDOW
assert all(w == BAND * d for w, d in DIL_PAIRS)
TN = 256
ROWS = 256
TILE_GROUP = 16
MOBA_GROUP = 8
EPILOGUE_LAG = 3
SINGLE_OP_STRIDE = 4
ONES_ROWS = 16
VMEM_LIMIT = 56 * 1024 * 1024

ACT = jnp.bfloat16
F32 = jnp.float32

PLAIN, NORM, SILU, SIGMOID, KNORM_VPLAIN, NORM_D4, PLAIN_D4, NORM_D16, PLAIN_D16, SIGMOID_FIRST = range(10)
_B_GROUP = lambda g, norm, plain: [(8 + 2 * g, plain), (9 + 2 * g, plain), (14 + 2 * g, norm), (15 + 2 * g, norm),
                                   (20 + 2 * g, plain), (21 + 2 * g, plain)]
_LAYOUT = ([(33, SIGMOID_FIRST), (34, SIGMOID_FIRST)] + [(j, SIGMOID) for j in range(35, 45)]
           + [(0, PLAIN), (1, PLAIN), (2, NORM), (3, NORM), (4, PLAIN), (5, PLAIN), (6, SILU), (7, SILU)]
           + _B_GROUP(0, NORM, PLAIN) + _B_GROUP(1, NORM_D4, PLAIN_D4) + _B_GROUP(2, NORM_D16, PLAIN_D16)
           + [(26, SILU), (27, SILU)]
           + [(28, PLAIN), (29, PLAIN), (31, SILU), (32, SILU), (30, KNORM_VPLAIN)])
C_IN = len(_LAYOUT) * TN
PAIR = 2
_LAYOUT = _LAYOUT + [(_LAYOUT[-1][0], KNORM_VPLAIN)] * (-len(_LAYOUT) % PAIR)
assert all(_LAYOUT[j][1] == _LAYOUT[j - j % PAIR][1] for j in range(len(_LAYOUT)))
assert [f for _, f in _LAYOUT].count(KNORM_VPLAIN) == PAIR
N_STEPS = len(_LAYOUT) // PAIR
P_W = len(_LAYOUT) * TN
COL_GATE, COL_A, COL_B, COL_SGB, COL_C = 0, 24, 40, 76, 80
GATE_W = 3 * D_MODEL
assert COL_GATE == 0


def _cparams(n_axes):
    return pltpu.CompilerParams(dimension_semantics=("arbitrary",) * n_axes, vmem_limit_bytes=VMEM_LIMIT)


def _dot(a, b):
    return jnp.dot(a, b, preferred_element_type=F32)


def _sigmoid(a):
    return 0.5 * jnp.tanh(0.5 * a) + 0.5


def _proj_kernel(src_ref, fn_ref, x_ref, lng_ref, *refs):
    del src_ref
    w_refs, gain_refs = refs[:PAIR], refs[PAIR:2 * PAIR]
    e_ref, o_ref, xn_ref, acc_ref, acc2_ref = refs[2 * PAIR:]
    j = pl.program_id(1)
    s = x_ref.shape[1]
    n_chunks = s // ROWS
    lanes_per_block = TN // LANE

    def normalise_rows():
        for i in range(n_chunks):
            xv = x_ref[0, i * ROWS:(i + 1) * ROWS, :]
            ms = jnp.mean(xv * xv, axis=-1, keepdims=True)
            xn_ref[i * ROWS:(i + 1) * ROWS, :] = (xv * lax.rsqrt(ms + EPS) * lng_ref[...]).astype(ACT)

    def mean_square(a):
        return _dot((a * a).astype(ACT), e_ref[...])

    def head_norm(a, ss, gain):
        return a * lax.rsqrt(ss * (1.0 / HEAD_DIM) + EPS) * gain

    def k_norm_v_plain(a, ss, gain):
        lane = lax.broadcasted_iota(jnp.int32, a.shape, 1)
        return jnp.where(lane < KV_C * HEAD_DIM, head_norm(a, ss, gain), a)

    plain = (None, lambda a, _, gain: a)
    silu = (None, lambda a, _, gain: a * _sigmoid(a))
    sigmoid = (None, lambda a, _, gain: _sigmoid(a))
    norm = (mean_square, head_norm)
    knorm_vplain = (mean_square, k_norm_v_plain)

    def pipeline(n_items, per_block, produce, epilogue, store):
        pre, post = epilogue
        tiles = {}
        for n in range(n_items + EPILOGUE_LAG):
            if n < n_items:
                tiles[n] = produce(n)
            if n >= EPILOGUE_LAG:
                m = n - EPILOGUE_LAG
                tile = tiles.pop(m)
                store(m, post(tile, pre(tile) if pre is not None else None, gain_refs[m // per_block][...]))

    def phase_rows(src_ref, slab, t, d):
        r, m0 = divmod(t * BAND, s // d)
        return src_ref.at[slab][pl.ds(m0 * d + r, BAND, stride=d), :]

    fn = fn_ref[j]

    def chunk(n):
        h, i = divmod(n, n_chunks)
        return _dot(xn_ref[i * ROWS:(i + 1) * ROWS, :], w_refs[h][...])

    def out_cols(h):
        return slice(h * TN, (h + 1) * TN)

    def direct(code, epilogue, live=PAIR, first=False):
        @pl.when(fn == code)
        def _():
            if first:
                normalise_rows()

            def store(n, val):
                h, i = divmod(n, n_chunks)
                o_ref[0, i * ROWS:(i + 1) * ROWS, out_cols(h)] = val.astype(o_ref.dtype)

            pipeline(live * n_chunks, n_chunks, chunk, epilogue, store)
            if live < PAIR:
                o_ref[0, :, live * TN:] = jnp.zeros((s, (PAIR - live) * TN), o_ref.dtype)

    def phase_major(code, epilogue, d):
        @pl.when(fn == code)
        def _():
            def stage(n, val):
                h, i = divmod(n, n_chunks)
                for c in range(lanes_per_block):
                    acc_ref[h * lanes_per_block + c, i * ROWS:(i + 1) * ROWS, :] = val[:, c * LANE:(c + 1) * LANE]

            pipeline(PAIR * n_chunks, n_chunks, chunk, epilogue, stage)
            src_ref, stride = acc_ref, d
            if d > SINGLE_OP_STRIDE:
                assert d == SINGLE_OP_STRIDE * SINGLE_OP_STRIDE
                for slab in range(PAIR * lanes_per_block):
                    for t in range(s // BAND):
                        acc2_ref[slab, t * BAND:(t + 1) * BAND, :] = phase_rows(acc_ref, slab, t, SINGLE_OP_STRIDE)
                src_ref, stride = acc2_ref, SINGLE_OP_STRIDE
            for h in range(PAIR):
                for t in range(s // BAND):
                    a = jnp.concatenate([phase_rows(src_ref, h * lanes_per_block + c, t, stride)
                                         for c in range(lanes_per_block)], axis=1)
                    o_ref[0, t * BAND:(t + 1) * BAND, out_cols(h)] = a.astype(o_ref.dtype)

    direct(PLAIN, plain)
    direct(NORM, norm)
    direct(SILU, silu)
    direct(SIGMOID, sigmoid)
    direct(SIGMOID_FIRST, sigmoid, first=True)
    direct(KNORM_VPLAIN, knorm_vplain, live=1)
    phase_major(NORM_D4, norm, DIL_PAIRS[1][1])
    phase_major(PLAIN_D4, plain, DIL_PAIRS[1][1])
    phase_major(NORM_D16, norm, DIL_PAIRS[2][1])
    phase_major(PLAIN_D16, plain, DIL_PAIRS[2][1])


def _w_index(bi, j, src_ref, fn_ref, *, layer, slot):
    return layer, 0, src_ref[PAIR * j + slot]


def _gain_index(bi, j, src_ref, fn_ref, *, slot):
    return 0, src_ref[PAIR * j + slot]


def _project(x, ln_g, w_in, gain_row, e_mat, src, fn, layer):
    b, s, d = x.shape
    n_in = PAIR
    grid_spec = pltpu.PrefetchScalarGridSpec(
        num_scalar_prefetch=2,
        grid=(b, N_STEPS),
        in_specs=[
            pl.BlockSpec((1, s, d), lambda bi, j, src_ref, fn_ref: (jnp.minimum(bi + jnp.minimum(j, 1), b - 1), 0, 0)),
            pl.BlockSpec((None, 1, d), lambda bi, j, src_ref, fn_ref: (layer, 0, 0)),
        ] + [pl.BlockSpec((None, d, TN), functools.partial(_w_index, layer=layer, slot=n)) for n in range(n_in)]
          + [pl.BlockSpec((1, TN), functools.partial(_gain_index, slot=n)) for n in range(n_in)]
          + [pl.BlockSpec((TN, TN), lambda bi, j, src_ref, fn_ref: (0, 0))],
        out_specs=pl.BlockSpec((1, s, n_in * TN), lambda bi, j, src_ref, fn_ref: (bi, 0, j)),
        scratch_shapes=[pltpu.VMEM((s, d), ACT), pltpu.VMEM((PAIR * TN // LANE, s, LANE), F32),
                        pltpu.VMEM((PAIR * TN // LANE, s, LANE), F32)],
    )
    return pl.pallas_call(
        _proj_kernel,
        grid_spec=grid_spec,
        out_shape=jax.ShapeDtypeStruct((b, s, P_W), ACT),
        compiler_params=_cparams(2),
        name="proj",
    )(src, fn, x, ln_g, *([w_in] * n_in), *([gain_row] * n_in), e_mat)


def _toeplitz(lo, hi, n):
    x = jnp.concatenate([jnp.broadcast_to(lo, (n, n)), jnp.broadcast_to(hi, (n, n))], axis=1)
    return pltpu.roll(x, 0, 1, stride=1, stride_axis=0)[:, n:]


def _unit_rms_heads(qt):
    halves = []
    for sh in range(2):
        x = qt[HEAD_DIM * sh:HEAD_DIM * (sh + 1), :].astype(F32)
        halves.append(x * lax.rsqrt(jnp.mean(x * x, axis=0, keepdims=True) + EPS))
    return jnp.concatenate(halves, axis=0).astype(qt.dtype)


def _head_rows(qt, lo):
    row = lax.broadcasted_iota(jnp.int32, qt.shape, 0)
    return jnp.where((row >= lo) & (row < lo + HEAD_DIM), qt, jnp.zeros_like(qt))


def _moba_bias_kernel(tv_ref, o_ref):
    h = pl.program_id(0)
    blk = MOBA_BLOCK
    kr = lax.broadcasted_iota(jnp.int32, (blk, blk), 0)
    qc = lax.broadcasted_iota(jnp.int32, (blk, blk), 1)

    def body(o, c):
        t = _toeplitz(tv_ref[h, o], tv_ref[h, o + 1], blk)
        o_ref[0, o] = jnp.where((qc >= kr) | (o > 0), t, NEG)
        return c
    lax.fori_loop(0, o_ref.shape[1], body, 0)


def _moba_bias(tv_a):
    n_heads, n_chunks = tv_a.shape[:2]
    blk = MOBA_BLOCK
    return pl.pallas_call(
        _moba_bias_kernel,
        grid=(n_heads,),
        in_specs=[pl.BlockSpec(tv_a.shape, lambda h: (0, 0, 0, 0))],
        out_specs=pl.BlockSpec((1, n_chunks - 1, blk, blk), lambda h: (h, 0, 0, 0)),
        out_shape=jax.ShapeDtypeStruct((n_heads, n_chunks - 1, blk, blk), F32),
        compiler_params=_cparams(1),
        name="moba_bias",
    )(tv_a)


def _moba_kernel(q_ref, k_ref, v_ref, sg_ref, bt_ref, o_ref, qts_ref, vt_ref, km_ref, sel_ref, m_ref, acc_ref):
    blk = MOBA_BLOCK
    n_blk = k_ref.shape[1] // blk

    def rows_of(i):
        return pl.ds(i * blk if isinstance(i, int) else pl.multiple_of(i * blk, blk), blk)

    for i in range(n_blk):
        vt = v_ref[0, rows_of(i), :].T
        for sh in range(2):
            vt_ref[i, sh] = jnp.concatenate(
                [vt[HEAD_DIM * sh:HEAD_DIM * (sh + 1), :], jnp.ones((ONES_ROWS, blk), ACT)], axis=0)
        km_ref[i:i + 1, :] = jnp.mean(k_ref[0, rows_of(i), :].astype(F32), axis=0, keepdims=True)
        qt = _unit_rms_heads(q_ref[0, rows_of(i), :].T)
        for sh in range(2):
            qts_ref[i, sh] = _head_rows(qt, HEAD_DIM * sh)

    km = km_ref[...]
    km0 = km.astype(ACT).astype(F32)
    km1 = (km - km0).astype(ACT).astype(F32)
    km2 = km - km0 - km1
    km_terms = jnp.concatenate([km0, km1, km2], axis=0).astype(ACT)
    blk_row = lax.broadcasted_iota(jnp.int32, (n_blk, blk), 0)

    ranked = [(qi, sh) for qi in range(MOBA_TOPK + 1, n_blk) for sh in range(2)]
    gates = [_dot(km_terms, qts_ref[qi, sh]) for qi, sh in ranked]
    for (qi, sh), g3 in zip(ranked, gates):
        g = g3[:n_blk] + g3[n_blk:2 * n_blk] + g3[2 * n_blk:]
        g = jnp.where(blk_row < qi, g, NEG)
        sel_rows = []
        for jb in range(qi):
            gj = g[jb:jb + 1, :]
            beats = (g > gj) | ((g == gj) & (blk_row < jb))
            cnt = jnp.sum(beats.astype(F32), axis=0, keepdims=True)
            sel_rows.append(jnp.where(cnt < MOBA_TOPK, 1.0, 0.0))
        sel_rows.append(jnp.zeros((n_blk - qi, blk), F32))
        sel_ref[qi, sh] = jnp.concatenate(sel_rows, axis=0)

    def tiles(off, qis):
        k_blks = [k_ref[0, rows_of(qi - off), :] for qi in qis]
        scores = [_dot(k_blk, qts_ref[qi, sh]) for qi, k_blk in zip(qis, k_blks) for sh in range(2)]
        work = [(qi, sh) for qi in qis for sh in range(2)]
        stats = []
        half = blk // 2
        for (qi, sh), st in zip(work, scores):
            if off == 0:
                bias = bt_ref[sh, 0]
                parts = [(st[:half, :half] + bias[:half, :half], slice(0, half)),
                         (st[:, half:] + bias[:, half:], slice(0, blk))]
                maxes = [jnp.max(s_part, axis=0, keepdims=True) for s_part, _ in parts]
                m_ref[qi, sh] = jnp.concatenate(maxes, axis=1)
                stats.append((None, [(keys, jnp.exp2(s_part - m_part).astype(ACT))
                                     for (s_part, keys), m_part in zip(parts, maxes)]))
                continue
            st = st + bt_ref[sh, off]
            tile_max = jnp.max(st, axis=0, keepdims=True)
            if qi <= MOBA_TOPK:
                m = m_ref[qi, sh]
                m_new = jnp.maximum(m, tile_max)
                p = jnp.exp2(st - m_new)
                alpha = jnp.exp2(m - m_new)
            else:
                sel = sel_ref[qi, sh, pl.ds(qi - off, 1), :] > 0.5
                m = m_ref[qi, sh]
                m_new = jnp.where(sel, jnp.maximum(m, tile_max), m)
                p = jnp.exp2(st - jnp.where(sel, m_new, -NEG))
                alpha = jnp.exp2(m - m_new)
            m_ref[qi, sh] = m_new
            stats.append((alpha, [(slice(0, blk), p.astype(ACT))]))
        pvs = [jnp.concatenate([_dot(vt_ref[qi - off, sh, :, keys], p) for keys, p in parts], axis=1)
               for (qi, sh), (_, parts) in zip(work, stats)]
        for (qi, sh), (alpha, _), pv in zip(work, stats, pvs):
            acc_ref[qi, sh] = pv if off == 0 else alpha * acc_ref[qi, sh] + pv

    def finish(qi):
        ot = jnp.concatenate([acc_ref[qi, sh, :HEAD_DIM, :] * (1.0 / acc_ref[qi, sh, HEAD_DIM:HEAD_DIM + 1, :])
                              for sh in range(2)], axis=0)
        o_ref[0, rows_of(qi), :] = (ot.T * sg_ref[0, rows_of(qi), :].astype(F32)).astype(o_ref.dtype)

    for off in range(n_blk):
        qis = list(range(off, n_blk))
        for lo in range(0, len(qis), MOBA_GROUP):
            tiles(off, qis[lo:lo + MOBA_GROUP])
        finish(off)


def _moba(p, bias_a):
    b, s, _ = p.shape
    blk = MOBA_BLOCK
    n_blk = s // blk
    col = lambda c: pl.BlockSpec((1, s, LANE), lambda hp, bi: (bi, 0, COL_A + c * N_PAIRS + hp))
    return pl.pallas_call(
        _moba_kernel,
        grid=(N_PAIRS, b),
        in_specs=[col(0), col(1), col(2), col(3),
                  pl.BlockSpec((2, n_blk, blk, blk), lambda hp, bi: (hp, 0, 0, 0))],
        out_specs=pl.BlockSpec((1, s, LANE), lambda hp, bi: (bi, 0, hp)),
        out_shape=jax.ShapeDtypeStruct((b, s, W_BR), ACT),
        scratch_shapes=[
            pltpu.VMEM((n_blk, 2, LANE, blk), ACT),
            pltpu.VMEM((n_blk, 2, HEAD_DIM + ONES_ROWS, blk), ACT),
            pltpu.VMEM((n_blk, LANE), F32),
            pltpu.VMEM((n_blk, 2, n_blk, blk), F32),
            pltpu.VMEM((n_blk, 2, 1, blk), F32),
            pltpu.VMEM((n_blk, 2, HEAD_DIM + ONES_ROWS, blk), F32),
        ],
        compiler_params=_cparams(2),
        name="moba",
    )(p, p, p, p, bias_a)


def _band_attend(items):
    def fuse(blocks):
        if len(blocks) > 1 and all(valid is None for _, _, _, valid in blocks):
            ks, vts, biases, _ = zip(*blocks)
            return [(jnp.concatenate(ks, axis=0), jnp.concatenate(vts, axis=1), jnp.concatenate(biases, axis=0), None)]
        return blocks

    items = [(qs, fuse(blocks), sink) for qs, blocks, sink in items]

    probs = []
    for qs, blocks, sink in items:
        sc = [_dot(k, qs) for k, _, _, _ in blocks]
        ss = [s + bias if valid is None else jnp.where(valid, s + bias, NEG)
              for s, (_, _, bias, valid) in zip(sc, blocks)]
        m = functools.reduce(jnp.maximum, [jnp.max(s, axis=0, keepdims=True) for s in ss])
        if sink is not None:
            m = jnp.maximum(m, sink)
        probs.append((m, [jnp.exp2(s - m).astype(ACT) for s in ss]))
    outs = []
    for (_, blocks, sink), (m, ps) in zip(items, probs):
        n_v = blocks[0][1].shape[0]
        acc = functools.reduce(operator.add, [
            _dot(jnp.concatenate([vt, jnp.ones((ONES_ROWS, vt.shape[1]), ACT)], axis=0), p)
            for (_, vt, _, _), p in zip(blocks, ps)])
        l = acc[n_v:n_v + 1, :]
        if sink is not None:
            l = l + jnp.exp2(sink - m)
        outs.append((acc[:n_v, :] * (1.0 / l), m + jnp.log2(l)))
    return outs


def _band_bias(tv_ref, head, prev_includes_window_edge):
    kr = lax.broadcasted_iota(jnp.int32, (BAND, BAND), 0)
    qc = lax.broadcasted_iota(jnp.int32, (BAND, BAND), 1)
    own, prev = [_toeplitz(tv_ref[head, which], tv_ref[head, which + 1], BAND) for which in range(2)]
    prev_mask = (qc <= kr) if prev_includes_window_edge else (qc < kr)
    return jnp.where(qc >= kr, own, NEG), jnp.where(prev_mask, prev, NEG)


def _dil_kernel(q0_ref, k0_ref, v0_ref, q1_ref, k1_ref, v1_ref, q2_ref, k2_ref, v2_ref, sg_ref, tv_ref,
                o_ref, og_ref, lg_ref, bt_ref):
    hp = pl.program_id(0)
    b = pl.program_id(1)
    s = q0_ref.shape[1]
    n_tiles = s // BAND

    @pl.when(b == 0)
    def _():
        for g in range(N_DIL):
            for sh in range(2):
                own, prev = _band_bias(tv_ref, g * N_HEADS + 2 * hp + sh, True)
                bt_ref[g, sh, 0] = own
                bt_ref[g, sh, 1] = prev

    groups = ((q0_ref, k0_ref, v0_ref), (q1_ref, k1_ref, v1_ref), (q2_ref, k2_ref, v2_ref))

    def mix(rows, o0, l0):
        l1, l2 = lg_ref[0, rows, :], lg_ref[1, rows, :]
        m = jnp.maximum(l0, jnp.maximum(l1, l2))
        e0, e1, e2 = jnp.exp2(l0 - m), jnp.exp2(l1 - m), jnp.exp2(l2 - m)
        ob = (e0 * o0 + e1 * og_ref[0, rows, :] + e2 * og_ref[1, rows, :]) / (e0 + e1 + e2)
        o_ref[0, rows, :] = (ob * sg_ref[0, rows, :].astype(F32)).astype(o_ref.dtype)

    for g in (*range(1, N_DIL), 0):
        q_ref, k_ref, v_ref = groups[g]
        d = DIL_PAIRS[g][1]
        assert (d == 1) == (g == 0)
        per_phase = n_tiles // d
        has_prev = per_phase > 1
        assert per_phase % TILE_GROUP == 0 or TILE_GROUP % per_phase == 0

        def body(i, c, g=g, d=d, per_phase=per_phase, has_prev=has_prev, q_ref=q_ref, k_ref=k_ref, v_ref=v_ref):
            items, dsts = [], []
            k_own = vt_own = None
            for n in range(TILE_GROUP):
                u = i * TILE_GROUP + n
                rows = pl.ds(pl.multiple_of(u * BAND, BAND), BAND)
                t = lax.rem(u, per_phase)
                r = lax.div(u, per_phase)
                qt = _unit_rms_heads(q_ref[0, rows, :].T)
                k_prev, vt_prev = k_own, vt_own
                k_own = k_ref[0, rows, :]
                vt_own = v_ref[0, rows, :].T
                if has_prev:
                    phase_start_known = TILE_GROUP % per_phase == 0
                    prev_exists = (t > 0) if (n == 0 and not phase_start_known) else None
                    if phase_start_known and n % per_phase == 0:
                        k_prev = None
                    elif n == 0:
                        prow = pl.ds(pl.multiple_of(jnp.maximum(u - 1, 0) * BAND, BAND), BAND)
                        k_prev = k_ref[0, prow, :]
                        vt_prev = v_ref[0, prow, :].T
                for sh in range(2):
                    hrows = slice(HEAD_DIM * sh, HEAD_DIM * (sh + 1))
                    blocks = [(k_own, vt_own[hrows, :], bt_ref[g, sh, 0], None)]
                    if has_prev and k_prev is not None:
                        blocks.append((k_prev, vt_prev[hrows, :], bt_ref[g, sh, 1], prev_exists))
                    items.append((_head_rows(qt, HEAD_DIM * sh), blocks, None))
                dsts.append(rows if d == 1 else pl.ds(t * (BAND * d) + r, BAND, stride=d))
            res = _band_attend(items)
            for n, dst in enumerate(dsts):
                (o0, lse0), (o1, lse1) = res[2 * n], res[2 * n + 1]
                out = jnp.concatenate([o0, o1], axis=0).T
                lse = jnp.concatenate(
                    [jnp.broadcast_to(lse0, (HEAD_DIM, BAND)), jnp.broadcast_to(lse1, (HEAD_DIM, BAND))], axis=0).T
                if g == 0:
                    mix(dst, out, lse)
                else:
                    og_ref.at[g - 1][dst, :] = out
                    lg_ref.at[g - 1][dst, :] = lse
            return c

        lax.fori_loop(0, n_tiles // TILE_GROUP, body, 0)


def _dilated(p, tv_b):
    b, s, _ = p.shape
    col = lambda c: pl.BlockSpec((1, s, LANE), lambda hp, bi: (bi, 0, c + hp))
    qkv = [col(COL_B + 3 * N_PAIRS * g + N_PAIRS * i) for g in range(N_DIL) for i in range(3)]
    return pl.pallas_call(
        _dil_kernel,
        grid=(N_PAIRS, b),
        in_specs=qkv + [col(COL_SGB), pl.BlockSpec(tv_b.shape, lambda hp, bi: (0, 0, 0, 0))],
        out_specs=pl.BlockSpec((1, s, LANE), lambda hp, bi: (bi, 0, hp)),
        out_shape=jax.ShapeDtypeStruct((b, s, W_BR), ACT),
        scratch_shapes=[
            pltpu.VMEM((N_DIL - 1, s, LANE), F32),
            pltpu.VMEM((N_DIL - 1, s, LANE), F32),
            pltpu.VMEM((N_DIL, 2, 2, BAND, BAND), F32),
        ],
        compiler_params=_cparams(2),
        name="dilated",
    )(*([p] * 10), tv_b)


def _swa_kernel(sink_ref, q_ref, sg_ref, k_ref, v_ref, tv_ref, o_ref, bt_ref, vts_ref):
    hp = pl.program_id(0)
    b = pl.program_id(1)
    n_tiles = q_ref.shape[1] // BAND
    pairs_per_kv = N_PAIRS // KV_C

    @pl.when(b == 0)
    def _():
        for sh in range(2):
            own, prev = _band_bias(tv_ref, 2 * hp + sh, False)
            bt_ref[sh, 0] = own
            bt_ref[sh, 1] = prev

    kv_lo = HEAD_DIM * (hp // pairs_per_kv)
    row = lax.broadcasted_iota(jnp.int32, (LANE, BAND), 0)
    kv_rows = (row >= kv_lo) & (row < kv_lo + HEAD_DIM)

    def kv_vt(rows, slot):
        vts_ref[slot] = v_ref[0, rows, :].T
        return vts_ref[slot, pl.ds(pl.multiple_of(kv_lo, HEAD_DIM), HEAD_DIM), :]

    def body(i, c):
        items, dsts = [], []
        zero = jnp.zeros((LANE, BAND), ACT)
        k_own = vt_own = None
        for n in range(TILE_GROUP):
            t = i * TILE_GROUP + n
            rows = pl.ds(pl.multiple_of(t * BAND, BAND), BAND)
            qt = _unit_rms_heads(q_ref[0, rows, :].T)
            k_prev, vt_prev = k_own, vt_own
            k_own = k_ref[0, rows, :]
            vt_own = kv_vt(rows, n)
            if n == 0:
                prow = pl.ds(pl.multiple_of(jnp.maximum(t - 1, 0) * BAND, BAND), BAND)
                k_prev = k_ref[0, prow, :]
                vt_prev = kv_vt(prow, TILE_GROUP)
            prev_exists = (t > 0) if n == 0 else None
            for sh in range(2):
                qh = qt[HEAD_DIM * sh:HEAD_DIM * (sh + 1), :]
                qs = jnp.where(kv_rows, jnp.concatenate([qh, qh], axis=0), zero)
                items.append((qs, [(k_own, vt_own, bt_ref[sh, 0], None), (k_prev, vt_prev, bt_ref[sh, 1], prev_exists)],
                              sink_ref[2 * hp + sh]))
            dsts.append(rows)
        res = _band_attend(items)
        for n, rows in enumerate(dsts):
            ot = jnp.concatenate([o for o, _ in res[2 * n:2 * n + 2]], axis=0).T
            o_ref[0, rows, :] = (ot * sg_ref[0, rows, :].astype(F32)).astype(o_ref.dtype)
        return c

    lax.fori_loop(0, n_tiles // TILE_GROUP, body, 0)


def _swa(p, tv_c, sinks):
    b, s, _ = p.shape
    grid_spec = pltpu.PrefetchScalarGridSpec(
        num_scalar_prefetch=1,
        grid=(N_PAIRS, b),
        in_specs=[
            pl.BlockSpec((1, s, LANE), lambda hp, bi, sk: (bi, 0, COL_C + hp)),
            pl.BlockSpec((1, s, LANE), lambda hp, bi, sk: (bi, 0, COL_C + N_PAIRS + hp)),
            pl.BlockSpec((1, s, LANE), lambda hp, bi, sk: (bi, 0, COL_C + 2 * N_PAIRS)),
            pl.BlockSpec((1, s, LANE), lambda hp, bi, sk: (bi, 0, COL_C + 2 * N_PAIRS + 1)),
            pl.BlockSpec(tv_c.shape, lambda hp, bi, sk: (0, 0, 0, 0)),
        ],
        out_specs=pl.BlockSpec((1, s, LANE), lambda hp, bi, sk: (bi, 0, hp)),
        scratch_shapes=[pltpu.VMEM((2, 2, BAND, BAND), F32),
                        pltpu.VMEM((TILE_GROUP + 1, LANE, BAND), ACT)],
    )
    return pl.pallas_call(
        _swa_kernel,
        grid_spec=grid_spec,
        out_shape=jax.ShapeDtypeStruct((b, s, W_BR), ACT),
        compiler_params=_cparams(2),
        name="swa",
    )(sinks, p, p, p, p, tv_c)


def _merge_kernel(bra_ref, brb_ref, brc_ref, g_ref, wb_ref, wo_ref, x_ref, out_ref):
    merged = None
    for i, br_ref in enumerate((bra_ref, brb_ref, brc_ref)):
        term = g_ref[:, i * D_MODEL:(i + 1) * D_MODEL].astype(F32) * _dot(br_ref[...], wb_ref[i])
        merged = term if merged is None else merged + term
    out_ref[...] = x_ref[...] + _dot(merged.astype(ACT), wo_ref[...])


def _merge(bra, brb, brc, p2, w_br, w_out, x2, layer, ts=1024):
    n = x2.shape[0]
    row = lambda w: pl.BlockSpec((ts, w), lambda i: (i, 0))
    return pl.pallas_call(
        _merge_kernel,
        grid=(n // ts,),
        in_specs=[row(W_BR), row(W_BR), row(W_BR), row(GATE_W),
                  pl.BlockSpec((None,) + w_br.shape[1:], lambda i: (layer, 0, 0, 0)),
                  pl.BlockSpec((None,) + w_out.shape[1:], lambda i: (layer, 0, 0)),
                  row(D_MODEL)],
        out_specs=row(D_MODEL),
        out_shape=jax.ShapeDtypeStruct(x2.shape, F32),
        compiler_params=_cparams(1),
        name="merge",
    )(bra, brb, brc, p2, w_br, w_out, x2)


def _rel_bucket(dist):
    dist = jnp.maximum(dist, 0)
    max_exact = N_BUCKETS // 2
    log_ratio = jnp.log(jnp.maximum(dist, 1).astype(F32) / max_exact) / math.log(REL_MAX_DIST / max_exact)
    large = jnp.minimum(max_exact + (log_ratio * (N_BUCKETS - max_exact)).astype(jnp.int32), N_BUCKETS - 1)
    return jnp.where(dist < max_exact, dist, large)


def _bias_by_distance(tab, dist, chunk):
    vals = tab.astype(F32)[:, _rel_bucket(dist)] * LOG2E
    vals = jnp.pad(vals, ((0, 0), (chunk, 0)))
    return vals.reshape(tab.shape[0], -1, 1, chunk)


def _gain_rows(qk_g):
    depth = qk_g.shape[0]
    scale = HEAD_DIM ** -0.5 * LOG2E
    one = jnp.ones((depth, HEAD_DIM), F32)
    tile = lambda v, n: jnp.tile(v, (1, n))
    parts = [tile(one, N_HEADS), tile(qk_g[:, 1] * qk_g[:, 0] * scale, N_HEADS), tile(one, 2 * N_HEADS),
             tile(one, N_DIL * N_HEADS), tile(qk_g[:, 3] * qk_g[:, 2] * scale, N_DIL * N_HEADS),
             tile(one, (N_DIL + 1) * N_HEADS),
             tile(one, N_HEADS), tile(qk_g[:, 5] * qk_g[:, 4] * scale, KV_C), tile(one, KV_C + N_HEADS),
             tile(one, GATE_W // HEAD_DIM)]
    return jnp.concatenate(parts, axis=1)


def kernel(x, ln_g, w_in, qk_g, sinks, w_branch, w_out, rel_bias):
    b, s, d = x.shape
    depth = ln_g.shape[0]
    assert d == D_MODEL and s % DIL_PAIRS[-1][0] == 0 and w_in.shape[-1] == C_IN

    bias_a = _moba_bias(_bias_by_distance(rel_bias[:N_HEADS], jnp.arange(s), MOBA_BLOCK))
    tv_b = jnp.concatenate(
        [_bias_by_distance(rel_bias[OFF_B + g * N_HEADS:OFF_B + (g + 1) * N_HEADS], dd * jnp.arange(2 * BAND), BAND)
         for g, (_, dd) in enumerate(DIL_PAIRS)], axis=0)
    tv_c = _bias_by_distance(rel_bias[OFF_C:OFF_C + N_HEADS], jnp.arange(2 * BAND), BAND)
    seg = jnp.arange(TN) // HEAD_DIM
    e_mat = (seg[:, None] == seg[None, :]).astype(ACT)
    src = jnp.asarray([c for c, _ in _LAYOUT], jnp.int32)
    fn = jnp.asarray([f for _, f in _LAYOUT[::PAIR]], jnp.int32)
    gains = _gain_rows(qk_g)

    w_in_b = w_in.astype(ACT)
    w_br_b = w_branch.astype(ACT)
    w_out_b = w_out.astype(ACT)

    for l in range(depth):
        p = _project(x, ln_g[:, None, :], w_in_b, gains[l][None, :], e_mat, src, fn, l)
        bra = _moba(p, bias_a)
        brb = _dilated(p, tv_b)
        brc = _swa(p, tv_c, sinks[l] * LOG2E)
        x2 = _merge(bra.reshape(b * s, W_BR), brb.reshape(b * s, W_BR), brc.reshape(b * s, W_BR),
                    p.reshape(b * s, P_W), w_br_b, w_out_b, x.reshape(b * s, d), l)
        x = x2.reshape(b, s, d)
    return x
```
